```python
import jax
import jax.numpy as jnp
from jax import lax
import numpy as np

D_MODEL = 1024
BATCH = 2
SEQ = 8192
DEPTH = 2
DEC_BATCH = 32
DEC_SEQ = 8
PAST_LEN = 16384
PAGE_SIZE = 128

N_MIXERS = 2
A_GROUPS = ((128, 1), (512, 4), (2048, 16))
A_HEAD_DIM = 64
A_HEADS = D_MODEL // A_HEAD_DIM
A_WIDTH = A_HEADS * A_HEAD_DIM
A_IN_WIDTH = len(A_GROUPS) * 3 * A_WIDTH
ROPE_THETA = 10000.0
B_HEAD_K = 128
B_HEAD_V = 128
B_QK_HEADS = D_MODEL // 128
B_V_HEADS = 2 * B_QK_HEADS
B_QK_WIDTH = B_QK_HEADS * B_HEAD_K
B_V_WIDTH = B_V_HEADS * B_HEAD_V
B_CONV_DIM = 2 * B_QK_WIDTH + B_V_WIDTH
B_IN_WIDTH = B_CONV_DIM + B_V_WIDTH + 2 * B_V_HEADS
B_CONV_WIDTH = 4
B_CHUNK = 64
D_FF = ((8 * D_MODEL // 3 + 255) // 256) * 256
FFN_CONV_WIDTH = 3
NORM_EPS = 1e-6

kernel_name = 'dilated_swa_gated_deltanet_convffn_adaln_step'


def rmsnorm(x, g):
    xf = x.astype(jnp.float32)
    return xf * lax.rsqrt(jnp.mean(xf * xf, axis=-1, keepdims=True) + NORM_EPS) * g.astype(jnp.float32)


def l2norm(x):
    xf = x.astype(jnp.float32)
    return xf * lax.rsqrt(jnp.sum(xf * xf, axis=-1, keepdims=True) + NORM_EPS)


def rope(x, pos):
    half = x.shape[-1] // 2
    inv = ROPE_THETA ** (-jnp.arange(half, dtype=jnp.float32) / half)
    ang = pos.astype(jnp.float32)[:, None] * inv[None, :]
    cos = jnp.cos(ang)[None, :, None, :]
    sin = jnp.sin(ang)[None, :, None, :]
    x1, x2 = x[..., :half], x[..., half:]
    return jnp.concatenate([x1 * cos - x2 * sin, x1 * sin + x2 * cos], axis=-1).astype(x.dtype)


def causal_dwconv(x, left, w):
    width, t = w.shape[0], x.shape[1]
    xc = jnp.concatenate([left.astype(x.dtype), x], axis=1)
    y = xc[:, 0:t] * w[0]
    for j in range(1, width):
        y = y + xc[:, j:j + t] * w[j]
    return y, xc[:, xc.shape[1] - (width - 1):]


def dilated_attn_prompt(q, k, v, window, dil):
    bx, s, nh, dh = q.shape
    blk = window // dil
    span = blk * dil
    s_pad = -(-s // span) * span
    sub = s_pad // dil
    nb = sub // blk

    def split(a):
        a = jnp.pad(a, ((0, 0), (0, s_pad - s), (0, 0), (0, 0)))
        a = a.reshape(bx, sub, dil, nh, dh).transpose(0, 2, 1, 3, 4)
        return a.reshape(bx, dil, nb, blk, nh, dh)

    def with_prev(a):
        prev = jnp.pad(a, ((0, 0), (0, 0), (1, 0), (0, 0), (0, 0), (0, 0)))[:, :, :nb]
        return jnp.concatenate([prev, a], axis=3)

    qb = split(q)
    kb = with_prev(split(k))
    vb = with_prev(split(v))
    sc = jnp.einsum('brnqhd,brnkhd->brnhqk', qb, kb, preferred_element_type=jnp.float32)
    qi = jnp.arange(blk)[:, None]
    kj = jnp.arange(2 * blk)[None, :]
    dist = blk + qi - kj
    band = (dist >= 0) & (dist <= blk)
    has_prev = (jnp.arange(nb) > 0)[:, None, None] | (kj >= blk)[None]
    valid = band[None] & has_prev
    sc = jnp.where(valid[None, None, :, None], sc, -jnp.inf)
    lse = jax.nn.logsumexp(sc, axis=-1)
    o = jnp.einsum('brnhqk,brnkhd->brnqhd', jnp.exp(sc - lse[..., None]), vb)
    o = o.reshape(bx, dil, sub, nh, dh).transpose(0, 2, 1, 3, 4).reshape(bx, s_pad, nh, dh)[:, :s]
    lse = jnp.moveaxis(lse, 3, 4).reshape(bx, dil, sub, nh).transpose(0, 2, 1, 3).reshape(bx, s_pad, nh)[:, :s]
    return o, lse


def dilated_attn_sample(q, k, v, k_buf, v_buf, window, dil):
    t, lb = q.shape[1], k_buf.shape[1]
    kc = jnp.concatenate([k_buf.astype(k.dtype), k], axis=1)
    vc = jnp.concatenate([v_buf.astype(v.dtype), v], axis=1)
    idx = lb + jnp.arange(t)[:, None] - dil * jnp.arange(window // dil + 1)[None, :]
    valid = idx >= 0
    idx = jnp.maximum(idx, 0)
    kg = jnp.take(kc, idx, axis=1)
    vg = jnp.take(vc, idx, axis=1)
    sc = jnp.einsum('bthd,btjhd->bhtj', q, kg, preferred_element_type=jnp.float32)
    sc = jnp.where(valid[None, None], sc, -jnp.inf)
    lse = jax.nn.logsumexp(sc, axis=-1)
    o = jnp.einsum('bhtj,btjhd->bthd', jnp.exp(sc - lse[..., None]), vg)
    keep = min(window, lb + t)
    return o, jnp.swapaxes(lse, 1, 2), kc[:, lb + t - keep:], vc[:, lb + t - keep:]


def mixer_a(h, pos, w_in, w_out, bufs):
    bx, t = h.shape[0], h.shape[1]
    proj = (h @ w_in).reshape(bx, t, len(A_GROUPS), 3, A_HEADS, A_HEAD_DIM)
    outs, lses, new_bufs = [], [], []
    for gi, (win, dil) in enumerate(A_GROUPS):
        q = rope(proj[:, :, gi, 0], pos) * (A_HEAD_DIM ** -0.5)
        k = rope(proj[:, :, gi, 1], pos)
        v = proj[:, :, gi, 2]
        if bufs is None:
            o, lse = dilated_attn_prompt(q, k, v, win, dil)
            keep = min(win, t)
            k_keep, v_keep = k[:, t - keep:], v[:, t - keep:]
        else:
            o, lse, k_keep, v_keep = dilated_attn_sample(q, k, v, bufs[2 * gi], bufs[2 * gi + 1], win, dil)
        outs.append(o)
        lses.append(lse)
        new_bufs += [k_keep, v_keep]
    wts = jax.nn.softmax(jnp.stack(lses), axis=0)
    o = jnp.sum(wts[..., None] * jnp.stack(outs), axis=0)
    return o.reshape(bx, t, A_WIDTH) @ w_out, new_bufs


def gated_delta_chunked(q, k, v, g, beta, s0):
    f32 = jnp.float32
    bx, t, nh, dk = q.shape
    dv = v.shape[-1]
    c = B_CHUNK
    n = -(-t // c)
    pad = n * c - t

    def chunks(a):
        a = jnp.pad(a.astype(f32), [(0, 0), (0, pad)] + [(0, 0)] * (a.ndim - 2))
        a = a.reshape((bx, n, c) + a.shape[2:])
        return jnp.moveaxis(a, 3, 2)

    qc, kc, vc, gc, bc = chunks(q), chunks(k), chunks(v), chunks(g), chunks(beta)
    cg = jnp.cumsum(gc, axis=-1)
    incl = jnp.tril(jnp.ones((c, c), bool))
    strict = jnp.tril(jnp.ones((c, c), bool), -1)
    decay = jnp.where(incl, jnp.exp(jnp.where(incl, cg[..., :, None] - cg[..., None, :], 0.0)), 0.0)
    a_mat = jnp.where(strict, bc[..., :, None] * jnp.einsum('bnhid,bnhjd->bnhij', kc, kc) * decay, 0.0)
    rhs = jnp.concatenate([bc[..., None] * vc, (bc * jnp.exp(cg))[..., None] * kc], axis=-1)
    sol = lax.linalg.triangular_solve(a_mat + jnp.eye(c, dtype=f32), rhs,
                                      left_side=True, lower=True, unit_diagonal=True)
    u0, wk = sol[..., :dv], sol[..., dv:]
    p_mat = jnp.einsum('bnhid,bnhjd->bnhij', qc, kc) * decay
    qg = qc * jnp.exp(cg)[..., None]
    g_last = cg[..., -1]
    kdec = kc * jnp.exp(g_last[..., None] - cg)[..., None]

    def step(s, xs):
        u0_i, wk_i, qg_i, p_i, kdec_i, gl_i = xs
        u = u0_i - wk_i @ s
        o = qg_i @ s + p_i @ u
        s = jnp.exp(gl_i)[..., None, None] * s + jnp.swapaxes(kdec_i, -1, -2) @ u
        return s, o

    xs = tuple(jnp.moveaxis(a, 1, 0) for a in (u0, wk, qg, p_mat, kdec, g_last))
    s_fin, o = lax.scan(step, s0.astype(f32), xs)
    o = jnp.moveaxis(o, 0, 1)
    o = jnp.moveaxis(o, 3, 2).reshape(bx, n * c, nh, dv)[:, :t]
    return o, s_fin


def mixer_b(h, w_in, conv_w, a_log, dt_bias, norm_g, w_out, ssm0, conv0):
    bx, t = h.shape[0], h.shape[1]
    proj = h @ w_in
    qkv, conv_new = causal_dwconv(proj[..., :B_CONV_DIM], conv0, conv_w)
    qkv = jax.nn.silu(qkv)
    z = proj[..., B_CONV_DIM:B_CONV_DIM + B_V_WIDTH].reshape(bx, t, B_V_HEADS, B_HEAD_V)
    b_raw = proj[..., B_CONV_DIM + B_V_WIDTH:B_CONV_DIM + B_V_WIDTH + B_V_HEADS]
    a_raw = proj[..., B_CONV_DIM + B_V_WIDTH + B_V_HEADS:]
    rep = B_V_HEADS // B_QK_HEADS
    q = jnp.repeat(l2norm(qkv[..., :B_QK_WIDTH].reshape(bx, t, B_QK_HEADS, B_HEAD_K)), rep, axis=2) * (B_HEAD_K ** -0.5)
    k = jnp.repeat(l2norm(qkv[..., B_QK_WIDTH:2 * B_QK_WIDTH].reshape(bx, t, B_QK_HEADS, B_HEAD_K)), rep, axis=2)
    v = qkv[..., 2 * B_QK_WIDTH:].reshape(bx, t, B_V_HEADS, B_HEAD_V)
    beta = jax.nn.sigmoid(b_raw.astype(jnp.float32))
    g = -jnp.exp(a_log.astype(jnp.float32)) * jax.nn.softplus(a_raw.astype(jnp.float32) + dt_bias.astype(jnp.float32))
    o, ssm_new = gated_delta_chunked(q, k, v, g, beta, ssm0)
    o = rmsnorm(o, norm_g) * jax.nn.silu(z.astype(jnp.float32))
    return o.reshape(bx, t, B_V_WIDTH) @ w_out, ssm_new, conv_new


def conv_ffn(h, w_up, conv_w, conv_b, w_down, buf0):
    up = h @ w_up
    gate, buf_new = causal_dwconv(up[..., :D_FF], buf0, conv_w)
    return (jax.nn.silu(gate + conv_b) * up[..., D_FF:]) @ w_down, buf_new


def trunk(x, c, pos, a_caches, b_ssm, b_conv, f_conv, weights):
    (norm_mix_g, norm_ffn_g, norm_final_g, w_mod, b_mod, a_w_in, a_w_out,
     b_w_in, b_conv_w, b_a_log, b_dt_bias, b_norm_g, b_w_out,
     ffn_w_up, ffn_conv_w, ffn_conv_b, ffn_w_down) = weights
    bx = x.shape[0]
    new_a = [[] for _ in range(2 * len(A_GROUPS))]
    new_ssm, new_bconv, new_fconv = [], [], []
    cs = jax.nn.silu(c)
    for i in range(DEPTH):
        mod = cs @ w_mod[i] + b_mod[i]
        sh1, sc1, g1, sh2, sc2, g2 = jnp.split(mod[:, None, :], 6, axis=-1)
        h = rmsnorm(x, norm_mix_g[i]) * (1.0 + sc1) + sh1
        j = i // N_MIXERS
        if i % N_MIXERS == 0:
            bufs = None if a_caches is None else [a[j] for a in a_caches]
            y, bufs_new = mixer_a(h, pos, a_w_in[j], a_w_out[j], bufs)
            for m, buf in enumerate(bufs_new):
                new_a[m].append(buf)
        else:
            s0 = jnp.zeros((bx, B_V_HEADS, B_HEAD_K, B_HEAD_V), jnp.float32) if b_ssm is None else b_ssm[j]
            cv0 = jnp.zeros((bx, B_CONV_WIDTH - 1, B_CONV_DIM), jnp.float32) if b_conv is None else b_conv[j]
            y, s1, cv1 = mixer_b(h, b_w_in[j], b_conv_w[j], b_a_log[j], b_dt_bias[j], b_norm_g[j], b_w_out[j], s0, cv0)
            new_ssm.append(s1)
            new_bconv.append(cv1)
        x = x + g1 * y
        h = rmsnorm(x, norm_ffn_g[i]) * (1.0 + sc2) + sh2
        f0 = jnp.zeros((bx, FFN_CONV_WIDTH - 1, D_FF), jnp.float32) if f_conv is None else f_conv[i]
        y, f1 = conv_ffn(h, ffn_w_up[i], ffn_conv_w[i], ffn_conv_b[i], ffn_w_down[i], f0)
        new_fconv.append(f1)
        x = x + g2 * y
    x = rmsnorm(x, norm_final_g)
    return x, [jnp.stack(a) for a in new_a], jnp.stack(new_ssm), jnp.stack(new_bconv), jnp.stack(new_fconv)


def _layers_of(m):
    return len(range(m, DEPTH, N_MIXERS))


def setup_inputs(seed: int = 0) -> dict:
    key = jax.random.key(seed)
    ks = jax.random.split(key, 32)
    f32 = jnp.float32
    D = D_MODEL

    def nrm(i, shape, scale=1.0):
        return scale * jax.random.normal(ks[i], shape, f32)

    n_a, n_b = _layers_of(0), _layers_of(1)
    inp = {}
    inp['x_prompt'] = nrm(0, (BATCH, SEQ, D))
    inp['x_sample'] = nrm(1, (DEC_BATCH, DEC_SEQ, D))
    inp['c_prompt'] = nrm(2, (BATCH, D))
    inp['c_sample'] = nrm(3, (DEC_BATCH, D))
    for gi, (w, _) in enumerate(A_GROUPS):
        lb = min(w, PAST_LEN)
        inp[f'cache_a_k_w{w}'] = nrm(4 + 2 * gi, (n_a, DEC_BATCH, lb, A_HEADS, A_HEAD_DIM))
        inp[f'cache_a_v_w{w}'] = nrm(5 + 2 * gi, (n_a, DEC_BATCH, lb, A_HEADS, A_HEAD_DIM))
    inp['state_b_ssm'] = nrm(10, (n_b, DEC_BATCH, B_V_HEADS, B_HEAD_K, B_HEAD_V), 0.1)
    inp['state_b_conv'] = nrm(11, (n_b, DEC_BATCH, B_CONV_WIDTH - 1, B_CONV_DIM))
    inp['state_ffn_conv'] = nrm(12, (DEPTH, DEC_BATCH, FFN_CONV_WIDTH - 1, D_FF))
    inp['norm_mix_g'] = 1.0 + nrm(13, (DEPTH, D), 0.05)
    inp['norm_ffn_g'] = 1.0 + nrm(14, (DEPTH, D), 0.05)
    inp['norm_final_g'] = 1.0 + nrm(15, (D,), 0.05)
    inp['w_mod'] = nrm(16, (DEPTH, D, 6 * D), 0.5 * D ** -0.5)
    inp['b_mod'] = nrm(17, (DEPTH, 6 * D), 0.02)
    inp['a_w_in'] = nrm(18, (n_a, D, A_IN_WIDTH), D ** -0.5)
    inp['a_w_out'] = nrm(19, (n_a, A_WIDTH, D), A_WIDTH ** -0.5)
    inp['b_w_in'] = nrm(20, (n_b, D, B_IN_WIDTH), D ** -0.5)
    inp['b_conv_w'] = nrm(21, (n_b, B_CONV_WIDTH, B_CONV_DIM), B_CONV_WIDTH ** -0.5)
    inp['b_a_log'] = jnp.log(jax.random.uniform(ks[22], (n_b, B_V_HEADS), f32, 1.0, 16.0))
    dt = jnp.exp(jax.random.uniform(ks[23], (n_b, B_V_HEADS), f32, np.log(1e-3), np.log(1e-1)))
    inp['b_dt_bias'] = dt + jnp.log(-jnp.expm1(-dt))
    inp['b_norm_g'] = 1.0 + nrm(24, (n_b, B_HEAD_V), 0.05)
    inp['b_w_out'] = nrm(25, (n_b, B_V_WIDTH, D), B_V_WIDTH ** -0.5)
    inp['ffn_w_up'] = nrm(26, (DEPTH, D, 2 * D_FF), D ** -0.5)
    inp['ffn_conv_w'] = nrm(27, (DEPTH, FFN_CONV_WIDTH, D_FF), FFN_CONV_WIDTH ** -0.5)
    inp['ffn_conv_b'] = nrm(28, (DEPTH, D_FF), 0.01)
    inp['ffn_w_down'] = nrm(29, (DEPTH, D_FF, D), D_FF ** -0.5)
    return inp


def reference(x_prompt, x_sample, c_prompt, c_sample,
              cache_a_k_w128, cache_a_v_w128, cache_a_k_w512, cache_a_v_w512,
              cache_a_k_w2048, cache_a_v_w2048, state_b_ssm, state_b_conv, state_ffn_conv,
              norm_mix_g, norm_ffn_g, norm_final_g, w_mod, b_mod, a_w_in, a_w_out,
              b_w_in, b_conv_w, b_a_log, b_dt_bias, b_norm_g, b_w_out,
              ffn_w_up, ffn_conv_w, ffn_conv_b, ffn_w_down):
    weights = (norm_mix_g, norm_ffn_g, norm_final_g, w_mod, b_mod, a_w_in, a_w_out,
               b_w_in, b_conv_w, b_a_log, b_dt_bias, b_norm_g, b_w_out,
               ffn_w_up, ffn_conv_w, ffn_conv_b, ffn_w_down)
    pos_p = jnp.arange(x_prompt.shape[1], dtype=jnp.int32)
    pos_s = PAST_LEN + jnp.arange(x_sample.shape[1], dtype=jnp.int32)
    y_p, a_p, ssm_p, bconv_p, fconv_p = trunk(x_prompt, c_prompt, pos_p, None, None, None, None, weights)
    a_caches = [cache_a_k_w128, cache_a_v_w128, cache_a_k_w512, cache_a_v_w512, cache_a_k_w2048, cache_a_v_w2048]
    y_s, a_s, ssm_s, bconv_s, fconv_s = trunk(x_sample, c_sample, pos_s, a_caches, state_b_ssm, state_b_conv,
                                              state_ffn_conv, weights)
    return (y_p.astype(x_prompt.dtype), y_s.astype(x_sample.dtype),
            a_p[0], a_s[0], a_p[1], a_s[1], a_p[2], a_s[2], a_p[3], a_s[3], a_p[4], a_s[4], a_p[5], a_s[5],
            ssm_p, ssm_s, bconv_p, bconv_s, fconv_p, fconv_s)
```

```python
import functools

import jax
import jax.numpy as jnp
from jax import lax
from jax.experimental import pallas as pl
from jax.experimental.pallas import tpu as pltpu

F32 = jnp.float32
BF16 = jnp.bfloat16

PAST_LEN = 16384
A_GROUPS = ((128, 1), (512, 4), (2048, 16))
A_HEADS = 16
A_HEAD_DIM = 64
ROPE_THETA = 10000.0
NORM_EPS = 1e-6
B_HEAD = 128
B_V_HEADS = 16
B_QK_HEADS = 8
B_CHUNK = 64
B_STACK = 4
B_INV_BASE = 16
NEG_BIG = -1e30

LANES = 128
SUBLANES = 8
VMEM_LIMIT_BYTES = 56 * 1024 * 1024


def _params(*sem):
    return pltpu.CompilerParams(dimension_semantics=sem, vmem_limit_bytes=VMEM_LIMIT_BYTES)


def _silu(x):
    return x * jax.nn.sigmoid(x)


def _dot(a, b):
    return jnp.dot(a, b, preferred_element_type=F32)


def _dot_nt(a, b):
    return lax.dot_general(a, b, (((1,), (1,)), ((), ())), preferred_element_type=F32)


def _shift_of(n):
    assert n > 0 and n & (n - 1) == 0, n
    return n.bit_length() - 1


def _div_pow2(v, n):
    return lax.shift_right_logical(v, _shift_of(n))


def _mod_pow2(v, n):
    assert n & (n - 1) == 0
    return v & (n - 1)


def _split3(x):
    hi = x.astype(BF16)
    r1 = x - hi.astype(F32)
    mid = r1.astype(BF16)
    lo = (r1 - mid.astype(F32)).astype(BF16)
    return hi, mid, lo


def _mod_kernel(c_ref, w_ref, b_ref, o_ref):
    cs = _silu(c_ref[...]).astype(BF16)
    o_ref[...] = _dot(cs, w_ref[...].astype(BF16)) + b_ref[...]


def _mod_call(c_all, w_mod, b_mod):
    depth, d, n = w_mod.shape
    rows = c_all.shape[0]
    tn = 1536
    return pl.pallas_call(
        _mod_kernel,
        grid=(depth, n // tn),
        in_specs=[
            pl.BlockSpec((rows, d), lambda l, j: (0, 0)),
            pl.BlockSpec((None, d, tn), lambda l, j: (l, 0, j)),
            pl.BlockSpec((None, 1, tn), lambda l, j: (l, 0, j)),
        ],
        out_specs=pl.BlockSpec((None, rows, tn), lambda l, j: (l, 0, j)),
        out_shape=jax.ShapeDtypeStruct((depth, rows, n), F32),
        compiler_params=_params("arbitrary", "arbitrary"),
        name="adaln_mod",
    )(c_all, w_mod, b_mod.reshape(depth, 1, n))


def _norm_mod(x, g, sc, sh):
    ms = jnp.mean(x * x, axis=-1, keepdims=True)
    return x * lax.rsqrt(ms + NORM_EPS) * g * (1.0 + sc) + sh


def _rope_tile(acc, cos, sin, scale):
    lane = lax.broadcasted_iota(jnp.int32, (acc.shape[0], LANES), 1)
    first_half = _mod_pow2(lane, A_HEAD_DIM) < (A_HEAD_DIM // 2)
    outs = []
    for c in range(acc.shape[1] // LANES):
        a = acc[:, c * LANES:(c + 1) * LANES]
        swapped = jnp.where(first_half,
                            pltpu.roll(a, LANES - A_HEAD_DIM // 2, 1),
                            pltpu.roll(a, A_HEAD_DIM // 2, 1))
        outs.append((a * cos + swapped * sin) * scale)
    return jnp.concatenate(outs, axis=1)


def _nm_plain_kernel(x_ref, g_ref, sc_ref, sh_ref, w_ref, o_ref, h_scr):
    @pl.when(pl.program_id(2) == 0)
    def _():
        h_scr[...] = _norm_mod(x_ref[...], g_ref[...], sc_ref[...], sh_ref[...]).astype(BF16)
    o_ref[...] = _dot(h_scr[...], w_ref[...])


def _nm_rope_kernel(x_ref, g_ref, sc_ref, sh_ref, w_ref, cos_ref, sin_ref, o_ref, tail_ref, h_scr,
                    *, tail_tile0):
    i, j = pl.program_id(1), pl.program_id(2)

    @pl.when(j == 0)
    def _():
        h_scr[...] = _norm_mod(x_ref[...], g_ref[...], sc_ref[...], sh_ref[...]).astype(BF16)

    acc = _dot(h_scr[...], w_ref[...])
    kind = lax.rem(j, 3)

    def emit(val):
        o_ref[...] = val.astype(o_ref.dtype)

        @pl.when(i >= tail_tile0)
        def _():
            tail_ref[...] = val

    @pl.when(kind == 2)
    def _():
        emit(acc)

    @pl.when(kind != 2)
    def _():
        scale = jnp.where(kind == 0, A_HEAD_DIM ** -0.5, 1.0).astype(F32)
        emit(_rope_tile(acc, cos_ref[...], sin_ref[...], scale))


def _nm_ffn_kernel(x_ref, g_ref, sc_ref, sh_ref, w_ref, gate_ref, val_ref, h_scr, *, gate_tiles):
    j = pl.program_id(2)

    @pl.when(j == 0)
    def _():
        h_scr[...] = _norm_mod(x_ref[...], g_ref[...], sc_ref[...], sh_ref[...]).astype(BF16)

    acc = _dot(h_scr[...], w_ref[...])

    @pl.when(j < gate_tiles)
    def _():
        gate_ref[...] = acc

    @pl.when(j >= gate_tiles)
    def _():
        val_ref[...] = acc.astype(val_ref.dtype)


def _row_tile(rows):
    return min(rows, 512)


def _nm_common_specs(x, sc, tm, tn, d):
    per_row = sc.shape[1] != 1
    mod_spec = (pl.BlockSpec((None, tm, d), lambda b, i, j: (b, i, 0)) if per_row
                else pl.BlockSpec((None, 1, d), lambda b, i, j: (b, 0, 0)))
    return [
        pl.BlockSpec((None, tm, d), lambda b, i, j: (b, i, 0)),
        pl.BlockSpec((1, d), lambda b, i, j: (0, 0)),
        mod_spec,
        mod_spec,
        pl.BlockSpec((d, tn), lambda b, i, j: (0, j)),
    ]


def _nm_plain_call(x, g, sc, sh, w, tn):
    bq, rows, d = x.shape
    n = w.shape[1]
    tm = _row_tile(rows)
    return pl.pallas_call(
        _nm_plain_kernel,
        grid=(bq, rows // tm, n // tn),
        in_specs=_nm_common_specs(x, sc, tm, tn, d),
        out_specs=pl.BlockSpec((None, tm, tn), lambda b, i, j: (b, i, j)),
        out_shape=jax.ShapeDtypeStruct((bq, rows, n), F32),
        scratch_shapes=[pltpu.VMEM((tm, d), BF16)],
        compiler_params=_params("arbitrary", "arbitrary", "arbitrary"),
        name="norm_mod_matmul",
    )(x, g, sc, sh, w)


def _nm_rope_call(x, g, sc, sh, w, cos, sin, tail_rows):
    bq, rows, d = x.shape
    n = w.shape[1]
    tn = A_HEADS * A_HEAD_DIM
    tm = _row_tile(min(rows, tail_rows))
    tail_tile0 = (rows - tail_rows) // tm
    kern = functools.partial(_nm_rope_kernel, tail_tile0=tail_tile0)
    tab_spec = pl.BlockSpec((None, tm, LANES), lambda b, i, j: (0, i, 0))

    def tail_map(b, i, j):
        in_tail = i >= tail_tile0
        return (b, jnp.maximum(i - tail_tile0, 0), jnp.where(in_tail, j, 0))

    return pl.pallas_call(
        kern,
        grid=(bq, rows // tm, n // tn),
        in_specs=_nm_common_specs(x, sc, tm, tn, d) + [tab_spec, tab_spec],
        out_specs=[pl.BlockSpec((None, tm, tn), lambda b, i, j: (b, i, j)),
                   pl.BlockSpec((None, tm, tn), tail_map)],
        out_shape=[jax.ShapeDtypeStruct((bq, rows, n), BF16),
                   jax.ShapeDtypeStruct((bq, tail_rows, n), F32)],
        scratch_shapes=[pltpu.VMEM((tm, d), BF16)],
        compiler_params=_params("arbitrary", "arbitrary", "arbitrary"),
        name="norm_mod_qkv_rope",
    )(x, g, sc, sh, w, cos, sin)


def _nm_ffn_call(x, g, sc, sh, w):
    bq, rows, d = x.shape
    n = w.shape[1]
    d_ff = n // 2
    gate_tiles = 2
    tn = d_ff // gate_tiles
    tm = _row_tile(rows)
    kern = functools.partial(_nm_ffn_kernel, gate_tiles=gate_tiles)
    return pl.pallas_call(
        kern,
        grid=(bq, rows // tm, n // tn),
        in_specs=_nm_common_specs(x, sc, tm, tn, d),
        out_specs=[
            pl.BlockSpec((None, tm, tn), lambda b, i, j: (b, i, jnp.minimum(j, gate_tiles - 1))),
            pl.BlockSpec((None, tm, tn), lambda b, i, j: (b, i, jnp.maximum(j - gate_tiles, 0))),
        ],
        out_shape=[jax.ShapeDtypeStruct((bq, rows, d_ff), F32),
                   jax.ShapeDtypeStruct((bq, rows, d_ff), BF16)],
        scratch_shapes=[pltpu.VMEM((tm, d), BF16)],
        compiler_params=_params("arbitrary", "arbitrary", "arbitrary"),
        name="norm_mod_ffn_up",
    )(x, g, sc, sh, w)


def _finish(y, x_ref, gate_ref, gfin_ref, o_ref, final_norm):
    xn = x_ref[...] + gate_ref[...] * y
    if final_norm:
        ms = jnp.mean(xn * xn, axis=-1, keepdims=True)
        xn = xn * lax.rsqrt(ms + NORM_EPS) * gfin_ref[...]
    o_ref[...] = xn


def _proj_res_kernel(a_ref, w_ref, x_ref, gate_ref, gfin_ref, o_ref, *, final_norm):
    _finish(_dot(a_ref[...], w_ref[...]), x_ref, gate_ref, gfin_ref, o_ref, final_norm)


def _attn_merge_proj_kernel(o0_ref, o1_ref, o2_ref, l0_ref, l1_ref, l2_ref, e_ref, w_ref,
                            x_ref, gate_ref, gfin_ref, o_ref, *, final_norm):
    l0, l1, l2 = l0_ref[...], l1_ref[...], l2_ref[...]
    m = jnp.maximum(jnp.maximum(l0, l1), l2)
    e0, e1, e2 = jnp.exp(l0 - m), jnp.exp(l1 - m), jnp.exp(l2 - m)
    inv = 1.0 / (e0 + e1 + e2)
    merged = None
    for e, og_ref in ((e0, o0_ref), (e1, o1_ref), (e2, o2_ref)):
        wgt = e * inv
        hi = wgt.astype(BF16)
        lo = (wgt - hi.astype(F32)).astype(BF16)
        wexp = _dot(jnp.concatenate([hi, lo], axis=1), e_ref[...])
        term = wexp * og_ref[...].astype(F32)
        merged = term if merged is None else merged + term
    _finish(_dot(merged.astype(BF16), w_ref[...]), x_ref, gate_ref, gfin_ref, o_ref, final_norm)


def _ffn_down_kernel(gate_ref, prev_ref, state_ref, val_ref, cw_ref, cb_ref, w_ref,
                     x_ref, mgate_ref, gfin_ref, o_ref, ext_scr, *, final_norm):
    tm = gate_ref.shape[0]
    first = pl.program_id(1) == 0
    ext_scr[0:SUBLANES, :] = jnp.where(first, state_ref[...], prev_ref[...])
    ext_scr[SUBLANES:SUBLANES + tm, :] = gate_ref[...]
    width = cw_ref.shape[0]
    y = cb_ref[...]
    for t in range(width):
        off = SUBLANES - (width - 1) + t
        y = y + cw_ref[t:t + 1, :] * ext_scr[off:off + tm, :]
    act = (_silu(y) * val_ref[...].astype(F32)).astype(BF16)
    _finish(_dot(act, w_ref[...]), x_ref, mgate_ref, gfin_ref, o_ref, final_norm)


def _res_specs(tm, d, gate):
    per_row = gate.shape[1] != 1
    gate_spec = (pl.BlockSpec((None, tm, d), lambda b, i: (b, i, 0)) if per_row
                 else pl.BlockSpec((None, 1, d), lambda b, i: (b, 0, 0)))
    return [pl.BlockSpec((None, tm, d), lambda b, i: (b, i, 0)),
            gate_spec,
            pl.BlockSpec((1, d), lambda b, i: (0, 0))]


def _proj_res_call(a, w, x, gate, gfin, final_norm):
    bq, rows, d = x.shape
    k = a.shape[2]
    tm = _row_tile(rows)
    kern = functools.partial(_proj_res_kernel, final_norm=final_norm)
    return pl.pallas_call(
        kern,
        grid=(bq, rows // tm),
        in_specs=[pl.BlockSpec((None, tm, k), lambda b, i: (b, i, 0)),
                  pl.BlockSpec((k, d), lambda b, i: (0, 0))] + _res_specs(tm, d, gate),
        out_specs=pl.BlockSpec((None, tm, d), lambda b, i: (b, i, 0)),
        out_shape=jax.ShapeDtypeStruct((bq, rows, d), F32),
        compiler_params=_params("arbitrary", "arbitrary"),
        name="proj_residual",
    )(a, w, x, gate, gfin)


def _attn_merge_proj_call(outs, lses, expand, w, x, gate, gfin, final_norm):
    bq, rows, d = x.shape
    tm = _row_tile(rows)
    kern = functools.partial(_attn_merge_proj_kernel, final_norm=final_norm)
    o_spec = pl.BlockSpec((None, tm, d), lambda b, i: (b, i, 0))
    l_spec = pl.BlockSpec((None, tm, LANES), lambda b, i: (b, i, 0))
    return pl.pallas_call(
        kern,
        grid=(bq, rows // tm),
        in_specs=[o_spec] * 3 + [l_spec] * 3 + [
            pl.BlockSpec(expand.shape, lambda b, i: (0, 0)),
            pl.BlockSpec(w.shape, lambda b, i: (0, 0))] + _res_specs(tm, d, gate),
        out_specs=pl.BlockSpec((None, tm, d), lambda b, i: (b, i, 0)),
        out_shape=jax.ShapeDtypeStruct((bq, rows, d), F32),
        compiler_params=_params("arbitrary", "arbitrary"),
        name="attn_merge_proj_residual",
    )(*outs, *lses, expand, w, x, gate, gfin)


def _ffn_down_call(gate_pre, state8, val, conv_w, conv_b, w, x, mgate, gfin, final_norm):
    bq, rows, d = x.shape
    f = gate_pre.shape[2]
    tm = min(rows, 256)
    kern = functools.partial(_ffn_down_kernel, final_norm=final_norm)
    blocks_per_tile = tm // SUBLANES
    return pl.pallas_call(
        kern,
        grid=(bq, rows // tm),
        in_specs=[
            pl.BlockSpec((None, tm, f), lambda b, i: (b, i, 0)),
            pl.BlockSpec((None, SUBLANES, f),
                         lambda b, i: (b, jnp.maximum(i * blocks_per_tile - 1, 0), 0)),
            pl.BlockSpec((None, SUBLANES, f), lambda b, i: (b, 0, 0)),
            pl.BlockSpec((None, tm, f), lambda b, i: (b, i, 0)),
            pl.BlockSpec(conv_w.shape, lambda b, i: (0, 0)),
            pl.BlockSpec((1, f), lambda b, i: (0, 0)),
            pl.BlockSpec((f, d), lambda b, i: (0, 0)),
        ] + _res_specs(tm, d, mgate),
        out_specs=pl.BlockSpec((None, tm, d), lambda b, i: (b, i, 0)),
        out_shape=jax.ShapeDtypeStruct((bq, rows, d), F32),
        scratch_shapes=[pltpu.VMEM((tm + SUBLANES, f), F32)],
        compiler_params=_params("arbitrary", "arbitrary"),
        name="ffn_conv_down_residual",
    )(gate_pre, gate_pre, state8, val, conv_w, conv_b, w, x, mgate, gfin)


def _attn_prompt_kernel(q_ref, kp_ref, kc_ref, vp_ref, vc_ref, o_ref, lse_ref):
    blk = q_ref.shape[0]
    n = pl.program_id(2)
    qi = lax.broadcasted_iota(jnp.int32, (blk, 2 * blk), 0)
    kj = lax.broadcasted_iota(jnp.int32, (blk, 2 * blk), 1)
    valid = (kj >= qi) & (kj <= qi + blk) & ((kj >= blk) | (n > 0))
    lane = lax.broadcasted_iota(jnp.int32, (blk, LANES), 1)
    low = lane < A_HEAD_DIM
    lse_all = jnp.zeros((blk, LANES), F32)
    for hp in range(A_HEADS // 2):
        cols = slice(hp * LANES, (hp + 1) * LANES)
        q2 = q_ref[:, cols]
        k2 = jnp.concatenate([kp_ref[:, cols], kc_ref[:, cols]], axis=0)
        v2 = jnp.concatenate([vp_ref[:, cols], vc_ref[:, cols]], axis=0)
        pair = []
        for s in range(2):
            mine = low if s == 0 else jnp.logical_not(low)
            qh = jnp.where(mine, q2, jnp.zeros_like(q2))
            sc = jnp.where(valid, _dot_nt(qh, k2), NEG_BIG)
            m = jnp.max(sc, axis=1, keepdims=True)
            p = jnp.exp(sc - m)
            l = jnp.sum(p, axis=1, keepdims=True)
            pair.append(_dot(p.astype(BF16), v2) / l)
            lse_all = jnp.where(lane == 2 * hp + s, m + jnp.log(l), lse_all)
        o_ref[:, cols] = jnp.where(low, pair[0], pair[1]).astype(o_ref.dtype)
    lse_ref[...] = lse_all


def _attn_prompt_call(qkv, gi, win, dil):
    b, s, n = qkv.shape
    width = A_HEADS * A_HEAD_DIM
    blk = win // dil
    assert s % win == 0 and n == 3 * len(A_GROUPS) * width
    sub = s // dil
    nb = sub // blk
    cols_per_row = n // width
    view = qkv.reshape(b, sub, dil * n)

    def col(kind):
        return lambda bb, r, i: (bb, i, r * cols_per_row + 3 * gi + kind)

    def col_prev(kind):
        return lambda bb, r, i: (bb, jnp.maximum(i - 1, 0), r * cols_per_row + 3 * gi + kind)

    blkspec = lambda f: pl.BlockSpec((None, blk, width), f)
    o, lse = pl.pallas_call(
        _attn_prompt_kernel,
        grid=(b, dil, nb),
        in_specs=[blkspec(col(0)), blkspec(col_prev(1)), blkspec(col(1)),
                  blkspec(col_prev(2)), blkspec(col(2))],
        out_specs=[pl.BlockSpec((None, blk, width), lambda bb, r, i: (bb, i, r)),
                   pl.BlockSpec((None, blk, LANES), lambda bb, r, i: (bb, i, r))],
        out_shape=[jax.ShapeDtypeStruct((b, sub, dil * width), BF16),
                   jax.ShapeDtypeStruct((b, sub, dil * LANES), F32)],
        compiler_params=_params("arbitrary", "arbitrary", "arbitrary"),
        name="dilated_attn_prompt",
    )(view, view, view, view, view)
    return o.reshape(b, s, width), lse.reshape(b, s, LANES)


def _attn_sample_kernel(q_ref, kn_ref, vn_ref, kc_ref, kx_ref, vc_ref, vx_ref,
                        ko_ref, vo_ref, o_ref, lse_ref, m_scr, l_scr, acc_scr, *, lb, dil):
    t_new = q_ref.shape[0]
    chunk = kc_ref.shape[0]
    width = q_ref.shape[1]
    rows = A_HEADS * t_new
    c = pl.program_id(1)
    last = c == pl.num_programs(1) - 1

    ko_ref[0:chunk - t_new, :] = kc_ref[t_new:chunk, :]
    vo_ref[0:chunk - t_new, :] = vc_ref[t_new:chunk, :]
    ko_ref[chunk - t_new:chunk, :] = jnp.where(last, kn_ref[...], kx_ref[...])
    vo_ref[chunk - t_new:chunk, :] = jnp.where(last, vn_ref[...], vx_ref[...])

    @pl.when(c == 0)
    def _():
        m_scr[...] = jnp.full_like(m_scr, NEG_BIG)
        l_scr[...] = jnp.zeros_like(l_scr)
        acc_scr[...] = jnp.zeros_like(acc_scr)

    rq = lax.broadcasted_iota(jnp.int32, (rows, width), 0)
    lq = lax.broadcasted_iota(jnp.int32, (rows, width), 1)
    own = _div_pow2(lq, A_HEAD_DIM) == _div_pow2(rq, t_new)
    qbd = jnp.where(own, jnp.concatenate([q_ref[...]] * A_HEADS, axis=0), 0.0).astype(BF16)

    def accumulate(keys, vals, valid):
        sc = jnp.where(valid, _dot_nt(qbd, keys.astype(BF16)), NEG_BIG)
        m_old = m_scr[...]
        m_new = jnp.maximum(m_old, jnp.max(sc, axis=1, keepdims=True))
        alpha = jnp.exp(m_old - m_new)
        p = jnp.where(valid, jnp.exp(sc - m_new), 0.0)
        l_scr[...] = alpha * l_scr[...] + jnp.sum(p, axis=1, keepdims=True)
        acc_scr[...] = alpha * acc_scr[...] + _dot(p.astype(BF16), vals.astype(BF16))
        m_scr[...] = m_new

    tq = _mod_pow2(lax.broadcasted_iota(jnp.int32, (rows, chunk), 0), t_new)
    ik = lax.broadcasted_iota(jnp.int32, (rows, chunk), 1) + c * chunk
    dist = lb + tq - ik
    accumulate(kc_ref[...], vc_ref[...], (_mod_pow2(dist, dil) == 0) & (dist <= lb))

    @pl.when(last)
    def _():
        pad = jnp.zeros((LANES - t_new, width), F32)
        tq2 = _mod_pow2(lax.broadcasted_iota(jnp.int32, (rows, LANES), 0), t_new)
        tk2 = lax.broadcasted_iota(jnp.int32, (rows, LANES), 1)
        d2 = tq2 - tk2
        accumulate(jnp.concatenate([kn_ref[...], pad], axis=0),
                   jnp.concatenate([vn_ref[...], pad], axis=0),
                   (tk2 < t_new) & (d2 >= 0) & (_mod_pow2(d2, dil) == 0))
        full = jnp.where(own, acc_scr[...] / l_scr[...], 0.0)
        o_ref[...] = jnp.sum(full.reshape(A_HEADS, t_new, width), axis=0).astype(o_ref.dtype)
        lse_col = m_scr[...] + jnp.log(l_scr[...])
        rl = lax.broadcasted_iota(jnp.int32, (rows, LANES), 0)
        ll = lax.broadcasted_iota(jnp.int32, (rows, LANES), 1)
        spread = jnp.where(ll == _div_pow2(rl, t_new), lse_col, 0.0)
        lse_ref[...] = jnp.sum(spread.reshape(A_HEADS, t_new, LANES), axis=0)


def _attn_sample_call(proj, gi, win, dil, k_cache, v_cache):
    db, t_new, n = proj.shape
    width = A_HEADS * A_HEAD_DIM
    lb = k_cache.shape[1]
    assert lb == win and t_new == SUBLANES
    chunk = min(lb, 512)
    nchunk = lb // chunk
    per8 = chunk // SUBLANES
    kern = functools.partial(_attn_sample_kernel, lb=lb, dil=dil)
    new_spec = lambda kind: pl.BlockSpec((None, t_new, width), lambda b, c: (b, 0, 3 * gi + kind))
    cur = pl.BlockSpec((None, chunk, width), lambda b, c: (b, c, 0))
    nxt = pl.BlockSpec((None, SUBLANES, width),
                       lambda b, c: (b, jnp.minimum((c + 1) * per8, lb // SUBLANES - 1), 0))
    return pl.pallas_call(
        kern,
        grid=(db, nchunk),
        in_specs=[new_spec(0), new_spec(1), new_spec(2), cur, nxt, cur, nxt],
        out_specs=[cur, cur,
                   pl.BlockSpec((None, t_new, width), lambda b, c: (b, 0, 0)),
                   pl.BlockSpec((None, t_new, LANES), lambda b, c: (b, 0, 0))],
        out_shape=[jax.ShapeDtypeStruct((db, lb, width), F32),
                   jax.ShapeDtypeStruct((db, lb, width), F32),
                   jax.ShapeDtypeStruct((db, t_new, width), BF16),
                   jax.ShapeDtypeStruct((db, t_new, LANES), F32)],
        scratch_shapes=[pltpu.VMEM((A_HEADS * t_new, 1), F32),
                        pltpu.VMEM((A_HEADS * t_new, 1), F32),
                        pltpu.VMEM((A_HEADS * t_new, width), F32)],
        compiler_params=_params("arbitrary", "arbitrary"),
        name="dilated_attn_sample",
    )(proj, proj, proj, k_cache, k_cache, v_cache, v_cache)


def _delta_kernel(p_ref, prev_ref, cstate_ref, cw_ref, alog_ref, dtb_ref, ng_ref, s0_ref,
                  o_ref, ssm_ref, cnew_ref, s_scr, ext_scr, *, t_valid):
    cl = B_CHUNK
    rows_in = p_ref.shape[0]
    conv_dim = cw_ref.shape[1]
    c = pl.program_id(1)
    last = c == pl.num_programs(1) - 1
    qk_width = B_QK_HEADS * B_HEAD

    @pl.when(c == 0)
    def _():
        s_scr[...] = s0_ref[...]

    ext_scr[0:SUBLANES, :] = jnp.where(c == 0, cstate_ref[...], prev_ref[:, 0:conv_dim])
    ext_scr[SUBLANES:SUBLANES + rows_in, :] = p_ref[:, 0:conv_dim]
    if rows_in < cl:
        ext_scr[SUBLANES + rows_in:SUBLANES + cl, :] = jnp.zeros((cl - rows_in, conv_dim), F32)
    width = cw_ref.shape[0]
    y = None
    for t in range(width):
        off = SUBLANES - (width - 1) + t
        term = cw_ref[t:t + 1, :] * ext_scr[off:off + cl, :]
        y = term if y is None else y + term
    qkv = _silu(y)
    if t_valid < cl:
        rmask = lax.broadcasted_iota(jnp.int32, (cl, 1), 0) < t_valid
        qkv = jnp.where(rmask, qkv, 0.0)

    @pl.when(last)
    def _():
        cnew_ref[...] = ext_scr[t_valid:t_valid + SUBLANES, :]

    graw = p_ref[:, conv_dim + B_V_HEADS * B_HEAD:conv_dim + B_V_HEADS * B_HEAD + LANES]
    if rows_in < cl:
        graw = jnp.concatenate([graw, jnp.zeros((cl - rows_in, LANES), F32)], axis=0)
    beta_all = jax.nn.sigmoid(graw)
    xg = graw + dtb_ref[...]
    softplus = jnp.maximum(xg, 0.0) + jnp.log1p(jnp.exp(-jnp.abs(xg)))
    g_all = -jnp.exp(alog_ref[...]) * softplus
    if t_valid < cl:
        beta_all = jnp.where(rmask, beta_all, 0.0)
        g_all = jnp.where(rmask, g_all, 0.0)

    ri = lax.broadcasted_iota(jnp.int32, (2 * cl, cl), 0)
    ci = lax.broadcasted_iota(jnp.int32, (2 * cl, cl), 1)
    summat = jnp.where((ci <= ri) | (ri >= cl), 1.0, 0.0).astype(BF16)
    sums = _dot(summat, jnp.concatenate(_split3(g_all), axis=1))
    sums = sums[:, 0:LANES] + sums[:, LANES:2 * LANES] + sums[:, 2 * LANES:3 * LANES]
    cg_all, gl_all = sums[0:cl], sums[cl:2 * cl]

    def l2n(a):
        return a * lax.rsqrt(jnp.sum(a * a, axis=1, keepdims=True) + NORM_EPS)

    nst = B_STACK * cl
    rr = lax.broadcasted_iota(jnp.int32, (nst, nst), 0)
    cc = lax.broadcasted_iota(jnp.int32, (nst, nst), 1)
    same = _div_pow2(rr, cl) == _div_pow2(cc, cl)
    incl = same & (cc <= rr)
    strict = same & (cc < rr)
    lane6 = lax.broadcasted_iota(jnp.int32, (nst, LANES), 1)
    rblk = _div_pow2(lax.broadcasted_iota(jnp.int32, (nst, B_HEAD), 0), cl)

    def col_stack(arr, lane0, heads):
        return jnp.concatenate([arr[:, lane0 + h:lane0 + h + 1] for h in heads], axis=0)

    for grp in range(B_V_HEADS // B_STACK):
        heads = list(range(grp * B_STACK, (grp + 1) * B_STACK))
        rep = B_V_HEADS // B_QK_HEADS
        qs, ks = {}, {}
        for hq in sorted({h // rep for h in heads}):
            qs[hq] = l2n(qkv[:, hq * B_HEAD:(hq + 1) * B_HEAD]) * (B_HEAD ** -0.5)
            ks[hq] = l2n(qkv[:, qk_width + hq * B_HEAD:qk_width + (hq + 1) * B_HEAD])
        q_st = jnp.concatenate([qs[h // rep] for h in heads], axis=0)
        k_st = jnp.concatenate([ks[h // rep] for h in heads], axis=0)
        v_st = jnp.concatenate(
            [qkv[:, 2 * qk_width + h * B_HEAD:2 * qk_width + (h + 1) * B_HEAD] for h in heads], axis=0)
        beta = col_stack(beta_all, 0, heads)
        cg = col_stack(cg_all, B_V_HEADS, heads)
        gl = col_stack(gl_all, B_V_HEADS, heads)

        c_hi, c_mid, c_lo = (p.astype(F32) for p in _split3(cg))
        lmat = jnp.where(lane6 == 0, c_hi, jnp.where(lane6 == 1, c_mid, jnp.where(
            lane6 == 2, c_lo, jnp.where(lane6 < 6, 1.0, 0.0))))
        rmat = jnp.where(lane6 == 3, -c_hi, jnp.where(lane6 == 4, -c_mid, jnp.where(
            lane6 == 5, -c_lo, jnp.where(lane6 < 3, 1.0, 0.0))))
        diff = _dot_nt(lmat.astype(BF16), rmat.astype(BF16))
        decay = jnp.where(incl, jnp.exp(jnp.where(incl, diff, 0.0)), 0.0)

        k_bf = k_st.astype(BF16)
        a_mat = jnp.where(strict, beta * _dot_nt(k_bf, k_bf) * decay, 0.0)
        p_mat = _dot_nt(q_st.astype(BF16), k_bf) * decay

        blk = _div_pow2(rr, B_INV_BASE) == _div_pow2(cc, B_INV_BASE)
        a_diag = jnp.where(blk, a_mat, 0.0)
        nmat = -a_diag
        apow = a_diag
        for _ in range((B_INV_BASE - 1).bit_length() - 1):
            ab = apow.astype(BF16)
            apow = _dot(ab, ab)
            nmat = nmat + apow + _dot(nmat.astype(BF16), apow.astype(BF16))
        size = B_INV_BASE
        while size < cl:
            size *= 2
            merged = _div_pow2(rr, size) == _div_pow2(cc, size)
            low = jnp.where(merged & jnp.logical_not(blk), a_mat, 0.0)
            nb = nmat.astype(BF16)
            dl = low + _dot(nb, low.astype(BF16))
            nmat = nmat - (dl + _dot(dl.astype(BF16), nb))
            blk = merged
        ecg = jnp.exp(cg)
        rhs = jnp.concatenate([beta * v_st, (beta * ecg) * k_st], axis=1)
        sol = rhs + _dot(nmat.astype(BF16), rhs.astype(BF16))
        u0, wk = sol[:, 0:B_HEAD], sol[:, B_HEAD:2 * B_HEAD]
        qg = q_st * ecg
        kdec_t = (k_st * jnp.exp(gl - cg)).T.astype(BF16)

        u_parts, qgs_parts = [], []
        for idx, h in enumerate(heads):
            rsl = slice(idx * cl, (idx + 1) * cl)
            s_h = s_scr[h].astype(BF16)
            both = _dot(jnp.concatenate([wk[rsl], qg[rsl]], axis=0).astype(BF16), s_h)
            u_parts.append(u0[rsl] - both[0:cl])
            qgs_parts.append(both[cl:2 * cl])
        u_st = jnp.concatenate(u_parts, axis=0)
        o_st = jnp.concatenate(qgs_parts, axis=0) + _dot(p_mat.astype(BF16), u_st.astype(BF16))
        for idx, h in enumerate(heads):
            rsl = slice(idx * cl, (idx + 1) * cl)
            u_own = jnp.where(rblk == idx, u_st, 0.0).astype(BF16)
            s_scr[h] = jnp.exp(gl[idx * cl:idx * cl + 1]) * s_scr[h] + _dot(kdec_t, u_own)
            o_h = o_st[rsl]
            ms = jnp.mean(o_h * o_h, axis=1, keepdims=True)
            z_h = p_ref[:, conv_dim + h * B_HEAD:conv_dim + (h + 1) * B_HEAD]
            res = o_h[0:rows_in] * lax.rsqrt(ms[0:rows_in] + NORM_EPS) * ng_ref[...] * _silu(z_h)
            o_ref[:, h * B_HEAD:(h + 1) * B_HEAD] = res.astype(o_ref.dtype)

    @pl.when(last)
    def _():
        ssm_ref[...] = s_scr[...]


def _delta_call(proj, cstate8, conv_w, alog_l, dtb_l, norm_g, s0):
    bq, t, n = proj.shape
    conv_dim = conv_w.shape[1]
    v_width = B_V_HEADS * B_HEAD
    rows_in = min(t, B_CHUNK)
    assert t % rows_in == 0
    nchunk = t // rows_in
    per8 = rows_in // SUBLANES
    kern = functools.partial(_delta_kernel, t_valid=rows_in)
    return pl.pallas_call(
        kern,
        grid=(bq, nchunk),
        in_specs=[
            pl.BlockSpec((None, rows_in, n), lambda b, c: (b, c, 0)),
            pl.BlockSpec((None, SUBLANES, n), lambda b, c: (b, jnp.maximum(c * per8 - 1, 0), 0)),
            pl.BlockSpec((None, SUBLANES, conv_dim), lambda b, c: (b, 0, 0)),
            pl.BlockSpec(conv_w.shape, lambda b, c: (0, 0)),
            pl.BlockSpec((1, LANES), lambda b, c: (0, 0)),
            pl.BlockSpec((1, LANES), lambda b, c: (0, 0)),
            pl.BlockSpec((1, B_HEAD), lambda b, c: (0, 0)),
            pl.BlockSpec((None, B_V_HEADS, B_HEAD, B_HEAD), lambda b, c: (b, 0, 0, 0)),
        ],
        out_specs=[
            pl.BlockSpec((None, rows_in, v_width), lambda b, c: (b, c, 0)),
            pl.BlockSpec((None, B_V_HEADS, B_HEAD, B_HEAD), lambda b, c: (b, 0, 0, 0)),
            pl.BlockSpec((None, SUBLANES, conv_dim), lambda b, c: (b, 0, 0)),
        ],
        out_shape=[jax.ShapeDtypeStruct((bq, t, v_width), BF16),
                   jax.ShapeDtypeStruct((bq, B_V_HEADS, B_HEAD, B_HEAD), F32),
                   jax.ShapeDtypeStruct((bq, SUBLANES, conv_dim), F32)],
        scratch_shapes=[pltpu.VMEM((B_V_HEADS, B_HEAD, B_HEAD), F32),
                        pltpu.VMEM((B_CHUNK + SUBLANES, conv_dim), F32)],
        compiler_params=_params("arbitrary", "arbitrary"),
        name="gated_deltanet",
    )(proj, proj, cstate8, conv_w, alog_l, dtb_l, norm_g, s0)


def _left_rows(state):
    return jnp.pad(state, ((0, 0), (SUBLANES - state.shape[1], 0), (0, 0)))


def _rope_tables(pos):
    half = A_HEAD_DIM // 2
    inv = ROPE_THETA ** (-jnp.arange(half, dtype=F32) / half)
    ang = pos.astype(F32)[:, None] * inv[None, :]
    cos, sin = jnp.cos(ang), jnp.sin(ang)
    reps = LANES // A_HEAD_DIM
    cos_t = jnp.tile(jnp.concatenate([cos, cos], axis=1), (1, reps))
    sin_t = jnp.tile(jnp.concatenate([-sin, sin], axis=1), (1, reps))
    return cos_t[None], sin_t[None]


def _trunk(x, mods, per_row, pos, a_caches, b_ssm, b_conv, f_conv, wts):
    seqs, t, d = x.shape
    (norm_mix_g, norm_ffn_g, norm_final_g, a_w_in, a_w_out, b_w_in, b_conv_w, b_a_log, b_dt_bias,
     b_norm_g, b_w_out, ffn_w_up, ffn_conv_w, ffn_conv_b, ffn_w_down, expand) = wts
    d_ff = ffn_w_down.shape[1]
    width = A_HEADS * A_HEAD_DIM

    def rows_view(a):
        return a.reshape(1, seqs * t, a.shape[2]) if per_row else a

    def seq_view(a):
        return a.reshape(seqs, t, a.shape[2])

    def mod_view(v):
        if per_row:
            return jnp.repeat(v, t, axis=0)[None]
        return v[:, None, :]

    cos_t, sin_t = _rope_tables(pos)
    if per_row:
        cos_t, sin_t = jnp.tile(cos_t, (1, seqs, 1)), jnp.tile(sin_t, (1, seqs, 1))
    gfin = norm_final_g.reshape(1, d)
    xr = rows_view(x)
    new_a, new_ssm, new_bconv, new_fconv = [], None, None, []
    depth = norm_mix_g.shape[0]
    for layer in range(depth):
        sh1, sc1, g1, sh2, sc2, g2 = (mod_view(v) for v in mods[layer])
        gmix = norm_mix_g[layer].reshape(1, d)
        if layer % 2 == 0:
            tail_rows = t if a_caches is not None else min(t, max(w for w, _ in A_GROUPS))
            if per_row:
                tail_rows = seqs * t
            qkv, tail = _nm_rope_call(xr, gmix, sc1, sh1, a_w_in[0], cos_t, sin_t, tail_rows)
            outs, lses = [], []
            for gi, (win, dil) in enumerate(A_GROUPS):
                kcol = slice((3 * gi + 1) * width, (3 * gi + 2) * width)
                vcol = slice((3 * gi + 2) * width, (3 * gi + 3) * width)
                if a_caches is None:
                    o_g, lse_g = _attn_prompt_call(qkv, gi, win, dil)
                    keep = min(win, t)
                    k_keep = tail[:, tail_rows - keep:, kcol]
                    v_keep = tail[:, tail_rows - keep:, vcol]
                else:
                    lb = a_caches[2 * gi].shape[2]
                    k_keep, v_keep, o_g, lse_g = _attn_sample_call(
                        seq_view(tail), gi, win, dil,
                        a_caches[2 * gi][0].reshape(seqs, lb, width),
                        a_caches[2 * gi + 1][0].reshape(seqs, lb, width))
                    o_g, lse_g = rows_view(o_g), rows_view(lse_g)
                outs.append(o_g)
                lses.append(lse_g)
                for buf in (k_keep, v_keep):
                    new_a.append(buf.reshape(1, seqs, buf.shape[1], A_HEADS, A_HEAD_DIM))
            xr = _attn_merge_proj_call(outs, lses, expand, a_w_out[0], xr, g1, gfin, False)
        else:
            proj = _nm_plain_call(xr, gmix, sc1, sh1, b_w_in[0], 1280)
            conv0 = (jnp.zeros((seqs, b_conv_w.shape[1] - 1, b_conv_w.shape[2]), F32)
                     if b_conv is None else b_conv[0])
            s0 = jnp.zeros((seqs, B_V_HEADS, B_HEAD, B_HEAD), F32) if b_ssm is None else b_ssm[0]
            lane_pad = lambda v: jnp.pad(v.reshape(1, -1), ((0, 0), (B_V_HEADS, LANES - 2 * B_V_HEADS)))
            o_b, ssm1, cnew = _delta_call(seq_view(proj), _left_rows(conv0), b_conv_w[0],
                                          lane_pad(b_a_log[0]), lane_pad(b_dt_bias[0]),
                                          b_norm_g[0].reshape(1, B_HEAD), s0)
            new_ssm = ssm1[None]
            new_bconv = cnew[None, :, SUBLANES - (b_conv_w.shape[1] - 1):, :]
            xr = _proj_res_call(rows_view(o_b), b_w_out[0], xr, g1, gfin, False)
        gate_pre, val = _nm_ffn_call(xr, norm_ffn_g[layer].reshape(1, d), sc2, sh2, ffn_w_up[layer])
        f0 = (jnp.zeros((seqs, ffn_conv_w.shape[1] - 1, d_ff), F32) if f_conv is None else f_conv[layer])
        gate_seq = seq_view(gate_pre)
        new_fconv.append(gate_seq[:, t - (ffn_conv_w.shape[1] - 1):, :])
        mg2 = mods[layer][5][:, None, :]
        x_seq = _ffn_down_call(gate_seq, _left_rows(f0), seq_view(val), ffn_conv_w[layer],
                               ffn_conv_b[layer].reshape(1, d_ff), ffn_w_down[layer],
                               seq_view(xr), mg2, gfin, layer == depth - 1)
        xr = rows_view(x_seq)
    return seq_view(xr), new_a, new_ssm, new_bconv, jnp.stack(new_fconv)


def kernel(x_prompt, x_sample, c_prompt, c_sample, cache_a_k_w128, cache_a_v_w128, cache_a_k_w512, cache_a_v_w512, cache_a_k_w2048, cache_a_v_w2048, state_b_ssm, state_b_conv, state_ffn_conv, norm_mix_g, norm_ffn_g, norm_final_g, w_mod, b_mod, a_w_in, a_w_out, b_w_in, b_conv_w, b_a_log, b_dt_bias, b_norm_g, b_w_out, ffn_w_up, ffn_conv_w, ffn_conv_b, ffn_w_down):
    bp, s, d = x_prompt.shape
    db, t_new, _ = x_sample.shape
    depth = w_mod.shape[0]

    c_all = jnp.concatenate([c_prompt, c_sample], axis=0)
    c_all = jnp.pad(c_all, ((0, (-c_all.shape[0]) % SUBLANES), (0, 0)))
    mod = _mod_call(c_all, w_mod, b_mod)

    def mods_of(lo, hi):
        return [[mod[l, lo:hi, k * d:(k + 1) * d] for k in range(6)] for l in range(depth)]

    b_in_cols = 6400
    b_w_in_p = jnp.pad(b_w_in, ((0, 0), (0, 0), (0, b_in_cols - b_w_in.shape[2])))
    heads = jnp.arange(A_HEADS * A_HEAD_DIM) // A_HEAD_DIM
    half = (jnp.arange(LANES)[:, None] == heads[None, :]).astype(BF16)
    expand = jnp.concatenate([half, half], axis=0)
    wts = (norm_mix_g, norm_ffn_g, norm_final_g, a_w_in.astype(BF16), a_w_out.astype(BF16),
           b_w_in_p.astype(BF16), b_conv_w, b_a_log, b_dt_bias, b_norm_g, b_w_out.astype(BF16),
           ffn_w_up.astype(BF16), ffn_conv_w, ffn_conv_b, ffn_w_down.astype(BF16), expand)

    pos_p = jnp.arange(s, dtype=jnp.int32)
    pos_s = PAST_LEN + jnp.arange(t_new, dtype=jnp.int32)
    y_p, a_p, ssm_p, bconv_p, fconv_p = _trunk(
        x_prompt, mods_of(0, bp), False, pos_p, None, None, None, None, wts)
    a_caches = [cache_a_k_w128, cache_a_v_w128, cache_a_k_w512, cache_a_v_w512,
                cache_a_k_w2048, cache_a_v_w2048]
    y_s, a_s, ssm_s, bconv_s, fconv_s = _trunk(
        x_sample, mods_of(bp, bp + db), True, pos_s, a_caches, state_b_ssm, state_b_conv,
        state_ffn_conv, wts)
    return (y_p, y_s,
            a_p[0], a_s[0], a_p[1], a_s[1], a_p[2], a_s[2], a_p[3], a_s[3], a_p[4], a_s[4],
            a_p[5], a_s[5], ssm_p, ssm_s, bconv_p, bconv_s, fconv_p, fconv_s)
```

```python
import functools

import jax
import jax.numpy as jnp
from jax import lax
from jax.experimental import pallas as pl
from jax.experimental.pallas import tpu as pltpu

F32 = jnp.float32
BF16 = jnp.bfloat16

PAST_LEN = 16384
A_GROUPS = ((128, 1), (512, 4), (2048, 16))
A_HEADS = 16
A_HEAD_DIM = 64
A_ROW_TILE = 512
ROPE_THETA = 10000.0
NORM_EPS = 1e-6
B_HEAD = 128
B_V_HEADS = 16
B_QK_HEADS = 8
B_CHUNK = 64
B_STACK = 4
B_INV_BASE = 16
NEG_BIG = -1e30

LANES = 128
SUBLANES = 8
VMEM_LIMIT_BYTES = 56 * 1024 * 1024


def _params(*sem):
    return pltpu.CompilerParams(dimension_semantics=sem, vmem_limit_bytes=VMEM_LIMIT_BYTES)


def _silu(x):
    return x * jax.nn.sigmoid(x)


def _dot(a, b):
    return jnp.dot(a, b, preferred_element_type=F32)


def _dot_nt(a, b):
    return lax.dot_general(a, b, (((1,), (1,)), ((), ())), preferred_element_type=F32)


def _shift_of(n):
    assert n > 0 and n & (n - 1) == 0, n
    return n.bit_length() - 1


def _div_pow2(v, n):
    return lax.shift_right_logical(v, _shift_of(n))


def _mod_pow2(v, n):
    assert n & (n - 1) == 0
    return v & (n - 1)


def _split3(x):
    hi = x.astype(BF16)
    r1 = x - hi.astype(F32)
    mid = r1.astype(BF16)
    lo = (r1 - mid.astype(F32)).astype(BF16)
    return hi, mid, lo


def _mod_kernel(c_ref, w_ref, b_ref, o_ref):
    cs = _silu(c_ref[...]).astype(BF16)
    o_ref[...] = _dot(cs, w_ref[...].astype(BF16)) + b_ref[...]


def _mod_call(c_all, w_mod, b_mod):
    depth, d, n = w_mod.shape
    rows = c_all.shape[0]
    tn = 1536
    return pl.pallas_call(
        _mod_kernel,
        grid=(depth, n // tn),
        in_specs=[
            pl.BlockSpec((rows, d), lambda l, j: (0, 0)),
            pl.BlockSpec((None, d, tn), lambda l, j: (l, 0, j)),
            pl.BlockSpec((None, 1, tn), lambda l, j: (l, 0, j)),
        ],
        out_specs=pl.BlockSpec((None, rows, tn), lambda l, j: (l, 0, j)),
        out_shape=jax.ShapeDtypeStruct((depth, rows, n), F32),
        compiler_params=_params("arbitrary", "arbitrary"),
        name="adaln_mod",
    )(c_all, w_mod, b_mod.reshape(depth, 1, n))


def _norm_mod(x, g, sc, sh):
    ms = jnp.mean(x * x, axis=-1, keepdims=True)
    return x * lax.rsqrt(ms + NORM_EPS) * g * (1.0 + sc) + sh


def _rope_tile(acc, cos, sin):
    lane = lax.broadcasted_iota(jnp.int32, (acc.shape[0], LANES), 1)
    first_half = _mod_pow2(lane, A_HEAD_DIM) < (A_HEAD_DIM // 2)
    outs = []
    for c in range(acc.shape[1] // LANES):
        a = acc[:, c * LANES:(c + 1) * LANES]
        swapped = jnp.where(first_half,
                            pltpu.roll(a, LANES - A_HEAD_DIM // 2, 1),
                            pltpu.roll(a, A_HEAD_DIM // 2, 1))
        outs.append(a * cos + swapped * sin)
    return jnp.concatenate(outs, axis=1)


def _nm_plain_kernel(x_ref, g_ref, sc_ref, sh_ref, w_ref, o_ref, h_scr):
    @pl.when(pl.program_id(2) == 0)
    def _():
        h_scr[...] = _norm_mod(x_ref[...], g_ref[...], sc_ref[...], sh_ref[...]).astype(BF16)
    o_ref[...] = _dot(h_scr[...], w_ref[...])


def _nm_rope_kernel(x_ref, g_ref, sc_ref, sh_ref, w_ref, cos_ref, sin_ref, *rest,
                    tail_tile0, grouped):
    if grouped:
        perm_ref, *group_refs, tail_ref, h_scr = rest
    else:
        tail_ref, h_scr = rest
    i, j = pl.program_id(1), pl.program_id(2)
    group = lax.div(j, 3)
    kind = j - 3 * group

    @pl.when(j == 0)
    def _():
        h = _norm_mod(x_ref[...], g_ref[...], sc_ref[...], sh_ref[...]).astype(BF16)
        h_scr[0] = h
        if grouped:
            for p in range(perm_ref.shape[0]):
                h_scr[p + 1] = _dot(perm_ref[p], h).astype(BF16)

    acc = _dot(h_scr[group] if grouped else h_scr[0], w_ref[...])
    scale = jnp.where(kind == 0, A_HEAD_DIM ** -0.5, 1.0).astype(F32)
    is_v = kind == 2
    val = _rope_tile(acc, jnp.where(is_v, 1.0, cos_ref[...] * scale),
                     jnp.where(is_v, 0.0, sin_ref[...] * scale))

    @pl.when(i >= tail_tile0)
    def _():
        tail_ref[...] = val

    if grouped:
        for gi, og_ref in enumerate(group_refs):
            @pl.when(group == gi)
            def _(og_ref=og_ref):
                og_ref[...] = val.reshape(og_ref.shape).astype(og_ref.dtype)


def _nm_ffn_kernel(x_ref, g_ref, sc_ref, sh_ref, w_ref, gate_ref, val_ref, h_scr, *, gate_tiles):
    j = pl.program_id(2)

    @pl.when(j == 0)
    def _():
        h_scr[...] = _norm_mod(x_ref[...], g_ref[...], sc_ref[...], sh_ref[...]).astype(BF16)

    acc = _dot(h_scr[...], w_ref[...])

    @pl.when(j < gate_tiles)
    def _():
        gate_ref[...] = acc

    @pl.when(j >= gate_tiles)
    def _():
        val_ref[...] = acc.astype(val_ref.dtype)


def _row_tile(rows):
    return min(rows, 512)


def _nm_common_specs(x, sc, tm, tn, d):
    per_row = sc.shape[1] != 1
    mod_spec = (pl.BlockSpec((None, tm, d), lambda b, i, j: (b, i, 0)) if per_row
                else pl.BlockSpec((None, 1, d), lambda b, i, j: (b, 0, 0)))
    return [
        pl.BlockSpec((None, tm, d), lambda b, i, j: (b, i, 0)),
        pl.BlockSpec((1, d), lambda b, i, j: (0, 0)),
        mod_spec,
        mod_spec,
        pl.BlockSpec((d, tn), lambda b, i, j: (0, j)),
    ]


def _nm_plain_call(x, g, sc, sh, w, tn):
    bq, rows, d = x.shape
    n = w.shape[1]
    tm = _row_tile(rows)
    return pl.pallas_call(
        _nm_plain_kernel,
        grid=(bq, rows // tm, n // tn),
        in_specs=_nm_common_specs(x, sc, tm, tn, d),
        out_specs=pl.BlockSpec((None, tm, tn), lambda b, i, j: (b, i, j)),
        out_shape=jax.ShapeDtypeStruct((bq, rows, n), F32),
        scratch_shapes=[pltpu.VMEM((tm, d), BF16)],
        compiler_params=_params("arbitrary", "arbitrary", "arbitrary"),
        name="norm_mod_matmul",
    )(x, g, sc, sh, w)


def _residue_major(a, dil, tm=A_ROW_TILE):
    lead, (rows, c) = a.shape[:-2], a.shape[-2:]
    a = a.reshape(lead + (rows // tm, tm // dil, dil, c))
    return jnp.swapaxes(a, -2, -3).reshape(lead + (rows, c))


def _natural_order(a, dil, tm=A_ROW_TILE):
    lead, (rows, c) = a.shape[:-2], a.shape[-2:]
    a = a.reshape(lead + (rows // tm, dil, tm // dil, c))
    return jnp.swapaxes(a, -2, -3).reshape(lead + (rows, c))


def _perm_matrices(tm=A_ROW_TILE):
    eye = jnp.eye(tm, dtype=BF16)
    return jnp.stack([_residue_major(eye, dil, tm) for _, dil in A_GROUPS if dil > 1])


def _nm_rope_call(x, g, sc, sh, w, cos, sin, tail_rows, perms=None):
    bq, rows, d = x.shape
    n = w.shape[1]
    tn = A_HEADS * A_HEAD_DIM
    grouped = perms is not None
    tm = A_ROW_TILE if grouped else _row_tile(min(rows, tail_rows))
    tail_tile0 = (rows - tail_rows) // tm
    kern = functools.partial(_nm_rope_kernel, tail_tile0=tail_tile0, grouped=grouped)
    tab_spec = pl.BlockSpec((None, tm, LANES),
                            (lambda b, i, j: (lax.div(j, 3), i, 0)) if grouped
                            else (lambda b, i, j: (0, i, 0)))

    def tail_map(b, i, j):
        in_tail = i >= tail_tile0
        return (b, jnp.maximum(i - tail_tile0, 0), jnp.where(in_tail, j, 0))

    in_specs = _nm_common_specs(x, sc, tm, tn, d) + [tab_spec, tab_spec]
    out_specs, out_shape, args = [], [], [x, g, sc, sh, w, cos, sin]
    if grouped:
        in_specs.append(pl.BlockSpec(perms.shape, lambda b, i, j: (0, 0, 0)))
        args.append(perms)
        for gi, (_, dil) in enumerate(A_GROUPS):
            out_specs.append(pl.BlockSpec(
                (None, None, None, dil, tm // dil, tn),
                lambda b, i, j, gi=gi: (b, jnp.clip(j - 3 * gi, 0, 2), i, 0, 0, 0)))
            out_shape.append(jax.ShapeDtypeStruct((bq, 3, rows // tm, dil, tm // dil, tn), BF16))
    out_specs.append(pl.BlockSpec((None, tm, tn), tail_map))
    out_shape.append(jax.ShapeDtypeStruct((bq, tail_rows, n), F32))
    n_h = 1 + (perms.shape[0] if grouped else 0)
    return pl.pallas_call(
        kern,
        grid=(bq, rows // tm, n // tn),
        in_specs=in_specs,
        out_specs=out_specs,
        out_shape=out_shape,
        scratch_shapes=[pltpu.VMEM((n_h, tm, d), BF16)],
        compiler_params=_params("arbitrary", "arbitrary", "arbitrary"),
        name="norm_mod_qkv_rope",
    )(*args)


def _nm_ffn_call(x, g, sc, sh, w):
    bq, rows, d = x.shape
    n = w.shape[1]
    d_ff = n // 2
    gate_tiles = 2
    tn = d_ff // gate_tiles
    tm = _row_tile(rows)
    kern = functools.partial(_nm_ffn_kernel, gate_tiles=gate_tiles)
    return pl.pallas_call(
        kern,
        grid=(bq, rows // tm, n // tn),
        in_specs=_nm_common_specs(x, sc, tm, tn, d),
        out_specs=[
            pl.BlockSpec((None, tm, tn), lambda b, i, j: (b, i, jnp.minimum(j, gate_tiles - 1))),
            pl.BlockSpec((None, tm, tn), lambda b, i, j: (b, i, jnp.maximum(j - gate_tiles, 0))),
        ],
        out_shape=[jax.ShapeDtypeStruct((bq, rows, d_ff), F32),
                   jax.ShapeDtypeStruct((bq, rows, d_ff), BF16)],
        scratch_shapes=[pltpu.VMEM((tm, d), BF16)],
        compiler_params=_params("arbitrary", "arbitrary", "arbitrary"),
        name="norm_mod_ffn_up",
    )(x, g, sc, sh, w)


def _finish(y, x_ref, gate_ref, gfin_ref, o_ref, final_norm):
    xn = x_ref[...] + gate_ref[...] * y
    if final_norm:
        ms = jnp.mean(xn * xn, axis=-1, keepdims=True)
        xn = xn * lax.rsqrt(ms + NORM_EPS) * gfin_ref[...]
    o_ref[...] = xn


def _proj_res_kernel(a_ref, w_ref, x_ref, gate_ref, gfin_ref, o_ref, *, final_norm):
    _finish(_dot(a_ref[...], w_ref[...]), x_ref, gate_ref, gfin_ref, o_ref, final_norm)


def _attn_merge_proj_kernel(o0_ref, o1_ref, o2_ref, l0_ref, l1_ref, l2_ref, e_ref, w_ref,
                            x_ref, gate_ref, gfin_ref, *rest, final_norm, grouped):
    if grouped:
        unperm_ref, o_ref = rest
    else:
        (o_ref,) = rest
    outs, lses = [], []
    slot = 0
    for (_, dil), og_ref, lg_ref in zip(A_GROUPS, (o0_ref, o1_ref, o2_ref), (l0_ref, l1_ref, l2_ref)):
        if grouped and dil > 1:
            pt = unperm_ref[slot]
            slot += 1
            outs.append(_dot(pt, og_ref[...]))
            parts = _dot(pt, jnp.concatenate(_split3(lg_ref[...]), axis=1))
            lses.append(parts[:, 0:LANES] + parts[:, LANES:2 * LANES] + parts[:, 2 * LANES:3 * LANES])
        else:
            outs.append(og_ref[...].astype(F32))
            lses.append(lg_ref[...])
    m = jnp.maximum(jnp.maximum(lses[0], lses[1]), lses[2])
    exps = [jnp.exp(l - m) for l in lses]
    inv = 1.0 / (exps[0] + exps[1] + exps[2])
    merged = None
    for e, og in zip(exps, outs):
        wgt = e * inv
        hi = wgt.astype(BF16)
        lo = (wgt - hi.astype(F32)).astype(BF16)
        term = _dot(jnp.concatenate([hi, lo], axis=1), e_ref[...]) * og
        merged = term if merged is None else merged + term
    _finish(_dot(merged.astype(BF16), w_ref[...]), x_ref, gate_ref, gfin_ref, o_ref, final_norm)


def _ffn_down_kernel(gate_ref, prev_ref, state_ref, val_ref, cw_ref, cb_ref, w_ref,
                     x_ref, mgate_ref, gfin_ref, o_ref, ext_scr, *, final_norm):
    tm = gate_ref.shape[0]
    first = pl.program_id(1) == 0
    ext_scr[0:SUBLANES, :] = jnp.where(first, state_ref[...], prev_ref[...])
    ext_scr[SUBLANES:SUBLANES + tm, :] = gate_ref[...]
    width = cw_ref.shape[0]
    y = cb_ref[...]
    for t in range(width):
        off = SUBLANES - (width - 1) + t
        y = y + cw_ref[t:t + 1, :] * ext_scr[off:off + tm, :]
    act = (_silu(y) * val_ref[...].astype(F32)).astype(BF16)
    _finish(_dot(act, w_ref[...]), x_ref, mgate_ref, gfin_ref, o_ref, final_norm)


def _res_specs(tm, d, gate):
    per_row = gate.shape[1] != 1
    gate_spec = (pl.BlockSpec((None, tm, d), lambda b, i: (b, i, 0)) if per_row
                 else pl.BlockSpec((None, 1, d), lambda b, i: (b, 0, 0)))
    return [pl.BlockSpec((None, tm, d), lambda b, i: (b, i, 0)),
            gate_spec,
            pl.BlockSpec((1, d), lambda b, i: (0, 0))]


def _proj_res_call(a, w, x, gate, gfin, final_norm):
    bq, rows, d = x.shape
    k = a.shape[2]
    tm = _row_tile(rows)
    kern = functools.partial(_proj_res_kernel, final_norm=final_norm)
    return pl.pallas_call(
        kern,
        grid=(bq, rows // tm),
        in_specs=[pl.BlockSpec((None, tm, k), lambda b, i: (b, i, 0)),
                  pl.BlockSpec((k, d), lambda b, i: (0, 0))] + _res_specs(tm, d, gate),
        out_specs=pl.BlockSpec((None, tm, d), lambda b, i: (b, i, 0)),
        out_shape=jax.ShapeDtypeStruct((bq, rows, d), F32),
        compiler_params=_params("arbitrary", "arbitrary"),
        name="proj_residual",
    )(a, w, x, gate, gfin)


def _attn_merge_proj_call(outs, lses, expand, w, x, gate, gfin, final_norm, unperms=None):
    bq, rows, d = x.shape
    grouped = unperms is not None
    tm = A_ROW_TILE if grouped else _row_tile(rows)
    kern = functools.partial(_attn_merge_proj_kernel, final_norm=final_norm, grouped=grouped)
    o_spec = pl.BlockSpec((None, tm, d), lambda b, i: (b, i, 0))
    l_spec = pl.BlockSpec((None, tm, LANES), lambda b, i: (b, i, 0))
    extra_specs, extra = [], []
    if grouped:
        extra_specs.append(pl.BlockSpec(unperms.shape, lambda b, i: (0, 0, 0)))
        extra.append(unperms)
    return pl.pallas_call(
        kern,
        grid=(bq, rows // tm),
        in_specs=[o_spec] * 3 + [l_spec] * 3 + [
            pl.BlockSpec(expand.shape, lambda b, i: (0, 0)),
            pl.BlockSpec(w.shape, lambda b, i: (0, 0))] + _res_specs(tm, d, gate) + extra_specs,
        out_specs=pl.BlockSpec((None, tm, d), lambda b, i: (b, i, 0)),
        out_shape=jax.ShapeDtypeStruct((bq, rows, d), F32),
        compiler_params=_params("arbitrary", "arbitrary"),
        name="attn_merge_proj_residual",
    )(*outs, *lses, expand, w, x, gate, gfin, *extra)


def _ffn_down_call(gate_pre, state8, val, conv_w, conv_b, w, x, mgate, gfin, final_norm):
    bq, rows, d = x.shape
    f = gate_pre.shape[2]
    tm = min(rows, 256)
    kern = functools.partial(_ffn_down_kernel, final_norm=final_norm)
    blocks_per_tile = tm // SUBLANES
    return pl.pallas_call(
        kern,
        grid=(bq, rows // tm),
        in_specs=[
            pl.BlockSpec((None, tm, f), lambda b, i: (b, i, 0)),
            pl.BlockSpec((None, SUBLANES, f),
                         lambda b, i: (b, jnp.maximum(i * blocks_per_tile - 1, 0), 0)),
            pl.BlockSpec((None, SUBLANES, f), lambda b, i: (b, 0, 0)),
            pl.BlockSpec((None, tm, f), lambda b, i: (b, i, 0)),
            pl.BlockSpec(conv_w.shape, lambda b, i: (0, 0)),
            pl.BlockSpec((1, f), lambda b, i: (0, 0)),
            pl.BlockSpec((f, d), lambda b, i: (0, 0)),
        ] + _res_specs(tm, d, mgate),
        out_specs=pl.BlockSpec((None, tm, d), lambda b, i: (b, i, 0)),
        out_shape=jax.ShapeDtypeStruct((bq, rows, d), F32),
        scratch_shapes=[pltpu.VMEM((tm + SUBLANES, f), F32)],
        compiler_params=_params("arbitrary", "arbitrary"),
        name="ffn_conv_down_residual",
    )(gate_pre, gate_pre, state8, val, conv_w, conv_b, w, x, mgate, gfin)


def _attn_prompt_kernel(q_ref, kp_ref, kc_ref, vp_ref, vc_ref, o_ref, lse_ref):
    pieces, prow = q_ref.shape[0], q_ref.shape[1]
    blk = pieces * prow

    def rows_of(ref, cols):
        return jnp.concatenate([ref[p, :, cols] for p in range(pieces)], axis=0)

    n = pl.program_id(2)
    qi = lax.broadcasted_iota(jnp.int32, (blk, 2 * blk), 0)
    kj = lax.broadcasted_iota(jnp.int32, (blk, 2 * blk), 1)
    valid = (kj >= qi) & (kj <= qi + blk) & ((kj >= blk) | (n > 0))
    lane = lax.broadcasted_iota(jnp.int32, (blk, LANES), 1)
    low = lane < A_HEAD_DIM
    lse_all = jnp.zeros((blk, LANES), F32)
    for hp in range(A_HEADS // 2):
        cols = slice(hp * LANES, (hp + 1) * LANES)
        q2 = rows_of(q_ref, cols)
        k2 = jnp.concatenate([rows_of(kp_ref, cols), rows_of(kc_ref, cols)], axis=0)
        v2 = jnp.concatenate([rows_of(vp_ref, cols), rows_of(vc_ref, cols)], axis=0)
        pair = []
        for s in range(2):
            mine = low if s == 0 else jnp.logical_not(low)
            qh = jnp.where(mine, q2, jnp.zeros_like(q2))
            sc = jnp.where(valid, _dot_nt(qh, k2), NEG_BIG)
            m = jnp.max(sc, axis=1, keepdims=True)
            p = jnp.exp(sc - m)
            l = jnp.sum(p, axis=1, keepdims=True)
            pair.append(_dot(p.astype(BF16), v2) / l)
            lse_all = jnp.where(lane == 2 * hp + s, m + jnp.log(l), lse_all)
        o2 = jnp.where(low, pair[0], pair[1]).astype(o_ref.dtype)
        for p in range(pieces):
            o_ref[p, :, cols] = o2[p * prow:(p + 1) * prow]
    for p in range(pieces):
        lse_ref[p] = lse_all[p * prow:(p + 1) * prow]


def _attn_prompt_call(qkv_g, win, dil):
    b, _, tiles, _, per_res, width = qkv_g.shape
    tm = dil * per_res
    blk = win // dil
    assert (tiles * tm) % win == 0
    prow = min(blk, per_res)
    pieces = blk // prow
    run_per_tile = per_res // prow
    nb = tiles * per_res // blk

    def where(kind, shift):
        def index(bb, r, i):
            i = jnp.maximum(i - shift, 0)
            return (bb, kind, lax.div(i, run_per_tile), r, lax.rem(i, run_per_tile), 0)
        return index

    spec = lambda kind, shift: pl.BlockSpec((None, None, pieces, None, prow, width), where(kind, shift))
    out_index = lambda bb, r, i: (bb, lax.div(i, run_per_tile), r, lax.rem(i, run_per_tile), 0)
    return pl.pallas_call(
        _attn_prompt_kernel,
        grid=(b, dil, nb),
        in_specs=[spec(0, 0), spec(1, 1), spec(1, 0), spec(2, 1), spec(2, 0)],
        out_specs=[pl.BlockSpec((None, pieces, None, prow, width), out_index),
                   pl.BlockSpec((None, pieces, None, prow, LANES), out_index)],
        out_shape=[jax.ShapeDtypeStruct((b, tiles, dil, per_res, width), BF16),
                   jax.ShapeDtypeStruct((b, tiles, dil, per_res, LANES), F32)],
        compiler_params=_params("arbitrary", "arbitrary", "arbitrary"),
        name="dilated_attn_prompt",
    )(qkv_g, qkv_g, qkv_g, qkv_g, qkv_g)


def _attn_sample_kernel(q_ref, kn_ref, vn_ref, kc_ref, vc_ref,
                        ko_ref, vo_ref, o_ref, lse_ref,
                        m_scr, l_scr, acc_scr, kcar_scr, vcar_scr, *, lb, dil):
    t_new = q_ref.shape[0]
    width = q_ref.shape[1]
    chunk = kc_ref.shape[1]
    rows = A_HEADS * t_new
    step = pl.program_id(1)
    c = pl.num_programs(1) - 1 - step

    rq = lax.broadcasted_iota(jnp.int32, (rows, width), 0)
    lq = lax.broadcasted_iota(jnp.int32, (rows, width), 1)
    own = _div_pow2(lq, A_HEAD_DIM) == _div_pow2(rq, t_new)
    qbd = jnp.where(own, jnp.concatenate([q_ref[...]] * A_HEADS, axis=0), 0.0).astype(BF16)

    def accumulate(keys_t, vals_t, valid):
        sc = jnp.where(valid, _dot(qbd, keys_t.astype(BF16)), NEG_BIG)
        m_old = m_scr[...]
        m_new = jnp.maximum(m_old, jnp.max(sc, axis=1, keepdims=True))
        alpha = jnp.exp(m_old - m_new)
        p = jnp.where(valid, jnp.exp(sc - m_new), 0.0)
        l_scr[...] = alpha * l_scr[...] + jnp.sum(p, axis=1, keepdims=True)
        acc_scr[...] = alpha * acc_scr[...] + _dot_nt(p.astype(BF16), vals_t.astype(BF16))
        m_scr[...] = m_new

    @pl.when(step == 0)
    def _():
        m_scr[...] = jnp.full_like(m_scr, NEG_BIG)
        l_scr[...] = jnp.zeros_like(l_scr)
        acc_scr[...] = jnp.zeros_like(acc_scr)
        kcar_scr[...] = jnp.zeros_like(kcar_scr)
        vcar_scr[...] = jnp.zeros_like(vcar_scr)
        kcar_scr[:, 0:t_new] = kn_ref[...]
        vcar_scr[:, 0:t_new] = vn_ref[...]
        tq2 = _mod_pow2(lax.broadcasted_iota(jnp.int32, (rows, LANES), 0), t_new)
        tk2 = lax.broadcasted_iota(jnp.int32, (rows, LANES), 1)
        d2 = tq2 - tk2
        accumulate(kcar_scr[...], vcar_scr[...],
                   (tk2 < t_new) & (d2 >= 0) & (_mod_pow2(d2, dil) == 0))

    lane = lax.broadcasted_iota(jnp.int32, (width, LANES), 1)
    keep = lane < LANES - t_new
    nblk = chunk // LANES
    for src_ref, car_ref, dst_ref in ((kc_ref, kcar_scr, ko_ref), (vc_ref, vcar_scr, vo_ref)):
        nxt = pltpu.roll(car_ref[...], LANES - t_new, 1)
        for b in reversed(range(nblk)):
            cur = pltpu.roll(src_ref[:, b * LANES:(b + 1) * LANES], LANES - t_new, 1)
            dst_ref[:, b * LANES:(b + 1) * LANES] = jnp.where(keep, cur, nxt)
            nxt = cur
        car_ref[...] = src_ref[:, 0:LANES]

    tq = _mod_pow2(lax.broadcasted_iota(jnp.int32, (rows, chunk), 0), t_new)
    ik = lax.broadcasted_iota(jnp.int32, (rows, chunk), 1) + c * chunk
    dist = lb + tq - ik
    accumulate(kc_ref[...], vc_ref[...], (_mod_pow2(dist, dil) == 0) & (dist <= lb))

    @pl.when(c == 0)
    def _():
        full = jnp.where(own, acc_scr[...] / l_scr[...], 0.0)
        o_ref[...] = jnp.sum(full.reshape(A_HEADS, t_new, width), axis=0).astype(o_ref.dtype)
        lse_col = m_scr[...] + jnp.log(l_scr[...])
        rl = lax.broadcasted_iota(jnp.int32, (rows, LANES), 0)
        ll = lax.broadcasted_iota(jnp.int32, (rows, LANES), 1)
        spread = jnp.where(ll == _div_pow2(rl, t_new), lse_col, 0.0)
        lse_ref[...] = jnp.sum(spread.reshape(A_HEADS, t_new, LANES), axis=0)


def _attn_sample_call(proj, gi, win, dil, k_cache_t, v_cache_t):
    db, t_new, n = proj.shape
    width = A_HEADS * A_HEAD_DIM
    lb = k_cache_t.shape[2]
    assert lb == win and t_new == SUBLANES
    chunk = min(lb, 512)
    nchunk = lb // chunk
    kern = functools.partial(_attn_sample_kernel, lb=lb, dil=dil)
    kcol = slice((3 * gi + 1) * width, (3 * gi + 2) * width)
    vcol = slice((3 * gi + 2) * width, (3 * gi + 3) * width)
    kn_t = jnp.swapaxes(proj[:, :, kcol], 1, 2)
    vn_t = jnp.swapaxes(proj[:, :, vcol], 1, 2)
    new_spec = pl.BlockSpec((None, width, t_new), lambda b, s: (b, 0, 0))
    cur = pl.BlockSpec((None, width, chunk), lambda b, s: (b, 0, nchunk - 1 - s))
    return pl.pallas_call(
        kern,
        grid=(db, nchunk),
        in_specs=[pl.BlockSpec((None, t_new, width), lambda b, s: (b, 0, 3 * gi)),
                  new_spec, new_spec, cur, cur],
        out_specs=[cur, cur,
                   pl.BlockSpec((None, t_new, width), lambda b, s: (b, 0, 0)),
                   pl.BlockSpec((None, t_new, LANES), lambda b, s: (b, 0, 0))],
        out_shape=[jax.ShapeDtypeStruct((db, width, lb), F32),
                   jax.ShapeDtypeStruct((db, width, lb), F32),
                   jax.ShapeDtypeStruct((db, t_new, width), BF16),
                   jax.ShapeDtypeStruct((db, t_new, LANES), F32)],
        scratch_shapes=[pltpu.VMEM((A_HEADS * t_new, 1), F32),
                        pltpu.VMEM((A_HEADS * t_new, 1), F32),
                        pltpu.VMEM((A_HEADS * t_new, width), F32),
                        pltpu.VMEM((width, LANES), F32),
                        pltpu.VMEM((width, LANES), F32)],
        compiler_params=_params("arbitrary", "arbitrary"),
        name="dilated_attn_sample",
    )(proj, kn_t, vn_t, k_cache_t, v_cache_t)


def _delta_kernel(p_ref, prev_ref, cstate_ref, cw_ref, alog_ref, dtb_ref, ng_ref, s0_ref,
                  o_ref, ssm_ref, cnew_ref, s_scr, ext_scr, *, t_valid):
    cl = B_CHUNK
    rows_in = p_ref.shape[0]
    conv_dim = cw_ref.shape[1]
    c = pl.program_id(1)
    last = c == pl.num_programs(1) - 1
    qk_width = B_QK_HEADS * B_HEAD

    @pl.when(c == 0)
    def _():
        s_scr[...] = s0_ref[...]

    ext_scr[0:SUBLANES, :] = jnp.where(c == 0, cstate_ref[...], prev_ref[:, 0:conv_dim])
    ext_scr[SUBLANES:SUBLANES + rows_in, :] = p_ref[:, 0:conv_dim]
    if rows_in < cl:
        ext_scr[SUBLANES + rows_in:SUBLANES + cl, :] = jnp.zeros((cl - rows_in, conv_dim), F32)
    width = cw_ref.shape[0]
    y = None
    for t in range(width):
        off = SUBLANES - (width - 1) + t
        term = cw_ref[t:t + 1, :] * ext_scr[off:off + cl, :]
        y = term if y is None else y + term
    qkv = _silu(y)
    if t_valid < cl:
        rmask = lax.broadcasted_iota(jnp.int32, (cl, 1), 0) < t_valid
        qkv = jnp.where(rmask, qkv, 0.0)

    @pl.when(last)
    def _():
        cnew_ref[...] = ext_scr[t_valid:t_valid + SUBLANES, :]

    graw = p_ref[:, conv_dim + B_V_HEADS * B_HEAD:conv_dim + B_V_HEADS * B_HEAD + LANES]
    if rows_in < cl:
        graw = jnp.concatenate([graw, jnp.zeros((cl - rows_in, LANES), F32)], axis=0)
    beta_all = jax.nn.sigmoid(graw)
    xg = graw + dtb_ref[...]
    softplus = jnp.maximum(xg, 0.0) + jnp.log1p(jnp.exp(-jnp.abs(xg)))
    g_all = -jnp.exp(alog_ref[...]) * softplus
    if t_valid < cl:
        beta_all = jnp.where(rmask, beta_all, 0.0)
        g_all = jnp.where(rmask, g_all, 0.0)

    ri = lax.broadcasted_iota(jnp.int32, (2 * cl, cl), 0)
    ci = lax.broadcasted_iota(jnp.int32, (2 * cl, cl), 1)
    summat = jnp.where((ci <= ri) | (ri >= cl), 1.0, 0.0).astype(BF16)
    sums = _dot(summat, jnp.concatenate(_split3(g_all), axis=1))
    sums = sums[:, 0:LANES] + sums[:, LANES:2 * LANES] + sums[:, 2 * LANES:3 * LANES]
    cg_all, gl_all = sums[0:cl], sums[cl:2 * cl]

    def l2n(a):
        return a * lax.rsqrt(jnp.sum(a * a, axis=1, keepdims=True) + NORM_EPS)

    nst = B_STACK * cl
    rr = lax.broadcasted_iota(jnp.int32, (nst, nst), 0)
    cc = lax.broadcasted_iota(jnp.int32, (nst, nst), 1)
    same = _div_pow2(rr, cl) == _div_pow2(cc, cl)
    incl = same & (cc <= rr)
    strict = same & (cc < rr)
    lane6 = lax.broadcasted_iota(jnp.int32, (nst, LANES), 1)
    rblk = _div_pow2(lax.broadcasted_iota(jnp.int32, (nst, B_HEAD), 0), cl)

    def col_stack(arr, lane0, heads):
        return jnp.concatenate([arr[:, lane0 + h:lane0 + h + 1] for h in heads], axis=0)

    for grp in range(B_V_HEADS // B_STACK):
        heads = list(range(grp * B_STACK, (grp + 1) * B_STACK))
        rep = B_V_HEADS // B_QK_HEADS
        qs, ks = {}, {}
        for hq in sorted({h // rep for h in heads}):
            qs[hq] = l2n(qkv[:, hq * B_HEAD:(hq + 1) * B_HEAD]) * (B_HEAD ** -0.5)
            ks[hq] = l2n(qkv[:, qk_width + hq * B_HEAD:qk_width + (hq + 1) * B_HEAD])
        q_st = jnp.concatenate([qs[h // rep] for h in heads], axis=0)
        k_st = jnp.concatenate([ks[h // rep] for h in heads], axis=0)
        v_st = jnp.concatenate(
            [qkv[:, 2 * qk_width + h * B_HEAD:2 * qk_width + (h + 1) * B_HEAD] for h in heads], axis=0)
        beta = col_stack(beta_all, 0, heads)
        cg = col_stack(cg_all, B_V_HEADS, heads)
        gl = col_stack(gl_all, B_V_HEADS, heads)

        c_hi, c_mid, c_lo = (p.astype(F32) for p in _split3(cg))
        lmat = jnp.where(lane6 == 0, c_hi, jnp.where(lane6 == 1, c_mid, jnp.where(
            lane6 == 2, c_lo, jnp.where(lane6 < 6, 1.0, 0.0))))
        rmat = jnp.where(lane6 == 3, -c_hi, jnp.where(lane6 == 4, -c_mid, jnp.where(
            lane6 == 5, -c_lo, jnp.where(lane6 < 3, 1.0, 0.0))))
        diff = _dot_nt(lmat.astype(BF16), rmat.astype(BF16))
        decay = jnp.where(incl, jnp.exp(jnp.where(incl, diff, 0.0)), 0.0)

        k_bf = k_st.astype(BF16)
        a_mat = jnp.where(strict, beta * _dot_nt(k_bf, k_bf) * decay, 0.0)
        p_mat = _dot_nt(q_st.astype(BF16), k_bf) * decay

        blk = _div_pow2(rr, B_INV_BASE) == _div_pow2(cc, B_INV_BASE)
        a_diag = jnp.where(blk, a_mat, 0.0)
        nmat = -a_diag
        apow = a_diag
        for _ in range((B_INV_BASE - 1).bit_length() - 1):
            ab = apow.astype(BF16)
            apow = _dot(ab, ab)
            nmat = nmat + apow + _dot(nmat.astype(BF16), apow.astype(BF16))
        size = B_INV_BASE
        while size < cl:
            size *= 2
            merged = _div_pow2(rr, size) == _div_pow2(cc, size)
            low = jnp.where(merged & jnp.logical_not(blk), a_mat, 0.0)
            nb = nmat.astype(BF16)
            dl = low + _dot(nb, low.astype(BF16))
            nmat = nmat - (dl + _dot(dl.astype(BF16), nb))
            blk = merged
        ecg = jnp.exp(cg)
        rhs = jnp.concatenate([beta * v_st, (beta * ecg) * k_st], axis=1)
        sol = rhs + _dot(nmat.astype(BF16), rhs.astype(BF16))
        u0, wk = sol[:, 0:B_HEAD], sol[:, B_HEAD:2 * B_HEAD]
        qg = q_st * ecg
        kdec_t = (k_st * jnp.exp(gl - cg)).T.astype(BF16)

        u_parts, qgs_parts = [], []
        for idx, h in enumerate(heads):
            rsl = slice(idx * cl, (idx + 1) * cl)
            s_h = s_scr[h].astype(BF16)
            both = _dot(jnp.concatenate([wk[rsl], qg[rsl]], axis=0).astype(BF16), s_h)
            u_parts.append(u0[rsl] - both[0:cl])
            qgs_parts.append(both[cl:2 * cl])
        u_st = jnp.concatenate(u_parts, axis=0)
        o_st = jnp.concatenate(qgs_parts, axis=0) + _dot(p_mat.astype(BF16), u_st.astype(BF16))
        for idx, h in enumerate(heads):
            rsl = slice(idx * cl, (idx + 1) * cl)
            u_own = jnp.where(rblk == idx, u_st, 0.0).astype(BF16)
            s_scr[h] = jnp.exp(gl[idx * cl:idx * cl + 1]) * s_scr[h] + _dot(kdec_t, u_own)
            o_h = o_st[rsl]
            ms = jnp.mean(o_h * o_h, axis=1, keepdims=True)
            z_h = p_ref[:, conv_dim + h * B_HEAD:conv_dim + (h + 1) * B_HEAD]
            res = o_h[0:rows_in] * lax.rsqrt(ms[0:rows_in] + NORM_EPS) * ng_ref[...] * _silu(z_h)
            o_ref[:, h * B_HEAD:(h + 1) * B_HEAD] = res.astype(o_ref.dtype)

    @pl.when(last)
    def _():
        ssm_ref[...] = s_scr[...]


def _delta_call(proj, cstate8, conv_w, alog_l, dtb_l, norm_g, s0):
    bq, t, n = proj.shape
    conv_dim = conv_w.shape[1]
    v_width = B_V_HEADS * B_HEAD
    rows_in = min(t, B_CHUNK)
    assert t % rows_in == 0
    nchunk = t // rows_in
    per8 = rows_in // SUBLANES
    kern = functools.partial(_delta_kernel, t_valid=rows_in)
    return pl.pallas_call(
        kern,
        grid=(bq, nchunk),
        in_specs=[
            pl.BlockSpec((None, rows_in, n), lambda b, c: (b, c, 0)),
            pl.BlockSpec((None, SUBLANES, n), lambda b, c: (b, jnp.maximum(c * per8 - 1, 0), 0)),
            pl.BlockSpec((None, SUBLANES, conv_dim), lambda b, c: (b, 0, 0)),
            pl.BlockSpec(conv_w.shape, lambda b, c: (0, 0)),
            pl.BlockSpec((1, LANES), lambda b, c: (0, 0)),
            pl.BlockSpec((1, LANES), lambda b, c: (0, 0)),
            pl.BlockSpec((1, B_HEAD), lambda b, c: (0, 0)),
            pl.BlockSpec((None, B_V_HEADS, B_HEAD, B_HEAD), lambda b, c: (b, 0, 0, 0)),
        ],
        out_specs=[
            pl.BlockSpec((None, rows_in, v_width), lambda b, c: (b, c, 0)),
            pl.BlockSpec((None, B_V_HEADS, B_HEAD, B_HEAD), lambda b, c: (b, 0, 0, 0)),
            pl.BlockSpec((None, SUBLANES, conv_dim), lambda b, c: (b, 0, 0)),
        ],
        out_shape=[jax.ShapeDtypeStruct((bq, t, v_width), BF16),
                   jax.ShapeDtypeStruct((bq, B_V_HEADS, B_HEAD, B_HEAD), F32),
                   jax.ShapeDtypeStruct((bq, SUBLANES, conv_dim), F32)],
        scratch_shapes=[pltpu.VMEM((B_V_HEADS, B_HEAD, B_HEAD), F32),
                        pltpu.VMEM((B_CHUNK + SUBLANES, conv_dim), F32)],
        compiler_params=_params("arbitrary", "arbitrary"),
        name="gated_deltanet",
    )(proj, proj, cstate8, conv_w, alog_l, dtb_l, norm_g, s0)


def _left_rows(state):
    return jnp.pad(state, ((0, 0), (SUBLANES - state.shape[1], 0), (0, 0)))


def _rope_tables(pos):
    half = A_HEAD_DIM // 2
    inv = ROPE_THETA ** (-jnp.arange(half, dtype=F32) / half)
    ang = pos.astype(F32)[:, None] * inv[None, :]
    cos, sin = jnp.cos(ang), jnp.sin(ang)
    reps = LANES // A_HEAD_DIM
    cos_t = jnp.tile(jnp.concatenate([cos, cos], axis=1), (1, reps))
    sin_t = jnp.tile(jnp.concatenate([-sin, sin], axis=1), (1, reps))
    return cos_t[None], sin_t[None]


def _trunk(x, mods, per_row, pos, a_caches, b_ssm, b_conv, f_conv, wts):
    seqs, t, d = x.shape
    (norm_mix_g, norm_ffn_g, norm_final_g, a_w_in, a_w_out, b_w_in, b_conv_w, b_a_log, b_dt_bias,
     b_norm_g, b_w_out, ffn_w_up, ffn_conv_w, ffn_conv_b, ffn_w_down, expand) = wts
    d_ff = ffn_w_down.shape[1]
    width = A_HEADS * A_HEAD_DIM

    def rows_view(a):
        return a.reshape(1, seqs * t, a.shape[2]) if per_row else a

    def seq_view(a):
        return a.reshape(seqs, t, a.shape[2])

    def mod_view(v):
        if per_row:
            return jnp.repeat(v, t, axis=0)[None]
        return v[:, None, :]

    cos_t, sin_t = _rope_tables(pos)
    prompt = a_caches is None
    if prompt:
        assert t % max(w for w, _ in A_GROUPS) == 0
        cos_t, sin_t = (jnp.concatenate([_residue_major(tab, dil) for _, dil in A_GROUPS])
                        for tab in (cos_t, sin_t))
        perms = _perm_matrices()
        unperms = jnp.swapaxes(perms, 1, 2)
    if per_row:
        cos_t, sin_t = jnp.tile(cos_t, (1, seqs, 1)), jnp.tile(sin_t, (1, seqs, 1))
    gfin = norm_final_g.reshape(1, d)
    xr = rows_view(x)
    new_a, new_ssm, new_bconv, new_fconv = [], None, None, []
    depth = norm_mix_g.shape[0]
    for layer in range(depth):
        sh1, sc1, g1, sh2, sc2, g2 = (mod_view(v) for v in mods[layer])
        gmix = norm_mix_g[layer].reshape(1, d)
        if layer % 2 == 0:
            outs, lses = [], []
            if prompt:
                tail_rows = max(w for w, _ in A_GROUPS)
                *qkv_groups, tail = _nm_rope_call(xr, gmix, sc1, sh1, a_w_in[0], cos_t, sin_t,
                                                  tail_rows, perms)
                for gi, (win, dil) in enumerate(A_GROUPS):
                    o_g, lse_g = _attn_prompt_call(qkv_groups[gi], win, dil)
                    outs.append(o_g.reshape(seqs, t, width))
                    lses.append(lse_g.reshape(seqs, t, LANES))
                    for kind in (1, 2):
                        col = slice((3 * gi + kind) * width, (3 * gi + kind + 1) * width)
                        keep = _natural_order(tail[:, :, col], dil)[:, tail_rows - win:]
                        new_a.append(keep.reshape(1, seqs, win, A_HEADS, A_HEAD_DIM))
                xr = _attn_merge_proj_call(outs, lses, expand, a_w_out[0], xr, g1, gfin, False,
                                           unperms)
            else:
                (tail,) = _nm_rope_call(xr, gmix, sc1, sh1, a_w_in[0], cos_t, sin_t, seqs * t)
                for gi, (win, dil) in enumerate(A_GROUPS):
                    as_t = lambda c: jnp.transpose(c[0], (0, 2, 3, 1)).reshape(seqs, width, c.shape[2])
                    k_new, v_new, o_g, lse_g = _attn_sample_call(
                        seq_view(tail), gi, win, dil, as_t(a_caches[2 * gi]), as_t(a_caches[2 * gi + 1]))
                    outs.append(rows_view(o_g))
                    lses.append(rows_view(lse_g))
                    for buf in (k_new, v_new):
                        buf = buf.reshape(seqs, A_HEADS, A_HEAD_DIM, buf.shape[2])
                        new_a.append(jnp.transpose(buf, (0, 3, 1, 2))[None])
                xr = _attn_merge_proj_call(outs, lses, expand, a_w_out[0], xr, g1, gfin, False)
        else:
            proj = _nm_plain_call(xr, gmix, sc1, sh1, b_w_in[0], 1280)
            conv0 = (jnp.zeros((seqs, b_conv_w.shape[1] - 1, b_conv_w.shape[2]), F32)
                     if b_conv is None else b_conv[0])
            s0 = jnp.zeros((seqs, B_V_HEADS, B_HEAD, B_HEAD), F32) if b_ssm is None else b_ssm[0]
            lane_pad = lambda v: jnp.pad(v.reshape(1, -1), ((0, 0), (B_V_HEADS, LANES - 2 * B_V_HEADS)))
            o_b, ssm1, cnew = _delta_call(seq_view(proj), _left_rows(conv0), b_conv_w[0],
                                          lane_pad(b_a_log[0]), lane_pad(b_dt_bias[0]),
                                          b_norm_g[0].reshape(1, B_HEAD), s0)
            new_ssm = ssm1[None]
            new_bconv = cnew[None, :, SUBLANES - (b_conv_w.shape[1] - 1):, :]
            xr = _proj_res_call(rows_view(o_b), b_w_out[0], xr, g1, gfin, False)
        gate_pre, val = _nm_ffn_call(xr, norm_ffn_g[layer].reshape(1, d), sc2, sh2, ffn_w_up[layer])
        f0 = (jnp.zeros((seqs, ffn_conv_w.shape[1] - 1, d_ff), F32) if f_conv is None else f_conv[layer])
        gate_seq = seq_view(gate_pre)
        new_fconv.append(gate_seq[:, t - (ffn_conv_w.shape[1] - 1):, :])
        mg2 = mods[layer][5][:, None, :]
        x_seq = _ffn_down_call(gate_seq, _left_rows(f0), seq_view(val), ffn_conv_w[layer],
                               ffn_conv_b[layer].reshape(1, d_ff), ffn_w_down[layer],
                               seq_view(xr), mg2, gfin, layer == depth - 1)
        xr = rows_view(x_seq)
    return seq_view(xr), new_a, new_ssm, new_bconv, jnp.stack(new_fconv)


def kernel(x_prompt, x_sample, c_prompt, c_sample, cache_a_k_w128, cache_a_v_w128, cache_a_k_w512, cache_a_v_w512, cache_a_k_w2048, cache_a_v_w2048, state_b_ssm, state_b_conv, state_ffn_conv, norm_mix_g, norm_ffn_g, norm_final_g, w_mod, b_mod, a_w_in, a_w_out, b_w_in, b_conv_w, b_a_log, b_dt_bias, b_norm_g, b_w_out, ffn_w_up, ffn_conv_w, ffn_conv_b, ffn_w_down):
    bp, s, d = x_prompt.shape
    db, t_new, _ = x_sample.shape
    depth = w_mod.shape[0]

    c_all = jnp.concatenate([c_prompt, c_sample], axis=0)
    c_all = jnp.pad(c_all, ((0, (-c_all.shape[0]) % SUBLANES), (0, 0)))
    mod = _mod_call(c_all, w_mod, b_mod)

    def mods_of(lo, hi):
        return [[mod[l, lo:hi, k * d:(k + 1) * d] for k in range(6)] for l in range(depth)]

    b_in_cols = 6400
    b_w_in_p = jnp.pad(b_w_in, ((0, 0), (0, 0), (0, b_in_cols - b_w_in.shape[2])))
    heads = jnp.arange(A_HEADS * A_HEAD_DIM) // A_HEAD_DIM
    half = (jnp.arange(LANES)[:, None] == heads[None, :]).astype(BF16)
    expand = jnp.concatenate([half, half], axis=0)
    wts = (norm_mix_g, norm_ffn_g, norm_final_g, a_w_in.astype(BF16), a_w_out.astype(BF16),
           b_w_in_p.astype(BF16), b_conv_w, b_a_log, b_dt_bias, b_norm_g, b_w_out.astype(BF16),
           ffn_w_up.astype(BF16), ffn_conv_w, ffn_conv_b, ffn_w_down.astype(BF16), expand)

    pos_p = jnp.arange(s, dtype=jnp.int32)
    pos_s = PAST_LEN + jnp.arange(t_new, dtype=jnp.int32)
    y_p, a_p, ssm_p, bconv_p, fconv_p = _trunk(
        x_prompt, mods_of(0, bp), False, pos_p, None, None, None, None, wts)
    a_caches = [cache_a_k_w128, cache_a_v_w128, cache_a_k_w512, cache_a_v_w512,
                cache_a_k_w2048, cache_a_v_w2048]
    y_s, a_s, ssm_s, bconv_s, fconv_s = _trunk(
        x_sample, mods_of(bp, bp + db), True, pos_s, a_caches, state_b_ssm, state_b_conv,
        state_ffn_conv, wts)
    return (y_p, y_s,
            a_p[0], a_s[0], a_p[1], a_s[1], a_p[2], a_s[2], a_p[3], a_s[3], a_p[4], a_s[4],
            a_p[5], a_s[5], ssm_p, ssm_s, bconv_p, bconv_s, fconv_p, fconv_s)
```

```python
import functools

import jax
import jax.numpy as jnp
from jax import lax
from jax.experimental import pallas as pl
from jax.experimental.pallas import tpu as pltpu

F32 = jnp.float32
BF16 = jnp.bfloat16

PAST_LEN = 16384
A_GROUPS = ((128, 1), (512, 4), (2048, 16))
A_HEADS = 16
A_HEAD_DIM = 64
A_ROW_TILE = 512
ROPE_THETA = 10000.0
NORM_EPS = 1e-6
B_HEAD = 128
B_V_HEADS = 16
B_QK_HEADS = 8
B_CHUNK = 64
B_STACK = 2
B_INV_BASE = 16
NEG_BIG = -1e30

LANES = 128
SUBLANES = 8
VMEM_LIMIT_BYTES = 56 * 1024 * 1024


def _params(*sem):
    return pltpu.CompilerParams(dimension_semantics=sem, vmem_limit_bytes=VMEM_LIMIT_BYTES)


def _silu(x):
    return x * jax.nn.sigmoid(x)


def _dot(a, b):
    return jnp.dot(a, b, preferred_element_type=F32)


def _dot_nt(a, b):
    return lax.dot_general(a, b, (((1,), (1,)), ((), ())), preferred_element_type=F32)


def _shift_of(n):
    assert n > 0 and n & (n - 1) == 0, n
    return n.bit_length() - 1


def _div_pow2(v, n):
    return lax.shift_right_logical(v, _shift_of(n))


def _mod_pow2(v, n):
    assert n & (n - 1) == 0
    return v & (n - 1)


def _split3(x):
    hi = x.astype(BF16)
    r1 = x - hi.astype(F32)
    mid = r1.astype(BF16)
    lo = (r1 - mid.astype(F32)).astype(BF16)
    return hi, mid, lo


def _mod_kernel(c_ref, w_ref, b_ref, o_ref):
    cs = _silu(c_ref[...]).astype(BF16)
    o_ref[...] = _dot(cs, w_ref[...].astype(BF16)) + b_ref[...]


def _mod_call(c_all, w_mod, b_mod):
    depth, d, n = w_mod.shape
    rows = c_all.shape[0]
    tn = 1536
    return pl.pallas_call(
        _mod_kernel,
        grid=(depth, n // tn),
        in_specs=[
            pl.BlockSpec((rows, d), lambda l, j: (0, 0)),
            pl.BlockSpec((None, d, tn), lambda l, j: (l, 0, j)),
            pl.BlockSpec((None, 1, tn), lambda l, j: (l, 0, j)),
        ],
        out_specs=pl.BlockSpec((None, rows, tn), lambda l, j: (l, 0, j)),
        out_shape=jax.ShapeDtypeStruct((depth, rows, n), F32),
        compiler_params=_params("arbitrary", "arbitrary"),
        name="adaln_mod",
    )(c_all, w_mod, b_mod.reshape(depth, 1, n))


def _norm_mod(x, g, sc, sh):
    ms = jnp.mean(x * x, axis=-1, keepdims=True)
    return x * lax.rsqrt(ms + NORM_EPS) * g * (1.0 + sc) + sh


def _rope_tile(acc, cos, sin):
    lane = lax.broadcasted_iota(jnp.int32, (acc.shape[0], LANES), 1)
    first_half = _mod_pow2(lane, A_HEAD_DIM) < (A_HEAD_DIM // 2)
    outs = []
    for c in range(acc.shape[1] // LANES):
        a = acc[:, c * LANES:(c + 1) * LANES]
        swapped = jnp.where(first_half,
                            pltpu.roll(a, LANES - A_HEAD_DIM // 2, 1),
                            pltpu.roll(a, A_HEAD_DIM // 2, 1))
        outs.append(a * cos + swapped * sin)
    return jnp.concatenate(outs, axis=1)


def _nm_plain_kernel(x_ref, g_ref, sc_ref, sh_ref, w_ref, o_ref, h_scr):
    @pl.when(pl.program_id(2) == 0)
    def _():
        h_scr[...] = _norm_mod(x_ref[...], g_ref[...], sc_ref[...], sh_ref[...]).astype(BF16)
    o_ref[...] = _dot(h_scr[...], w_ref[...])


def _nm_rope_kernel(x_ref, g_ref, sc_ref, sh_ref, w_ref, cos_ref, sin_ref, *rest,
                    tail_tile0, grouped):
    if grouped:
        perm_ref, *group_refs, tail_ref, h_scr = rest
    else:
        tail_ref, h_scr = rest
    i, j = pl.program_id(1), pl.program_id(2)
    group = lax.div(j, 3)
    kind = j - 3 * group

    @pl.when(j == 0)
    def _():
        h = _norm_mod(x_ref[...], g_ref[...], sc_ref[...], sh_ref[...]).astype(BF16)
        h_scr[0] = h
        if grouped:
            for p in range(perm_ref.shape[0]):
                h_scr[p + 1] = _dot(perm_ref[p], h).astype(BF16)

    acc = _dot(h_scr[group] if grouped else h_scr[0], w_ref[...])
    scale = jnp.where(kind == 0, A_HEAD_DIM ** -0.5, 1.0).astype(F32)
    is_v = kind == 2
    val = _rope_tile(acc, jnp.where(is_v, 1.0, cos_ref[...] * scale),
                     jnp.where(is_v, 0.0, sin_ref[...] * scale))

    @pl.when(i >= tail_tile0)
    def _():
        tail_ref[...] = val

    if grouped:
        for gi, og_ref in enumerate(group_refs):
            @pl.when(group == gi)
            def _(og_ref=og_ref):
                og_ref[...] = val.reshape(og_ref.shape).astype(og_ref.dtype)


def _nm_ffn_kernel(x_ref, g_ref, sc_ref, sh_ref, w_ref, gate_ref, val_ref, h_scr, *, gate_tiles):
    j = pl.program_id(2)

    @pl.when(j == 0)
    def _():
        h_scr[...] = _norm_mod(x_ref[...], g_ref[...], sc_ref[...], sh_ref[...]).astype(BF16)

    acc = _dot(h_scr[...], w_ref[...])

    @pl.when(j < gate_tiles)
    def _():
        gate_ref[...] = acc

    @pl.when(j >= gate_tiles)
    def _():
        val_ref[...] = acc.astype(val_ref.dtype)


def _row_tile(rows):
    return min(rows, 512)


def _nm_common_specs(x, sc, tm, tn, d):
    per_row = sc.shape[1] != 1
    mod_spec = (pl.BlockSpec((None, tm, d), lambda b, i, j: (b, i, 0)) if per_row
                else pl.BlockSpec((None, 1, d), lambda b, i, j: (b, 0, 0)))
    return [
        pl.BlockSpec((None, tm, d), lambda b, i, j: (b, i, 0)),
        pl.BlockSpec((1, d), lambda b, i, j: (0, 0)),
        mod_spec,
        mod_spec,
        pl.BlockSpec((d, tn), lambda b, i, j: (0, j)),
    ]


def _nm_plain_call(x, g, sc, sh, w, tn):
    bq, rows, d = x.shape
    n = w.shape[1]
    tm = _row_tile(rows)
    return pl.pallas_call(
        _nm_plain_kernel,
        grid=(bq, rows // tm, n // tn),
        in_specs=_nm_common_specs(x, sc, tm, tn, d),
        out_specs=pl.BlockSpec((None, tm, tn), lambda b, i, j: (b, i, j)),
        out_shape=jax.ShapeDtypeStruct((bq, rows, n), F32),
        scratch_shapes=[pltpu.VMEM((tm, d), BF16)],
        compiler_params=_params("arbitrary", "arbitrary", "arbitrary"),
        name="norm_mod_matmul",
    )(x, g, sc, sh, w)


def _residue_major(a, dil, tm=A_ROW_TILE):
    lead, (rows, c) = a.shape[:-2], a.shape[-2:]
    a = a.reshape(lead + (rows // tm, tm // dil, dil, c))
    return jnp.swapaxes(a, -2, -3).reshape(lead + (rows, c))


def _natural_order(a, dil, tm=A_ROW_TILE):
    lead, (rows, c) = a.shape[:-2], a.shape[-2:]
    a = a.reshape(lead + (rows // tm, dil, tm // dil, c))
    return jnp.swapaxes(a, -2, -3).reshape(lead + (rows, c))


def _perm_matrices(tm=A_ROW_TILE):
    eye = jnp.eye(tm, dtype=BF16)
    return jnp.stack([_residue_major(eye, dil, tm) for _, dil in A_GROUPS if dil > 1])


def _nm_rope_call(x, g, sc, sh, w, cos, sin, tail_rows, perms=None):
    bq, rows, d = x.shape
    n = w.shape[1]
    tn = A_HEADS * A_HEAD_DIM
    grouped = perms is not None
    tm = A_ROW_TILE if grouped else _row_tile(min(rows, tail_rows))
    tail_tile0 = (rows - tail_rows) // tm
    kern = functools.partial(_nm_rope_kernel, tail_tile0=tail_tile0, grouped=grouped)
    tab_spec = pl.BlockSpec((None, tm, LANES),
                            (lambda b, i, j: (lax.div(j, 3), i, 0)) if grouped
                            else (lambda b, i, j: (0, i, 0)))

    def tail_map(b, i, j):
        in_tail = i >= tail_tile0
        return (b, jnp.maximum(i - tail_tile0, 0), jnp.where(in_tail, j, 0))

    in_specs = _nm_common_specs(x, sc, tm, tn, d) + [tab_spec, tab_spec]
    out_specs, out_shape, args = [], [], [x, g, sc, sh, w, cos, sin]
    if grouped:
        in_specs.append(pl.BlockSpec(perms.shape, lambda b, i, j: (0, 0, 0)))
        args.append(perms)
        for gi, (_, dil) in enumerate(A_GROUPS):
            out_specs.append(pl.BlockSpec(
                (None, None, None, dil, tm // dil, tn),
                lambda b, i, j, gi=gi: (b, jnp.clip(j - 3 * gi, 0, 2), i, 0, 0, 0)))
            out_shape.append(jax.ShapeDtypeStruct((bq, 3, rows // tm, dil, tm // dil, tn), BF16))
    out_specs.append(pl.BlockSpec((None, tm, tn), tail_map))
    out_shape.append(jax.ShapeDtypeStruct((bq, tail_rows, n), F32))
    n_h = 1 + (perms.shape[0] if grouped else 0)
    return pl.pallas_call(
        kern,
        grid=(bq, rows // tm, n // tn),
        in_specs=in_specs,
        out_specs=out_specs,
        out_shape=out_shape,
        scratch_shapes=[pltpu.VMEM((n_h, tm, d), BF16)],
        compiler_params=_params("arbitrary", "arbitrary", "arbitrary"),
        name="norm_mod_qkv_rope",
    )(*args)


def _nm_ffn_call(x, g, sc, sh, w):
    bq, rows, d = x.shape
    n = w.shape[1]
    d_ff = n // 2
    gate_tiles = 2
    tn = d_ff // gate_tiles
    tm = _row_tile(rows)
    kern = functools.partial(_nm_ffn_kernel, gate_tiles=gate_tiles)
    return pl.pallas_call(
        kern,
        grid=(bq, rows // tm, n // tn),
        in_specs=_nm_common_specs(x, sc, tm, tn, d),
        out_specs=[
            pl.BlockSpec((None, tm, tn), lambda b, i, j: (b, i, jnp.minimum(j, gate_tiles - 1))),
            pl.BlockSpec((None, tm, tn), lambda b, i, j: (b, i, jnp.maximum(j - gate_tiles, 0))),
        ],
        out_shape=[jax.ShapeDtypeStruct((bq, rows, d_ff), F32),
                   jax.ShapeDtypeStruct((bq, rows, d_ff), BF16)],
        scratch_shapes=[pltpu.VMEM((tm, d), BF16)],
        compiler_params=_params("arbitrary", "arbitrary", "arbitrary"),
        name="norm_mod_ffn_up",
    )(x, g, sc, sh, w)


def _finish(y, x_ref, gate_ref, gfin_ref, o_ref, final_norm):
    xn = x_ref[...] + gate_ref[...] * y
    if final_norm:
        ms = jnp.mean(xn * xn, axis=-1, keepdims=True)
        xn = xn * lax.rsqrt(ms + NORM_EPS) * gfin_ref[...]
    o_ref[...] = xn


def _proj_res_kernel(a_ref, w_ref, x_ref, gate_ref, gfin_ref, o_ref, *, final_norm):
    _finish(_dot(a_ref[...], w_ref[...]), x_ref, gate_ref, gfin_ref, o_ref, final_norm)


def _attn_merge_proj_kernel(o0_ref, o1_ref, o2_ref, l0_ref, l1_ref, l2_ref, e_ref, w_ref,
                            x_ref, gate_ref, gfin_ref, *rest, final_norm, grouped):
    if grouped:
        unperm_ref, o_ref = rest
    else:
        (o_ref,) = rest
    outs, lses = [], []
    slot = 0
    for (_, dil), og_ref, lg_ref in zip(A_GROUPS, (o0_ref, o1_ref, o2_ref), (l0_ref, l1_ref, l2_ref)):
        if grouped and dil > 1:
            pt = unperm_ref[slot]
            slot += 1
            outs.append(_dot(pt, og_ref[...]))
            parts = _dot(pt, jnp.concatenate(_split3(lg_ref[...]), axis=1))
            lses.append(parts[:, 0:LANES] + parts[:, LANES:2 * LANES] + parts[:, 2 * LANES:3 * LANES])
        else:
            outs.append(og_ref[...].astype(F32))
            lses.append(lg_ref[...])
    m = jnp.maximum(jnp.maximum(lses[0], lses[1]), lses[2])
    exps = [jnp.exp(l - m) for l in lses]
    inv = 1.0 / (exps[0] + exps[1] + exps[2])
    merged = None
    for e, og in zip(exps, outs):
        wgt = e * inv
        hi = wgt.astype(BF16)
        lo = (wgt - hi.astype(F32)).astype(BF16)
        term = _dot(jnp.concatenate([hi, lo], axis=1), e_ref[...]) * og
        merged = term if merged is None else merged + term
    _finish(_dot(merged.astype(BF16), w_ref[...]), x_ref, gate_ref, gfin_ref, o_ref, final_norm)


def _ffn_down_kernel(gate_ref, prev_ref, state_ref, val_ref, cw_ref, cb_ref, w_ref,
                     x_ref, mgate_ref, gfin_ref, o_ref, ext_scr, *, final_norm):
    tm = gate_ref.shape[0]
    first = pl.program_id(1) == 0
    ext_scr[0:SUBLANES, :] = jnp.where(first, state_ref[...], prev_ref[...])
    ext_scr[SUBLANES:SUBLANES + tm, :] = gate_ref[...]
    width = cw_ref.shape[0]
    y = cb_ref[...]
    for t in range(width):
        off = SUBLANES - (width - 1) + t
        y = y + cw_ref[t:t + 1, :] * ext_scr[off:off + tm, :]
    act = (_silu(y) * val_ref[...].astype(F32)).astype(BF16)
    _finish(_dot(act, w_ref[...]), x_ref, mgate_ref, gfin_ref, o_ref, final_norm)


def _res_specs(tm, d, gate):
    per_row = gate.shape[1] != 1
    gate_spec = (pl.BlockSpec((None, tm, d), lambda b, i: (b, i, 0)) if per_row
                 else pl.BlockSpec((None, 1, d), lambda b, i: (b, 0, 0)))
    return [pl.BlockSpec((None, tm, d), lambda b, i: (b, i, 0)),
            gate_spec,
            pl.BlockSpec((1, d), lambda b, i: (0, 0))]


def _proj_res_call(a, w, x, gate, gfin, final_norm):
    bq, rows, d = x.shape
    k = a.shape[2]
    tm = _row_tile(rows)
    kern = functools.partial(_proj_res_kernel, final_norm=final_norm)
    return pl.pallas_call(
        kern,
        grid=(bq, rows // tm),
        in_specs=[pl.BlockSpec((None, tm, k), lambda b, i: (b, i, 0)),
                  pl.BlockSpec((k, d), lambda b, i: (0, 0))] + _res_specs(tm, d, gate),
        out_specs=pl.BlockSpec((None, tm, d), lambda b, i: (b, i, 0)),
        out_shape=jax.ShapeDtypeStruct((bq, rows, d), F32),
        compiler_params=_params("arbitrary", "arbitrary"),
        name="proj_residual",
    )(a, w, x, gate, gfin)


def _attn_merge_proj_call(outs, lses, expand, w, x, gate, gfin, final_norm, unperms=None):
    bq, rows, d = x.shape
    grouped = unperms is not None
    tm = A_ROW_TILE if grouped else _row_tile(rows)
    kern = functools.partial(_attn_merge_proj_kernel, final_norm=final_norm, grouped=grouped)
    o_spec = pl.BlockSpec((None, tm, d), lambda b, i: (b, i, 0))
    l_spec = pl.BlockSpec((None, tm, LANES), lambda b, i: (b, i, 0))
    extra_specs, extra = [], []
    if grouped:
        extra_specs.append(pl.BlockSpec(unperms.shape, lambda b, i: (0, 0, 0)))
        extra.append(unperms)
    return pl.pallas_call(
        kern,
        grid=(bq, rows // tm),
        in_specs=[o_spec] * 3 + [l_spec] * 3 + [
            pl.BlockSpec(expand.shape, lambda b, i: (0, 0)),
            pl.BlockSpec(w.shape, lambda b, i: (0, 0))] + _res_specs(tm, d, gate) + extra_specs,
        out_specs=pl.BlockSpec((None, tm, d), lambda b, i: (b, i, 0)),
        out_shape=jax.ShapeDtypeStruct((bq, rows, d), F32),
        compiler_params=_params("arbitrary", "arbitrary"),
        name="attn_merge_proj_residual",
    )(*outs, *lses, expand, w, x, gate, gfin, *extra)


def _ffn_down_call(gate_pre, state8, val, conv_w, conv_b, w, x, mgate, gfin, final_norm):
    bq, rows, d = x.shape
    f = gate_pre.shape[2]
    tm = min(rows, 256)
    kern = functools.partial(_ffn_down_kernel, final_norm=final_norm)
    blocks_per_tile = tm // SUBLANES
    return pl.pallas_call(
        kern,
        grid=(bq, rows // tm),
        in_specs=[
            pl.BlockSpec((None, tm, f), lambda b, i: (b, i, 0)),
            pl.BlockSpec((None, SUBLANES, f),
                         lambda b, i: (b, jnp.maximum(i * blocks_per_tile - 1, 0), 0)),
            pl.BlockSpec((None, SUBLANES, f), lambda b, i: (b, 0, 0)),
            pl.BlockSpec((None, tm, f), lambda b, i: (b, i, 0)),
            pl.BlockSpec(conv_w.shape, lambda b, i: (0, 0)),
            pl.BlockSpec((1, f), lambda b, i: (0, 0)),
            pl.BlockSpec((f, d), lambda b, i: (0, 0)),
        ] + _res_specs(tm, d, mgate),
        out_specs=pl.BlockSpec((None, tm, d), lambda b, i: (b, i, 0)),
        out_shape=jax.ShapeDtypeStruct((bq, rows, d), F32),
        scratch_shapes=[pltpu.VMEM((tm + SUBLANES, f), F32)],
        compiler_params=_params("arbitrary", "arbitrary"),
        name="ffn_conv_down_residual",
    )(gate_pre, gate_pre, state8, val, conv_w, conv_b, w, x, mgate, gfin)


def _attn_prompt_kernel(q_ref, kp_ref, kc_ref, vp_ref, vc_ref, o_ref, lse_ref):
    pieces, prow = q_ref.shape[0], q_ref.shape[1]
    blk = pieces * prow

    def rows_of(ref, cols):
        return jnp.concatenate([ref[p, :, cols] for p in range(pieces)], axis=0)

    n = pl.program_id(2)
    qi = lax.broadcasted_iota(jnp.int32, (blk, 2 * blk), 0)
    kj = lax.broadcasted_iota(jnp.int32, (blk, 2 * blk), 1)
    valid = (kj >= qi) & (kj <= qi + blk) & ((kj >= blk) | (n > 0))
    lane = lax.broadcasted_iota(jnp.int32, (blk, LANES), 1)
    low = lane < A_HEAD_DIM
    lse_all = jnp.zeros((blk, LANES), F32)
    for hp in range(A_HEADS // 2):
        cols = slice(hp * LANES, (hp + 1) * LANES)
        q2 = rows_of(q_ref, cols)
        k2 = jnp.concatenate([rows_of(kp_ref, cols), rows_of(kc_ref, cols)], axis=0)
        v2 = jnp.concatenate([rows_of(vp_ref, cols), rows_of(vc_ref, cols)], axis=0)
        pair = []
        for s in range(2):
            mine = low if s == 0 else jnp.logical_not(low)
            qh = jnp.where(mine, q2, jnp.zeros_like(q2))
            sc = jnp.where(valid, _dot_nt(qh, k2), NEG_BIG)
            m = jnp.max(sc, axis=1, keepdims=True)
            p = jnp.exp(sc - m)
            l = jnp.sum(p, axis=1, keepdims=True)
            pair.append(_dot(p.astype(BF16), v2) / l)
            lse_all = jnp.where(lane == 2 * hp + s, m + jnp.log(l), lse_all)
        o2 = jnp.where(low, pair[0], pair[1]).astype(o_ref.dtype)
        for p in range(pieces):
            o_ref[p, :, cols] = o2[p * prow:(p + 1) * prow]
    for p in range(pieces):
        lse_ref[p] = lse_all[p * prow:(p + 1) * prow]


def _attn_prompt_call(qkv_g, win, dil):
    b, _, tiles, _, per_res, width = qkv_g.shape
    tm = dil * per_res
    blk = win // dil
    assert (tiles * tm) % win == 0
    prow = min(blk, per_res)
    pieces = blk // prow
    run_per_tile = per_res // prow
    nb = tiles * per_res // blk

    def where(kind, shift):
        def index(bb, r, i):
            i = jnp.maximum(i - shift, 0)
            return (bb, kind, lax.div(i, run_per_tile), r, lax.rem(i, run_per_tile), 0)
        return index

    spec = lambda kind, shift: pl.BlockSpec((None, None, pieces, None, prow, width), where(kind, shift))
    out_index = lambda bb, r, i: (bb, lax.div(i, run_per_tile), r, lax.rem(i, run_per_tile), 0)
    return pl.pallas_call(
        _attn_prompt_kernel,
        grid=(b, dil, nb),
        in_specs=[spec(0, 0), spec(1, 1), spec(1, 0), spec(2, 1), spec(2, 0)],
        out_specs=[pl.BlockSpec((None, pieces, None, prow, width), out_index),
                   pl.BlockSpec((None, pieces, None, prow, LANES), out_index)],
        out_shape=[jax.ShapeDtypeStruct((b, tiles, dil, per_res, width), BF16),
                   jax.ShapeDtypeStruct((b, tiles, dil, per_res, LANES), F32)],
        compiler_params=_params("arbitrary", "arbitrary", "arbitrary"),
        name="dilated_attn_prompt",
    )(qkv_g, qkv_g, qkv_g, qkv_g, qkv_g)


def _attn_sample_kernel(q_ref, kn_ref, vn_ref, kc_ref, vc_ref,
                        ko_ref, vo_ref, o_ref, lse_ref,
                        m_scr, l_scr, acc_scr, kcar_scr, vcar_scr, *, lb, dil):
    t_new = q_ref.shape[0]
    width = q_ref.shape[1]
    chunk = kc_ref.shape[1]
    rows = A_HEADS * t_new
    step = pl.program_id(1)
    c = pl.num_programs(1) - 1 - step

    rq = lax.broadcasted_iota(jnp.int32, (rows, width), 0)
    lq = lax.broadcasted_iota(jnp.int32, (rows, width), 1)
    own = _div_pow2(lq, A_HEAD_DIM) == _div_pow2(rq, t_new)
    qbd = jnp.where(own, jnp.concatenate([q_ref[...]] * A_HEADS, axis=0), 0.0).astype(BF16)

    def accumulate(keys_t, vals_t, valid):
        sc = jnp.where(valid, _dot(qbd, keys_t.astype(BF16)), NEG_BIG)
        m_old = m_scr[...]
        m_new = jnp.maximum(m_old, jnp.max(sc, axis=1, keepdims=True))
        alpha = jnp.exp(m_old - m_new)
        p = jnp.where(valid, jnp.exp(sc - m_new), 0.0)
        l_scr[...] = alpha * l_scr[...] + jnp.sum(p, axis=1, keepdims=True)
        acc_scr[...] = alpha * acc_scr[...] + _dot_nt(p.astype(BF16), vals_t.astype(BF16))
        m_scr[...] = m_new

    @pl.when(step == 0)
    def _():
        m_scr[...] = jnp.full_like(m_scr, NEG_BIG)
        l_scr[...] = jnp.zeros_like(l_scr)
        acc_scr[...] = jnp.zeros_like(acc_scr)
        kcar_scr[...] = jnp.zeros_like(kcar_scr)
        vcar_scr[...] = jnp.zeros_like(vcar_scr)
        kcar_scr[:, 0:t_new] = kn_ref[...]
        vcar_scr[:, 0:t_new] = vn_ref[...]
        tq2 = _mod_pow2(lax.broadcasted_iota(jnp.int32, (rows, LANES), 0), t_new)
        tk2 = lax.broadcasted_iota(jnp.int32, (rows, LANES), 1)
        d2 = tq2 - tk2
        accumulate(kcar_scr[...], vcar_scr[...],
                   (tk2 < t_new) & (d2 >= 0) & (_mod_pow2(d2, dil) == 0))

    lane = lax.broadcasted_iota(jnp.int32, (width, LANES), 1)
    keep = lane < LANES - t_new
    nblk = chunk // LANES
    for src_ref, car_ref, dst_ref in ((kc_ref, kcar_scr, ko_ref), (vc_ref, vcar_scr, vo_ref)):
        nxt = pltpu.roll(car_ref[...], LANES - t_new, 1)
        for b in reversed(range(nblk)):
            cur = pltpu.roll(src_ref[:, b * LANES:(b + 1) * LANES], LANES - t_new, 1)
            dst_ref[:, b * LANES:(b + 1) * LANES] = jnp.where(keep, cur, nxt)
            nxt = cur
        car_ref[...] = src_ref[:, 0:LANES]

    tq = _mod_pow2(lax.broadcasted_iota(jnp.int32, (rows, chunk), 0), t_new)
    ik = lax.broadcasted_iota(jnp.int32, (rows, chunk), 1) + c * chunk
    dist = lb + tq - ik
    accumulate(kc_ref[...], vc_ref[...], (_mod_pow2(dist, dil) == 0) & (dist <= lb))

    @pl.when(c == 0)
    def _():
        full = jnp.where(own, acc_scr[...] / l_scr[...], 0.0)
        o_ref[...] = jnp.sum(full.reshape(A_HEADS, t_new, width), axis=0).astype(o_ref.dtype)
        lse_col = m_scr[...] + jnp.log(l_scr[...])
        rl = lax.broadcasted_iota(jnp.int32, (rows, LANES), 0)
        ll = lax.broadcasted_iota(jnp.int32, (rows, LANES), 1)
        spread = jnp.where(ll == _div_pow2(rl, t_new), lse_col, 0.0)
        lse_ref[...] = jnp.sum(spread.reshape(A_HEADS, t_new, LANES), axis=0)


def _attn_sample_call(proj, gi, win, dil, k_cache_t, v_cache_t):
    db, t_new, n = proj.shape
    width = A_HEADS * A_HEAD_DIM
    lb = k_cache_t.shape[2]
    assert lb == win and t_new == SUBLANES
    chunk = min(lb, 512)
    nchunk = lb // chunk
    kern = functools.partial(_attn_sample_kernel, lb=lb, dil=dil)
    kcol = slice((3 * gi + 1) * width, (3 * gi + 2) * width)
    vcol = slice((3 * gi + 2) * width, (3 * gi + 3) * width)
    kn_t = jnp.swapaxes(proj[:, :, kcol], 1, 2)
    vn_t = jnp.swapaxes(proj[:, :, vcol], 1, 2)
    new_spec = pl.BlockSpec((None, width, t_new), lambda b, s: (b, 0, 0))
    cur = pl.BlockSpec((None, width, chunk), lambda b, s: (b, 0, nchunk - 1 - s))
    return pl.pallas_call(
        kern,
        grid=(db, nchunk),
        in_specs=[pl.BlockSpec((None, t_new, width), lambda b, s: (b, 0, 3 * gi)),
                  new_spec, new_spec, cur, cur],
        out_specs=[cur, cur,
                   pl.BlockSpec((None, t_new, width), lambda b, s: (b, 0, 0)),
                   pl.BlockSpec((None, t_new, LANES), lambda b, s: (b, 0, 0))],
        out_shape=[jax.ShapeDtypeStruct((db, width, lb), F32),
                   jax.ShapeDtypeStruct((db, width, lb), F32),
                   jax.ShapeDtypeStruct((db, t_new, width), BF16),
                   jax.ShapeDtypeStruct((db, t_new, LANES), F32)],
        scratch_shapes=[pltpu.VMEM((A_HEADS * t_new, 1), F32),
                        pltpu.VMEM((A_HEADS * t_new, 1), F32),
                        pltpu.VMEM((A_HEADS * t_new, width), F32),
                        pltpu.VMEM((width, LANES), F32),
                        pltpu.VMEM((width, LANES), F32)],
        compiler_params=_params("arbitrary", "arbitrary"),
        name="dilated_attn_sample",
    )(proj, kn_t, vn_t, k_cache_t, v_cache_t)


def _delta_kernel(p_ref, prev_ref, cstate_ref, cw_ref, alog_ref, dtb_ref, ng_ref, s0_ref,
                  o_ref, ssm_ref, cnew_ref, s_scr, ext_scr, *, t_valid):
    cl = B_CHUNK
    rows_in = p_ref.shape[0]
    conv_dim = cw_ref.shape[1]
    c = pl.program_id(1)
    last = c == pl.num_programs(1) - 1
    qk_width = B_QK_HEADS * B_HEAD

    @pl.when(c == 0)
    def _():
        s_scr[...] = s0_ref[...]

    ext_scr[0:SUBLANES, :] = jnp.where(c == 0, cstate_ref[...], prev_ref[:, 0:conv_dim])
    ext_scr[SUBLANES:SUBLANES + rows_in, :] = p_ref[:, 0:conv_dim]
    if rows_in < cl:
        ext_scr[SUBLANES + rows_in:SUBLANES + cl, :] = jnp.zeros((cl - rows_in, conv_dim), F32)
    width = cw_ref.shape[0]
    ext = ext_scr[...]
    y = cw_ref[width - 1:width, :] * ext[SUBLANES:SUBLANES + cl]
    for back in range(1, width):
        shifted = pltpu.roll(ext, back, 0)[SUBLANES:SUBLANES + cl]
        y = y + cw_ref[width - 1 - back:width - back, :] * shifted
    qkv = _silu(y)
    if t_valid < cl:
        rmask = lax.broadcasted_iota(jnp.int32, (cl, 1), 0) < t_valid
        qkv = jnp.where(rmask, qkv, 0.0)

    @pl.when(last)
    def _():
        cnew_ref[...] = ext_scr[t_valid:t_valid + SUBLANES, :]

    graw = p_ref[:, conv_dim + B_V_HEADS * B_HEAD:conv_dim + B_V_HEADS * B_HEAD + LANES]
    if rows_in < cl:
        graw = jnp.concatenate([graw, jnp.zeros((cl - rows_in, LANES), F32)], axis=0)
    beta_all = jax.nn.sigmoid(graw)
    xg = graw + dtb_ref[...]
    softplus = jnp.maximum(xg, 0.0) + jnp.log1p(jnp.exp(-jnp.abs(xg)))
    g_all = -jnp.exp(alog_ref[...]) * softplus
    if t_valid < cl:
        beta_all = jnp.where(rmask, beta_all, 0.0)
        g_all = jnp.where(rmask, g_all, 0.0)

    ri = lax.broadcasted_iota(jnp.int32, (2 * cl, cl), 0)
    ci = lax.broadcasted_iota(jnp.int32, (2 * cl, cl), 1)
    summat = jnp.where((ci <= ri) | (ri >= cl), 1.0, 0.0).astype(BF16)
    sums = _dot(summat, jnp.concatenate(_split3(g_all), axis=1))
    sums = sums[:, 0:LANES] + sums[:, LANES:2 * LANES] + sums[:, 2 * LANES:3 * LANES]
    cg_all, gl_all = sums[0:cl], sums[cl:2 * cl]

    def l2n(a):
        return a * lax.rsqrt(jnp.sum(a * a, axis=1, keepdims=True) + NORM_EPS)

    nst = B_STACK * cl
    rr = lax.broadcasted_iota(jnp.int32, (nst, nst), 0)
    cc = lax.broadcasted_iota(jnp.int32, (nst, nst), 1)
    same = _div_pow2(rr, cl) == _div_pow2(cc, cl)
    incl = same & (cc <= rr)
    strict = same & (cc < rr)
    lane6 = lax.broadcasted_iota(jnp.int32, (nst, LANES), 1)
    rblk = _div_pow2(lax.broadcasted_iota(jnp.int32, (nst, B_HEAD), 0), cl)

    def col_stack(arr, lane0, heads):
        return jnp.concatenate([arr[:, lane0 + h:lane0 + h + 1] for h in heads], axis=0)

    groups = [list(range(g * B_STACK, (g + 1) * B_STACK)) for g in range(B_V_HEADS // B_STACK)]
    every = range(len(groups))
    rep = B_V_HEADS // B_QK_HEADS
    qn = [l2n(qkv[:, hq * B_HEAD:(hq + 1) * B_HEAD]) * (B_HEAD ** -0.5) for hq in range(B_QK_HEADS)]
    kn = [l2n(qkv[:, qk_width + hq * B_HEAD:qk_width + (hq + 1) * B_HEAD]) for hq in range(B_QK_HEADS)]
    q_st = [jnp.concatenate([qn[h // rep] for h in hs], axis=0) for hs in groups]
    k_st = [jnp.concatenate([kn[h // rep] for h in hs], axis=0) for hs in groups]
    v_st = [jnp.concatenate([qkv[:, 2 * qk_width + h * B_HEAD:2 * qk_width + (h + 1) * B_HEAD]
                             for h in hs], axis=0) for hs in groups]
    beta = [col_stack(beta_all, 0, hs) for hs in groups]
    cg = [col_stack(cg_all, B_V_HEADS, hs) for hs in groups]
    gl = [col_stack(gl_all, B_V_HEADS, hs) for hs in groups]

    def diff_operands(c):
        c_hi, c_mid, c_lo = (p.astype(F32) for p in _split3(c))
        lmat = jnp.where(lane6 == 0, c_hi, jnp.where(lane6 == 1, c_mid, jnp.where(
            lane6 == 2, c_lo, jnp.where(lane6 < 6, 1.0, 0.0))))
        rmat = jnp.where(lane6 == 3, -c_hi, jnp.where(lane6 == 4, -c_mid, jnp.where(
            lane6 == 5, -c_lo, jnp.where(lane6 < 3, 1.0, 0.0))))
        return lmat.astype(BF16), rmat.astype(BF16)

    diff = [_dot_nt(*diff_operands(cg[g])) for g in every]
    k_bf = [k_st[g].astype(BF16) for g in every]
    gram = [_dot_nt(k_bf[g], k_bf[g]) for g in every]
    qk = [_dot_nt(q_st[g].astype(BF16), k_bf[g]) for g in every]
    decay = [jnp.where(incl, jnp.exp(jnp.where(incl, diff[g], 0.0)), 0.0) for g in every]
    a_mat = [jnp.where(strict, beta[g] * gram[g] * decay[g], 0.0) for g in every]
    p_mat = [(qk[g] * decay[g]).astype(BF16) for g in every]

    blk = _div_pow2(rr, B_INV_BASE) == _div_pow2(cc, B_INV_BASE)
    apow = [jnp.where(blk, a_mat[g], 0.0) for g in every]
    nmat = [-apow[g] for g in every]
    for _ in range((B_INV_BASE - 1).bit_length() - 1):
        ab = [apow[g].astype(BF16) for g in every]
        apow = [_dot(ab[g], ab[g]) for g in every]
        nmat = [nmat[g] + apow[g] + _dot(nmat[g].astype(BF16), apow[g].astype(BF16)) for g in every]
    size = B_INV_BASE
    while size < cl:
        size *= 2
        merged = _div_pow2(rr, size) == _div_pow2(cc, size)
        join = merged & jnp.logical_not(blk)
        low = [jnp.where(join, a_mat[g], 0.0) for g in every]
        nb = [nmat[g].astype(BF16) for g in every]
        dl = [low[g] + _dot(nb[g], low[g].astype(BF16)) for g in every]
        nmat = [nmat[g] - (dl[g] + _dot(dl[g].astype(BF16), nb[g])) for g in every]
        blk = merged
    ecg = [jnp.exp(cg[g]) for g in every]
    rhs = [jnp.concatenate([beta[g] * v_st[g], (beta[g] * ecg[g]) * k_st[g]], axis=1) for g in every]
    sol = [rhs[g] + _dot(nmat[g].astype(BF16), rhs[g].astype(BF16)) for g in every]
    qg = [q_st[g] * ecg[g] for g in every]
    kdec_t = [(k_st[g] * jnp.exp(gl[g] - cg[g])).T.astype(BF16) for g in every]

    both = {}
    for g, hs in enumerate(groups):
        for idx, h in enumerate(hs):
            rsl = slice(idx * cl, (idx + 1) * cl)
            lhs = jnp.concatenate([sol[g][rsl, B_HEAD:2 * B_HEAD], qg[g][rsl]], axis=0)
            both[h] = _dot(lhs.astype(BF16), s_scr[h].astype(BF16))
    u_st = [jnp.concatenate([sol[g][idx * cl:(idx + 1) * cl, 0:B_HEAD] - both[h][0:cl]
                             for idx, h in enumerate(hs)], axis=0) for g, hs in enumerate(groups)]
    o_st = [jnp.concatenate([both[h][cl:2 * cl] for h in hs], axis=0)
            + _dot(p_mat[g], u_st[g].astype(BF16)) for g, hs in enumerate(groups)]
    for g, hs in enumerate(groups):
        for idx, h in enumerate(hs):
            u_own = jnp.where(rblk == idx, u_st[g], 0.0).astype(BF16)
            s_scr[h] = jnp.exp(gl[g][idx * cl:idx * cl + 1]) * s_scr[h] + _dot(kdec_t[g], u_own)
    for g, hs in enumerate(groups):
        for idx, h in enumerate(hs):
            o_h = o_st[g][idx * cl:(idx + 1) * cl]
            ms = jnp.mean(o_h * o_h, axis=1, keepdims=True)
            z_h = p_ref[:, conv_dim + h * B_HEAD:conv_dim + (h + 1) * B_HEAD]
            res = o_h[0:rows_in] * lax.rsqrt(ms[0:rows_in] + NORM_EPS) * ng_ref[...] * _silu(z_h)
            o_ref[:, h * B_HEAD:(h + 1) * B_HEAD] = res.astype(o_ref.dtype)

    @pl.when(last)
    def _():
        ssm_ref[...] = s_scr[...]


def _delta_call(proj, cstate8, conv_w, alog_l, dtb_l, norm_g, s0):
    bq, t, n = proj.shape
    conv_dim = conv_w.shape[1]
    v_width = B_V_HEADS * B_HEAD
    rows_in = min(t, B_CHUNK)
    assert t % rows_in == 0
    nchunk = t // rows_in
    per8 = rows_in // SUBLANES
    kern = functools.partial(_delta_kernel, t_valid=rows_in)
    return pl.pallas_call(
        kern,
        grid=(bq, nchunk),
        in_specs=[
            pl.BlockSpec((None, rows_in, n), lambda b, c: (b, c, 0)),
            pl.BlockSpec((None, SUBLANES, n), lambda b, c: (b, jnp.maximum(c * per8 - 1, 0), 0)),
            pl.BlockSpec((None, SUBLANES, conv_dim), lambda b, c: (b, 0, 0)),
            pl.BlockSpec(conv_w.shape, lambda b, c: (0, 0)),
            pl.BlockSpec((1, LANES), lambda b, c: (0, 0)),
            pl.BlockSpec((1, LANES), lambda b, c: (0, 0)),
            pl.BlockSpec((1, B_HEAD), lambda b, c: (0, 0)),
            pl.BlockSpec((None, B_V_HEADS, B_HEAD, B_HEAD), lambda b, c: (b, 0, 0, 0)),
        ],
        out_specs=[
            pl.BlockSpec((None, rows_in, v_width), lambda b, c: (b, c, 0)),
            pl.BlockSpec((None, B_V_HEADS, B_HEAD, B_HEAD), lambda b, c: (b, 0, 0, 0)),
            pl.BlockSpec((None, SUBLANES, conv_dim), lambda b, c: (b, 0, 0)),
        ],
        out_shape=[jax.ShapeDtypeStruct((bq, t, v_width), BF16),
                   jax.ShapeDtypeStruct((bq, B_V_HEADS, B_HEAD, B_HEAD), F32),
                   jax.ShapeDtypeStruct((bq, SUBLANES, conv_dim), F32)],
        scratch_shapes=[pltpu.VMEM((B_V_HEADS, B_HEAD, B_HEAD), F32),
                        pltpu.VMEM((B_CHUNK + SUBLANES, conv_dim), F32)],
        compiler_params=_params("arbitrary", "arbitrary"),
        name="gated_deltanet",
    )(proj, proj, cstate8, conv_w, alog_l, dtb_l, norm_g, s0)


def _left_rows(state):
    return jnp.pad(state, ((0, 0), (SUBLANES - state.shape[1], 0), (0, 0)))


def _rope_tables(pos):
    half = A_HEAD_DIM // 2
    inv = ROPE_THETA ** (-jnp.arange(half, dtype=F32) / half)
    ang = pos.astype(F32)[:, None] * inv[None, :]
    cos, sin = jnp.cos(ang), jnp.sin(ang)
    reps = LANES // A_HEAD_DIM
    cos_t = jnp.tile(jnp.concatenate([cos, cos], axis=1), (1, reps))
    sin_t = jnp.tile(jnp.concatenate([-sin, sin], axis=1), (1, reps))
    return cos_t[None], sin_t[None]


def _trunk(x, mods, per_row, pos, a_caches, b_ssm, b_conv, f_conv, wts):
    seqs, t, d = x.shape
    (norm_mix_g, norm_ffn_g, norm_final_g, a_w_in, a_w_out, b_w_in, b_conv_w, b_a_log, b_dt_bias,
     b_norm_g, b_w_out, ffn_w_up, ffn_conv_w, ffn_conv_b, ffn_w_down, expand) = wts
    d_ff = ffn_w_down.shape[1]
    width = A_HEADS * A_HEAD_DIM

    def rows_view(a):
        return a.reshape(1, seqs * t, a.shape[2]) if per_row else a

    def seq_view(a):
        return a.reshape(seqs, t, a.shape[2])

    def mod_view(v):
        if per_row:
            return jnp.repeat(v, t, axis=0)[None]
        return v[:, None, :]

    cos_t, sin_t = _rope_tables(pos)
    prompt = a_caches is None
    if prompt:
        assert t % max(w for w, _ in A_GROUPS) == 0
        cos_t, sin_t = (jnp.concatenate([_residue_major(tab, dil) for _, dil in A_GROUPS])
                        for tab in (cos_t, sin_t))
        perms = _perm_matrices()
        unperms = jnp.swapaxes(perms, 1, 2)
    if per_row:
        cos_t, sin_t = jnp.tile(cos_t, (1, seqs, 1)), jnp.tile(sin_t, (1, seqs, 1))
    gfin = norm_final_g.reshape(1, d)
    xr = rows_view(x)
    new_a, new_ssm, new_bconv, new_fconv = [], None, None, []
    depth = norm_mix_g.shape[0]
    for layer in range(depth):
        sh1, sc1, g1, sh2, sc2, g2 = (mod_view(v) for v in mods[layer])
        gmix = norm_mix_g[layer].reshape(1, d)
        if layer % 2 == 0:
            outs, lses = [], []
            if prompt:
                tail_rows = max(w for w, _ in A_GROUPS)
                *qkv_groups, tail = _nm_rope_call(xr, gmix, sc1, sh1, a_w_in[0], cos_t, sin_t,
                                                  tail_rows, perms)
                for gi, (win, dil) in enumerate(A_GROUPS):
                    o_g, lse_g = _attn_prompt_call(qkv_groups[gi], win, dil)
                    outs.append(o_g.reshape(seqs, t, width))
                    lses.append(lse_g.reshape(seqs, t, LANES))
                    for kind in (1, 2):
                        col = slice((3 * gi + kind) * width, (3 * gi + kind + 1) * width)
                        keep = _natural_order(tail[:, :, col], dil)[:, tail_rows - win:]
                        new_a.append(keep.reshape(1, seqs, win, A_HEADS, A_HEAD_DIM))
                xr = _attn_merge_proj_call(outs, lses, expand, a_w_out[0], xr, g1, gfin, False,
                                           unperms)
            else:
                (tail,) = _nm_rope_call(xr, gmix, sc1, sh1, a_w_in[0], cos_t, sin_t, seqs * t)
                for gi, (win, dil) in enumerate(A_GROUPS):
                    as_t = lambda c: jnp.transpose(c[0], (0, 2, 3, 1)).reshape(seqs, width, c.shape[2])
                    k_new, v_new, o_g, lse_g = _attn_sample_call(
                        seq_view(tail), gi, win, dil, as_t(a_caches[2 * gi]), as_t(a_caches[2 * gi + 1]))
                    outs.append(rows_view(o_g))
                    lses.append(rows_view(lse_g))
                    for buf in (k_new, v_new):
                        buf = buf.reshape(seqs, A_HEADS, A_HEAD_DIM, buf.shape[2])
                        new_a.append(jnp.transpose(buf, (0, 3, 1, 2))[None])
                xr = _attn_merge_proj_call(outs, lses, expand, a_w_out[0], xr, g1, gfin, False)
        else:
            proj = _nm_plain_call(xr, gmix, sc1, sh1, b_w_in[0], 1280)
            conv0 = (jnp.zeros((seqs, b_conv_w.shape[1] - 1, b_conv_w.shape[2]), F32)
                     if b_conv is None else b_conv[0])
            s0 = jnp.zeros((seqs, B_V_HEADS, B_HEAD, B_HEAD), F32) if b_ssm is None else b_ssm[0]
            lane_pad = lambda v: jnp.pad(v.reshape(1, -1), ((0, 0), (B_V_HEADS, LANES - 2 * B_V_HEADS)))
            o_b, ssm1, cnew = _delta_call(seq_view(proj), _left_rows(conv0), b_conv_w[0],
                                          lane_pad(b_a_log[0]), lane_pad(b_dt_bias[0]),
                                          b_norm_g[0].reshape(1, B_HEAD), s0)
            new_ssm = ssm1[None]
            new_bconv = cnew[None, :, SUBLANES - (b_conv_w.shape[1] - 1):, :]
            xr = _proj_res_call(rows_view(o_b), b_w_out[0], xr, g1, gfin, False)
        gate_pre, val = _nm_ffn_call(xr, norm_ffn_g[layer].reshape(1, d), sc2, sh2, ffn_w_up[layer])
        f0 = (jnp.zeros((seqs, ffn_conv_w.shape[1] - 1, d_ff), F32) if f_conv is None else f_conv[layer])
        gate_seq = seq_view(gate_pre)
        new_fconv.append(gate_seq[:, t - (ffn_conv_w.shape[1] - 1):, :])
        mg2 = mods[layer][5][:, None, :]
        x_seq = _ffn_down_call(gate_seq, _left_rows(f0), seq_view(val), ffn_conv_w[layer],
                               ffn_conv_b[layer].reshape(1, d_ff), ffn_w_down[layer],
                               seq_view(xr), mg2, gfin, layer == depth - 1)
        xr = rows_view(x_seq)
    return seq_view(xr), new_a, new_ssm, new_bconv, jnp.stack(new_fconv)


def kernel(x_prompt, x_sample, c_prompt, c_sample, cache_a_k_w128, cache_a_v_w128, cache_a_k_w512, cache_a_v_w512, cache_a_k_w2048, cache_a_v_w2048, state_b_ssm, state_b_conv, state_ffn_conv, norm_mix_g, norm_ffn_g, norm_final_g, w_mod, b_mod, a_w_in, a_w_out, b_w_in, b_conv_w, b_a_log, b_dt_bias, b_norm_g, b_w_out, ffn_w_up, ffn_conv_w, ffn_conv_b, ffn_w_down):
    bp, s, d = x_prompt.shape
    db, t_new, _ = x_sample.shape
    depth = w_mod.shape[0]

    c_all = jnp.concatenate([c_prompt, c_sample], axis=0)
    c_all = jnp.pad(c_all, ((0, (-c_all.shape[0]) % SUBLANES), (0, 0)))
    mod = _mod_call(c_all, w_mod, b_mod)

    def mods_of(lo, hi):
        return [[mod[l, lo:hi, k * d:(k + 1) * d] for k in range(6)] for l in range(depth)]

    b_in_cols = 6400
    b_w_in_p = jnp.pad(b_w_in, ((0, 0), (0, 0), (0, b_in_cols - b_w_in.shape[2])))
    heads = jnp.arange(A_HEADS * A_HEAD_DIM) // A_HEAD_DIM
    half = (jnp.arange(LANES)[:, None] == heads[None, :]).astype(BF16)
    expand = jnp.concatenate([half, half], axis=0)
    wts = (norm_mix_g, norm_ffn_g, norm_final_g, a_w_in.astype(BF16), a_w_out.astype(BF16),
           b_w_in_p.astype(BF16), b_conv_w, b_a_log, b_dt_bias, b_norm_g, b_w_out.astype(BF16),
           ffn_w_up.astype(BF16), ffn_conv_w, ffn_conv_b, ffn_w_down.astype(BF16), expand)

    pos_p = jnp.arange(s, dtype=jnp.int32)
    pos_s = PAST_LEN + jnp.arange(t_new, dtype=jnp.int32)
    y_p, a_p, ssm_p, bconv_p, fconv_p = _trunk(
        x_prompt, mods_of(0, bp), False, pos_p, None, None, None, None, wts)
    a_caches = [cache_a_k_w128, cache_a_v_w128, cache_a_k_w512, cache_a_v_w512,
                cache_a_k_w2048, cache_a_v_w2048]
    y_s, a_s, ssm_s, bconv_s, fconv_s = _trunk(
        x_sample, mods_of(bp, bp + db), True, pos_s, a_caches, state_b_ssm, state_b_conv,
        state_ffn_conv, wts)
    return (y_p, y_s,
            a_p[0], a_s[0], a_p[1], a_s[1], a_p[2], a_s[2], a_p[3], a_s[3], a_p[4], a_s[4],
            a_p[5], a_s[5], ssm_p, ssm_s, bconv_p, bconv_s, fconv_p, fconv_s)
```

```python
import functools

import jax
import jax.numpy as jnp
from jax import lax
from jax.experimental import pallas as pl
from jax.experimental.pallas import tpu as pltpu

F32 = jnp.float32
BF16 = jnp.bfloat16

PAST_LEN = 16384
A_GROUPS = ((128, 1), (512, 4), (2048, 16))
A_HEADS = 16
A_HEAD_DIM = 64
A_ROW_TILE = 512
ROPE_THETA = 10000.0
NORM_EPS = 1e-6
B_HEAD = 128
B_V_HEADS = 16
B_QK_HEADS = 8
B_CHUNK = 64
B_STACK = 2
B_INV_BASE = 16
NEG_BIG = -1e30

LANES = 128
SUBLANES = 8
VMEM_LIMIT_BYTES = 56 * 1024 * 1024


def _params(*sem):
    return pltpu.CompilerParams(dimension_semantics=sem, vmem_limit_bytes=VMEM_LIMIT_BYTES)


def _silu(x):
    return x * jax.nn.sigmoid(x)


def _dot(a, b):
    return jnp.dot(a, b, preferred_element_type=F32)


def _dot_nt(a, b):
    return lax.dot_general(a, b, (((1,), (1,)), ((), ())), preferred_element_type=F32)


def _shift_of(n):
    assert n > 0 and n & (n - 1) == 0, n
    return n.bit_length() - 1


def _div_pow2(v, n):
    return lax.shift_right_logical(v, _shift_of(n))


def _mod_pow2(v, n):
    assert n & (n - 1) == 0
    return v & (n - 1)


def _split3(x):
    hi = x.astype(BF16)
    r1 = x - hi.astype(F32)
    mid = r1.astype(BF16)
    lo = (r1 - mid.astype(F32)).astype(BF16)
    return hi, mid, lo


def _mod_kernel(c_ref, w_ref, b_ref, o_ref):
    cs = _silu(c_ref[...]).astype(BF16)
    o_ref[...] = _dot(cs, w_ref[...].astype(BF16)) + b_ref[...]


def _mod_call(c_all, w_mod, b_mod):
    depth, d, n = w_mod.shape
    rows = c_all.shape[0]
    tn = 1536
    return pl.pallas_call(
        _mod_kernel,
        grid=(depth, n // tn),
        in_specs=[
            pl.BlockSpec((rows, d), lambda l, j: (0, 0)),
            pl.BlockSpec((None, d, tn), lambda l, j: (l, 0, j)),
            pl.BlockSpec((None, 1, tn), lambda l, j: (l, 0, j)),
        ],
        out_specs=pl.BlockSpec((None, rows, tn), lambda l, j: (l, 0, j)),
        out_shape=jax.ShapeDtypeStruct((depth, rows, n), F32),
        compiler_params=_params("arbitrary", "arbitrary"),
        name="adaln_mod",
    )(c_all, w_mod, b_mod.reshape(depth, 1, n))


def _norm_mod(x, g, sc, sh):
    ms = jnp.mean(x * x, axis=-1, keepdims=True)
    return x * lax.rsqrt(ms + NORM_EPS) * g * (1.0 + sc) + sh


def _rope_tile(acc, cos, sin):
    lane = lax.broadcasted_iota(jnp.int32, (acc.shape[0], LANES), 1)
    first_half = _mod_pow2(lane, A_HEAD_DIM) < (A_HEAD_DIM // 2)
    outs = []
    for c in range(acc.shape[1] // LANES):
        a = acc[:, c * LANES:(c + 1) * LANES]
        swapped = jnp.where(first_half,
                            pltpu.roll(a, LANES - A_HEAD_DIM // 2, 1),
                            pltpu.roll(a, A_HEAD_DIM // 2, 1))
        outs.append(a * cos + swapped * sin)
    return jnp.concatenate(outs, axis=1)


def _nm_delta_in_kernel(x_ref, g_ref, sc_ref, sh_ref, w_ref, main_ref, gates_ref, tail_ref,
                        *, col_tile):
    h = _norm_mod(x_ref[...], g_ref[...], sc_ref[...], sh_ref[...]).astype(BF16)
    tm, n_main = main_ref.shape
    for c0 in range(0, n_main, col_tile):
        acc = _dot(h, w_ref[:, c0:c0 + col_tile])
        main_ref[:, c0:c0 + col_tile] = acc.astype(main_ref.dtype)
        if c0 < tail_ref.shape[1]:
            tail_ref[:, c0:c0 + col_tile] = acc[tm - SUBLANES:tm]
    gates_ref[...] = _dot(h, w_ref[:, n_main:n_main + LANES])


def _nm_rope_kernel(x_ref, g_ref, sc_ref, sh_ref, w_ref, cos_ref, sin_ref, *rest, grouped):
    if grouped:
        perm_ref, o_ref, tail_ref, h_scr = rest
    else:
        tail_ref, h_scr = rest
    group = pl.program_id(2)

    @pl.when(group == 0)
    def _():
        h = _norm_mod(x_ref[...], g_ref[...], sc_ref[...], sh_ref[...]).astype(BF16)
        h_scr[0] = h
        if grouped:
            for p in range(perm_ref.shape[0]):
                h_scr[p + 1] = _dot(perm_ref[p], h).astype(BF16)

    h = h_scr[group] if grouped else h_scr[0]
    width = A_HEADS * A_HEAD_DIM
    cos, sin = cos_ref[...], sin_ref[...]
    qscale = A_HEAD_DIM ** -0.5
    for kind in range(3):
        cols = slice(kind * width, (kind + 1) * width)
        val = _dot(h, w_ref[group, :, cols])
        if kind == 0:
            val = _rope_tile(val, cos * qscale, sin * qscale)
        elif kind == 1:
            val = _rope_tile(val, cos, sin)
        tail_ref[:, cols] = val
        if grouped:
            o_ref[kind] = val.astype(o_ref.dtype)


def _nm_ffn_kernel(x_ref, g_ref, sc_ref, sh_ref, w_ref, gate_ref, val_ref, *, col_tile):
    h = _norm_mod(x_ref[...], g_ref[...], sc_ref[...], sh_ref[...]).astype(BF16)
    d_ff = gate_ref.shape[1]
    for c0 in range(0, d_ff, col_tile):
        gate_ref[:, c0:c0 + col_tile] = _dot(h, w_ref[:, c0:c0 + col_tile])
        val_ref[:, c0:c0 + col_tile] = _dot(
            h, w_ref[:, d_ff + c0:d_ff + c0 + col_tile]).astype(val_ref.dtype)


def _row_tile(rows):
    return min(rows, 512)


def _resident(shape):
    zeros = (0,) * len(shape)
    return pl.BlockSpec(shape, lambda *_: zeros, pipeline_mode=pl.Buffered(1))


def _nm_common_specs(sc, tm, d):
    per_row = sc.shape[1] != 1
    mod_spec = (pl.BlockSpec((None, tm, d), lambda b, i, *_: (b, i, 0)) if per_row
                else pl.BlockSpec((None, 1, d), lambda b, i, *_: (b, 0, 0)))
    return [
        pl.BlockSpec((None, tm, d), lambda b, i, *_: (b, i, 0)),
        pl.BlockSpec((1, d), lambda b, i, *_: (0, 0)),
        mod_spec,
        mod_spec,
    ]


def _nm_delta_in_call(x, g, sc, sh, w, n_main, conv_dim, main_dtype):
    bq, rows, d = x.shape
    tm = _row_tile(rows)
    tiles = rows // tm
    kern = functools.partial(_nm_delta_in_kernel, col_tile=1024)
    return pl.pallas_call(
        kern,
        grid=(bq, tiles),
        in_specs=_nm_common_specs(sc, tm, d) + [_resident(w.shape)],
        out_specs=[pl.BlockSpec((None, tm, n_main), lambda b, i: (b, i, 0)),
                   pl.BlockSpec((None, tm, LANES), lambda b, i: (b, i, 0)),
                   pl.BlockSpec((None, None, SUBLANES, conv_dim), lambda b, i: (b, i, 0, 0))],
        out_shape=[jax.ShapeDtypeStruct((bq, rows, n_main), main_dtype),
                   jax.ShapeDtypeStruct((bq, rows, LANES), F32),
                   jax.ShapeDtypeStruct((bq, tiles, SUBLANES, conv_dim), F32)],
        compiler_params=_params("arbitrary", "arbitrary"),
        name="norm_mod_delta_in",
    )(x, g, sc, sh, w)


def _residue_major(a, dil, tm=A_ROW_TILE):
    lead, (rows, c) = a.shape[:-2], a.shape[-2:]
    a = a.reshape(lead + (rows // tm, tm // dil, dil, c))
    return jnp.swapaxes(a, -2, -3).reshape(lead + (rows, c))


def _natural_order(a, dil, tm=A_ROW_TILE):
    lead, (rows, c) = a.shape[:-2], a.shape[-2:]
    a = a.reshape(lead + (rows // tm, dil, tm // dil, c))
    return jnp.swapaxes(a, -2, -3).reshape(lead + (rows, c))


def _perm_matrices(tm=A_ROW_TILE):
    eye = jnp.eye(tm, dtype=BF16)
    return jnp.stack([_residue_major(eye, dil, tm) for _, dil in A_GROUPS if dil > 1])


def _nm_rope_call(x, g, sc, sh, w, cos, sin, tail_rows, perms=None):
    bq, rows, d = x.shape
    ngroups, _, gw = w.shape
    width = gw // 3
    grouped = perms is not None
    tm = A_ROW_TILE if grouped else _row_tile(min(rows, tail_rows))
    tiles = rows // tm
    tail_tile0 = (rows - tail_rows) // tm
    kern = functools.partial(_nm_rope_kernel, grouped=grouped)
    tab_spec = pl.BlockSpec((None, tm, LANES),
                            (lambda b, i, j: (j, i, 0)) if grouped else (lambda b, i, j: (0, i, 0)))

    def tail_map(b, i, j):
        return (b, jnp.maximum(i - tail_tile0, 0), jnp.where(i >= tail_tile0, j, 0))

    in_specs = _nm_common_specs(sc, tm, d) + [_resident(w.shape), tab_spec, tab_spec]
    out_specs, out_shape, args = [], [], [x, g, sc, sh, w, cos, sin]
    if grouped:
        in_specs.append(_resident(perms.shape))
        args.append(perms)
        out_specs.append(pl.BlockSpec((None, 3, None, tm, width), lambda b, i, j: (b, j, i, 0, 0)))
        out_shape.append(jax.ShapeDtypeStruct((bq, 3 * ngroups, tiles, tm, width), BF16))
    out_specs.append(pl.BlockSpec((None, tm, gw), tail_map))
    out_shape.append(jax.ShapeDtypeStruct((bq, tail_rows, ngroups * gw), F32))
    n_h = 1 + (perms.shape[0] if grouped else 0)
    return pl.pallas_call(
        kern,
        grid=(bq, tiles, ngroups),
        in_specs=in_specs,
        out_specs=out_specs,
        out_shape=out_shape,
        scratch_shapes=[pltpu.VMEM((n_h, tm, d), BF16)],
        compiler_params=_params("arbitrary", "arbitrary", "arbitrary"),
        name="norm_mod_qkv_rope",
    )(*args)


def _nm_ffn_call(x, g, sc, sh, w):
    bq, rows, d = x.shape
    d_ff = w.shape[1] // 2
    tm = _row_tile(rows)
    kern = functools.partial(_nm_ffn_kernel, col_tile=d_ff // 2)
    row_spec = pl.BlockSpec((None, tm, d_ff), lambda b, i: (b, i, 0))
    return pl.pallas_call(
        kern,
        grid=(bq, rows // tm),
        in_specs=_nm_common_specs(sc, tm, d) + [_resident(w.shape)],
        out_specs=[row_spec, row_spec],
        out_shape=[jax.ShapeDtypeStruct((bq, rows, d_ff), F32),
                   jax.ShapeDtypeStruct((bq, rows, d_ff), BF16)],
        compiler_params=_params("arbitrary", "arbitrary"),
        name="norm_mod_ffn_up",
    )(x, g, sc, sh, w)


def _finish(y, x_ref, gate_ref, gfin_ref, o_ref, final_norm):
    xn = x_ref[...] + gate_ref[...] * y
    if final_norm:
        ms = jnp.mean(xn * xn, axis=-1, keepdims=True)
        xn = xn * lax.rsqrt(ms + NORM_EPS) * gfin_ref[...]
    o_ref[...] = xn


def _proj_res_kernel(a_ref, w_ref, x_ref, gate_ref, gfin_ref, o_ref, *, final_norm):
    _finish(_dot(a_ref[...], w_ref[...]), x_ref, gate_ref, gfin_ref, o_ref, final_norm)


def _attn_merge_proj_kernel(o0_ref, o1_ref, o2_ref, l0_ref, l1_ref, l2_ref, e_ref, w_ref,
                            x_ref, gate_ref, gfin_ref, *rest, final_norm, grouped):
    if grouped:
        unperm_ref, o_ref = rest
    else:
        (o_ref,) = rest
    outs, lses = [], []
    slot = 0
    for (_, dil), og_ref, lg_ref in zip(A_GROUPS, (o0_ref, o1_ref, o2_ref), (l0_ref, l1_ref, l2_ref)):
        if grouped and dil > 1:
            pt = unperm_ref[slot]
            slot += 1
            outs.append(_dot(pt, og_ref[...]))
            parts = _dot(pt, jnp.concatenate(_split3(lg_ref[...]), axis=1))
            lses.append(parts[:, 0:LANES] + parts[:, LANES:2 * LANES] + parts[:, 2 * LANES:3 * LANES])
        else:
            outs.append(og_ref[...].astype(F32))
            lses.append(lg_ref[...])
    m = jnp.maximum(jnp.maximum(lses[0], lses[1]), lses[2])
    exps = [jnp.exp(l - m) for l in lses]
    inv = 1.0 / (exps[0] + exps[1] + exps[2])
    merged = None
    for e, og in zip(exps, outs):
        wgt = e * inv
        hi = wgt.astype(BF16)
        lo = (wgt - hi.astype(F32)).astype(BF16)
        term = _dot(jnp.concatenate([hi, lo], axis=1), e_ref[...]) * og
        merged = term if merged is None else merged + term
    _finish(_dot(merged.astype(BF16), w_ref[...]), x_ref, gate_ref, gfin_ref, o_ref, final_norm)


def _ffn_down_kernel(gate_ref, prev_ref, state_ref, val_ref, cw_ref, cb_ref, w_ref,
                     x_ref, mgate_ref, gfin_ref, o_ref, ext_scr, *, final_norm):
    tm = gate_ref.shape[0]
    first = pl.program_id(1) == 0
    ext_scr[0:SUBLANES, :] = jnp.where(first, state_ref[...], prev_ref[...])
    ext_scr[SUBLANES:SUBLANES + tm, :] = gate_ref[...]
    width = cw_ref.shape[0]
    y = cb_ref[...]
    for t in range(width):
        off = SUBLANES - (width - 1) + t
        y = y + cw_ref[t:t + 1, :] * ext_scr[off:off + tm, :]
    act = (_silu(y) * val_ref[...].astype(F32)).astype(BF16)
    _finish(_dot(act, w_ref[...]), x_ref, mgate_ref, gfin_ref, o_ref, final_norm)


def _res_specs(tm, d, gate):
    per_row = gate.shape[1] != 1
    gate_spec = (pl.BlockSpec((None, tm, d), lambda b, i: (b, i, 0)) if per_row
                 else pl.BlockSpec((None, 1, d), lambda b, i: (b, 0, 0)))
    return [pl.BlockSpec((None, tm, d), lambda b, i: (b, i, 0)),
            gate_spec,
            pl.BlockSpec((1, d), lambda b, i: (0, 0))]


def _proj_res_call(a, w, x, gate, gfin, final_norm):
    bq, rows, d = x.shape
    k = a.shape[2]
    tm = _row_tile(rows)
    kern = functools.partial(_proj_res_kernel, final_norm=final_norm)
    return pl.pallas_call(
        kern,
        grid=(bq, rows // tm),
        in_specs=[pl.BlockSpec((None, tm, k), lambda b, i: (b, i, 0)),
                  pl.BlockSpec((k, d), lambda b, i: (0, 0))] + _res_specs(tm, d, gate),
        out_specs=pl.BlockSpec((None, tm, d), lambda b, i: (b, i, 0)),
        out_shape=jax.ShapeDtypeStruct((bq, rows, d), F32),
        compiler_params=_params("arbitrary", "arbitrary"),
        name="proj_residual",
    )(a, w, x, gate, gfin)


def _attn_merge_proj_call(outs, lses, expand, w, x, gate, gfin, final_norm, unperms=None):
    bq, rows, d = x.shape
    grouped = unperms is not None
    tm = A_ROW_TILE if grouped else _row_tile(rows)
    kern = functools.partial(_attn_merge_proj_kernel, final_norm=final_norm, grouped=grouped)
    o_spec = pl.BlockSpec((None, tm, d), lambda b, i: (b, i, 0))
    l_spec = pl.BlockSpec((None, tm, LANES), lambda b, i: (b, i, 0))
    extra_specs, extra = [], []
    if grouped:
        extra_specs.append(pl.BlockSpec(unperms.shape, lambda b, i: (0, 0, 0)))
        extra.append(unperms)
    return pl.pallas_call(
        kern,
        grid=(bq, rows // tm),
        in_specs=[o_spec] * 3 + [l_spec] * 3 + [
            pl.BlockSpec(expand.shape, lambda b, i: (0, 0)),
            pl.BlockSpec(w.shape, lambda b, i: (0, 0))] + _res_specs(tm, d, gate) + extra_specs,
        out_specs=pl.BlockSpec((None, tm, d), lambda b, i: (b, i, 0)),
        out_shape=jax.ShapeDtypeStruct((bq, rows, d), F32),
        compiler_params=_params("arbitrary", "arbitrary"),
        name="attn_merge_proj_residual",
    )(*outs, *lses, expand, w, x, gate, gfin, *extra)


def _ffn_down_call(gate_pre, state8, val, conv_w, conv_b, w, x, mgate, gfin, final_norm):
    bq, rows, d = x.shape
    f = gate_pre.shape[2]
    tm = min(rows, 256)
    kern = functools.partial(_ffn_down_kernel, final_norm=final_norm)
    blocks_per_tile = tm // SUBLANES
    return pl.pallas_call(
        kern,
        grid=(bq, rows // tm),
        in_specs=[
            pl.BlockSpec((None, tm, f), lambda b, i: (b, i, 0)),
            pl.BlockSpec((None, SUBLANES, f),
                         lambda b, i: (b, jnp.maximum(i * blocks_per_tile - 1, 0), 0)),
            pl.BlockSpec((None, SUBLANES, f), lambda b, i: (b, 0, 0)),
            pl.BlockSpec((None, tm, f), lambda b, i: (b, i, 0)),
            pl.BlockSpec(conv_w.shape, lambda b, i: (0, 0)),
            pl.BlockSpec((1, f), lambda b, i: (0, 0)),
            pl.BlockSpec((f, d), lambda b, i: (0, 0)),
        ] + _res_specs(tm, d, mgate),
        out_specs=pl.BlockSpec((None, tm, d), lambda b, i: (b, i, 0)),
        out_shape=jax.ShapeDtypeStruct((bq, rows, d), F32),
        scratch_shapes=[pltpu.VMEM((tm + SUBLANES, f), F32)],
        compiler_params=_params("arbitrary", "arbitrary"),
        name="ffn_conv_down_residual",
    )(gate_pre, gate_pre, state8, val, conv_w, conv_b, w, x, mgate, gfin)


def _attn_prompt_kernel(q_ref, kp_ref, kc_ref, vp_ref, vc_ref, o_ref, lse_ref):
    pieces, prow = q_ref.shape[0], q_ref.shape[1]
    blk = pieces * prow

    def rows_of(ref, cols):
        return jnp.concatenate([ref[p, :, cols] for p in range(pieces)], axis=0)

    n = pl.program_id(2)
    qi = lax.broadcasted_iota(jnp.int32, (blk, 2 * blk), 0)
    kj = lax.broadcasted_iota(jnp.int32, (blk, 2 * blk), 1)
    valid = (kj >= qi) & (kj <= qi + blk) & ((kj >= blk) | (n > 0))
    lane = lax.broadcasted_iota(jnp.int32, (blk, LANES), 1)
    low = lane < A_HEAD_DIM
    lse_all = jnp.zeros((blk, LANES), F32)
    for hp in range(A_HEADS // 2):
        cols = slice(hp * LANES, (hp + 1) * LANES)
        q2 = rows_of(q_ref, cols)
        k2 = jnp.concatenate([rows_of(kp_ref, cols), rows_of(kc_ref, cols)], axis=0)
        v2 = jnp.concatenate([rows_of(vp_ref, cols), rows_of(vc_ref, cols)], axis=0)
        pair = []
        for s in range(2):
            mine = low if s == 0 else jnp.logical_not(low)
            qh = jnp.where(mine, q2, jnp.zeros_like(q2))
            sc = jnp.where(valid, _dot_nt(qh, k2), NEG_BIG)
            m = jnp.max(sc, axis=1, keepdims=True)
            p = jnp.exp(sc - m)
            l = jnp.sum(p, axis=1, keepdims=True)
            pair.append(_dot(p.astype(BF16), v2) / l)
            lse_all = jnp.where(lane == 2 * hp + s, m + jnp.log(l), lse_all)
        o2 = jnp.where(low, pair[0], pair[1]).astype(o_ref.dtype)
        for p in range(pieces):
            o_ref[p, :, cols] = o2[p * prow:(p + 1) * prow]
    for p in range(pieces):
        lse_ref[p] = lse_all[p * prow:(p + 1) * prow]


def _attn_prompt_call(qkv, gi, win, dil):
    b, nkinds, tiles, tm, width = qkv.shape
    per_res = tm // dil
    qkv_g = qkv.reshape(b, nkinds, tiles, dil, per_res, width)
    blk = win // dil
    assert (tiles * tm) % win == 0
    prow = min(blk, per_res)
    pieces = blk // prow
    run_per_tile = per_res // prow
    nb = tiles * per_res // blk

    def where(kind, shift):
        def index(bb, r, i):
            i = jnp.maximum(i - shift, 0)
            return (bb, 3 * gi + kind, lax.div(i, run_per_tile), r, lax.rem(i, run_per_tile), 0)
        return index

    spec = lambda kind, shift: pl.BlockSpec((None, None, pieces, None, prow, width), where(kind, shift))
    out_index = lambda bb, r, i: (bb, lax.div(i, run_per_tile), r, lax.rem(i, run_per_tile), 0)
    return pl.pallas_call(
        _attn_prompt_kernel,
        grid=(b, dil, nb),
        in_specs=[spec(0, 0), spec(1, 1), spec(1, 0), spec(2, 1), spec(2, 0)],
        out_specs=[pl.BlockSpec((None, pieces, None, prow, width), out_index),
                   pl.BlockSpec((None, pieces, None, prow, LANES), out_index)],
        out_shape=[jax.ShapeDtypeStruct((b, tiles, dil, per_res, width), BF16),
                   jax.ShapeDtypeStruct((b, tiles, dil, per_res, LANES), F32)],
        compiler_params=_params("arbitrary", "arbitrary", "arbitrary"),
        name="dilated_attn_prompt",
    )(qkv_g, qkv_g, qkv_g, qkv_g, qkv_g)


def _attn_sample_kernel(q_ref, kn_ref, vn_ref, kc_ref, vc_ref,
                        ko_ref, vo_ref, o_ref, lse_ref,
                        m_scr, l_scr, acc_scr, kcar_scr, vcar_scr, *, lb, dil):
    t_new = q_ref.shape[0]
    width = q_ref.shape[1]
    chunk = kc_ref.shape[1]
    rows = A_HEADS * t_new
    step = pl.program_id(1)
    c = pl.num_programs(1) - 1 - step

    rq = lax.broadcasted_iota(jnp.int32, (rows, width), 0)
    lq = lax.broadcasted_iota(jnp.int32, (rows, width), 1)
    own = _div_pow2(lq, A_HEAD_DIM) == _div_pow2(rq, t_new)
    qbd = jnp.where(own, jnp.concatenate([q_ref[...]] * A_HEADS, axis=0), 0.0).astype(BF16)

    def accumulate(keys_t, vals_t, valid):
        sc = jnp.where(valid, _dot(qbd, keys_t.astype(BF16)), NEG_BIG)
        m_old = m_scr[...]
        m_new = jnp.maximum(m_old, jnp.max(sc, axis=1, keepdims=True))
        alpha = jnp.exp(m_old - m_new)
        p = jnp.where(valid, jnp.exp(sc - m_new), 0.0)
        l_scr[...] = alpha * l_scr[...] + jnp.sum(p, axis=1, keepdims=True)
        acc_scr[...] = alpha * acc_scr[...] + _dot_nt(p.astype(BF16), vals_t.astype(BF16))
        m_scr[...] = m_new

    @pl.when(step == 0)
    def _():
        m_scr[...] = jnp.full_like(m_scr, NEG_BIG)
        l_scr[...] = jnp.zeros_like(l_scr)
        acc_scr[...] = jnp.zeros_like(acc_scr)
        kcar_scr[...] = jnp.zeros_like(kcar_scr)
        vcar_scr[...] = jnp.zeros_like(vcar_scr)
        kcar_scr[:, 0:t_new] = kn_ref[...]
        vcar_scr[:, 0:t_new] = vn_ref[...]
        tq2 = _mod_pow2(lax.broadcasted_iota(jnp.int32, (rows, LANES), 0), t_new)
        tk2 = lax.broadcasted_iota(jnp.int32, (rows, LANES), 1)
        d2 = tq2 - tk2
        accumulate(kcar_scr[...], vcar_scr[...],
                   (tk2 < t_new) & (d2 >= 0) & (_mod_pow2(d2, dil) == 0))

    lane = lax.broadcasted_iota(jnp.int32, (width, LANES), 1)
    keep = lane < LANES - t_new
    nblk = chunk // LANES
    for src_ref, car_ref, dst_ref in ((kc_ref, kcar_scr, ko_ref), (vc_ref, vcar_scr, vo_ref)):
        nxt = pltpu.roll(car_ref[...], LANES - t_new, 1)
        for b in reversed(range(nblk)):
            cur = pltpu.roll(src_ref[:, b * LANES:(b + 1) * LANES], LANES - t_new, 1)
            dst_ref[:, b * LANES:(b + 1) * LANES] = jnp.where(keep, cur, nxt)
            nxt = cur
        car_ref[...] = src_ref[:, 0:LANES]

    tq = _mod_pow2(lax.broadcasted_iota(jnp.int32, (rows, chunk), 0), t_new)
    ik = lax.broadcasted_iota(jnp.int32, (rows, chunk), 1) + c * chunk
    dist = lb + tq - ik
    accumulate(kc_ref[...], vc_ref[...], (_mod_pow2(dist, dil) == 0) & (dist <= lb))

    @pl.when(c == 0)
    def _():
        full = jnp.where(own, acc_scr[...] / l_scr[...], 0.0)
        o_ref[...] = jnp.sum(full.reshape(A_HEADS, t_new, width), axis=0).astype(o_ref.dtype)
        lse_col = m_scr[...] + jnp.log(l_scr[...])
        rl = lax.broadcasted_iota(jnp.int32, (rows, LANES), 0)
        ll = lax.broadcasted_iota(jnp.int32, (rows, LANES), 1)
        spread = jnp.where(ll == _div_pow2(rl, t_new), lse_col, 0.0)
        lse_ref[...] = jnp.sum(spread.reshape(A_HEADS, t_new, LANES), axis=0)


def _attn_sample_call(proj, gi, win, dil, k_cache_t, v_cache_t):
    db, t_new, n = proj.shape
    width = A_HEADS * A_HEAD_DIM
    lb = k_cache_t.shape[2]
    assert lb == win and t_new == SUBLANES
    chunk = min(lb, 512)
    nchunk = lb // chunk
    kern = functools.partial(_attn_sample_kernel, lb=lb, dil=dil)
    kcol = slice((3 * gi + 1) * width, (3 * gi + 2) * width)
    vcol = slice((3 * gi + 2) * width, (3 * gi + 3) * width)
    kn_t = jnp.swapaxes(proj[:, :, kcol], 1, 2)
    vn_t = jnp.swapaxes(proj[:, :, vcol], 1, 2)
    new_spec = pl.BlockSpec((None, width, t_new), lambda b, s: (b, 0, 0))
    cur = pl.BlockSpec((None, width, chunk), lambda b, s: (b, 0, nchunk - 1 - s))
    return pl.pallas_call(
        kern,
        grid=(db, nchunk),
        in_specs=[pl.BlockSpec((None, t_new, width), lambda b, s: (b, 0, 3 * gi)),
                  new_spec, new_spec, cur, cur],
        out_specs=[cur, cur,
                   pl.BlockSpec((None, t_new, width), lambda b, s: (b, 0, 0)),
                   pl.BlockSpec((None, t_new, LANES), lambda b, s: (b, 0, 0))],
        out_shape=[jax.ShapeDtypeStruct((db, width, lb), F32),
                   jax.ShapeDtypeStruct((db, width, lb), F32),
                   jax.ShapeDtypeStruct((db, t_new, width), BF16),
                   jax.ShapeDtypeStruct((db, t_new, LANES), F32)],
        scratch_shapes=[pltpu.VMEM((A_HEADS * t_new, 1), F32),
                        pltpu.VMEM((A_HEADS * t_new, 1), F32),
                        pltpu.VMEM((A_HEADS * t_new, width), F32),
                        pltpu.VMEM((width, LANES), F32),
                        pltpu.VMEM((width, LANES), F32)],
        compiler_params=_params("arbitrary", "arbitrary"),
        name="dilated_attn_sample",
    )(proj, kn_t, vn_t, k_cache_t, v_cache_t)


def _delta_kernel(p_ref, prev_ref, gates_ref, cstate_ref, cw_ref, alog_ref, dtb_ref, ng_ref, s0_ref,
                  o_ref, ssm_ref, cnew_ref, s_scr, ext_scr, *, t_valid):
    cl = B_CHUNK
    rows_in = p_ref.shape[0]
    conv_dim = cw_ref.shape[1]
    c = pl.program_id(1)
    last = c == pl.num_programs(1) - 1
    qk_width = B_QK_HEADS * B_HEAD

    @pl.when(c == 0)
    def _():
        s_scr[...] = s0_ref[...]

    halo = prev_ref.shape[0]
    left = prev_ref[:, 0:conv_dim].astype(F32)
    if halo > SUBLANES:
        ext_scr[0:halo - SUBLANES, :] = left[0:halo - SUBLANES]
    ext_scr[halo - SUBLANES:halo, :] = jnp.where(c == 0, cstate_ref[...], left[halo - SUBLANES:halo])
    ext_scr[halo:halo + rows_in, :] = p_ref[:, 0:conv_dim].astype(F32)
    if rows_in < cl:
        ext_scr[halo + rows_in:halo + cl, :] = jnp.zeros((cl - rows_in, conv_dim), F32)
    width = cw_ref.shape[0]
    ext = ext_scr[...]
    y = cw_ref[width - 1:width, :] * ext[halo:halo + cl]
    for back in range(1, width):
        shifted = pltpu.roll(ext, back, 0)[halo:halo + cl]
        y = y + cw_ref[width - 1 - back:width - back, :] * shifted
    qkv = _silu(y)
    if t_valid < cl:
        rmask = lax.broadcasted_iota(jnp.int32, (cl, 1), 0) < t_valid
        qkv = jnp.where(rmask, qkv, 0.0)

    @pl.when(last)
    def _():
        cnew_ref[...] = ext_scr[halo + t_valid - SUBLANES:halo + t_valid, :]

    graw = gates_ref[...]
    if rows_in < cl:
        graw = jnp.concatenate([graw, jnp.zeros((cl - rows_in, LANES), F32)], axis=0)
    beta_all = jax.nn.sigmoid(graw)
    xg = graw + dtb_ref[...]
    softplus = jnp.maximum(xg, 0.0) + jnp.log1p(jnp.exp(-jnp.abs(xg)))
    g_all = -jnp.exp(alog_ref[...]) * softplus
    if t_valid < cl:
        beta_all = jnp.where(rmask, beta_all, 0.0)
        g_all = jnp.where(rmask, g_all, 0.0)

    ri = lax.broadcasted_iota(jnp.int32, (2 * cl, cl), 0)
    ci = lax.broadcasted_iota(jnp.int32, (2 * cl, cl), 1)
    summat = jnp.where((ci <= ri) | (ri >= cl), 1.0, 0.0).astype(BF16)
    sums = _dot(summat, jnp.concatenate(_split3(g_all), axis=1))
    sums = sums[:, 0:LANES] + sums[:, LANES:2 * LANES] + sums[:, 2 * LANES:3 * LANES]
    cg_all, gl_all = sums[0:cl], sums[cl:2 * cl]

    def l2n(a):
        return a * lax.rsqrt(jnp.sum(a * a, axis=1, keepdims=True) + NORM_EPS)

    nst = B_STACK * cl
    rr = lax.broadcasted_iota(jnp.int32, (nst, nst), 0)
    cc = lax.broadcasted_iota(jnp.int32, (nst, nst), 1)
    same = _div_pow2(rr, cl) == _div_pow2(cc, cl)
    incl = same & (cc <= rr)
    strict = same & (cc < rr)
    lane6 = lax.broadcasted_iota(jnp.int32, (nst, LANES), 1)
    rblk = _div_pow2(lax.broadcasted_iota(jnp.int32, (nst, B_HEAD), 0), cl)

    def col_stack(arr, lane0, heads):
        return jnp.concatenate([arr[:, lane0 + h:lane0 + h + 1] for h in heads], axis=0)

    groups = [list(range(g * B_STACK, (g + 1) * B_STACK)) for g in range(B_V_HEADS // B_STACK)]
    every = range(len(groups))
    rep = B_V_HEADS // B_QK_HEADS
    qn = [l2n(qkv[:, hq * B_HEAD:(hq + 1) * B_HEAD]) * (B_HEAD ** -0.5) for hq in range(B_QK_HEADS)]
    kn = [l2n(qkv[:, qk_width + hq * B_HEAD:qk_width + (hq + 1) * B_HEAD]) for hq in range(B_QK_HEADS)]
    q_st = [jnp.concatenate([qn[h // rep] for h in hs], axis=0) for hs in groups]
    k_st = [jnp.concatenate([kn[h // rep] for h in hs], axis=0) for hs in groups]
    v_st = [jnp.concatenate([qkv[:, 2 * qk_width + h * B_HEAD:2 * qk_width + (h + 1) * B_HEAD]
                             for h in hs], axis=0) for hs in groups]
    beta = [col_stack(beta_all, 0, hs) for hs in groups]
    cg = [col_stack(cg_all, B_V_HEADS, hs) for hs in groups]
    gl = [col_stack(gl_all, B_V_HEADS, hs) for hs in groups]

    def diff_operands(c):
        c_hi, c_mid, c_lo = (p.astype(F32) for p in _split3(c))
        lmat = jnp.where(lane6 == 0, c_hi, jnp.where(lane6 == 1, c_mid, jnp.where(
            lane6 == 2, c_lo, jnp.where(lane6 < 6, 1.0, 0.0))))
        rmat = jnp.where(lane6 == 3, -c_hi, jnp.where(lane6 == 4, -c_mid, jnp.where(
            lane6 == 5, -c_lo, jnp.where(lane6 < 3, 1.0, 0.0))))
        return lmat.astype(BF16), rmat.astype(BF16)

    diff = [_dot_nt(*diff_operands(cg[g])) for g in every]
    k_bf = [k_st[g].astype(BF16) for g in every]
    gram = [_dot_nt(k_bf[g], k_bf[g]) for g in every]
    qk = [_dot_nt(q_st[g].astype(BF16), k_bf[g]) for g in every]
    decay = [jnp.where(incl, jnp.exp(jnp.where(incl, diff[g], 0.0)), 0.0) for g in every]
    a_mat = [jnp.where(strict, beta[g] * gram[g] * decay[g], 0.0) for g in every]
    p_mat = [(qk[g] * decay[g]).astype(BF16) for g in every]

    blk = _div_pow2(rr, B_INV_BASE) == _div_pow2(cc, B_INV_BASE)
    apow = [jnp.where(blk, a_mat[g], 0.0) for g in every]
    nmat = [-apow[g] for g in every]
    for _ in range((B_INV_BASE - 1).bit_length() - 1):
        ab = [apow[g].astype(BF16) for g in every]
        apow = [_dot(ab[g], ab[g]) for g in every]
        nmat = [nmat[g] + apow[g] + _dot(nmat[g].astype(BF16), apow[g].astype(BF16)) for g in every]
    size = B_INV_BASE
    while size < cl:
        size *= 2
        merged = _div_pow2(rr, size) == _div_pow2(cc, size)
        join = merged & jnp.logical_not(blk)
        low = [jnp.where(join, a_mat[g], 0.0) for g in every]
        nb = [nmat[g].astype(BF16) for g in every]
        dl = [low[g] + _dot(nb[g], low[g].astype(BF16)) for g in every]
        nmat = [nmat[g] - (dl[g] + _dot(dl[g].astype(BF16), nb[g])) for g in every]
        blk = merged
    ecg = [jnp.exp(cg[g]) for g in every]
    rhs = [jnp.concatenate([beta[g] * v_st[g], (beta[g] * ecg[g]) * k_st[g]], axis=1) for g in every]
    sol = [rhs[g] + _dot(nmat[g].astype(BF16), rhs[g].astype(BF16)) for g in every]
    qg = [q_st[g] * ecg[g] for g in every]
    kdec_t = [(k_st[g] * jnp.exp(gl[g] - cg[g])).T.astype(BF16) for g in every]

    both = {}
    for g, hs in enumerate(groups):
        for idx, h in enumerate(hs):
            rsl = slice(idx * cl, (idx + 1) * cl)
            lhs = jnp.concatenate([sol[g][rsl, B_HEAD:2 * B_HEAD], qg[g][rsl]], axis=0)
            both[h] = _dot(lhs.astype(BF16), s_scr[h].astype(BF16))
    u_st = [jnp.concatenate([sol[g][idx * cl:(idx + 1) * cl, 0:B_HEAD] - both[h][0:cl]
                             for idx, h in enumerate(hs)], axis=0) for g, hs in enumerate(groups)]
    o_st = [jnp.concatenate([both[h][cl:2 * cl] for h in hs], axis=0)
            + _dot(p_mat[g], u_st[g].astype(BF16)) for g, hs in enumerate(groups)]
    for g, hs in enumerate(groups):
        for idx, h in enumerate(hs):
            u_own = jnp.where(rblk == idx, u_st[g], 0.0).astype(BF16)
            s_scr[h] = jnp.exp(gl[g][idx * cl:idx * cl + 1]) * s_scr[h] + _dot(kdec_t[g], u_own)
    for g, hs in enumerate(groups):
        for idx, h in enumerate(hs):
            o_h = o_st[g][idx * cl:(idx + 1) * cl]
            ms = jnp.mean(o_h * o_h, axis=1, keepdims=True)
            z_h = p_ref[:, conv_dim + h * B_HEAD:conv_dim + (h + 1) * B_HEAD].astype(F32)
            res = o_h[0:rows_in] * lax.rsqrt(ms[0:rows_in] + NORM_EPS) * ng_ref[...] * _silu(z_h)
            o_ref[:, h * B_HEAD:(h + 1) * B_HEAD] = res.astype(o_ref.dtype)

    @pl.when(last)
    def _():
        ssm_ref[...] = s_scr[...]


def _delta_call(proj, gates, cstate8, conv_w, alog_l, dtb_l, norm_g, s0):
    bq, t, n = proj.shape
    conv_dim = conv_w.shape[1]
    v_width = B_V_HEADS * B_HEAD
    rows_in = min(t, B_CHUNK)
    assert t % rows_in == 0
    nchunk = t // rows_in
    halo = SUBLANES * (4 // proj.dtype.itemsize)
    per_halo = max(rows_in // halo, 1)
    kern = functools.partial(_delta_kernel, t_valid=rows_in)
    return pl.pallas_call(
        kern,
        grid=(bq, nchunk),
        in_specs=[
            pl.BlockSpec((None, rows_in, n), lambda b, c: (b, c, 0)),
            pl.BlockSpec((None, halo, n), lambda b, c: (b, jnp.maximum(c * per_halo - 1, 0), 0)),
            pl.BlockSpec((None, rows_in, LANES), lambda b, c: (b, c, 0)),
            pl.BlockSpec((None, SUBLANES, conv_dim), lambda b, c: (b, 0, 0)),
            pl.BlockSpec(conv_w.shape, lambda b, c: (0, 0)),
            pl.BlockSpec((1, LANES), lambda b, c: (0, 0)),
            pl.BlockSpec((1, LANES), lambda b, c: (0, 0)),
            pl.BlockSpec((1, B_HEAD), lambda b, c: (0, 0)),
            pl.BlockSpec((None, B_V_HEADS, B_HEAD, B_HEAD), lambda b, c: (b, 0, 0, 0)),
        ],
        out_specs=[
            pl.BlockSpec((None, rows_in, v_width), lambda b, c: (b, c, 0)),
            pl.BlockSpec((None, B_V_HEADS, B_HEAD, B_HEAD), lambda b, c: (b, 0, 0, 0)),
            pl.BlockSpec((None, SUBLANES, conv_dim), lambda b, c: (b, 0, 0)),
        ],
        out_shape=[jax.ShapeDtypeStruct((bq, t, v_width), BF16),
                   jax.ShapeDtypeStruct((bq, B_V_HEADS, B_HEAD, B_HEAD), F32),
                   jax.ShapeDtypeStruct((bq, SUBLANES, conv_dim), F32)],
        scratch_shapes=[pltpu.VMEM((B_V_HEADS, B_HEAD, B_HEAD), F32),
                        pltpu.VMEM((B_CHUNK + halo, conv_dim), F32)],
        compiler_params=_params("arbitrary", "arbitrary"),
        name="gated_deltanet",
    )(proj, proj, gates, cstate8, conv_w, alog_l, dtb_l, norm_g, s0)


def _left_rows(state):
    return jnp.pad(state, ((0, 0), (SUBLANES - state.shape[1], 0), (0, 0)))


def _rope_tables(pos):
    half = A_HEAD_DIM // 2
    inv = ROPE_THETA ** (-jnp.arange(half, dtype=F32) / half)
    ang = pos.astype(F32)[:, None] * inv[None, :]
    cos, sin = jnp.cos(ang), jnp.sin(ang)
    reps = LANES // A_HEAD_DIM
    cos_t = jnp.tile(jnp.concatenate([cos, cos], axis=1), (1, reps))
    sin_t = jnp.tile(jnp.concatenate([-sin, sin], axis=1), (1, reps))
    return cos_t[None], sin_t[None]


def _trunk(x, mods, per_row, pos, a_caches, b_ssm, b_conv, f_conv, wts):
    seqs, t, d = x.shape
    (norm_mix_g, norm_ffn_g, norm_final_g, a_w_in, a_w_out, b_w_in, b_conv_w, b_a_log, b_dt_bias,
     b_norm_g, b_w_out, ffn_w_up, ffn_conv_w, ffn_conv_b, ffn_w_down, expand) = wts
    d_ff = ffn_w_down.shape[1]
    width = A_HEADS * A_HEAD_DIM

    def rows_view(a):
        return a.reshape(1, seqs * t, a.shape[2]) if per_row else a

    def seq_view(a):
        return a.reshape(seqs, t, a.shape[2])

    def mod_view(v):
        if per_row:
            return jnp.repeat(v, t, axis=0)[None]
        return v[:, None, :]

    cos_t, sin_t = _rope_tables(pos)
    prompt = a_caches is None
    if prompt:
        assert t % max(w for w, _ in A_GROUPS) == 0
        cos_t, sin_t = (jnp.concatenate([_residue_major(tab, dil) for _, dil in A_GROUPS])
                        for tab in (cos_t, sin_t))
        perms = _perm_matrices()
        unperms = jnp.swapaxes(perms, 1, 2)
    if per_row:
        cos_t, sin_t = jnp.tile(cos_t, (1, seqs, 1)), jnp.tile(sin_t, (1, seqs, 1))
    gfin = norm_final_g.reshape(1, d)
    xr = rows_view(x)
    new_a, new_ssm, new_bconv, new_fconv = [], None, None, []
    depth = norm_mix_g.shape[0]
    for layer in range(depth):
        sh1, sc1, g1, sh2, sc2, g2 = (mod_view(v) for v in mods[layer])
        gmix = norm_mix_g[layer].reshape(1, d)
        if layer % 2 == 0:
            outs, lses = [], []
            if prompt:
                tail_rows = max(w for w, _ in A_GROUPS)
                qkv, tail = _nm_rope_call(xr, gmix, sc1, sh1, a_w_in, cos_t, sin_t, tail_rows, perms)
                for gi, (win, dil) in enumerate(A_GROUPS):
                    o_g, lse_g = _attn_prompt_call(qkv, gi, win, dil)
                    outs.append(o_g.reshape(seqs, t, width))
                    lses.append(lse_g.reshape(seqs, t, LANES))
                    for kind in (1, 2):
                        col = slice((3 * gi + kind) * width, (3 * gi + kind + 1) * width)
                        keep = _natural_order(tail[:, :, col], dil)[:, tail_rows - win:]
                        new_a.append(keep.reshape(1, seqs, win, A_HEADS, A_HEAD_DIM))
                xr = _attn_merge_proj_call(outs, lses, expand, a_w_out[0], xr, g1, gfin, False,
                                           unperms)
            else:
                (tail,) = _nm_rope_call(xr, gmix, sc1, sh1, a_w_in, cos_t, sin_t, seqs * t)
                for gi, (win, dil) in enumerate(A_GROUPS):
                    as_t = lambda c: jnp.transpose(c[0], (0, 2, 3, 1)).reshape(seqs, width, c.shape[2])
                    k_new, v_new, o_g, lse_g = _attn_sample_call(
                        seq_view(tail), gi, win, dil, as_t(a_caches[2 * gi]), as_t(a_caches[2 * gi + 1]))
                    outs.append(rows_view(o_g))
                    lses.append(rows_view(lse_g))
                    for buf in (k_new, v_new):
                        buf = buf.reshape(seqs, A_HEADS, A_HEAD_DIM, buf.shape[2])
                        new_a.append(jnp.transpose(buf, (0, 3, 1, 2))[None])
                xr = _attn_merge_proj_call(outs, lses, expand, a_w_out[0], xr, g1, gfin, False)
        else:
            conv_w = b_conv_w[0]
            conv_dim = conv_w.shape[1]
            proj, gates, conv_tail = _nm_delta_in_call(
                xr, gmix, sc1, sh1, b_w_in[0], conv_dim + B_V_HEADS * B_HEAD, conv_dim,
                BF16 if prompt else F32)
            conv0 = jnp.zeros((seqs, conv_w.shape[0] - 1, conv_dim), F32) if b_conv is None else b_conv[0]
            s0 = jnp.zeros((seqs, B_V_HEADS, B_HEAD, B_HEAD), F32) if b_ssm is None else b_ssm[0]
            lane_pad = lambda v: jnp.pad(v.reshape(1, -1), ((0, 0), (B_V_HEADS, LANES - 2 * B_V_HEADS)))
            o_b, ssm1, cnew = _delta_call(seq_view(proj), seq_view(gates), _left_rows(conv0), conv_w,
                                          lane_pad(b_a_log[0]), lane_pad(b_dt_bias[0]),
                                          b_norm_g[0].reshape(1, B_HEAD), s0)
            new_ssm = ssm1[None]
            if prompt:
                cnew = conv_tail[:, -1]
            new_bconv = cnew[None, :, SUBLANES - (conv_w.shape[0] - 1):, :]
            xr = _proj_res_call(rows_view(o_b), b_w_out[0], xr, g1, gfin, False)
        gate_pre, val = _nm_ffn_call(xr, norm_ffn_g[layer].reshape(1, d), sc2, sh2, ffn_w_up[layer])
        f0 = (jnp.zeros((seqs, ffn_conv_w.shape[1] - 1, d_ff), F32) if f_conv is None else f_conv[layer])
        gate_seq = seq_view(gate_pre)
        new_fconv.append(gate_seq[:, t - (ffn_conv_w.shape[1] - 1):, :])
        mg2 = mods[layer][5][:, None, :]
        x_seq = _ffn_down_call(gate_seq, _left_rows(f0), seq_view(val), ffn_conv_w[layer],
                               ffn_conv_b[layer].reshape(1, d_ff), ffn_w_down[layer],
                               seq_view(xr), mg2, gfin, layer == depth - 1)
        xr = rows_view(x_seq)
    return seq_view(xr), new_a, new_ssm, new_bconv, jnp.stack(new_fconv)


def kernel(x_prompt, x_sample, c_prompt, c_sample, cache_a_k_w128, cache_a_v_w128, cache_a_k_w512, cache_a_v_w512, cache_a_k_w2048, cache_a_v_w2048, state_b_ssm, state_b_conv, state_ffn_conv, norm_mix_g, norm_ffn_g, norm_final_g, w_mod, b_mod, a_w_in, a_w_out, b_w_in, b_conv_w, b_a_log, b_dt_bias, b_norm_g, b_w_out, ffn_w_up, ffn_conv_w, ffn_conv_b, ffn_w_down):
    bp, s, d = x_prompt.shape
    db, t_new, _ = x_sample.shape
    depth = w_mod.shape[0]

    c_all = jnp.concatenate([c_prompt, c_sample], axis=0)
    c_all = jnp.pad(c_all, ((0, (-c_all.shape[0]) % SUBLANES), (0, 0)))
    mod = _mod_call(c_all, w_mod, b_mod)

    def mods_of(lo, hi):
        return [[mod[l, lo:hi, k * d:(k + 1) * d] for k in range(6)] for l in range(depth)]

    b_w_in_p = jnp.pad(b_w_in, ((0, 0), (0, 0), (0, (-b_w_in.shape[2]) % LANES)))
    ngroups = len(A_GROUPS)
    a_w_in_g = jnp.swapaxes(a_w_in[0].astype(BF16).reshape(d, ngroups, -1), 0, 1)
    heads = jnp.arange(A_HEADS * A_HEAD_DIM) // A_HEAD_DIM
    half = (jnp.arange(LANES)[:, None] == heads[None, :]).astype(BF16)
    expand = jnp.concatenate([half, half], axis=0)
    wts = (norm_mix_g, norm_ffn_g, norm_final_g, a_w_in_g, a_w_out.astype(BF16),
           b_w_in_p.astype(BF16), b_conv_w, b_a_log, b_dt_bias, b_norm_g, b_w_out.astype(BF16),
           ffn_w_up.astype(BF16), ffn_conv_w, ffn_conv_b, ffn_w_down.astype(BF16), expand)

    pos_p = jnp.arange(s, dtype=jnp.int32)
    pos_s = PAST_LEN + jnp.arange(t_new, dtype=jnp.int32)
    y_p, a_p, ssm_p, bconv_p, fconv_p = _trunk(
        x_prompt, mods_of(0, bp), False, pos_p, None, None, None, None, wts)
    a_caches = [cache_a_k_w128, cache_a_v_w128, cache_a_k_w512, cache_a_v_w512,
                cache_a_k_w2048, cache_a_v_w2048]
    y_s, a_s, ssm_s, bconv_s, fconv_s = _trunk(
        x_sample, mods_of(bp, bp + db), True, pos_s, a_caches, state_b_ssm, state_b_conv,
        state_ffn_conv, wts)
    return (y_p, y_s,
            a_p[0], a_s[0], a_p[1], a_s[1], a_p[2], a_s[2], a_p[3], a_s[3], a_p[4], a_s[4],
            a_p[5], a_s[5], ssm_p, ssm_s, bconv_p, bconv_s, fconv_p, fconv_s)
```

```python
import functools

import jax
import jax.numpy as jnp
from jax import lax
from jax.experimental import pallas as pl
from jax.experimental.pallas import tpu as pltpu

F32 = jnp.float32
BF16 = jnp.bfloat16

PAST_LEN = 16384
A_GROUPS = ((128, 1), (512, 4), (2048, 16))
A_HEADS = 16
A_HEAD_DIM = 64
A_ROW_TILE = 512
ROPE_THETA = 10000.0
NORM_EPS = 1e-6
B_HEAD = 128
B_V_HEADS = 16
B_QK_HEADS = 8
B_CHUNK = 64
B_STACK = 2
B_INV_BASE = 16
NEG_BIG = -1e30

LANES = 128
SUBLANES = 8
VMEM_LIMIT_BYTES = 56 * 1024 * 1024


def _params(*sem):
    return pltpu.CompilerParams(dimension_semantics=sem, vmem_limit_bytes=VMEM_LIMIT_BYTES)


def _silu(x):
    return x * jax.nn.sigmoid(x)


def _dot(a, b):
    return jnp.dot(a, b, preferred_element_type=F32)


def _dot_nt(a, b):
    return lax.dot_general(a, b, (((1,), (1,)), ((), ())), preferred_element_type=F32)


def _shift_of(n):
    assert n > 0 and n & (n - 1) == 0, n
    return n.bit_length() - 1


def _div_pow2(v, n):
    return lax.shift_right_logical(v, _shift_of(n))


def _mod_pow2(v, n):
    assert n & (n - 1) == 0
    return v & (n - 1)


def _split3(x):
    hi = x.astype(BF16)
    r1 = x - hi.astype(F32)
    mid = r1.astype(BF16)
    lo = (r1 - mid.astype(F32)).astype(BF16)
    return hi, mid, lo


def _mod_kernel(c_ref, w_ref, b_ref, o_ref):
    cs = _silu(c_ref[...]).astype(BF16)
    o_ref[...] = _dot(cs, w_ref[...].astype(BF16)) + b_ref[...]


def _mod_call(c_all, w_mod, b_mod):
    depth, d, n = w_mod.shape
    rows = c_all.shape[0]
    tn = 1536
    return pl.pallas_call(
        _mod_kernel,
        grid=(depth, n // tn),
        in_specs=[
            pl.BlockSpec((rows, d), lambda l, j: (0, 0)),
            pl.BlockSpec((None, d, tn), lambda l, j: (l, 0, j)),
            pl.BlockSpec((None, 1, tn), lambda l, j: (l, 0, j)),
        ],
        out_specs=pl.BlockSpec((None, rows, tn), lambda l, j: (l, 0, j)),
        out_shape=jax.ShapeDtypeStruct((depth, rows, n), F32),
        compiler_params=_params("arbitrary", "arbitrary"),
        name="adaln_mod",
    )(c_all, w_mod, b_mod.reshape(depth, 1, n))


def _norm_mod(x, g, sc, sh):
    ms = jnp.mean(x * x, axis=-1, keepdims=True)
    return x * lax.rsqrt(ms + NORM_EPS) * g * (1.0 + sc) + sh


def _rope_tile(acc, cos, sin):
    lane = lax.broadcasted_iota(jnp.int32, (acc.shape[0], LANES), 1)
    first_half = _mod_pow2(lane, A_HEAD_DIM) < (A_HEAD_DIM // 2)
    outs = []
    for c in range(acc.shape[1] // LANES):
        a = acc[:, c * LANES:(c + 1) * LANES]
        swapped = jnp.where(first_half,
                            pltpu.roll(a, LANES - A_HEAD_DIM // 2, 1),
                            pltpu.roll(a, A_HEAD_DIM // 2, 1))
        outs.append(a * cos + swapped * sin)
    return jnp.concatenate(outs, axis=1)


def _nm_delta_in_kernel(x_ref, g_ref, sc_ref, sh_ref, w_ref, main_ref, gates_ref, tail_ref,
                        *, col_tile):
    h = _norm_mod(x_ref[...], g_ref[...], sc_ref[...], sh_ref[...]).astype(BF16)
    tm, n_main = main_ref.shape
    for c0 in range(0, n_main, col_tile):
        acc = _dot(h, w_ref[:, c0:c0 + col_tile])
        main_ref[:, c0:c0 + col_tile] = acc.astype(main_ref.dtype)
        if c0 < tail_ref.shape[1]:
            tail_ref[:, c0:c0 + col_tile] = acc[tm - SUBLANES:tm]
    gates_ref[...] = _dot(h, w_ref[:, n_main:n_main + LANES])


def _nm_rope_kernel(x_ref, g_ref, sc_ref, sh_ref, w_ref, cos_ref, sin_ref, *rest, grouped):
    if grouped:
        perm_ref, o_ref, tail_ref, h_scr, tab_scr = rest
    else:
        tail_ref, h_scr = rest
    group = pl.program_id(2)

    @pl.when(group == 0)
    def _():
        h = _norm_mod(x_ref[...], g_ref[...], sc_ref[...], sh_ref[...]).astype(BF16)
        h_scr[0] = h
        if grouped:
            tab_scr[0, 0] = cos_ref[...]
            tab_scr[0, 1] = sin_ref[...]
            parts = jnp.concatenate(_split3(cos_ref[...]) + _split3(sin_ref[...]), axis=1)
            for p in range(perm_ref.shape[0]):
                h_scr[p + 1] = _dot(perm_ref[p], h).astype(BF16)
                moved = _dot(perm_ref[p], parts)
                for tab in range(2):
                    lo = 3 * tab * LANES
                    tab_scr[p + 1, tab] = (moved[:, lo:lo + LANES] + moved[:, lo + LANES:lo + 2 * LANES]
                                           + moved[:, lo + 2 * LANES:lo + 3 * LANES])

    h = h_scr[group] if grouped else h_scr[0]
    width = A_HEADS * A_HEAD_DIM
    if grouped:
        cos, sin = tab_scr[group, 0], tab_scr[group, 1]
    else:
        cos, sin = cos_ref[...], sin_ref[...]
    qscale = A_HEAD_DIM ** -0.5
    for kind in range(3):
        cols = slice(kind * width, (kind + 1) * width)
        val = _dot(h, w_ref[group, :, cols])
        if kind == 0:
            val = _rope_tile(val, cos * qscale, sin * qscale)
        elif kind == 1:
            val = _rope_tile(val, cos, sin)
        tail_ref[:, cols] = val
        if grouped:
            o_ref[kind] = val.astype(o_ref.dtype)


def _nm_ffn_kernel(x_ref, g_ref, sc_ref, sh_ref, w_ref, gate_ref, val_ref, *, col_tile):
    h = _norm_mod(x_ref[...], g_ref[...], sc_ref[...], sh_ref[...]).astype(BF16)
    d_ff = gate_ref.shape[1]
    for c0 in range(0, d_ff, col_tile):
        gate_ref[:, c0:c0 + col_tile] = _dot(h, w_ref[:, c0:c0 + col_tile])
        val_ref[:, c0:c0 + col_tile] = _dot(
            h, w_ref[:, d_ff + c0:d_ff + c0 + col_tile]).astype(val_ref.dtype)


def _row_tile(rows):
    return min(rows, 512)


def _resident(shape):
    zeros = (0,) * len(shape)
    return pl.BlockSpec(shape, lambda *_: zeros, pipeline_mode=pl.Buffered(1))


def _nm_common_specs(sc, tm, d):
    per_row = sc.shape[1] != 1
    mod_spec = (pl.BlockSpec((None, tm, d), lambda b, i, *_: (b, i, 0)) if per_row
                else pl.BlockSpec((None, 1, d), lambda b, i, *_: (b, 0, 0)))
    return [
        pl.BlockSpec((None, tm, d), lambda b, i, *_: (b, i, 0)),
        pl.BlockSpec((1, d), lambda b, i, *_: (0, 0)),
        mod_spec,
        mod_spec,
    ]


def _nm_delta_in_call(x, g, sc, sh, w, n_main, conv_dim, main_dtype):
    bq, rows, d = x.shape
    tm = _row_tile(rows)
    tiles = rows // tm
    kern = functools.partial(_nm_delta_in_kernel, col_tile=1024)
    return pl.pallas_call(
        kern,
        grid=(bq, tiles),
        in_specs=_nm_common_specs(sc, tm, d) + [_resident(w.shape)],
        out_specs=[pl.BlockSpec((None, tm, n_main), lambda b, i: (b, i, 0)),
                   pl.BlockSpec((None, tm, LANES), lambda b, i: (b, i, 0)),
                   pl.BlockSpec((None, None, SUBLANES, conv_dim), lambda b, i: (b, i, 0, 0))],
        out_shape=[jax.ShapeDtypeStruct((bq, rows, n_main), main_dtype),
                   jax.ShapeDtypeStruct((bq, rows, LANES), F32),
                   jax.ShapeDtypeStruct((bq, tiles, SUBLANES, conv_dim), F32)],
        compiler_params=_params("arbitrary", "arbitrary"),
        name="norm_mod_delta_in",
    )(x, g, sc, sh, w)


def _residue_major(a, dil, tm=A_ROW_TILE):
    lead, (rows, c) = a.shape[:-2], a.shape[-2:]
    a = a.reshape(lead + (rows // tm, tm // dil, dil, c))
    return jnp.swapaxes(a, -2, -3).reshape(lead + (rows, c))


def _natural_order(a, dil, tm=A_ROW_TILE):
    lead, (rows, c) = a.shape[:-2], a.shape[-2:]
    a = a.reshape(lead + (rows // tm, dil, tm // dil, c))
    return jnp.swapaxes(a, -2, -3).reshape(lead + (rows, c))


def _perm_matrices(tm=A_ROW_TILE):
    eye = jnp.eye(tm, dtype=BF16)
    return jnp.stack([_residue_major(eye, dil, tm) for _, dil in A_GROUPS if dil > 1])


def _nm_rope_call(x, g, sc, sh, w, cos, sin, tail_rows, perms=None):
    bq, rows, d = x.shape
    ngroups, _, gw = w.shape
    width = gw // 3
    grouped = perms is not None
    tm = A_ROW_TILE if grouped else _row_tile(min(rows, tail_rows))
    tiles = rows // tm
    tail_tile0 = (rows - tail_rows) // tm
    kern = functools.partial(_nm_rope_kernel, grouped=grouped)
    tab_spec = pl.BlockSpec((None, tm, LANES), lambda b, i, j: (0, i, 0))

    def tail_map(b, i, j):
        return (b, jnp.maximum(i - tail_tile0, 0), jnp.where(i >= tail_tile0, j, 0))

    in_specs = _nm_common_specs(sc, tm, d) + [_resident(w.shape), tab_spec, tab_spec]
    out_specs, out_shape, args = [], [], [x, g, sc, sh, w, cos, sin]
    if grouped:
        in_specs.append(_resident(perms.shape))
        args.append(perms)
        out_specs.append(pl.BlockSpec((None, 3, None, tm, width), lambda b, i, j: (b, j, i, 0, 0)))
        out_shape.append(jax.ShapeDtypeStruct((bq, 3 * ngroups, tiles, tm, width), BF16))
    out_specs.append(pl.BlockSpec((None, tm, gw), tail_map))
    out_shape.append(jax.ShapeDtypeStruct((bq, tail_rows, ngroups * gw), F32))
    n_h = 1 + (perms.shape[0] if grouped else 0)
    scratch = [pltpu.VMEM((n_h, tm, d), BF16)]
    if grouped:
        scratch.append(pltpu.VMEM((n_h, 2, tm, LANES), F32))
    return pl.pallas_call(
        kern,
        grid=(bq, tiles, ngroups),
        in_specs=in_specs,
        out_specs=out_specs,
        out_shape=out_shape,
        scratch_shapes=scratch,
        compiler_params=_params("arbitrary", "arbitrary", "arbitrary"),
        name="norm_mod_qkv_rope",
    )(*args)


def _nm_ffn_call(x, g, sc, sh, w):
    bq, rows, d = x.shape
    d_ff = w.shape[1] // 2
    tm = _row_tile(rows)
    kern = functools.partial(_nm_ffn_kernel, col_tile=d_ff // 2)
    row_spec = pl.BlockSpec((None, tm, d_ff), lambda b, i: (b, i, 0))
    return pl.pallas_call(
        kern,
        grid=(bq, rows // tm),
        in_specs=_nm_common_specs(sc, tm, d) + [_resident(w.shape)],
        out_specs=[row_spec, row_spec],
        out_shape=[jax.ShapeDtypeStruct((bq, rows, d_ff), F32),
                   jax.ShapeDtypeStruct((bq, rows, d_ff), BF16)],
        compiler_params=_params("arbitrary", "arbitrary"),
        name="norm_mod_ffn_up",
    )(x, g, sc, sh, w)


def _finish(y, x_ref, gate_ref, gfin_ref, o_ref, final_norm):
    xn = x_ref[...] + gate_ref[...] * y
    if final_norm:
        ms = jnp.mean(xn * xn, axis=-1, keepdims=True)
        xn = xn * lax.rsqrt(ms + NORM_EPS) * gfin_ref[...]
    o_ref[...] = xn


def _proj_res_kernel(a_ref, w_ref, x_ref, gate_ref, gfin_ref, o_ref, *, final_norm):
    _finish(_dot(a_ref[...], w_ref[...]), x_ref, gate_ref, gfin_ref, o_ref, final_norm)


def _attn_merge_proj_kernel(o0_ref, o1_ref, o2_ref, l0_ref, l1_ref, l2_ref, e_ref, w_ref,
                            x_ref, gate_ref, gfin_ref, *rest, final_norm, grouped):
    if grouped:
        unperm_ref, o_ref = rest
    else:
        (o_ref,) = rest
    outs, lses = [], []
    slot = 0
    for (_, dil), og_ref, lg_ref in zip(A_GROUPS, (o0_ref, o1_ref, o2_ref), (l0_ref, l1_ref, l2_ref)):
        if grouped and dil > 1:
            pt = unperm_ref[slot]
            slot += 1
            outs.append(_dot(pt, og_ref[...]))
            parts = _dot(pt, jnp.concatenate(_split3(lg_ref[...]), axis=1))
            lses.append(parts[:, 0:LANES] + parts[:, LANES:2 * LANES] + parts[:, 2 * LANES:3 * LANES])
        else:
            outs.append(og_ref[...].astype(F32))
            lses.append(lg_ref[...])
    m = jnp.maximum(jnp.maximum(lses[0], lses[1]), lses[2])
    exps = [jnp.exp(l - m) for l in lses]
    inv = 1.0 / (exps[0] + exps[1] + exps[2])
    merged = None
    for e, og in zip(exps, outs):
        wgt = e * inv
        hi = wgt.astype(BF16)
        lo = (wgt - hi.astype(F32)).astype(BF16)
        term = _dot(jnp.concatenate([hi, lo], axis=1), e_ref[...]) * og
        merged = term if merged is None else merged + term
    _finish(_dot(merged.astype(BF16), w_ref[...]), x_ref, gate_ref, gfin_ref, o_ref, final_norm)


def _ffn_down_kernel(gate_ref, prev_ref, state_ref, val_ref, cw_ref, cb_ref, w_ref,
                     x_ref, mgate_ref, gfin_ref, o_ref, ext_scr, *, final_norm):
    tm = gate_ref.shape[0]
    first = pl.program_id(1) == 0
    ext_scr[0:SUBLANES, :] = jnp.where(first, state_ref[...], prev_ref[...])
    ext_scr[SUBLANES:SUBLANES + tm, :] = gate_ref[...]
    width = cw_ref.shape[0]
    y = cb_ref[...]
    for t in range(width):
        off = SUBLANES - (width - 1) + t
        y = y + cw_ref[t:t + 1, :] * ext_scr[off:off + tm, :]
    act = (_silu(y) * val_ref[...].astype(F32)).astype(BF16)
    _finish(_dot(act, w_ref[...]), x_ref, mgate_ref, gfin_ref, o_ref, final_norm)


def _res_specs(tm, d, gate):
    per_row = gate.shape[1] != 1
    gate_spec = (pl.BlockSpec((None, tm, d), lambda b, i: (b, i, 0)) if per_row
                 else pl.BlockSpec((None, 1, d), lambda b, i: (b, 0, 0)))
    return [pl.BlockSpec((None, tm, d), lambda b, i: (b, i, 0)),
            gate_spec,
            pl.BlockSpec((1, d), lambda b, i: (0, 0))]


def _proj_res_call(a, w, x, gate, gfin, final_norm):
    bq, rows, d = x.shape
    k = a.shape[2]
    tm = _row_tile(rows)
    kern = functools.partial(_proj_res_kernel, final_norm=final_norm)
    return pl.pallas_call(
        kern,
        grid=(bq, rows // tm),
        in_specs=[pl.BlockSpec((None, tm, k), lambda b, i: (b, i, 0)),
                  pl.BlockSpec((k, d), lambda b, i: (0, 0))] + _res_specs(tm, d, gate),
        out_specs=pl.BlockSpec((None, tm, d), lambda b, i: (b, i, 0)),
        out_shape=jax.ShapeDtypeStruct((bq, rows, d), F32),
        compiler_params=_params("arbitrary", "arbitrary"),
        name="proj_residual",
    )(a, w, x, gate, gfin)


def _attn_merge_proj_call(outs, lses, expand, w, x, gate, gfin, final_norm, unperms=None):
    bq, rows, d = x.shape
    grouped = unperms is not None
    tm = A_ROW_TILE if grouped else _row_tile(rows)
    kern = functools.partial(_attn_merge_proj_kernel, final_norm=final_norm, grouped=grouped)
    o_spec = pl.BlockSpec((None, tm, d), lambda b, i: (b, i, 0))
    l_spec = pl.BlockSpec((None, tm, LANES), lambda b, i: (b, i, 0))
    extra_specs, extra = [], []
    if grouped:
        extra_specs.append(pl.BlockSpec(unperms.shape, lambda b, i: (0, 0, 0)))
        extra.append(unperms)
    return pl.pallas_call(
        kern,
        grid=(bq, rows // tm),
        in_specs=[o_spec] * 3 + [l_spec] * 3 + [
            pl.BlockSpec(expand.shape, lambda b, i: (0, 0)),
            pl.BlockSpec(w.shape, lambda b, i: (0, 0))] + _res_specs(tm, d, gate) + extra_specs,
        out_specs=pl.BlockSpec((None, tm, d), lambda b, i: (b, i, 0)),
        out_shape=jax.ShapeDtypeStruct((bq, rows, d), F32),
        compiler_params=_params("arbitrary", "arbitrary"),
        name="attn_merge_proj_residual",
    )(*outs, *lses, expand, w, x, gate, gfin, *extra)


def _ffn_down_call(gate_pre, state8, val, conv_w, conv_b, w, x, mgate, gfin, final_norm):
    bq, rows, d = x.shape
    f = gate_pre.shape[2]
    tm = min(rows, 256)
    kern = functools.partial(_ffn_down_kernel, final_norm=final_norm)
    blocks_per_tile = tm // SUBLANES
    return pl.pallas_call(
        kern,
        grid=(bq, rows // tm),
        in_specs=[
            pl.BlockSpec((None, tm, f), lambda b, i: (b, i, 0)),
            pl.BlockSpec((None, SUBLANES, f),
                         lambda b, i: (b, jnp.maximum(i * blocks_per_tile - 1, 0), 0)),
            pl.BlockSpec((None, SUBLANES, f), lambda b, i: (b, 0, 0)),
            pl.BlockSpec((None, tm, f), lambda b, i: (b, i, 0)),
            pl.BlockSpec(conv_w.shape, lambda b, i: (0, 0)),
            pl.BlockSpec((1, f), lambda b, i: (0, 0)),
            pl.BlockSpec((f, d), lambda b, i: (0, 0)),
        ] + _res_specs(tm, d, mgate),
        out_specs=pl.BlockSpec((None, tm, d), lambda b, i: (b, i, 0)),
        out_shape=jax.ShapeDtypeStruct((bq, rows, d), F32),
        scratch_shapes=[pltpu.VMEM((tm + SUBLANES, f), F32)],
        compiler_params=_params("arbitrary", "arbitrary"),
        name="ffn_conv_down_residual",
    )(gate_pre, gate_pre, state8, val, conv_w, conv_b, w, x, mgate, gfin)


def _ffn_fused_kernel(x_ref, g_ref, sc_ref, sh_ref, wup_ref, state_ref, cw_ref, cb_ref, wdn_ref,
                      mgate_ref, gfin_ref, o_ref, gtail_ref, carry_scr, *, final_norm, col_tile):
    tm = x_ref.shape[0]
    d_ff = wdn_ref.shape[0]
    width = cw_ref.shape[0]

    @pl.when(pl.program_id(1) == 0)
    def _():
        carry_scr[...] = state_ref[...]

    h = _norm_mod(x_ref[...], g_ref[...], sc_ref[...], sh_ref[...]).astype(BF16)
    acc = None
    for c0 in range(0, d_ff, col_tile):
        cols = slice(c0, c0 + col_tile)
        gate = _dot(h, wup_ref[:, cols])
        val = _dot(h, wup_ref[:, d_ff + c0:d_ff + c0 + col_tile])
        ext = jnp.concatenate([carry_scr[:, cols], gate], axis=0)
        y = cb_ref[:, cols] + cw_ref[width - 1:width, cols] * gate
        for back in range(1, width):
            y = y + (cw_ref[width - 1 - back:width - back, cols]
                     * pltpu.roll(ext, back, 0)[SUBLANES:SUBLANES + tm])
        act = (_silu(y) * val).astype(BF16)
        part = _dot(act, wdn_ref[cols, :])
        acc = part if acc is None else acc + part
        carry_scr[:, cols] = gate[tm - SUBLANES:tm]
        gtail_ref[:, cols] = gate[tm - SUBLANES:tm]
    _finish(acc, x_ref, mgate_ref, gfin_ref, o_ref, final_norm)


def _ffn_fused_call(x, g, sc, sh, w_up, state8, conv_w, conv_b, w_down, mgate, gfin, final_norm):
    bq, rows, d = x.shape
    d_ff = w_down.shape[0]
    tm = _row_tile(rows)
    tiles = rows // tm
    col_tile = 2 * LANES
    assert d_ff % col_tile == 0
    kern = functools.partial(_ffn_fused_kernel, final_norm=final_norm, col_tile=col_tile)
    return pl.pallas_call(
        kern,
        grid=(bq, tiles),
        in_specs=_nm_common_specs(sc, tm, d) + [
            _resident(w_up.shape),
            pl.BlockSpec((None, SUBLANES, d_ff), lambda b, i: (b, 0, 0)),
            _resident(conv_w.shape),
            _resident(conv_b.shape),
            _resident(w_down.shape),
        ] + _res_specs(tm, d, mgate)[1:],
        out_specs=[pl.BlockSpec((None, tm, d), lambda b, i: (b, i, 0)),
                   pl.BlockSpec((None, None, SUBLANES, d_ff), lambda b, i: (b, i, 0, 0))],
        out_shape=[jax.ShapeDtypeStruct((bq, rows, d), F32),
                   jax.ShapeDtypeStruct((bq, tiles, SUBLANES, d_ff), F32)],
        scratch_shapes=[pltpu.VMEM((SUBLANES, d_ff), F32)],
        compiler_params=_params("arbitrary", "arbitrary"),
        name="conv_ffn",
    )(x, g, sc, sh, w_up, state8, conv_w, conv_b, w_down, mgate, gfin)


def _attn_prompt_kernel(q_ref, kp_ref, kc_ref, vp_ref, vc_ref, o_ref, lse_ref):
    pieces, prow = q_ref.shape[0], q_ref.shape[1]
    blk = pieces * prow

    def rows_of(ref, cols):
        return jnp.concatenate([ref[p, :, cols] for p in range(pieces)], axis=0)

    n = pl.program_id(2)
    qi = lax.broadcasted_iota(jnp.int32, (blk, 2 * blk), 0)
    kj = lax.broadcasted_iota(jnp.int32, (blk, 2 * blk), 1)
    valid = (kj >= qi) & (kj <= qi + blk) & ((kj >= blk) | (n > 0))
    lane = lax.broadcasted_iota(jnp.int32, (blk, LANES), 1)
    low = lane < A_HEAD_DIM
    lse_all = jnp.zeros((blk, LANES), F32)
    for hp in range(A_HEADS // 2):
        cols = slice(hp * LANES, (hp + 1) * LANES)
        q2 = rows_of(q_ref, cols)
        k2 = jnp.concatenate([rows_of(kp_ref, cols), rows_of(kc_ref, cols)], axis=0)
        v2 = jnp.concatenate([rows_of(vp_ref, cols), rows_of(vc_ref, cols)], axis=0)
        pair = []
        for s in range(2):
            mine = low if s == 0 else jnp.logical_not(low)
            qh = jnp.where(mine, q2, jnp.zeros_like(q2))
            sc = jnp.where(valid, _dot_nt(qh, k2), NEG_BIG)
            m = jnp.max(sc, axis=1, keepdims=True)
            p = jnp.exp(sc - m)
            l = jnp.sum(p, axis=1, keepdims=True)
            pair.append(_dot(p.astype(BF16), v2) / l)
            lse_all = jnp.where(lane == 2 * hp + s, m + jnp.log(l), lse_all)
        o2 = jnp.where(low, pair[0], pair[1]).astype(o_ref.dtype)
        for p in range(pieces):
            o_ref[p, :, cols] = o2[p * prow:(p + 1) * prow]
    for p in range(pieces):
        lse_ref[p] = lse_all[p * prow:(p + 1) * prow]


def _attn_prompt_call(qkv, gi, win, dil):
    b, nkinds, tiles, tm, width = qkv.shape
    per_res = tm // dil
    qkv_g = qkv.reshape(b, nkinds, tiles, dil, per_res, width)
    blk = win // dil
    assert (tiles * tm) % win == 0
    prow = min(blk, per_res)
    pieces = blk // prow
    run_per_tile = per_res // prow
    nb = tiles * per_res // blk

    def where(kind, shift):
        def index(bb, r, i):
            i = jnp.maximum(i - shift, 0)
            return (bb, 3 * gi + kind, lax.div(i, run_per_tile), r, lax.rem(i, run_per_tile), 0)
        return index

    spec = lambda kind, shift: pl.BlockSpec((None, None, pieces, None, prow, width), where(kind, shift))
    out_index = lambda bb, r, i: (bb, lax.div(i, run_per_tile), r, lax.rem(i, run_per_tile), 0)
    return pl.pallas_call(
        _attn_prompt_kernel,
        grid=(b, dil, nb),
        in_specs=[spec(0, 0), spec(1, 1), spec(1, 0), spec(2, 1), spec(2, 0)],
        out_specs=[pl.BlockSpec((None, pieces, None, prow, width), out_index),
                   pl.BlockSpec((None, pieces, None, prow, LANES), out_index)],
        out_shape=[jax.ShapeDtypeStruct((b, tiles, dil, per_res, width), BF16),
                   jax.ShapeDtypeStruct((b, tiles, dil, per_res, LANES), F32)],
        compiler_params=_params("arbitrary", "arbitrary", "arbitrary"),
        name="dilated_attn_prompt",
    )(qkv_g, qkv_g, qkv_g, qkv_g, qkv_g)


def _attn_sample_kernel(q_ref, kn_ref, vn_ref, kc_ref, vc_ref,
                        ko_ref, vo_ref, o_ref, lse_ref,
                        m_scr, l_scr, acc_scr, kcar_scr, vcar_scr, *, lb, dil):
    t_new = q_ref.shape[0]
    width = q_ref.shape[1]
    chunk = kc_ref.shape[1]
    rows = A_HEADS * t_new
    step = pl.program_id(1)
    c = pl.num_programs(1) - 1 - step

    rq = lax.broadcasted_iota(jnp.int32, (rows, width), 0)
    lq = lax.broadcasted_iota(jnp.int32, (rows, width), 1)
    own = _div_pow2(lq, A_HEAD_DIM) == _div_pow2(rq, t_new)
    qbd = jnp.where(own, jnp.concatenate([q_ref[...]] * A_HEADS, axis=0), 0.0).astype(BF16)

    def accumulate(keys_t, vals_t, valid):
        sc = jnp.where(valid, _dot(qbd, keys_t.astype(BF16)), NEG_BIG)
        m_old = m_scr[...]
        m_new = jnp.maximum(m_old, jnp.max(sc, axis=1, keepdims=True))
        alpha = jnp.exp(m_old - m_new)
        p = jnp.where(valid, jnp.exp(sc - m_new), 0.0)
        l_scr[...] = alpha * l_scr[...] + jnp.sum(p, axis=1, keepdims=True)
        acc_scr[...] = alpha * acc_scr[...] + _dot_nt(p.astype(BF16), vals_t.astype(BF16))
        m_scr[...] = m_new

    @pl.when(step == 0)
    def _():
        m_scr[...] = jnp.full_like(m_scr, NEG_BIG)
        l_scr[...] = jnp.zeros_like(l_scr)
        acc_scr[...] = jnp.zeros_like(acc_scr)
        pad = jnp.zeros((LANES - t_new, width), F32)
        kcar_scr[...] = jnp.concatenate([kn_ref[...], pad], axis=0).T
        vcar_scr[...] = jnp.concatenate([vn_ref[...], pad], axis=0).T
        tq2 = _mod_pow2(lax.broadcasted_iota(jnp.int32, (rows, LANES), 0), t_new)
        tk2 = lax.broadcasted_iota(jnp.int32, (rows, LANES), 1)
        d2 = tq2 - tk2
        accumulate(kcar_scr[...], vcar_scr[...],
                   (tk2 < t_new) & (d2 >= 0) & (_mod_pow2(d2, dil) == 0))

    lane = lax.broadcasted_iota(jnp.int32, (width, LANES), 1)
    keep = lane < LANES - t_new
    nblk = chunk // LANES
    for src_ref, car_ref, dst_ref in ((kc_ref, kcar_scr, ko_ref), (vc_ref, vcar_scr, vo_ref)):
        nxt = pltpu.roll(car_ref[...], LANES - t_new, 1)
        for b in reversed(range(nblk)):
            cur = pltpu.roll(src_ref[:, b * LANES:(b + 1) * LANES], LANES - t_new, 1)
            dst_ref[:, b * LANES:(b + 1) * LANES] = jnp.where(keep, cur, nxt)
            nxt = cur
        car_ref[...] = src_ref[:, 0:LANES]

    tq = _mod_pow2(lax.broadcasted_iota(jnp.int32, (rows, chunk), 0), t_new)
    ik = lax.broadcasted_iota(jnp.int32, (rows, chunk), 1) + c * chunk
    dist = lb + tq - ik
    accumulate(kc_ref[...], vc_ref[...], (_mod_pow2(dist, dil) == 0) & (dist <= lb))

    @pl.when(c == 0)
    def _():
        full = jnp.where(own, acc_scr[...] / l_scr[...], 0.0)
        o_ref[...] = jnp.sum(full.reshape(A_HEADS, t_new, width), axis=0).astype(o_ref.dtype)
        lse_col = m_scr[...] + jnp.log(l_scr[...])
        rl = lax.broadcasted_iota(jnp.int32, (rows, LANES), 0)
        ll = lax.broadcasted_iota(jnp.int32, (rows, LANES), 1)
        spread = jnp.where(ll == _div_pow2(rl, t_new), lse_col, 0.0)
        lse_ref[...] = jnp.sum(spread.reshape(A_HEADS, t_new, LANES), axis=0)


def _attn_sample_call(proj, gi, win, dil, k_cache_t, v_cache_t):
    db, t_new, n = proj.shape
    width = A_HEADS * A_HEAD_DIM
    lb = k_cache_t.shape[2]
    assert lb == win and t_new == SUBLANES
    chunk = min(lb, 512)
    nchunk = lb // chunk
    kern = functools.partial(_attn_sample_kernel, lb=lb, dil=dil)
    new_spec = lambda kind: pl.BlockSpec((None, t_new, width), lambda b, s: (b, 0, 3 * gi + kind))
    cur = pl.BlockSpec((None, width, chunk), lambda b, s: (b, 0, nchunk - 1 - s))
    return pl.pallas_call(
        kern,
        grid=(db, nchunk),
        in_specs=[new_spec(0), new_spec(1), new_spec(2), cur, cur],
        out_specs=[cur, cur,
                   pl.BlockSpec((None, t_new, width), lambda b, s: (b, 0, 0)),
                   pl.BlockSpec((None, t_new, LANES), lambda b, s: (b, 0, 0))],
        out_shape=[jax.ShapeDtypeStruct((db, width, lb), F32),
                   jax.ShapeDtypeStruct((db, width, lb), F32),
                   jax.ShapeDtypeStruct((db, t_new, width), BF16),
                   jax.ShapeDtypeStruct((db, t_new, LANES), F32)],
        scratch_shapes=[pltpu.VMEM((A_HEADS * t_new, 1), F32),
                        pltpu.VMEM((A_HEADS * t_new, 1), F32),
                        pltpu.VMEM((A_HEADS * t_new, width), F32),
                        pltpu.VMEM((width, LANES), F32),
                        pltpu.VMEM((width, LANES), F32)],
        compiler_params=_params("arbitrary", "arbitrary"),
        name="dilated_attn_sample",
    )(proj, proj, proj, k_cache_t, v_cache_t)


def _delta_kernel(p_ref, prev_ref, gates_ref, cstate_ref, cw_ref, alog_ref, dtb_ref, ng_ref, s0_ref,
                  o_ref, ssm_ref, cnew_ref, s_scr, ext_scr, *, t_valid):
    cl = B_CHUNK
    rows_in = p_ref.shape[0]
    conv_dim = cw_ref.shape[1]
    c = pl.program_id(1)
    last = c == pl.num_programs(1) - 1
    qk_width = B_QK_HEADS * B_HEAD

    @pl.when(c == 0)
    def _():
        s_scr[...] = s0_ref[...]

    halo = prev_ref.shape[0]
    left = prev_ref[:, 0:conv_dim].astype(F32)
    if halo > SUBLANES:
        ext_scr[0:halo - SUBLANES, :] = left[0:halo - SUBLANES]
    ext_scr[halo - SUBLANES:halo, :] = jnp.where(c == 0, cstate_ref[...], left[halo - SUBLANES:halo])
    ext_scr[halo:halo + rows_in, :] = p_ref[:, 0:conv_dim].astype(F32)
    if rows_in < cl:
        ext_scr[halo + rows_in:halo + cl, :] = jnp.zeros((cl - rows_in, conv_dim), F32)
    width = cw_ref.shape[0]
    ext = ext_scr[...]
    y = cw_ref[width - 1:width, :] * ext[halo:halo + cl]
    for back in range(1, width):
        shifted = pltpu.roll(ext, back, 0)[halo:halo + cl]
        y = y + cw_ref[width - 1 - back:width - back, :] * shifted
    qkv = _silu(y)
    if t_valid < cl:
        rmask = lax.broadcasted_iota(jnp.int32, (cl, 1), 0) < t_valid
        qkv = jnp.where(rmask, qkv, 0.0)

    @pl.when(last)
    def _():
        cnew_ref[...] = ext_scr[halo + t_valid - SUBLANES:halo + t_valid, :]

    graw = gates_ref[...]
    if rows_in < cl:
        graw = jnp.concatenate([graw, jnp.zeros((cl - rows_in, LANES), F32)], axis=0)
    beta_all = jax.nn.sigmoid(graw)
    xg = graw + dtb_ref[...]
    softplus = jnp.maximum(xg, 0.0) + jnp.log1p(jnp.exp(-jnp.abs(xg)))
    g_all = -jnp.exp(alog_ref[...]) * softplus
    if t_valid < cl:
        beta_all = jnp.where(rmask, beta_all, 0.0)
        g_all = jnp.where(rmask, g_all, 0.0)

    ri = lax.broadcasted_iota(jnp.int32, (2 * cl, cl), 0)
    ci = lax.broadcasted_iota(jnp.int32, (2 * cl, cl), 1)
    summat = jnp.where((ci <= ri) | (ri >= cl), 1.0, 0.0).astype(BF16)
    sums = _dot(summat, jnp.concatenate(_split3(g_all), axis=1))
    sums = sums[:, 0:LANES] + sums[:, LANES:2 * LANES] + sums[:, 2 * LANES:3 * LANES]
    cg_all, gl_all = sums[0:cl], sums[cl:2 * cl]

    def l2n(a):
        return a * lax.rsqrt(jnp.sum(a * a, axis=1, keepdims=True) + NORM_EPS)

    nst = B_STACK * cl
    rr = lax.broadcasted_iota(jnp.int32, (nst, nst), 0)
    cc = lax.broadcasted_iota(jnp.int32, (nst, nst), 1)
    same = _div_pow2(rr, cl) == _div_pow2(cc, cl)
    incl = same & (cc <= rr)
    strict = same & (cc < rr)
    lane6 = lax.broadcasted_iota(jnp.int32, (nst, LANES), 1)
    rblk = _div_pow2(lax.broadcasted_iota(jnp.int32, (nst, B_HEAD), 0), cl)

    def col_stack(arr, lane0, heads):
        return jnp.concatenate([arr[:, lane0 + h:lane0 + h + 1] for h in heads], axis=0)

    groups = [list(range(g * B_STACK, (g + 1) * B_STACK)) for g in range(B_V_HEADS // B_STACK)]
    every = range(len(groups))
    rep = B_V_HEADS // B_QK_HEADS
    qn = [l2n(qkv[:, hq * B_HEAD:(hq + 1) * B_HEAD]) * (B_HEAD ** -0.5) for hq in range(B_QK_HEADS)]
    kn = [l2n(qkv[:, qk_width + hq * B_HEAD:qk_width + (hq + 1) * B_HEAD]) for hq in range(B_QK_HEADS)]
    q_st = [jnp.concatenate([qn[h // rep] for h in hs], axis=0) for hs in groups]
    k_st = [jnp.concatenate([kn[h // rep] for h in hs], axis=0) for hs in groups]
    v_st = [jnp.concatenate([qkv[:, 2 * qk_width + h * B_HEAD:2 * qk_width + (h + 1) * B_HEAD]
                             for h in hs], axis=0) for hs in groups]
    beta = [col_stack(beta_all, 0, hs) for hs in groups]
    cg = [col_stack(cg_all, B_V_HEADS, hs) for hs in groups]
    gl = [col_stack(gl_all, B_V_HEADS, hs) for hs in groups]

    def diff_operands(c):
        c_hi, c_mid, c_lo = (p.astype(F32) for p in _split3(c))
        lmat = jnp.where(lane6 == 0, c_hi, jnp.where(lane6 == 1, c_mid, jnp.where(
            lane6 == 2, c_lo, jnp.where(lane6 < 6, 1.0, 0.0))))
        rmat = jnp.where(lane6 == 3, -c_hi, jnp.where(lane6 == 4, -c_mid, jnp.where(
            lane6 == 5, -c_lo, jnp.where(lane6 < 3, 1.0, 0.0))))
        return lmat.astype(BF16), rmat.astype(BF16)

    diff = [_dot_nt(*diff_operands(cg[g])) for g in every]
    k_bf = [k_st[g].astype(BF16) for g in every]
    gram = [_dot_nt(k_bf[g], k_bf[g]) for g in every]
    qk = [_dot_nt(q_st[g].astype(BF16), k_bf[g]) for g in every]
    decay = [jnp.where(incl, jnp.exp(jnp.where(incl, diff[g], 0.0)), 0.0) for g in every]
    a_mat = [jnp.where(strict, beta[g] * gram[g] * decay[g], 0.0) for g in every]
    p_mat = [(qk[g] * decay[g]).astype(BF16) for g in every]

    blk = _div_pow2(rr, B_INV_BASE) == _div_pow2(cc, B_INV_BASE)
    apow = [jnp.where(blk, a_mat[g], 0.0) for g in every]
    nmat = [-apow[g] for g in every]
    for _ in range((B_INV_BASE - 1).bit_length() - 1):
        ab = [apow[g].astype(BF16) for g in every]
        apow = [_dot(ab[g], ab[g]) for g in every]
        nmat = [nmat[g] + apow[g] + _dot(nmat[g].astype(BF16), apow[g].astype(BF16)) for g in every]
    size = B_INV_BASE
    while size < cl:
        size *= 2
        merged = _div_pow2(rr, size) == _div_pow2(cc, size)
        join = merged & jnp.logical_not(blk)
        low = [jnp.where(join, a_mat[g], 0.0) for g in every]
        nb = [nmat[g].astype(BF16) for g in every]
        dl = [low[g] + _dot(nb[g], low[g].astype(BF16)) for g in every]
        nmat = [nmat[g] - (dl[g] + _dot(dl[g].astype(BF16), nb[g])) for g in every]
        blk = merged
    ecg = [jnp.exp(cg[g]) for g in every]
    rhs = [jnp.concatenate([beta[g] * v_st[g], (beta[g] * ecg[g]) * k_st[g]], axis=1) for g in every]
    sol = [rhs[g] + _dot(nmat[g].astype(BF16), rhs[g].astype(BF16)) for g in every]
    qg = [q_st[g] * ecg[g] for g in every]
    kdec_t = [(k_st[g] * jnp.exp(gl[g] - cg[g])).T.astype(BF16) for g in every]

    both = {}
    for g, hs in enumerate(groups):
        for idx, h in enumerate(hs):
            rsl = slice(idx * cl, (idx + 1) * cl)
            lhs = jnp.concatenate([sol[g][rsl, B_HEAD:2 * B_HEAD], qg[g][rsl]], axis=0)
            both[h] = _dot(lhs.astype(BF16), s_scr[h].astype(BF16))
    u_st = [jnp.concatenate([sol[g][idx * cl:(idx + 1) * cl, 0:B_HEAD] - both[h][0:cl]
                             for idx, h in enumerate(hs)], axis=0) for g, hs in enumerate(groups)]
    o_st = [jnp.concatenate([both[h][cl:2 * cl] for h in hs], axis=0)
            + _dot(p_mat[g], u_st[g].astype(BF16)) for g, hs in enumerate(groups)]
    for g, hs in enumerate(groups):
        for idx, h in enumerate(hs):
            u_own = jnp.where(rblk == idx, u_st[g], 0.0).astype(BF16)
            s_scr[h] = jnp.exp(gl[g][idx * cl:idx * cl + 1]) * s_scr[h] + _dot(kdec_t[g], u_own)
    for g, hs in enumerate(groups):
        for idx, h in enumerate(hs):
            o_h = o_st[g][idx * cl:(idx + 1) * cl]
            ms = jnp.mean(o_h * o_h, axis=1, keepdims=True)
            z_h = p_ref[:, conv_dim + h * B_HEAD:conv_dim + (h + 1) * B_HEAD].astype(F32)
            res = o_h[0:rows_in] * lax.rsqrt(ms[0:rows_in] + NORM_EPS) * ng_ref[...] * _silu(z_h)
            o_ref[:, h * B_HEAD:(h + 1) * B_HEAD] = res.astype(o_ref.dtype)

    @pl.when(last)
    def _():
        ssm_ref[...] = s_scr[...]


def _delta_call(proj, gates, cstate8, conv_w, alog_l, dtb_l, norm_g, s0):
    bq, t, n = proj.shape
    conv_dim = conv_w.shape[1]
    v_width = B_V_HEADS * B_HEAD
    rows_in = min(t, B_CHUNK)
    assert t % rows_in == 0
    nchunk = t // rows_in
    halo = SUBLANES * (4 // proj.dtype.itemsize)
    per_halo = max(rows_in // halo, 1)
    kern = functools.partial(_delta_kernel, t_valid=rows_in)
    return pl.pallas_call(
        kern,
        grid=(bq, nchunk),
        in_specs=[
            pl.BlockSpec((None, rows_in, n), lambda b, c: (b, c, 0)),
            pl.BlockSpec((None, halo, n), lambda b, c: (b, jnp.maximum(c * per_halo - 1, 0), 0)),
            pl.BlockSpec((None, rows_in, LANES), lambda b, c: (b, c, 0)),
            pl.BlockSpec((None, SUBLANES, conv_dim), lambda b, c: (b, 0, 0)),
            pl.BlockSpec(conv_w.shape, lambda b, c: (0, 0)),
            pl.BlockSpec((1, LANES), lambda b, c: (0, 0)),
            pl.BlockSpec((1, LANES), lambda b, c: (0, 0)),
            pl.BlockSpec((1, B_HEAD), lambda b, c: (0, 0)),
            pl.BlockSpec((None, B_V_HEADS, B_HEAD, B_HEAD), lambda b, c: (b, 0, 0, 0)),
        ],
        out_specs=[
            pl.BlockSpec((None, rows_in, v_width), lambda b, c: (b, c, 0)),
            pl.BlockSpec((None, B_V_HEADS, B_HEAD, B_HEAD), lambda b, c: (b, 0, 0, 0)),
            pl.BlockSpec((None, SUBLANES, conv_dim), lambda b, c: (b, 0, 0)),
        ],
        out_shape=[jax.ShapeDtypeStruct((bq, t, v_width), BF16),
                   jax.ShapeDtypeStruct((bq, B_V_HEADS, B_HEAD, B_HEAD), F32),
                   jax.ShapeDtypeStruct((bq, SUBLANES, conv_dim), F32)],
        scratch_shapes=[pltpu.VMEM((B_V_HEADS, B_HEAD, B_HEAD), F32),
                        pltpu.VMEM((B_CHUNK + halo, conv_dim), F32)],
        compiler_params=_params("arbitrary", "arbitrary"),
        name="gated_deltanet",
    )(proj, proj, gates, cstate8, conv_w, alog_l, dtb_l, norm_g, s0)


def _left_rows(state):
    return jnp.pad(state, ((0, 0), (SUBLANES - state.shape[1], 0), (0, 0)))


def _rope_tables(pos):
    half = A_HEAD_DIM // 2
    inv = ROPE_THETA ** (-jnp.arange(half, dtype=F32) / half)
    ang = pos.astype(F32)[:, None] * inv[None, :]
    cos, sin = jnp.cos(ang), jnp.sin(ang)
    reps = LANES // A_HEAD_DIM
    cos_t = jnp.tile(jnp.concatenate([cos, cos], axis=1), (1, reps))
    sin_t = jnp.tile(jnp.concatenate([-sin, sin], axis=1), (1, reps))
    return cos_t[None], sin_t[None]


def _trunk(x, mods, per_row, pos, a_caches, b_ssm, b_conv, f_conv, wts):
    seqs, t, d = x.shape
    (norm_mix_g, norm_ffn_g, norm_final_g, a_w_in, a_w_out, b_w_in, b_conv_w, b_a_log, b_dt_bias,
     b_norm_g, b_w_out, ffn_w_up, ffn_conv_w, ffn_conv_b, ffn_w_down, expand) = wts
    d_ff = ffn_w_down.shape[1]
    width = A_HEADS * A_HEAD_DIM

    def rows_view(a):
        return a.reshape(1, seqs * t, a.shape[2]) if per_row else a

    def seq_view(a):
        return a.reshape(seqs, t, a.shape[2])

    def mod_view(v):
        if per_row:
            return jnp.repeat(v, t, axis=0)[None]
        return v[:, None, :]

    cos_t, sin_t = _rope_tables(pos)
    prompt = a_caches is None
    if prompt:
        assert t % max(w for w, _ in A_GROUPS) == 0
        perms = _perm_matrices()
        unperms = jnp.swapaxes(perms, 1, 2)
    if per_row:
        cos_t, sin_t = jnp.tile(cos_t, (1, seqs, 1)), jnp.tile(sin_t, (1, seqs, 1))
    gfin = norm_final_g.reshape(1, d)
    xr = rows_view(x)
    new_a, new_ssm, new_bconv, new_fconv = [], None, None, []
    depth = norm_mix_g.shape[0]
    for layer in range(depth):
        sh1, sc1, g1, sh2, sc2, g2 = (mod_view(v) for v in mods[layer])
        gmix = norm_mix_g[layer].reshape(1, d)
        if layer % 2 == 0:
            outs, lses = [], []
            if prompt:
                tail_rows = max(w for w, _ in A_GROUPS)
                qkv, tail = _nm_rope_call(xr, gmix, sc1, sh1, a_w_in, cos_t, sin_t, tail_rows, perms)
                for gi, (win, dil) in enumerate(A_GROUPS):
                    o_g, lse_g = _attn_prompt_call(qkv, gi, win, dil)
                    outs.append(o_g.reshape(seqs, t, width))
                    lses.append(lse_g.reshape(seqs, t, LANES))
                    for kind in (1, 2):
                        col = slice((3 * gi + kind) * width, (3 * gi + kind + 1) * width)
                        keep = _natural_order(tail[:, :, col], dil)[:, tail_rows - win:]
                        new_a.append(keep.reshape(1, seqs, win, A_HEADS, A_HEAD_DIM))
                xr = _attn_merge_proj_call(outs, lses, expand, a_w_out[0], xr, g1, gfin, False,
                                           unperms)
            else:
                (tail,) = _nm_rope_call(xr, gmix, sc1, sh1, a_w_in, cos_t, sin_t, seqs * t)
                for gi, (win, dil) in enumerate(A_GROUPS):
                    as_t = lambda c: jnp.transpose(c[0], (0, 2, 3, 1)).reshape(seqs, width, c.shape[2])
                    k_new, v_new, o_g, lse_g = _attn_sample_call(
                        seq_view(tail), gi, win, dil, as_t(a_caches[2 * gi]), as_t(a_caches[2 * gi + 1]))
                    outs.append(rows_view(o_g))
                    lses.append(rows_view(lse_g))
                    for buf in (k_new, v_new):
                        buf = buf.reshape(seqs, A_HEADS, A_HEAD_DIM, buf.shape[2])
                        new_a.append(jnp.transpose(buf, (0, 3, 1, 2))[None])
                xr = _attn_merge_proj_call(outs, lses, expand, a_w_out[0], xr, g1, gfin, False)
        else:
            conv_w = b_conv_w[0]
            conv_dim = conv_w.shape[1]
            proj, gates, conv_tail = _nm_delta_in_call(
                xr, gmix, sc1, sh1, b_w_in[0], conv_dim + B_V_HEADS * B_HEAD, conv_dim,
                BF16 if prompt else F32)
            conv0 = jnp.zeros((seqs, conv_w.shape[0] - 1, conv_dim), F32) if b_conv is None else b_conv[0]
            s0 = jnp.zeros((seqs, B_V_HEADS, B_HEAD, B_HEAD), F32) if b_ssm is None else b_ssm[0]
            lane_pad = lambda v: jnp.pad(v.reshape(1, -1), ((0, 0), (B_V_HEADS, LANES - 2 * B_V_HEADS)))
            o_b, ssm1, cnew = _delta_call(seq_view(proj), seq_view(gates), _left_rows(conv0), conv_w,
                                          lane_pad(b_a_log[0]), lane_pad(b_dt_bias[0]),
                                          b_norm_g[0].reshape(1, B_HEAD), s0)
            new_ssm = ssm1[None]
            if prompt:
                cnew = conv_tail[:, -1]
            new_bconv = cnew[None, :, SUBLANES - (conv_w.shape[0] - 1):, :]
            xr = _proj_res_call(rows_view(o_b), b_w_out[0], xr, g1, gfin, False)
        gffn = norm_ffn_g[layer].reshape(1, d)
        keep = ffn_conv_w.shape[1] - 1
        f0 = jnp.zeros((seqs, keep, d_ff), F32) if f_conv is None else f_conv[layer]
        conv_b = ffn_conv_b[layer].reshape(1, d_ff)
        mg2 = mods[layer][5][:, None, :]
        final = layer == depth - 1
        if prompt:
            xr, gate_tail = _ffn_fused_call(xr, gffn, sc2, sh2, ffn_w_up[layer], _left_rows(f0),
                                            ffn_conv_w[layer], conv_b, ffn_w_down[layer], mg2, gfin, final)
            new_fconv.append(gate_tail[:, -1, SUBLANES - keep:, :])
        else:
            gate_pre, val = _nm_ffn_call(xr, gffn, sc2, sh2, ffn_w_up[layer])
            gate_seq = seq_view(gate_pre)
            new_fconv.append(gate_seq[:, t - keep:, :])
            x_seq = _ffn_down_call(gate_seq, _left_rows(f0), seq_view(val), ffn_conv_w[layer],
                                   conv_b, ffn_w_down[layer], seq_view(xr), mg2, gfin, final)
            xr = rows_view(x_seq)
    return seq_view(xr), new_a, new_ssm, new_bconv, jnp.stack(new_fconv)


def kernel(x_prompt, x_sample, c_prompt, c_sample, cache_a_k_w128, cache_a_v_w128, cache_a_k_w512, cache_a_v_w512, cache_a_k_w2048, cache_a_v_w2048, state_b_ssm, state_b_conv, state_ffn_conv, norm_mix_g, norm_ffn_g, norm_final_g, w_mod, b_mod, a_w_in, a_w_out, b_w_in, b_conv_w, b_a_log, b_dt_bias, b_norm_g, b_w_out, ffn_w_up, ffn_conv_w, ffn_conv_b, ffn_w_down):
    bp, s, d = x_prompt.shape
    db, t_new, _ = x_sample.shape
    depth = w_mod.shape[0]

    c_all = jnp.concatenate([c_prompt, c_sample], axis=0)
    c_all = jnp.pad(c_all, ((0, (-c_all.shape[0]) % SUBLANES), (0, 0)))
    mod = _mod_call(c_all, w_mod, b_mod)

    def mods_of(lo, hi):
        return [[mod[l, lo:hi, k * d:(k + 1) * d] for k in range(6)] for l in range(depth)]

    b_w_in_p = jnp.pad(b_w_in, ((0, 0), (0, 0), (0, (-b_w_in.shape[2]) % LANES)))
    ngroups = len(A_GROUPS)
    a_w_in_g = jnp.swapaxes(a_w_in[0].astype(BF16).reshape(d, ngroups, -1), 0, 1)
    heads = jnp.arange(A_HEADS * A_HEAD_DIM) // A_HEAD_DIM
    half = (jnp.arange(LANES)[:, None] == heads[None, :]).astype(BF16)
    expand = jnp.concatenate([half, half], axis=0)
    wts = (norm_mix_g, norm_ffn_g, norm_final_g, a_w_in_g, a_w_out.astype(BF16),
           b_w_in_p.astype(BF16), b_conv_w, b_a_log, b_dt_bias, b_norm_g, b_w_out.astype(BF16),
           ffn_w_up.astype(BF16), ffn_conv_w, ffn_conv_b, ffn_w_down.astype(BF16), expand)

    pos_p = jnp.arange(s, dtype=jnp.int32)
    pos_s = PAST_LEN + jnp.arange(t_new, dtype=jnp.int32)
    y_p, a_p, ssm_p, bconv_p, fconv_p = _trunk(
        x_prompt, mods_of(0, bp), False, pos_p, None, None, None, None, wts)
    a_caches = [cache_a_k_w128, cache_a_v_w128, cache_a_k_w512, cache_a_v_w512,
                cache_a_k_w2048, cache_a_v_w2048]
    y_s, a_s, ssm_s, bconv_s, fconv_s = _trunk(
        x_sample, mods_of(bp, bp + db), True, pos_s, a_caches, state_b_ssm, state_b_conv,
        state_ffn_conv, wts)
    return (y_p, y_s,
            a_p[0], a_s[0], a_p[1], a_s[1], a_p[2], a_s[2], a_p[3], a_s[3], a_p[4], a_s[4],
            a_p[5], a_s[5], ssm_p, ssm_s, bconv_p, bconv_s, fconv_p, fconv_s)
```

```python
import functools

import jax
import jax.numpy as jnp
from jax import lax
from jax.experimental import pallas as pl
from jax.experimental.pallas import tpu as pltpu

F32 = jnp.float32
BF16 = jnp.bfloat16

PAST_LEN = 16384
A_GROUPS = ((128, 1), (512, 4), (2048, 16))
A_HEADS = 16
A_HEAD_DIM = 64
A_ROW_TILE = 512
ROPE_THETA = 10000.0
NORM_EPS = 1e-6
B_HEAD = 128
B_V_HEADS = 16
B_QK_HEADS = 8
B_CHUNK = 64
B_CHUNKS_PER_STEP = 2
B_STACK = 2
B_INV_BASE = 16
NEG_BIG = -1e30

LANES = 128
SUBLANES = 8
VMEM_LIMIT_BYTES = 56 * 1024 * 1024


def _params(*sem):
    return pltpu.CompilerParams(dimension_semantics=sem, vmem_limit_bytes=VMEM_LIMIT_BYTES)


def _silu(x):
    return x * jax.nn.sigmoid(x)


def _dot(a, b):
    return jnp.dot(a, b, preferred_element_type=F32)


def _dot_nt(a, b):
    return lax.dot_general(a, b, (((1,), (1,)), ((), ())), preferred_element_type=F32)


def _shift_of(n):
    assert n > 0 and n & (n - 1) == 0, n
    return n.bit_length() - 1


def _div_pow2(v, n):
    return lax.shift_right_logical(v, _shift_of(n))


def _mod_pow2(v, n):
    assert n & (n - 1) == 0
    return v & (n - 1)


def _split3(x):
    hi = x.astype(BF16)
    r1 = x - hi.astype(F32)
    mid = r1.astype(BF16)
    lo = (r1 - mid.astype(F32)).astype(BF16)
    return hi, mid, lo


def _mod_kernel(c_ref, w_ref, b_ref, o_ref):
    cs = _silu(c_ref[...]).astype(BF16)
    o_ref[...] = _dot(cs, w_ref[...].astype(BF16)) + b_ref[...]


def _mod_call(c_all, w_mod, b_mod):
    depth, d, n = w_mod.shape
    rows = c_all.shape[0]
    tn = 1536
    return pl.pallas_call(
        _mod_kernel,
        grid=(depth, n // tn),
        in_specs=[
            pl.BlockSpec((rows, d), lambda l, j: (0, 0)),
            pl.BlockSpec((None, d, tn), lambda l, j: (l, 0, j)),
            pl.BlockSpec((None, 1, tn), lambda l, j: (l, 0, j)),
        ],
        out_specs=pl.BlockSpec((None, rows, tn), lambda l, j: (l, 0, j)),
        out_shape=jax.ShapeDtypeStruct((depth, rows, n), F32),
        compiler_params=_params("arbitrary", "arbitrary"),
        name="adaln_mod",
    )(c_all, w_mod, b_mod.reshape(depth, 1, n))


def _norm_mod(x, g, sc, sh):
    ms = jnp.mean(x * x, axis=-1, keepdims=True)
    return x * lax.rsqrt(ms + NORM_EPS) * g * (1.0 + sc) + sh


def _rope_tile(acc, cos, sin):
    lane = lax.broadcasted_iota(jnp.int32, (acc.shape[0], LANES), 1)
    first_half = _mod_pow2(lane, A_HEAD_DIM) < (A_HEAD_DIM // 2)
    outs = []
    for c in range(acc.shape[1] // LANES):
        a = acc[:, c * LANES:(c + 1) * LANES]
        swapped = jnp.where(first_half,
                            pltpu.roll(a, LANES - A_HEAD_DIM // 2, 1),
                            pltpu.roll(a, A_HEAD_DIM // 2, 1))
        outs.append(a * cos + swapped * sin)
    return jnp.concatenate(outs, axis=1)


def _nm_delta_in_kernel(x_ref, g_ref, sc_ref, sh_ref, w_ref, main_ref, gates_ref, tail_ref,
                        *, col_tile):
    h = _norm_mod(x_ref[...], g_ref[...], sc_ref[...], sh_ref[...]).astype(BF16)
    tm, n_main = main_ref.shape
    for c0 in range(0, n_main, col_tile):
        acc = _dot(h, w_ref[:, c0:c0 + col_tile])
        main_ref[:, c0:c0 + col_tile] = acc.astype(main_ref.dtype)
        if c0 < tail_ref.shape[1]:
            tail_ref[:, c0:c0 + col_tile] = acc[tm - SUBLANES:tm]
    gates_ref[...] = _dot(h, w_ref[:, n_main:n_main + LANES])


def _nm_rope_kernel(x_ref, g_ref, sc_ref, sh_ref, w_ref, cos_ref, sin_ref, *rest, grouped):
    if grouped:
        perm_ref, o_ref, tail_ref, h_scr, tab_scr = rest
    else:
        tail_ref, h_scr = rest
    group = pl.program_id(2)

    @pl.when(group == 0)
    def _():
        h = _norm_mod(x_ref[...], g_ref[...], sc_ref[...], sh_ref[...]).astype(BF16)
        h_scr[0] = h
        if grouped:
            tab_scr[0, 0] = cos_ref[...]
            tab_scr[0, 1] = sin_ref[...]
            parts = jnp.concatenate(_split3(cos_ref[...]) + _split3(sin_ref[...]), axis=1)
            for p in range(perm_ref.shape[0]):
                h_scr[p + 1] = _dot(perm_ref[p], h).astype(BF16)
                moved = _dot(perm_ref[p], parts)
                for tab in range(2):
                    lo = 3 * tab * LANES
                    tab_scr[p + 1, tab] = (moved[:, lo:lo + LANES] + moved[:, lo + LANES:lo + 2 * LANES]
                                           + moved[:, lo + 2 * LANES:lo + 3 * LANES])

    h = h_scr[group] if grouped else h_scr[0]
    width = A_HEADS * A_HEAD_DIM
    if grouped:
        cos, sin = tab_scr[group, 0], tab_scr[group, 1]
    else:
        cos, sin = cos_ref[...], sin_ref[...]
    qscale = A_HEAD_DIM ** -0.5
    raw = [_dot(h, w_ref[group, :, kind * width:(kind + 1) * width]) for kind in range(3)]
    for kind in range(3):
        cols = slice(kind * width, (kind + 1) * width)
        val = raw[kind]
        if kind == 0:
            val = _rope_tile(val, cos * qscale, sin * qscale)
        elif kind == 1:
            val = _rope_tile(val, cos, sin)
        tail_ref[:, cols] = val
        if grouped:
            o_ref[kind] = val.astype(o_ref.dtype)


def _nm_ffn_kernel(x_ref, g_ref, sc_ref, sh_ref, w_ref, gate_ref, val_ref, *, col_tile):
    h = _norm_mod(x_ref[...], g_ref[...], sc_ref[...], sh_ref[...]).astype(BF16)
    d_ff = gate_ref.shape[1]
    for c0 in range(0, d_ff, col_tile):
        gate_ref[:, c0:c0 + col_tile] = _dot(h, w_ref[:, c0:c0 + col_tile])
        val_ref[:, c0:c0 + col_tile] = _dot(
            h, w_ref[:, d_ff + c0:d_ff + c0 + col_tile]).astype(val_ref.dtype)


def _row_tile(rows):
    return min(rows, 512)


def _resident(shape):
    zeros = (0,) * len(shape)
    return pl.BlockSpec(shape, lambda *_: zeros, pipeline_mode=pl.Buffered(1))


def _nm_common_specs(sc, tm, d):
    per_row = sc.shape[1] != 1
    mod_spec = (pl.BlockSpec((None, tm, d), lambda b, i, *_: (b, i, 0)) if per_row
                else pl.BlockSpec((None, 1, d), lambda b, i, *_: (b, 0, 0)))
    return [
        pl.BlockSpec((None, tm, d), lambda b, i, *_: (b, i, 0)),
        pl.BlockSpec((1, d), lambda b, i, *_: (0, 0)),
        mod_spec,
        mod_spec,
    ]


def _nm_delta_in_call(x, g, sc, sh, w, n_main, conv_dim, main_dtype):
    bq, rows, d = x.shape
    tm = _row_tile(rows)
    tiles = rows // tm
    kern = functools.partial(_nm_delta_in_kernel, col_tile=1024)
    return pl.pallas_call(
        kern,
        grid=(bq, tiles),
        in_specs=_nm_common_specs(sc, tm, d) + [_resident(w.shape)],
        out_specs=[pl.BlockSpec((None, tm, n_main), lambda b, i: (b, i, 0)),
                   pl.BlockSpec((None, tm, LANES), lambda b, i: (b, i, 0)),
                   pl.BlockSpec((None, None, SUBLANES, conv_dim), lambda b, i: (b, i, 0, 0))],
        out_shape=[jax.ShapeDtypeStruct((bq, rows, n_main), main_dtype),
                   jax.ShapeDtypeStruct((bq, rows, LANES), F32),
                   jax.ShapeDtypeStruct((bq, tiles, SUBLANES, conv_dim), F32)],
        compiler_params=_params("arbitrary", "arbitrary"),
        name="norm_mod_delta_in",
    )(x, g, sc, sh, w)


def _residue_major(a, dil, tm=A_ROW_TILE):
    lead, (rows, c) = a.shape[:-2], a.shape[-2:]
    a = a.reshape(lead + (rows // tm, tm // dil, dil, c))
    return jnp.swapaxes(a, -2, -3).reshape(lead + (rows, c))


def _natural_order(a, dil, tm=A_ROW_TILE):
    lead, (rows, c) = a.shape[:-2], a.shape[-2:]
    a = a.reshape(lead + (rows // tm, dil, tm // dil, c))
    return jnp.swapaxes(a, -2, -3).reshape(lead + (rows, c))


def _perm_matrices(tm=A_ROW_TILE):
    eye = jnp.eye(tm, dtype=BF16)
    return jnp.stack([_residue_major(eye, dil, tm) for _, dil in A_GROUPS if dil > 1])


def _nm_rope_call(x, g, sc, sh, w, cos, sin, tail_rows, perms=None):
    bq, rows, d = x.shape
    ngroups, _, gw = w.shape
    width = gw // 3
    grouped = perms is not None
    tm = A_ROW_TILE if grouped else _row_tile(min(rows, tail_rows))
    tiles = rows // tm
    tail_tile0 = (rows - tail_rows) // tm
    kern = functools.partial(_nm_rope_kernel, grouped=grouped)
    tab_spec = pl.BlockSpec((None, tm, LANES), lambda b, i, j: (0, i, 0))

    def tail_map(b, i, j):
        return (b, jnp.maximum(i - tail_tile0, 0), jnp.where(i >= tail_tile0, j, 0))

    in_specs = _nm_common_specs(sc, tm, d) + [_resident(w.shape), tab_spec, tab_spec]
    out_specs, out_shape, args = [], [], [x, g, sc, sh, w, cos, sin]
    if grouped:
        in_specs.append(_resident(perms.shape))
        args.append(perms)
        out_specs.append(pl.BlockSpec((None, 3, None, tm, width), lambda b, i, j: (b, j, i, 0, 0)))
        out_shape.append(jax.ShapeDtypeStruct((bq, 3 * ngroups, tiles, tm, width), BF16))
    out_specs.append(pl.BlockSpec((None, tm, gw), tail_map))
    out_shape.append(jax.ShapeDtypeStruct((bq, tail_rows, ngroups * gw), F32))
    n_h = 1 + (perms.shape[0] if grouped else 0)
    scratch = [pltpu.VMEM((n_h, tm, d), BF16)]
    if grouped:
        scratch.append(pltpu.VMEM((n_h, 2, tm, LANES), F32))
    return pl.pallas_call(
        kern,
        grid=(bq, tiles, ngroups),
        in_specs=in_specs,
        out_specs=out_specs,
        out_shape=out_shape,
        scratch_shapes=scratch,
        compiler_params=_params("arbitrary", "arbitrary", "arbitrary"),
        name="norm_mod_qkv_rope",
    )(*args)


def _nm_ffn_call(x, g, sc, sh, w):
    bq, rows, d = x.shape
    d_ff = w.shape[1] // 2
    tm = _row_tile(rows)
    kern = functools.partial(_nm_ffn_kernel, col_tile=d_ff // 2)
    row_spec = pl.BlockSpec((None, tm, d_ff), lambda b, i: (b, i, 0))
    return pl.pallas_call(
        kern,
        grid=(bq, rows // tm),
        in_specs=_nm_common_specs(sc, tm, d) + [_resident(w.shape)],
        out_specs=[row_spec, row_spec],
        out_shape=[jax.ShapeDtypeStruct((bq, rows, d_ff), F32),
                   jax.ShapeDtypeStruct((bq, rows, d_ff), BF16)],
        compiler_params=_params("arbitrary", "arbitrary"),
        name="norm_mod_ffn_up",
    )(x, g, sc, sh, w)


def _finish(y, x_ref, gate_ref, gfin_ref, o_ref, final_norm):
    xn = x_ref[...] + gate_ref[...] * y
    if final_norm:
        ms = jnp.mean(xn * xn, axis=-1, keepdims=True)
        xn = xn * lax.rsqrt(ms + NORM_EPS) * gfin_ref[...]
    o_ref[...] = xn


def _proj_res_kernel(a_ref, w_ref, x_ref, gate_ref, gfin_ref, o_ref, *, final_norm):
    _finish(_dot(a_ref[...], w_ref[...]), x_ref, gate_ref, gfin_ref, o_ref, final_norm)


def _attn_merge_proj_kernel(o0_ref, o1_ref, o2_ref, l0_ref, l1_ref, l2_ref, e_ref, w_ref,
                            x_ref, gate_ref, gfin_ref, *rest, final_norm, grouped):
    if grouped:
        unperm_ref, o_ref = rest
    else:
        (o_ref,) = rest
    outs, lses = [], []
    slot = 0
    for (_, dil), og_ref, lg_ref in zip(A_GROUPS, (o0_ref, o1_ref, o2_ref), (l0_ref, l1_ref, l2_ref)):
        if grouped and dil > 1:
            pt = unperm_ref[slot]
            slot += 1
            outs.append(_dot(pt, og_ref[...]))
            parts = _dot(pt, jnp.concatenate(_split3(lg_ref[...]), axis=1))
            lses.append(parts[:, 0:LANES] + parts[:, LANES:2 * LANES] + parts[:, 2 * LANES:3 * LANES])
        else:
            outs.append(og_ref[...].astype(F32))
            lses.append(lg_ref[...])
    m = jnp.maximum(jnp.maximum(lses[0], lses[1]), lses[2])
    exps = [jnp.exp(l - m) for l in lses]
    inv = 1.0 / (exps[0] + exps[1] + exps[2])
    merged = None
    for e, og in zip(exps, outs):
        wgt = e * inv
        hi = wgt.astype(BF16)
        lo = (wgt - hi.astype(F32)).astype(BF16)
        term = _dot(jnp.concatenate([hi, lo], axis=1), e_ref[...]) * og
        merged = term if merged is None else merged + term
    _finish(_dot(merged.astype(BF16), w_ref[...]), x_ref, gate_ref, gfin_ref, o_ref, final_norm)


def _ffn_down_kernel(gate_ref, prev_ref, state_ref, val_ref, cw_ref, cb_ref, w_ref,
                     x_ref, mgate_ref, gfin_ref, o_ref, ext_scr, *, final_norm):
    tm = gate_ref.shape[0]
    first = pl.program_id(1) == 0
    ext_scr[0:SUBLANES, :] = jnp.where(first, state_ref[...], prev_ref[...])
    ext_scr[SUBLANES:SUBLANES + tm, :] = gate_ref[...]
    width = cw_ref.shape[0]
    y = cb_ref[...]
    for t in range(width):
        off = SUBLANES - (width - 1) + t
        y = y + cw_ref[t:t + 1, :] * ext_scr[off:off + tm, :]
    act = (_silu(y) * val_ref[...].astype(F32)).astype(BF16)
    _finish(_dot(act, w_ref[...]), x_ref, mgate_ref, gfin_ref, o_ref, final_norm)


def _res_specs(tm, d, gate):
    per_row = gate.shape[1] != 1
    gate_spec = (pl.BlockSpec((None, tm, d), lambda b, i: (b, i, 0)) if per_row
                 else pl.BlockSpec((None, 1, d), lambda b, i: (b, 0, 0)))
    return [pl.BlockSpec((None, tm, d), lambda b, i: (b, i, 0)),
            gate_spec,
            pl.BlockSpec((1, d), lambda b, i: (0, 0))]


def _proj_res_call(a, w, x, gate, gfin, final_norm):
    bq, rows, d = x.shape
    k = a.shape[2]
    tm = _row_tile(rows)
    kern = functools.partial(_proj_res_kernel, final_norm=final_norm)
    return pl.pallas_call(
        kern,
        grid=(bq, rows // tm),
        in_specs=[pl.BlockSpec((None, tm, k), lambda b, i: (b, i, 0)),
                  pl.BlockSpec((k, d), lambda b, i: (0, 0))] + _res_specs(tm, d, gate),
        out_specs=pl.BlockSpec((None, tm, d), lambda b, i: (b, i, 0)),
        out_shape=jax.ShapeDtypeStruct((bq, rows, d), F32),
        compiler_params=_params("arbitrary", "arbitrary"),
        name="proj_residual",
    )(a, w, x, gate, gfin)


def _attn_merge_proj_call(outs, lses, expand, w, x, gate, gfin, final_norm, unperms=None):
    bq, rows, d = x.shape
    grouped = unperms is not None
    tm = A_ROW_TILE if grouped else _row_tile(rows)
    kern = functools.partial(_attn_merge_proj_kernel, final_norm=final_norm, grouped=grouped)
    o_spec = pl.BlockSpec((None, tm, d), lambda b, i: (b, i, 0))
    l_spec = pl.BlockSpec((None, tm, LANES), lambda b, i: (b, i, 0))
    extra_specs, extra = [], []
    if grouped:
        extra_specs.append(pl.BlockSpec(unperms.shape, lambda b, i: (0, 0, 0)))
        extra.append(unperms)
    return pl.pallas_call(
        kern,
        grid=(bq, rows // tm),
        in_specs=[o_spec] * 3 + [l_spec] * 3 + [
            pl.BlockSpec(expand.shape, lambda b, i: (0, 0)),
            pl.BlockSpec(w.shape, lambda b, i: (0, 0))] + _res_specs(tm, d, gate) + extra_specs,
        out_specs=pl.BlockSpec((None, tm, d), lambda b, i: (b, i, 0)),
        out_shape=jax.ShapeDtypeStruct((bq, rows, d), F32),
        compiler_params=_params("arbitrary", "arbitrary"),
        name="attn_merge_proj_residual",
    )(*outs, *lses, expand, w, x, gate, gfin, *extra)


def _ffn_down_call(gate_pre, state8, val, conv_w, conv_b, w, x, mgate, gfin, final_norm):
    bq, rows, d = x.shape
    f = gate_pre.shape[2]
    tm = min(rows, 256)
    kern = functools.partial(_ffn_down_kernel, final_norm=final_norm)
    blocks_per_tile = tm // SUBLANES
    return pl.pallas_call(
        kern,
        grid=(bq, rows // tm),
        in_specs=[
            pl.BlockSpec((None, tm, f), lambda b, i: (b, i, 0)),
            pl.BlockSpec((None, SUBLANES, f),
                         lambda b, i: (b, jnp.maximum(i * blocks_per_tile - 1, 0), 0)),
            pl.BlockSpec((None, SUBLANES, f), lambda b, i: (b, 0, 0)),
            pl.BlockSpec((None, tm, f), lambda b, i: (b, i, 0)),
            pl.BlockSpec(conv_w.shape, lambda b, i: (0, 0)),
            pl.BlockSpec((1, f), lambda b, i: (0, 0)),
            pl.BlockSpec((f, d), lambda b, i: (0, 0)),
        ] + _res_specs(tm, d, mgate),
        out_specs=pl.BlockSpec((None, tm, d), lambda b, i: (b, i, 0)),
        out_shape=jax.ShapeDtypeStruct((bq, rows, d), F32),
        scratch_shapes=[pltpu.VMEM((tm + SUBLANES, f), F32)],
        compiler_params=_params("arbitrary", "arbitrary"),
        name="ffn_conv_down_residual",
    )(gate_pre, gate_pre, state8, val, conv_w, conv_b, w, x, mgate, gfin)


def _ffn_fused_kernel(x_ref, g_ref, sc_ref, sh_ref, wup_ref, state_ref, cw_ref, cb_ref, wdn_ref,
                      mgate_ref, gfin_ref, o_ref, gtail_ref, carry_scr, *, final_norm, col_tile):
    tm = x_ref.shape[0]
    d_ff = wdn_ref.shape[0]
    width = cw_ref.shape[0]

    @pl.when(pl.program_id(1) == 0)
    def _():
        carry_scr[...] = state_ref[...]

    h = _norm_mod(x_ref[...], g_ref[...], sc_ref[...], sh_ref[...]).astype(BF16)

    def up(c0):
        return (_dot(h, wup_ref[:, c0:c0 + col_tile]),
                _dot(h, wup_ref[:, d_ff + c0:d_ff + c0 + col_tile]))

    acc = None
    ahead = up(0)
    for c0 in range(0, d_ff, col_tile):
        cols = slice(c0, c0 + col_tile)
        gate, val = ahead
        if c0 + col_tile < d_ff:
            ahead = up(c0 + col_tile)
        ext = jnp.concatenate([carry_scr[:, cols], gate], axis=0)
        y = cb_ref[:, cols] + cw_ref[width - 1:width, cols] * gate
        for back in range(1, width):
            y = y + (cw_ref[width - 1 - back:width - back, cols]
                     * pltpu.roll(ext, back, 0)[SUBLANES:SUBLANES + tm])
        act = (_silu(y) * val).astype(BF16)
        part = _dot(act, wdn_ref[cols, :])
        acc = part if acc is None else acc + part
        carry_scr[:, cols] = gate[tm - SUBLANES:tm]
        gtail_ref[:, cols] = gate[tm - SUBLANES:tm]
    _finish(acc, x_ref, mgate_ref, gfin_ref, o_ref, final_norm)


def _ffn_fused_call(x, g, sc, sh, w_up, state8, conv_w, conv_b, w_down, mgate, gfin, final_norm):
    bq, rows, d = x.shape
    d_ff = w_down.shape[0]
    tm = _row_tile(rows)
    tiles = rows // tm
    col_tile = 2 * LANES
    assert d_ff % col_tile == 0
    kern = functools.partial(_ffn_fused_kernel, final_norm=final_norm, col_tile=col_tile)
    return pl.pallas_call(
        kern,
        grid=(bq, tiles),
        in_specs=_nm_common_specs(sc, tm, d) + [
            _resident(w_up.shape),
            pl.BlockSpec((None, SUBLANES, d_ff), lambda b, i: (b, 0, 0)),
            _resident(conv_w.shape),
            _resident(conv_b.shape),
            _resident(w_down.shape),
        ] + _res_specs(tm, d, mgate)[1:],
        out_specs=[pl.BlockSpec((None, tm, d), lambda b, i: (b, i, 0)),
                   pl.BlockSpec((None, None, SUBLANES, d_ff), lambda b, i: (b, i, 0, 0))],
        out_shape=[jax.ShapeDtypeStruct((bq, rows, d), F32),
                   jax.ShapeDtypeStruct((bq, tiles, SUBLANES, d_ff), F32)],
        scratch_shapes=[pltpu.VMEM((SUBLANES, d_ff), F32)],
        compiler_params=_params("arbitrary", "arbitrary"),
        name="conv_ffn",
    )(x, g, sc, sh, w_up, state8, conv_w, conv_b, w_down, mgate, gfin)


def _attn_prompt_kernel(q_ref, kp_ref, kc_ref, vp_ref, vc_ref, o_ref, lse_ref):
    pieces, prow = q_ref.shape[0], q_ref.shape[1]
    blk = pieces * prow

    def rows_of(ref, cols):
        return jnp.concatenate([ref[p, :, cols] for p in range(pieces)], axis=0)

    n = pl.program_id(2)
    qi = lax.broadcasted_iota(jnp.int32, (blk, 2 * blk), 0)
    kj = lax.broadcasted_iota(jnp.int32, (blk, 2 * blk), 1)
    valid = (kj >= qi) & (kj <= qi + blk) & ((kj >= blk) | (n > 0))
    lane = lax.broadcasted_iota(jnp.int32, (blk, LANES), 1)
    low = lane < A_HEAD_DIM
    lse_all = jnp.zeros((blk, LANES), F32)
    for hp in range(A_HEADS // 2):
        cols = slice(hp * LANES, (hp + 1) * LANES)
        q2 = rows_of(q_ref, cols)
        k2 = jnp.concatenate([rows_of(kp_ref, cols), rows_of(kc_ref, cols)], axis=0)
        v2 = jnp.concatenate([rows_of(vp_ref, cols), rows_of(vc_ref, cols)], axis=0)
        pair = []
        for s in range(2):
            mine = low if s == 0 else jnp.logical_not(low)
            qh = jnp.where(mine, q2, jnp.zeros_like(q2))
            sc = jnp.where(valid, _dot_nt(qh, k2), NEG_BIG)
            m = jnp.max(sc, axis=1, keepdims=True)
            p = jnp.exp(sc - m)
            l = jnp.sum(p, axis=1, keepdims=True)
            pair.append(_dot(p.astype(BF16), v2) / l)
            lse_all = jnp.where(lane == 2 * hp + s, m + jnp.log(l), lse_all)
        o2 = jnp.where(low, pair[0], pair[1]).astype(o_ref.dtype)
        for p in range(pieces):
            o_ref[p, :, cols] = o2[p * prow:(p + 1) * prow]
    for p in range(pieces):
        lse_ref[p] = lse_all[p * prow:(p + 1) * prow]


def _attn_prompt_call(qkv, gi, win, dil):
    b, nkinds, tiles, tm, width = qkv.shape
    per_res = tm // dil
    qkv_g = qkv.reshape(b, nkinds, tiles, dil, per_res, width)
    blk = win // dil
    assert (tiles * tm) % win == 0
    prow = min(blk, per_res)
    pieces = blk // prow
    run_per_tile = per_res // prow
    nb = tiles * per_res // blk

    def where(kind, shift):
        def index(bb, r, i):
            i = jnp.maximum(i - shift, 0)
            return (bb, 3 * gi + kind, lax.div(i, run_per_tile), r, lax.rem(i, run_per_tile), 0)
        return index

    spec = lambda kind, shift: pl.BlockSpec((None, None, pieces, None, prow, width), where(kind, shift))
    out_index = lambda bb, r, i: (bb, lax.div(i, run_per_tile), r, lax.rem(i, run_per_tile), 0)
    return pl.pallas_call(
        _attn_prompt_kernel,
        grid=(b, dil, nb),
        in_specs=[spec(0, 0), spec(1, 1), spec(1, 0), spec(2, 1), spec(2, 0)],
        out_specs=[pl.BlockSpec((None, pieces, None, prow, width), out_index),
                   pl.BlockSpec((None, pieces, None, prow, LANES), out_index)],
        out_shape=[jax.ShapeDtypeStruct((b, tiles, dil, per_res, width), BF16),
                   jax.ShapeDtypeStruct((b, tiles, dil, per_res, LANES), F32)],
        compiler_params=_params("arbitrary", "arbitrary", "arbitrary"),
        name="dilated_attn_prompt",
    )(qkv_g, qkv_g, qkv_g, qkv_g, qkv_g)


def _attn_sample_kernel(q_ref, kn_ref, vn_ref, kc_ref, vc_ref,
                        ko_ref, vo_ref, o_ref, lse_ref,
                        m_scr, l_scr, acc_scr, kcar_scr, vcar_scr, *, lb, dil):
    t_new = q_ref.shape[0]
    width = q_ref.shape[1]
    chunk = kc_ref.shape[1]
    rows = A_HEADS * t_new
    step = pl.program_id(1)
    c = pl.num_programs(1) - 1 - step

    rq = lax.broadcasted_iota(jnp.int32, (rows, width), 0)
    lq = lax.broadcasted_iota(jnp.int32, (rows, width), 1)
    own = _div_pow2(lq, A_HEAD_DIM) == _div_pow2(rq, t_new)
    qbd = jnp.where(own, jnp.concatenate([q_ref[...]] * A_HEADS, axis=0), 0.0).astype(BF16)

    def accumulate(keys_t, vals_t, valid):
        sc = jnp.where(valid, _dot(qbd, keys_t.astype(BF16)), NEG_BIG)
        m_old = m_scr[...]
        m_new = jnp.maximum(m_old, jnp.max(sc, axis=1, keepdims=True))
        alpha = jnp.exp(m_old - m_new)
        p = jnp.where(valid, jnp.exp(sc - m_new), 0.0)
        l_scr[...] = alpha * l_scr[...] + jnp.sum(p, axis=1, keepdims=True)
        acc_scr[...] = alpha * acc_scr[...] + _dot_nt(p.astype(BF16), vals_t.astype(BF16))
        m_scr[...] = m_new

    @pl.when(step == 0)
    def _():
        m_scr[...] = jnp.full_like(m_scr, NEG_BIG)
        l_scr[...] = jnp.zeros_like(l_scr)
        acc_scr[...] = jnp.zeros_like(acc_scr)
        pad = jnp.zeros((LANES - t_new, width), F32)
        kcar_scr[...] = jnp.concatenate([kn_ref[...], pad], axis=0).T
        vcar_scr[...] = jnp.concatenate([vn_ref[...], pad], axis=0).T
        tq2 = _mod_pow2(lax.broadcasted_iota(jnp.int32, (rows, LANES), 0), t_new)
        tk2 = lax.broadcasted_iota(jnp.int32, (rows, LANES), 1)
        d2 = tq2 - tk2
        accumulate(kcar_scr[...], vcar_scr[...],
                   (tk2 < t_new) & (d2 >= 0) & (_mod_pow2(d2, dil) == 0))

    lane = lax.broadcasted_iota(jnp.int32, (width, LANES), 1)
    keep = lane < LANES - t_new
    nblk = chunk // LANES
    for src_ref, car_ref, dst_ref in ((kc_ref, kcar_scr, ko_ref), (vc_ref, vcar_scr, vo_ref)):
        nxt = pltpu.roll(car_ref[...], LANES - t_new, 1)
        for b in reversed(range(nblk)):
            cur = pltpu.roll(src_ref[:, b * LANES:(b + 1) * LANES], LANES - t_new, 1)
            dst_ref[:, b * LANES:(b + 1) * LANES] = jnp.where(keep, cur, nxt)
            nxt = cur
        car_ref[...] = src_ref[:, 0:LANES]

    tq = _mod_pow2(lax.broadcasted_iota(jnp.int32, (rows, chunk), 0), t_new)
    ik = lax.broadcasted_iota(jnp.int32, (rows, chunk), 1) + c * chunk
    dist = lb + tq - ik
    accumulate(kc_ref[...], vc_ref[...], (_mod_pow2(dist, dil) == 0) & (dist <= lb))

    @pl.when(c == 0)
    def _():
        full = jnp.where(own, acc_scr[...] / l_scr[...], 0.0)
        o_ref[...] = jnp.sum(full.reshape(A_HEADS, t_new, width), axis=0).astype(o_ref.dtype)
        lse_col = m_scr[...] + jnp.log(l_scr[...])
        rl = lax.broadcasted_iota(jnp.int32, (rows, LANES), 0)
        ll = lax.broadcasted_iota(jnp.int32, (rows, LANES), 1)
        spread = jnp.where(ll == _div_pow2(rl, t_new), lse_col, 0.0)
        lse_ref[...] = jnp.sum(spread.reshape(A_HEADS, t_new, LANES), axis=0)


def _attn_sample_call(proj, gi, win, dil, k_cache_t, v_cache_t):
    db, t_new, n = proj.shape
    width = A_HEADS * A_HEAD_DIM
    lb = k_cache_t.shape[2]
    assert lb == win and t_new == SUBLANES
    chunk = min(lb, 512)
    nchunk = lb // chunk
    kern = functools.partial(_attn_sample_kernel, lb=lb, dil=dil)
    new_spec = lambda kind: pl.BlockSpec((None, t_new, width), lambda b, s: (b, 0, 3 * gi + kind))
    cur = pl.BlockSpec((None, width, chunk), lambda b, s: (b, 0, nchunk - 1 - s))
    return pl.pallas_call(
        kern,
        grid=(db, nchunk),
        in_specs=[new_spec(0), new_spec(1), new_spec(2), cur, cur],
        out_specs=[cur, cur,
                   pl.BlockSpec((None, t_new, width), lambda b, s: (b, 0, 0)),
                   pl.BlockSpec((None, t_new, LANES), lambda b, s: (b, 0, 0))],
        out_shape=[jax.ShapeDtypeStruct((db, width, lb), F32),
                   jax.ShapeDtypeStruct((db, width, lb), F32),
                   jax.ShapeDtypeStruct((db, t_new, width), BF16),
                   jax.ShapeDtypeStruct((db, t_new, LANES), F32)],
        scratch_shapes=[pltpu.VMEM((A_HEADS * t_new, 1), F32),
                        pltpu.VMEM((A_HEADS * t_new, 1), F32),
                        pltpu.VMEM((A_HEADS * t_new, width), F32),
                        pltpu.VMEM((width, LANES), F32),
                        pltpu.VMEM((width, LANES), F32)],
        compiler_params=_params("arbitrary", "arbitrary"),
        name="dilated_attn_sample",
    )(proj, proj, proj, k_cache_t, v_cache_t)


def _delta_kernel(p_ref, prev_ref, gates_ref, cstate_ref, cw_ref, alog_ref, dtb_ref, ng_ref, s0_ref,
                  o_ref, ssm_ref, cnew_ref, s_scr, ext_scr, *, t_valid):
    cl = B_CHUNK
    rows_in = p_ref.shape[0]
    nsub = max(rows_in // cl, 1)
    rows_pad = nsub * cl
    conv_dim = cw_ref.shape[1]
    c = pl.program_id(1)
    last = c == pl.num_programs(1) - 1
    qk_width = B_QK_HEADS * B_HEAD

    @pl.when(c == 0)
    def _():
        s_scr[...] = s0_ref[...]

    halo = prev_ref.shape[0]
    left = prev_ref[:, 0:conv_dim].astype(F32)
    if halo > SUBLANES:
        ext_scr[0:halo - SUBLANES, :] = left[0:halo - SUBLANES]
    ext_scr[halo - SUBLANES:halo, :] = jnp.where(c == 0, cstate_ref[...], left[halo - SUBLANES:halo])
    ext_scr[halo:halo + rows_in, :] = p_ref[:, 0:conv_dim].astype(F32)
    if rows_in < rows_pad:
        ext_scr[halo + rows_in:halo + rows_pad, :] = jnp.zeros((rows_pad - rows_in, conv_dim), F32)
    width = cw_ref.shape[0]
    ext = ext_scr[...]
    y = cw_ref[width - 1:width, :] * ext[halo:halo + rows_pad]
    for back in range(1, width):
        shifted = pltpu.roll(ext, back, 0)[halo:halo + rows_pad]
        y = y + cw_ref[width - 1 - back:width - back, :] * shifted
    qkv = _silu(y)
    if t_valid < rows_pad:
        rmask = lax.broadcasted_iota(jnp.int32, (rows_pad, 1), 0) < t_valid
        qkv = jnp.where(rmask, qkv, 0.0)

    @pl.when(last)
    def _():
        cnew_ref[...] = ext_scr[halo + t_valid - SUBLANES:halo + t_valid, :]

    graw = gates_ref[...]
    if rows_in < rows_pad:
        graw = jnp.concatenate([graw, jnp.zeros((rows_pad - rows_in, LANES), F32)], axis=0)
    beta_all = jax.nn.sigmoid(graw)
    xg = graw + dtb_ref[...]
    softplus = jnp.maximum(xg, 0.0) + jnp.log1p(jnp.exp(-jnp.abs(xg)))
    g_all = -jnp.exp(alog_ref[...]) * softplus
    if t_valid < rows_pad:
        beta_all = jnp.where(rmask, beta_all, 0.0)
        g_all = jnp.where(rmask, g_all, 0.0)

    def rows_of(sub):
        return slice(sub * cl, (sub + 1) * cl)

    ri = lax.broadcasted_iota(jnp.int32, (2 * cl, cl), 0)
    ci = lax.broadcasted_iota(jnp.int32, (2 * cl, cl), 1)
    summat = jnp.where((ci <= ri) | (ri >= cl), 1.0, 0.0).astype(BF16)
    cg_all, gl_all = [], []
    for sub in range(nsub):
        sums = _dot(summat, jnp.concatenate(_split3(g_all[rows_of(sub)]), axis=1))
        sums = sums[:, 0:LANES] + sums[:, LANES:2 * LANES] + sums[:, 2 * LANES:3 * LANES]
        cg_all.append(sums[0:cl])
        gl_all.append(sums[cl:2 * cl])

    def l2n_heads(arr, col0):
        outs = []
        for hq in range(B_QK_HEADS):
            a = arr[:, col0 + hq * B_HEAD:col0 + (hq + 1) * B_HEAD]
            outs.append(a * lax.rsqrt(jnp.sum(a * a, axis=1, keepdims=True) + NORM_EPS))
        return jnp.concatenate(outs, axis=1)

    nst = B_STACK * cl
    rr = lax.broadcasted_iota(jnp.int32, (nst, nst), 0)
    cc = lax.broadcasted_iota(jnp.int32, (nst, nst), 1)
    same = _div_pow2(rr, cl) == _div_pow2(cc, cl)
    incl = same & (cc <= rr)
    strict = same & (cc < rr)
    lane6 = lax.broadcasted_iota(jnp.int32, (nst, LANES), 1)
    rblk = _div_pow2(lax.broadcasted_iota(jnp.int32, (nst, B_HEAD), 0), cl)

    def col_stack(arr, lane0, heads):
        return jnp.concatenate([arr[:, lane0 + h:lane0 + h + 1] for h in heads], axis=0)

    groups = [list(range(g * B_STACK, (g + 1) * B_STACK)) for g in range(B_V_HEADS // B_STACK)]
    problems = [(sub, hs) for sub in range(nsub) for hs in groups]
    every = range(len(problems))
    rep = B_V_HEADS // B_QK_HEADS
    qn = l2n_heads(qkv, 0) * (B_HEAD ** -0.5)
    kn = l2n_heads(qkv, qk_width)

    def head_stack(arr, col0, sub, heads):
        return jnp.concatenate([arr[rows_of(sub), col0 + h * B_HEAD:col0 + (h + 1) * B_HEAD]
                                for h in heads], axis=0)

    q_st = [head_stack(qn, 0, sub, [h // rep for h in hs]) for sub, hs in problems]
    k_st = [head_stack(kn, 0, sub, [h // rep for h in hs]) for sub, hs in problems]
    v_st = [head_stack(qkv, 2 * qk_width, sub, hs) for sub, hs in problems]
    beta = [col_stack(beta_all[rows_of(sub)], 0, hs) for sub, hs in problems]
    cg = [col_stack(cg_all[sub], B_V_HEADS, hs) for sub, hs in problems]
    gl = [col_stack(gl_all[sub], B_V_HEADS, hs) for sub, hs in problems]

    def diff_operands(c):
        c_hi, c_mid, c_lo = (p.astype(F32) for p in _split3(c))
        lmat = jnp.where(lane6 == 0, c_hi, jnp.where(lane6 == 1, c_mid, jnp.where(
            lane6 == 2, c_lo, jnp.where(lane6 < 6, 1.0, 0.0))))
        rmat = jnp.where(lane6 == 3, -c_hi, jnp.where(lane6 == 4, -c_mid, jnp.where(
            lane6 == 5, -c_lo, jnp.where(lane6 < 3, 1.0, 0.0))))
        return lmat.astype(BF16), rmat.astype(BF16)

    diff = [_dot_nt(*diff_operands(cg[g])) for g in every]
    k_bf = [k_st[g].astype(BF16) for g in every]
    gram = [_dot_nt(k_bf[g], k_bf[g]) for g in every]
    qk = [_dot_nt(q_st[g].astype(BF16), k_bf[g]) for g in every]
    decay = [jnp.where(incl, jnp.exp(jnp.where(incl, diff[g], 0.0)), 0.0) for g in every]
    a_mat = [jnp.where(strict, beta[g] * gram[g] * decay[g], 0.0) for g in every]
    p_mat = [(qk[g] * decay[g]).astype(BF16) for g in every]

    blk = _div_pow2(rr, B_INV_BASE) == _div_pow2(cc, B_INV_BASE)
    apow = [jnp.where(blk, a_mat[g], 0.0) for g in every]
    nmat = [-apow[g] for g in every]
    for _ in range((B_INV_BASE - 1).bit_length() - 1):
        ab = [apow[g].astype(BF16) for g in every]
        apow = [_dot(ab[g], ab[g]) for g in every]
        nmat = [nmat[g] + apow[g] + _dot(nmat[g].astype(BF16), apow[g].astype(BF16)) for g in every]
    size = B_INV_BASE
    while size < cl:
        size *= 2
        merged = _div_pow2(rr, size) == _div_pow2(cc, size)
        join = merged & jnp.logical_not(blk)
        low = [jnp.where(join, a_mat[g], 0.0) for g in every]
        nb = [nmat[g].astype(BF16) for g in every]
        dl = [low[g] + _dot(nb[g], low[g].astype(BF16)) for g in every]
        nmat = [nmat[g] - (dl[g] + _dot(dl[g].astype(BF16), nb[g])) for g in every]
        blk = merged
    ecg = [jnp.exp(cg[g]) for g in every]
    rhs = [jnp.concatenate([beta[g] * v_st[g], (beta[g] * ecg[g]) * k_st[g]], axis=1) for g in every]
    sol = [rhs[g] + _dot(nmat[g].astype(BF16), rhs[g].astype(BF16)) for g in every]
    qg = [q_st[g] * ecg[g] for g in every]
    kdec_t = [(k_st[g] * jnp.exp(gl[g] - cg[g])).T.astype(BF16) for g in every]

    live = min(rows_in, cl)
    for sub in range(nsub):
        mine = [g for g in every if problems[g][0] == sub]
        both = {}
        for g in mine:
            for idx, h in enumerate(problems[g][1]):
                rsl = slice(idx * cl, (idx + 1) * cl)
                lhs = jnp.concatenate([sol[g][rsl, B_HEAD:2 * B_HEAD], qg[g][rsl]], axis=0)
                both[h] = _dot(lhs.astype(BF16), s_scr[h].astype(BF16))
        u_st = {g: jnp.concatenate([sol[g][idx * cl:(idx + 1) * cl, 0:B_HEAD] - both[h][0:cl]
                                    for idx, h in enumerate(problems[g][1])], axis=0) for g in mine}
        o_st = {g: jnp.concatenate([both[h][cl:2 * cl] for h in problems[g][1]], axis=0)
                + _dot(p_mat[g], u_st[g].astype(BF16)) for g in mine}
        for g in mine:
            for idx, h in enumerate(problems[g][1]):
                u_own = jnp.where(rblk == idx, u_st[g], 0.0).astype(BF16)
                s_scr[h] = jnp.exp(gl[g][idx * cl:idx * cl + 1]) * s_scr[h] + _dot(kdec_t[g], u_own)
        out_rows = slice(sub * cl, sub * cl + live)
        for g in mine:
            for idx, h in enumerate(problems[g][1]):
                o_h = o_st[g][idx * cl:idx * cl + live]
                ms = jnp.mean(o_h * o_h, axis=1, keepdims=True)
                z_h = p_ref[out_rows, conv_dim + h * B_HEAD:conv_dim + (h + 1) * B_HEAD].astype(F32)
                res = o_h * lax.rsqrt(ms + NORM_EPS) * ng_ref[...] * _silu(z_h)
                o_ref[out_rows, h * B_HEAD:(h + 1) * B_HEAD] = res.astype(o_ref.dtype)

    @pl.when(last)
    def _():
        ssm_ref[...] = s_scr[...]


def _delta_call(proj, gates, cstate8, conv_w, alog_l, dtb_l, norm_g, s0):
    bq, t, n = proj.shape
    conv_dim = conv_w.shape[1]
    v_width = B_V_HEADS * B_HEAD
    rows_in = min(t, B_CHUNK * B_CHUNKS_PER_STEP)
    assert t % rows_in == 0 and (rows_in % B_CHUNK == 0 or rows_in < B_CHUNK)
    nchunk = t // rows_in
    halo = SUBLANES * (4 // proj.dtype.itemsize)
    per_halo = max(rows_in // halo, 1)
    kern = functools.partial(_delta_kernel, t_valid=rows_in)
    return pl.pallas_call(
        kern,
        grid=(bq, nchunk),
        in_specs=[
            pl.BlockSpec((None, rows_in, n), lambda b, c: (b, c, 0)),
            pl.BlockSpec((None, halo, n), lambda b, c: (b, jnp.maximum(c * per_halo - 1, 0), 0)),
            pl.BlockSpec((None, rows_in, LANES), lambda b, c: (b, c, 0)),
            pl.BlockSpec((None, SUBLANES, conv_dim), lambda b, c: (b, 0, 0)),
            pl.BlockSpec(conv_w.shape, lambda b, c: (0, 0)),
            pl.BlockSpec((1, LANES), lambda b, c: (0, 0)),
            pl.BlockSpec((1, LANES), lambda b, c: (0, 0)),
            pl.BlockSpec((1, B_HEAD), lambda b, c: (0, 0)),
            pl.BlockSpec((None, B_V_HEADS, B_HEAD, B_HEAD), lambda b, c: (b, 0, 0, 0)),
        ],
        out_specs=[
            pl.BlockSpec((None, rows_in, v_width), lambda b, c: (b, c, 0)),
            pl.BlockSpec((None, B_V_HEADS, B_HEAD, B_HEAD), lambda b, c: (b, 0, 0, 0)),
            pl.BlockSpec((None, SUBLANES, conv_dim), lambda b, c: (b, 0, 0)),
        ],
        out_shape=[jax.ShapeDtypeStruct((bq, t, v_width), BF16),
                   jax.ShapeDtypeStruct((bq, B_V_HEADS, B_HEAD, B_HEAD), F32),
                   jax.ShapeDtypeStruct((bq, SUBLANES, conv_dim), F32)],
        scratch_shapes=[pltpu.VMEM((B_V_HEADS, B_HEAD, B_HEAD), F32),
                        pltpu.VMEM((max(rows_in, B_CHUNK) + halo, conv_dim), F32)],
        compiler_params=_params("arbitrary", "arbitrary"),
        name="gated_deltanet",
    )(proj, proj, gates, cstate8, conv_w, alog_l, dtb_l, norm_g, s0)


def _left_rows(state):
    return jnp.pad(state, ((0, 0), (SUBLANES - state.shape[1], 0), (0, 0)))


def _rope_tables(pos):
    half = A_HEAD_DIM // 2
    inv = ROPE_THETA ** (-jnp.arange(half, dtype=F32) / half)
    ang = pos.astype(F32)[:, None] * inv[None, :]
    cos, sin = jnp.cos(ang), jnp.sin(ang)
    reps = LANES // A_HEAD_DIM
    cos_t = jnp.tile(jnp.concatenate([cos, cos], axis=1), (1, reps))
    sin_t = jnp.tile(jnp.concatenate([-sin, sin], axis=1), (1, reps))
    return cos_t[None], sin_t[None]


def _trunk(x, mods, per_row, pos, a_caches, b_ssm, b_conv, f_conv, wts):
    seqs, t, d = x.shape
    (norm_mix_g, norm_ffn_g, norm_final_g, a_w_in, a_w_out, b_w_in, b_conv_w, b_a_log, b_dt_bias,
     b_norm_g, b_w_out, ffn_w_up, ffn_conv_w, ffn_conv_b, ffn_w_down, expand) = wts
    d_ff = ffn_w_down.shape[1]
    width = A_HEADS * A_HEAD_DIM

    def rows_view(a):
        return a.reshape(1, seqs * t, a.shape[2]) if per_row else a

    def seq_view(a):
        return a.reshape(seqs, t, a.shape[2])

    def mod_view(v):
        if per_row:
            return jnp.repeat(v, t, axis=0)[None]
        return v[:, None, :]

    cos_t, sin_t = _rope_tables(pos)
    prompt = a_caches is None
    if prompt:
        assert t % max(w for w, _ in A_GROUPS) == 0
        perms = _perm_matrices()
        unperms = jnp.swapaxes(perms, 1, 2)
    if per_row:
        cos_t, sin_t = jnp.tile(cos_t, (1, seqs, 1)), jnp.tile(sin_t, (1, seqs, 1))
    gfin = norm_final_g.reshape(1, d)
    xr = rows_view(x)
    new_a, new_ssm, new_bconv, new_fconv = [], None, None, []
    depth = norm_mix_g.shape[0]
    for layer in range(depth):
        sh1, sc1, g1, sh2, sc2, g2 = (mod_view(v) for v in mods[layer])
        gmix = norm_mix_g[layer].reshape(1, d)
        if layer % 2 == 0:
            outs, lses = [], []
            if prompt:
                tail_rows = max(w for w, _ in A_GROUPS)
                qkv, tail = _nm_rope_call(xr, gmix, sc1, sh1, a_w_in, cos_t, sin_t, tail_rows, perms)
                for gi, (win, dil) in enumerate(A_GROUPS):
                    o_g, lse_g = _attn_prompt_call(qkv, gi, win, dil)
                    outs.append(o_g.reshape(seqs, t, width))
                    lses.append(lse_g.reshape(seqs, t, LANES))
                    for kind in (1, 2):
                        col = slice((3 * gi + kind) * width, (3 * gi + kind + 1) * width)
                        keep = _natural_order(tail[:, :, col], dil)[:, tail_rows - win:]
                        new_a.append(keep.reshape(1, seqs, win, A_HEADS, A_HEAD_DIM))
                xr = _attn_merge_proj_call(outs, lses, expand, a_w_out[0], xr, g1, gfin, False,
                                           unperms)
            else:
                (tail,) = _nm_rope_call(xr, gmix, sc1, sh1, a_w_in, cos_t, sin_t, seqs * t)
                for gi, (win, dil) in enumerate(A_GROUPS):
                    as_t = lambda c: jnp.transpose(c[0], (0, 2, 3, 1)).reshape(seqs, width, c.shape[2])
                    k_new, v_new, o_g, lse_g = _attn_sample_call(
                        seq_view(tail), gi, win, dil, as_t(a_caches[2 * gi]), as_t(a_caches[2 * gi + 1]))
                    outs.append(rows_view(o_g))
                    lses.append(rows_view(lse_g))
                    for buf in (k_new, v_new):
                        buf = buf.reshape(seqs, A_HEADS, A_HEAD_DIM, buf.shape[2])
                        new_a.append(jnp.transpose(buf, (0, 3, 1, 2))[None])
                xr = _attn_merge_proj_call(outs, lses, expand, a_w_out[0], xr, g1, gfin, False)
        else:
            conv_w = b_conv_w[0]
            conv_dim = conv_w.shape[1]
            proj, gates, conv_tail = _nm_delta_in_call(
                xr, gmix, sc1, sh1, b_w_in[0], conv_dim + B_V_HEADS * B_HEAD, conv_dim,
                BF16 if prompt else F32)
            conv0 = jnp.zeros((seqs, conv_w.shape[0] - 1, conv_dim), F32) if b_conv is None else b_conv[0]
            s0 = jnp.zeros((seqs, B_V_HEADS, B_HEAD, B_HEAD), F32) if b_ssm is None else b_ssm[0]
            lane_pad = lambda v: jnp.pad(v.reshape(1, -1), ((0, 0), (B_V_HEADS, LANES - 2 * B_V_HEADS)))
            o_b, ssm1, cnew = _delta_call(seq_view(proj), seq_view(gates), _left_rows(conv0), conv_w,
                                          lane_pad(b_a_log[0]), lane_pad(b_dt_bias[0]),
                                          b_norm_g[0].reshape(1, B_HEAD), s0)
            new_ssm = ssm1[None]
            if prompt:
                cnew = conv_tail[:, -1]
            new_bconv = cnew[None, :, SUBLANES - (conv_w.shape[0] - 1):, :]
            xr = _proj_res_call(rows_view(o_b), b_w_out[0], xr, g1, gfin, False)
        gffn = norm_ffn_g[layer].reshape(1, d)
        keep = ffn_conv_w.shape[1] - 1
        f0 = jnp.zeros((seqs, keep, d_ff), F32) if f_conv is None else f_conv[layer]
        conv_b = ffn_conv_b[layer].reshape(1, d_ff)
        mg2 = mods[layer][5][:, None, :]
        final = layer == depth - 1
        if prompt:
            xr, gate_tail = _ffn_fused_call(xr, gffn, sc2, sh2, ffn_w_up[layer], _left_rows(f0),
                                            ffn_conv_w[layer], conv_b, ffn_w_down[layer], mg2, gfin, final)
            new_fconv.append(gate_tail[:, -1, SUBLANES - keep:, :])
        else:
            gate_pre, val = _nm_ffn_call(xr, gffn, sc2, sh2, ffn_w_up[layer])
            gate_seq = seq_view(gate_pre)
            new_fconv.append(gate_seq[:, t - keep:, :])
            x_seq = _ffn_down_call(gate_seq, _left_rows(f0), seq_view(val), ffn_conv_w[layer],
                                   conv_b, ffn_w_down[layer], seq_view(xr), mg2, gfin, final)
            xr = rows_view(x_seq)
    return seq_view(xr), new_a, new_ssm, new_bconv, jnp.stack(new_fconv)


def kernel(x_prompt, x_sample, c_prompt, c_sample, cache_a_k_w128, cache_a_v_w128, cache_a_k_w512, cache_a_v_w512, cache_a_k_w2048, cache_a_v_w2048, state_b_ssm, state_b_conv, state_ffn_conv, norm_mix_g, norm_ffn_g, norm_final_g, w_mod, b_mod, a_w_in, a_w_out, b_w_in, b_conv_w, b_a_log, b_dt_bias, b_norm_g, b_w_out, ffn_w_up, ffn_conv_w, ffn_conv_b, ffn_w_down):
    bp, s, d = x_prompt.shape
    db, t_new, _ = x_sample.shape
    depth = w_mod.shape[0]

    c_all = jnp.concatenate([c_prompt, c_sample], axis=0)
    c_all = jnp.pad(c_all, ((0, (-c_all.shape[0]) % SUBLANES), (0, 0)))
    mod = _mod_call(c_all, w_mod, b_mod)

    def mods_of(lo, hi):
        return [[mod[l, lo:hi, k * d:(k + 1) * d] for k in range(6)] for l in range(depth)]

    b_w_in_p = jnp.pad(b_w_in, ((0, 0), (0, 0), (0, (-b_w_in.shape[2]) % LANES)))
    ngroups = len(A_GROUPS)
    a_w_in_g = jnp.swapaxes(a_w_in[0].astype(BF16).reshape(d, ngroups, -1), 0, 1)
    heads = jnp.arange(A_HEADS * A_HEAD_DIM) // A_HEAD_DIM
    half = (jnp.arange(LANES)[:, None] == heads[None, :]).astype(BF16)
    expand = jnp.concatenate([half, half], axis=0)
    wts = (norm_mix_g, norm_ffn_g, norm_final_g, a_w_in_g, a_w_out.astype(BF16),
           b_w_in_p.astype(BF16), b_conv_w, b_a_log, b_dt_bias, b_norm_g, b_w_out.astype(BF16),
           ffn_w_up.astype(BF16), ffn_conv_w, ffn_conv_b, ffn_w_down.astype(BF16), expand)

    pos_p = jnp.arange(s, dtype=jnp.int32)
    pos_s = PAST_LEN + jnp.arange(t_new, dtype=jnp.int32)
    y_p, a_p, ssm_p, bconv_p, fconv_p = _trunk(
        x_prompt, mods_of(0, bp), False, pos_p, None, None, None, None, wts)
    a_caches = [cache_a_k_w128, cache_a_v_w128, cache_a_k_w512, cache_a_v_w512,
                cache_a_k_w2048, cache_a_v_w2048]
    y_s, a_s, ssm_s, bconv_s, fconv_s = _trunk(
        x_sample, mods_of(bp, bp + db), True, pos_s, a_caches, state_b_ssm, state_b_conv,
        state_ffn_conv, wts)
    return (y_p, y_s,
            a_p[0], a_s[0], a_p[1], a_s[1], a_p[2], a_s[2], a_p[3], a_s[3], a_p[4], a_s[4],
            a_p[5], a_s[5], ssm_p, ssm_s, bconv_p, bconv_s, fconv_p, fconv_s)
```

```python
import functools

import jax
import jax.numpy as jnp
from jax import lax
from jax.experimental import pallas as pl
from jax.experimental.pallas import tpu as pltpu

F32 = jnp.float32
BF16 = jnp.bfloat16

PAST_LEN = 16384
A_GROUPS = ((128, 1), (512, 4), (2048, 16))
A_HEADS = 16
A_HEAD_DIM = 64
A_ROW_TILE = 512
A_BLOCKS_PER_STEP = 2
ROPE_THETA = 10000.0
NORM_EPS = 1e-6
B_HEAD = 128
B_V_HEADS = 16
B_QK_HEADS = 8
B_CHUNK = 64
B_CHUNKS_PER_STEP = 2
B_STACK = 2
B_INV_BASE = 16
NEG_BIG = -1e30
LOG2E = 1.4426950408889634
LN2 = 0.6931471805599453

LANES = 128
SUBLANES = 8
VMEM_LIMIT_BYTES = 56 * 1024 * 1024


def _params(*sem):
    return pltpu.CompilerParams(dimension_semantics=sem, vmem_limit_bytes=VMEM_LIMIT_BYTES)


def _silu(x):
    return x * jax.nn.sigmoid(x)


def _dot(a, b):
    return jnp.dot(a, b, preferred_element_type=F32)


def _dot_nt(a, b):
    return lax.dot_general(a, b, (((1,), (1,)), ((), ())), preferred_element_type=F32)


def _shift_of(n):
    assert n > 0 and n & (n - 1) == 0, n
    return n.bit_length() - 1


def _div_pow2(v, n):
    return lax.shift_right_logical(v, _shift_of(n))


def _mod_pow2(v, n):
    assert n & (n - 1) == 0
    return v & (n - 1)


def _split3(x):
    hi = x.astype(BF16)
    r1 = x - hi.astype(F32)
    mid = r1.astype(BF16)
    lo = (r1 - mid.astype(F32)).astype(BF16)
    return hi, mid, lo


def _mod_kernel(c_ref, w_ref, b_ref, o_ref):
    cs = _silu(c_ref[...]).astype(BF16)
    o_ref[...] = _dot(cs, w_ref[...].astype(BF16)) + b_ref[...]


def _mod_call(c_all, w_mod, b_mod):
    depth, d, n = w_mod.shape
    rows = c_all.shape[0]
    tn = 1536
    return pl.pallas_call(
        _mod_kernel,
        grid=(depth, n // tn),
        in_specs=[
            pl.BlockSpec((rows, d), lambda l, j: (0, 0)),
            pl.BlockSpec((None, d, tn), lambda l, j: (l, 0, j)),
            pl.BlockSpec((None, 1, tn), lambda l, j: (l, 0, j)),
        ],
        out_specs=pl.BlockSpec((None, rows, tn), lambda l, j: (l, 0, j)),
        out_shape=jax.ShapeDtypeStruct((depth, rows, n), F32),
        compiler_params=_params("arbitrary", "arbitrary"),
        name="adaln_mod",
    )(c_all, w_mod, b_mod.reshape(depth, 1, n))


def _norm_mod(x, g, sc, sh):
    ms = jnp.mean(x * x, axis=-1, keepdims=True)
    return x * lax.rsqrt(ms + NORM_EPS) * g * (1.0 + sc) + sh


def _rope_tile(acc, cos, sin):
    lane = lax.broadcasted_iota(jnp.int32, (acc.shape[0], LANES), 1)
    first_half = _mod_pow2(lane, A_HEAD_DIM) < (A_HEAD_DIM // 2)
    outs = []
    for c in range(acc.shape[1] // LANES):
        a = acc[:, c * LANES:(c + 1) * LANES]
        swapped = jnp.where(first_half,
                            pltpu.roll(a, LANES - A_HEAD_DIM // 2, 1),
                            pltpu.roll(a, A_HEAD_DIM // 2, 1))
        outs.append(a * cos + swapped * sin)
    return jnp.concatenate(outs, axis=1)


def _nm_delta_in_kernel(x_ref, g_ref, sc_ref, sh_ref, w_ref, main_ref, gates_ref, tail_ref,
                        *, col_tile):
    h = _norm_mod(x_ref[...], g_ref[...], sc_ref[...], sh_ref[...]).astype(BF16)
    tm, n_main = main_ref.shape
    for c0 in range(0, n_main, col_tile):
        acc = _dot(h, w_ref[:, c0:c0 + col_tile])
        main_ref[:, c0:c0 + col_tile] = acc.astype(main_ref.dtype)
        if c0 < tail_ref.shape[1]:
            tail_ref[:, c0:c0 + col_tile] = acc[tm - SUBLANES:tm]
    gates_ref[...] = _dot(h, w_ref[:, n_main:n_main + LANES])


def _delta_gates(logits, alog, dtb):
    xg = logits + dtb
    softplus = jnp.maximum(xg, 0.0) + jnp.log1p(jnp.exp(-jnp.abs(xg)))
    return jax.nn.sigmoid(logits), -jnp.exp(alog) * softplus


def _l2n_heads(arr):
    outs = []
    for h0 in range(0, arr.shape[1], B_HEAD):
        a = arr[:, h0:h0 + B_HEAD]
        outs.append(a * lax.rsqrt(jnp.sum(a * a, axis=1, keepdims=True) + NORM_EPS))
    return jnp.concatenate(outs, axis=1)


def _nm_delta_act_kernel(x_ref, g_ref, sc_ref, sh_ref, w_ref, cstate_ref, cw_ref, alog_ref, dtb_ref,
                         main_ref, gates_ref, tail_ref, carry_scr, *, col_tile):
    tm, n_main = main_ref.shape
    width, conv_dim = cw_ref.shape
    qk_width = B_QK_HEADS * B_HEAD

    @pl.when(pl.program_id(1) == 0)
    def _():
        carry_scr[...] = cstate_ref[...]

    h = _norm_mod(x_ref[...], g_ref[...], sc_ref[...], sh_ref[...]).astype(BF16)
    ahead = _dot(h, w_ref[:, 0:col_tile])
    for c0 in range(0, n_main, col_tile):
        cols = slice(c0, c0 + col_tile)
        acc = ahead
        if c0 + col_tile < n_main:
            ahead = _dot(h, w_ref[:, c0 + col_tile:c0 + 2 * col_tile])
        if c0 < conv_dim:
            ext = jnp.concatenate([carry_scr[:, cols], acc], axis=0)
            y = cw_ref[width - 1:width, cols] * acc
            for back in range(1, width):
                y = y + (cw_ref[width - 1 - back:width - back, cols]
                         * pltpu.roll(ext, back, 0)[SUBLANES:SUBLANES + tm])
            carry_scr[:, cols] = acc[tm - SUBLANES:tm]
            tail_ref[:, cols] = acc[tm - SUBLANES:tm]
            acc = _silu(y)
            if c0 < 2 * qk_width:
                acc = _l2n_heads(acc)
            if c0 < qk_width:
                acc = acc * (B_HEAD ** -0.5)
        main_ref[:, cols] = acc.astype(main_ref.dtype)
    beta, decay = _delta_gates(_dot(h, w_ref[:, n_main:n_main + LANES]), alog_ref[...], dtb_ref[...])
    lane = lax.broadcasted_iota(jnp.int32, beta.shape, 1)
    gates_ref[...] = jnp.where(lane < B_V_HEADS, beta, decay)


def _nm_rope_kernel(x_ref, g_ref, sc_ref, sh_ref, w_ref, cos_ref, sin_ref, *rest, grouped):
    if grouped:
        perm_ref, o_ref, tail_ref, h_scr, tab_scr = rest
    else:
        tail_ref, h_scr = rest
    group = pl.program_id(2)

    @pl.when(group == 0)
    def _():
        h = _norm_mod(x_ref[...], g_ref[...], sc_ref[...], sh_ref[...]).astype(BF16)
        h_scr[0] = h
        if grouped:
            tab_scr[0, 0] = cos_ref[...]
            tab_scr[0, 1] = sin_ref[...]
            parts = jnp.concatenate(_split3(cos_ref[...]) + _split3(sin_ref[...]), axis=1)
            for p in range(perm_ref.shape[0]):
                h_scr[p + 1] = _dot(perm_ref[p], h).astype(BF16)
                moved = _dot(perm_ref[p], parts)
                for tab in range(2):
                    lo = 3 * tab * LANES
                    tab_scr[p + 1, tab] = (moved[:, lo:lo + LANES] + moved[:, lo + LANES:lo + 2 * LANES]
                                           + moved[:, lo + 2 * LANES:lo + 3 * LANES])

    h = h_scr[group] if grouped else h_scr[0]
    width = A_HEADS * A_HEAD_DIM
    if grouped:
        cos, sin = tab_scr[group, 0], tab_scr[group, 1]
    else:
        cos, sin = cos_ref[...], sin_ref[...]
    qscale = A_HEAD_DIM ** -0.5 * LOG2E
    raw = [_dot(h, w_ref[group, :, kind * width:(kind + 1) * width]) for kind in range(3)]
    for kind in range(3):
        cols = slice(kind * width, (kind + 1) * width)
        val = raw[kind]
        if kind == 0:
            val = _rope_tile(val, cos * qscale, sin * qscale)
        elif kind == 1:
            val = _rope_tile(val, cos, sin)
        tail_ref[:, cols] = val
        if grouped:
            o_ref[kind] = val.astype(o_ref.dtype)


def _nm_ffn_kernel(x_ref, g_ref, sc_ref, sh_ref, w_ref, gate_ref, val_ref, *, col_tile):
    h = _norm_mod(x_ref[...], g_ref[...], sc_ref[...], sh_ref[...]).astype(BF16)
    d_ff = gate_ref.shape[1]
    for c0 in range(0, d_ff, col_tile):
        gate_ref[:, c0:c0 + col_tile] = _dot(h, w_ref[:, c0:c0 + col_tile])
        val_ref[:, c0:c0 + col_tile] = _dot(
            h, w_ref[:, d_ff + c0:d_ff + c0 + col_tile]).astype(val_ref.dtype)


def _row_tile(rows):
    return min(rows, 512)


def _resident(shape):
    zeros = (0,) * len(shape)
    return pl.BlockSpec(shape, lambda *_: zeros, pipeline_mode=pl.Buffered(1))


def _nm_common_specs(sc, tm, d):
    per_row = sc.shape[1] != 1
    mod_spec = (pl.BlockSpec((None, tm, d), lambda b, i, *_: (b, i, 0)) if per_row
                else pl.BlockSpec((None, 1, d), lambda b, i, *_: (b, 0, 0)))
    return [
        pl.BlockSpec((None, tm, d), lambda b, i, *_: (b, i, 0)),
        pl.BlockSpec((1, d), lambda b, i, *_: (0, 0)),
        mod_spec,
        mod_spec,
    ]


def _nm_delta_in_call(x, g, sc, sh, w, n_main, conv_dim, main_dtype):
    bq, rows, d = x.shape
    tm = _row_tile(rows)
    tiles = rows // tm
    kern = functools.partial(_nm_delta_in_kernel, col_tile=1024)
    return pl.pallas_call(
        kern,
        grid=(bq, tiles),
        in_specs=_nm_common_specs(sc, tm, d) + [_resident(w.shape)],
        out_specs=[pl.BlockSpec((None, tm, n_main), lambda b, i: (b, i, 0)),
                   pl.BlockSpec((None, tm, LANES), lambda b, i: (b, i, 0)),
                   pl.BlockSpec((None, None, SUBLANES, conv_dim), lambda b, i: (b, i, 0, 0))],
        out_shape=[jax.ShapeDtypeStruct((bq, rows, n_main), main_dtype),
                   jax.ShapeDtypeStruct((bq, rows, LANES), F32),
                   jax.ShapeDtypeStruct((bq, tiles, SUBLANES, conv_dim), F32)],
        compiler_params=_params("arbitrary", "arbitrary"),
        name="norm_mod_delta_in",
    )(x, g, sc, sh, w)


def _nm_delta_act_call(x, g, sc, sh, w, cstate8, conv_w, alog_l, dtb_l):
    bq, rows, d = x.shape
    conv_dim = conv_w.shape[1]
    n_main = w.shape[1] - LANES
    tm = _row_tile(rows)
    tiles = rows // tm
    kern = functools.partial(_nm_delta_act_kernel, col_tile=1024)
    return pl.pallas_call(
        kern,
        grid=(bq, tiles),
        in_specs=_nm_common_specs(sc, tm, d) + [
            _resident(w.shape),
            pl.BlockSpec((None, SUBLANES, conv_dim), lambda b, i: (b, 0, 0)),
            _resident(conv_w.shape), _resident(alog_l.shape), _resident(dtb_l.shape)],
        out_specs=[pl.BlockSpec((None, tm, n_main), lambda b, i: (b, i, 0)),
                   pl.BlockSpec((None, tm, LANES), lambda b, i: (b, i, 0)),
                   pl.BlockSpec((None, None, SUBLANES, conv_dim), lambda b, i: (b, i, 0, 0))],
        out_shape=[jax.ShapeDtypeStruct((bq, rows, n_main), BF16),
                   jax.ShapeDtypeStruct((bq, rows, LANES), F32),
                   jax.ShapeDtypeStruct((bq, tiles, SUBLANES, conv_dim), F32)],
        scratch_shapes=[pltpu.VMEM((SUBLANES, conv_dim), F32)],
        compiler_params=_params("arbitrary", "arbitrary"),
        name="norm_mod_delta_in_act",
    )(x, g, sc, sh, w, cstate8, conv_w, alog_l, dtb_l)


def _residue_major(a, dil, tm=A_ROW_TILE):
    lead, (rows, c) = a.shape[:-2], a.shape[-2:]
    a = a.reshape(lead + (rows // tm, tm // dil, dil, c))
    return jnp.swapaxes(a, -2, -3).reshape(lead + (rows, c))


def _natural_order(a, dil, tm=A_ROW_TILE):
    lead, (rows, c) = a.shape[:-2], a.shape[-2:]
    a = a.reshape(lead + (rows // tm, dil, tm // dil, c))
    return jnp.swapaxes(a, -2, -3).reshape(lead + (rows, c))


def _perm_matrices(tm=A_ROW_TILE):
    eye = jnp.eye(tm, dtype=BF16)
    return jnp.stack([_residue_major(eye, dil, tm) for _, dil in A_GROUPS if dil > 1])


def _nm_rope_call(x, g, sc, sh, w, cos, sin, tail_rows, perms=None):
    bq, rows, d = x.shape
    ngroups, _, gw = w.shape
    width = gw // 3
    grouped = perms is not None
    tm = A_ROW_TILE if grouped else _row_tile(min(rows, tail_rows))
    tiles = rows // tm
    tail_tile0 = (rows - tail_rows) // tm
    kern = functools.partial(_nm_rope_kernel, grouped=grouped)
    tab_spec = pl.BlockSpec((None, tm, LANES), lambda b, i, j: (0, i, 0))

    def tail_map(b, i, j):
        return (b, jnp.maximum(i - tail_tile0, 0), jnp.where(i >= tail_tile0, j, 0))

    in_specs = _nm_common_specs(sc, tm, d) + [_resident(w.shape), tab_spec, tab_spec]
    out_specs, out_shape, args = [], [], [x, g, sc, sh, w, cos, sin]
    if grouped:
        in_specs.append(_resident(perms.shape))
        args.append(perms)
        out_specs.append(pl.BlockSpec((None, 3, None, tm, width), lambda b, i, j: (b, j, i, 0, 0)))
        out_shape.append(jax.ShapeDtypeStruct((bq, 3 * ngroups, tiles, tm, width), BF16))
    out_specs.append(pl.BlockSpec((None, tm, gw), tail_map))
    out_shape.append(jax.ShapeDtypeStruct((bq, tail_rows, ngroups * gw), F32))
    n_h = 1 + (perms.shape[0] if grouped else 0)
    scratch = [pltpu.VMEM((n_h, tm, d), BF16)]
    if grouped:
        scratch.append(pltpu.VMEM((n_h, 2, tm, LANES), F32))
    return pl.pallas_call(
        kern,
        grid=(bq, tiles, ngroups),
        in_specs=in_specs,
        out_specs=out_specs,
        out_shape=out_shape,
        scratch_shapes=scratch,
        compiler_params=_params("arbitrary", "arbitrary", "arbitrary"),
        name="norm_mod_qkv_rope",
    )(*args)


def _nm_ffn_call(x, g, sc, sh, w):
    bq, rows, d = x.shape
    d_ff = w.shape[1] // 2
    tm = _row_tile(rows)
    kern = functools.partial(_nm_ffn_kernel, col_tile=d_ff // 2)
    row_spec = pl.BlockSpec((None, tm, d_ff), lambda b, i: (b, i, 0))
    return pl.pallas_call(
        kern,
        grid=(bq, rows // tm),
        in_specs=_nm_common_specs(sc, tm, d) + [_resident(w.shape)],
        out_specs=[row_spec, row_spec],
        out_shape=[jax.ShapeDtypeStruct((bq, rows, d_ff), F32),
                   jax.ShapeDtypeStruct((bq, rows, d_ff), BF16)],
        compiler_params=_params("arbitrary", "arbitrary"),
        name="norm_mod_ffn_up",
    )(x, g, sc, sh, w)


def _finish(y, x_ref, gate_ref, gfin_ref, o_ref, final_norm):
    xn = x_ref[...] + gate_ref[...] * y
    if final_norm:
        ms = jnp.mean(xn * xn, axis=-1, keepdims=True)
        xn = xn * lax.rsqrt(ms + NORM_EPS) * gfin_ref[...]
    o_ref[...] = xn


def _proj_res_kernel(a_ref, w_ref, x_ref, gate_ref, gfin_ref, o_ref, *, final_norm):
    _finish(_dot(a_ref[...], w_ref[...]), x_ref, gate_ref, gfin_ref, o_ref, final_norm)


def _attn_merge_proj_kernel(o0_ref, o1_ref, o2_ref, l0_ref, l1_ref, l2_ref, e_ref, w_ref,
                            x_ref, gate_ref, gfin_ref, *rest, final_norm, grouped):
    if grouped:
        unperm_ref, o_ref = rest
    else:
        (o_ref,) = rest
    outs, lses = [], []
    slot = 0
    for (_, dil), og_ref, lg_ref in zip(A_GROUPS, (o0_ref, o1_ref, o2_ref), (l0_ref, l1_ref, l2_ref)):
        if grouped and dil > 1:
            pt = unperm_ref[slot]
            slot += 1
            outs.append(_dot(pt, og_ref[...]))
            parts = _dot(pt, jnp.concatenate(_split3(lg_ref[...]), axis=1))
            lses.append(parts[:, 0:LANES] + parts[:, LANES:2 * LANES] + parts[:, 2 * LANES:3 * LANES])
        else:
            outs.append(og_ref[...].astype(F32))
            lses.append(lg_ref[...])
    m = jnp.maximum(jnp.maximum(lses[0], lses[1]), lses[2])
    exps = [jnp.exp(l - m) for l in lses]
    inv = 1.0 / (exps[0] + exps[1] + exps[2])
    merged = None
    for e, og in zip(exps, outs):
        wgt = e * inv
        hi = wgt.astype(BF16)
        lo = (wgt - hi.astype(F32)).astype(BF16)
        term = _dot(jnp.concatenate([hi, lo], axis=1), e_ref[...]) * og
        merged = term if merged is None else merged + term
    _finish(_dot(merged.astype(BF16), w_ref[...]), x_ref, gate_ref, gfin_ref, o_ref, final_norm)


def _ffn_down_kernel(gate_ref, prev_ref, state_ref, val_ref, cw_ref, cb_ref, w_ref,
                     x_ref, mgate_ref, gfin_ref, o_ref, ext_scr, *, final_norm):
    tm = gate_ref.shape[0]
    first = pl.program_id(1) == 0
    ext_scr[0:SUBLANES, :] = jnp.where(first, state_ref[...], prev_ref[...])
    ext_scr[SUBLANES:SUBLANES + tm, :] = gate_ref[...]
    width = cw_ref.shape[0]
    y = cb_ref[...]
    for t in range(width):
        off = SUBLANES - (width - 1) + t
        y = y + cw_ref[t:t + 1, :] * ext_scr[off:off + tm, :]
    act = (_silu(y) * val_ref[...].astype(F32)).astype(BF16)
    _finish(_dot(act, w_ref[...]), x_ref, mgate_ref, gfin_ref, o_ref, final_norm)


def _res_specs(tm, d, gate):
    per_row = gate.shape[1] != 1
    gate_spec = (pl.BlockSpec((None, tm, d), lambda b, i: (b, i, 0)) if per_row
                 else pl.BlockSpec((None, 1, d), lambda b, i: (b, 0, 0)))
    return [pl.BlockSpec((None, tm, d), lambda b, i: (b, i, 0)),
            gate_spec,
            pl.BlockSpec((1, d), lambda b, i: (0, 0))]


def _proj_res_call(a, w, x, gate, gfin, final_norm):
    bq, rows, d = x.shape
    k = a.shape[2]
    tm = _row_tile(rows)
    kern = functools.partial(_proj_res_kernel, final_norm=final_norm)
    return pl.pallas_call(
        kern,
        grid=(bq, rows // tm),
        in_specs=[pl.BlockSpec((None, tm, k), lambda b, i: (b, i, 0)),
                  pl.BlockSpec((k, d), lambda b, i: (0, 0))] + _res_specs(tm, d, gate),
        out_specs=pl.BlockSpec((None, tm, d), lambda b, i: (b, i, 0)),
        out_shape=jax.ShapeDtypeStruct((bq, rows, d), F32),
        compiler_params=_params("arbitrary", "arbitrary"),
        name="proj_residual",
    )(a, w, x, gate, gfin)


def _attn_merge_proj_call(outs, lses, expand, w, x, gate, gfin, final_norm, unperms=None):
    bq, rows, d = x.shape
    grouped = unperms is not None
    tm = A_ROW_TILE if grouped else _row_tile(rows)
    kern = functools.partial(_attn_merge_proj_kernel, final_norm=final_norm, grouped=grouped)
    o_spec = pl.BlockSpec((None, tm, d), lambda b, i: (b, i, 0))
    l_spec = pl.BlockSpec((None, tm, LANES), lambda b, i: (b, i, 0))
    extra_specs, extra = [], []
    if grouped:
        extra_specs.append(pl.BlockSpec(unperms.shape, lambda b, i: (0, 0, 0)))
        extra.append(unperms)
    return pl.pallas_call(
        kern,
        grid=(bq, rows // tm),
        in_specs=[o_spec] * 3 + [l_spec] * 3 + [
            pl.BlockSpec(expand.shape, lambda b, i: (0, 0)),
            pl.BlockSpec(w.shape, lambda b, i: (0, 0))] + _res_specs(tm, d, gate) + extra_specs,
        out_specs=pl.BlockSpec((None, tm, d), lambda b, i: (b, i, 0)),
        out_shape=jax.ShapeDtypeStruct((bq, rows, d), F32),
        compiler_params=_params("arbitrary", "arbitrary"),
        name="attn_merge_proj_residual",
    )(*outs, *lses, expand, w, x, gate, gfin, *extra)


def _ffn_down_call(gate_pre, state8, val, conv_w, conv_b, w, x, mgate, gfin, final_norm):
    bq, rows, d = x.shape
    f = gate_pre.shape[2]
    tm = min(rows, 256)
    kern = functools.partial(_ffn_down_kernel, final_norm=final_norm)
    blocks_per_tile = tm // SUBLANES
    return pl.pallas_call(
        kern,
        grid=(bq, rows // tm),
        in_specs=[
            pl.BlockSpec((None, tm, f), lambda b, i: (b, i, 0)),
            pl.BlockSpec((None, SUBLANES, f),
                         lambda b, i: (b, jnp.maximum(i * blocks_per_tile - 1, 0), 0)),
            pl.BlockSpec((None, SUBLANES, f), lambda b, i: (b, 0, 0)),
            pl.BlockSpec((None, tm, f), lambda b, i: (b, i, 0)),
            pl.BlockSpec(conv_w.shape, lambda b, i: (0, 0)),
            pl.BlockSpec((1, f), lambda b, i: (0, 0)),
            pl.BlockSpec((f, d), lambda b, i: (0, 0)),
        ] + _res_specs(tm, d, mgate),
        out_specs=pl.BlockSpec((None, tm, d), lambda b, i: (b, i, 0)),
        out_shape=jax.ShapeDtypeStruct((bq, rows, d), F32),
        scratch_shapes=[pltpu.VMEM((tm + SUBLANES, f), F32)],
        compiler_params=_params("arbitrary", "arbitrary"),
        name="ffn_conv_down_residual",
    )(gate_pre, gate_pre, state8, val, conv_w, conv_b, w, x, mgate, gfin)


def _ffn_fused_kernel(x_ref, g_ref, sc_ref, sh_ref, wup_ref, state_ref, cw_ref, cb_ref, wdn_ref,
                      mgate_ref, gfin_ref, o_ref, gtail_ref, carry_scr, *, final_norm, col_tile):
    tm = x_ref.shape[0]
    d_ff = wdn_ref.shape[0]
    width = cw_ref.shape[0]

    @pl.when(pl.program_id(1) == 0)
    def _():
        carry_scr[...] = state_ref[...]

    h = _norm_mod(x_ref[...], g_ref[...], sc_ref[...], sh_ref[...]).astype(BF16)

    def up(c0):
        return (_dot(h, wup_ref[:, c0:c0 + col_tile]),
                _dot(h, wup_ref[:, d_ff + c0:d_ff + c0 + col_tile]))

    acc = None
    ahead = up(0)
    for c0 in range(0, d_ff, col_tile):
        cols = slice(c0, c0 + col_tile)
        gate, val = ahead
        if c0 + col_tile < d_ff:
            ahead = up(c0 + col_tile)
        ext = jnp.concatenate([carry_scr[:, cols], gate], axis=0)
        y = cb_ref[:, cols] + cw_ref[width - 1:width, cols] * gate
        for back in range(1, width):
            y = y + (cw_ref[width - 1 - back:width - back, cols]
                     * pltpu.roll(ext, back, 0)[SUBLANES:SUBLANES + tm])
        act = (_silu(y) * val).astype(BF16)
        part = _dot(act, wdn_ref[cols, :])
        acc = part if acc is None else acc + part
        carry_scr[:, cols] = gate[tm - SUBLANES:tm]
        gtail_ref[:, cols] = gate[tm - SUBLANES:tm]
    _finish(acc, x_ref, mgate_ref, gfin_ref, o_ref, final_norm)


def _ffn_fused_call(x, g, sc, sh, w_up, state8, conv_w, conv_b, w_down, mgate, gfin, final_norm):
    bq, rows, d = x.shape
    d_ff = w_down.shape[0]
    tm = _row_tile(rows)
    tiles = rows // tm
    col_tile = 2 * LANES
    assert d_ff % col_tile == 0
    kern = functools.partial(_ffn_fused_kernel, final_norm=final_norm, col_tile=col_tile)
    return pl.pallas_call(
        kern,
        grid=(bq, tiles),
        in_specs=_nm_common_specs(sc, tm, d) + [
            _resident(w_up.shape),
            pl.BlockSpec((None, SUBLANES, d_ff), lambda b, i: (b, 0, 0)),
            _resident(conv_w.shape),
            _resident(conv_b.shape),
            _resident(w_down.shape),
        ] + _res_specs(tm, d, mgate)[1:],
        out_specs=[pl.BlockSpec((None, tm, d), lambda b, i: (b, i, 0)),
                   pl.BlockSpec((None, None, SUBLANES, d_ff), lambda b, i: (b, i, 0, 0))],
        out_shape=[jax.ShapeDtypeStruct((bq, rows, d), F32),
                   jax.ShapeDtypeStruct((bq, tiles, SUBLANES, d_ff), F32)],
        scratch_shapes=[pltpu.VMEM((SUBLANES, d_ff), F32)],
        compiler_params=_params("arbitrary", "arbitrary"),
        name="conv_ffn",
    )(x, g, sc, sh, w_up, state8, conv_w, conv_b, w_down, mgate, gfin)


def _attn_prompt_kernel(q_ref, kp_ref, kc_ref, vp_ref, vc_ref, o_ref, lse_ref):
    blk = kp_ref.shape[0] * kp_ref.shape[1]
    prow = q_ref.shape[1]
    run = q_ref.shape[0] * prow
    nq = run // blk

    def rows_of(ref, cols):
        return jnp.concatenate([ref[p, :, cols] for p in range(ref.shape[0])], axis=0)

    n = pl.program_id(2)
    qi = lax.broadcasted_iota(jnp.int32, (blk, 2 * blk), 0)
    kj = lax.broadcasted_iota(jnp.int32, (blk, 2 * blk), 1)
    band = (kj >= qi) & (kj <= qi + blk)
    first = band & ((kj >= blk) | (n > 0))
    lane = lax.broadcasted_iota(jnp.int32, (blk, LANES), 1)
    low = lane < A_HEAD_DIM
    lse_all = [jnp.zeros((blk, LANES), F32) for _ in range(nq)]
    for hp in range(A_HEADS // 2):
        cols = slice(hp * LANES, (hp + 1) * LANES)
        q_run = rows_of(q_ref, cols)
        k_run = jnp.concatenate([rows_of(kp_ref, cols), rows_of(kc_ref, cols)], axis=0)
        v_run = jnp.concatenate([rows_of(vp_ref, cols), rows_of(vc_ref, cols)], axis=0)
        outs = []
        for j in range(nq):
            q2 = q_run[j * blk:(j + 1) * blk]
            k2 = k_run[j * blk:(j + 2) * blk]
            v2 = v_run[j * blk:(j + 2) * blk]
            valid = first if j == 0 else band
            pair = []
            for s in range(2):
                mine = low if s == 0 else jnp.logical_not(low)
                qh = jnp.where(mine, q2, jnp.zeros_like(q2))
                sc = jnp.where(valid, _dot_nt(qh, k2), NEG_BIG)
                m = jnp.max(sc, axis=1, keepdims=True)
                p = jnp.exp2(sc - m)
                l = jnp.sum(p, axis=1, keepdims=True)
                pair.append(_dot(p.astype(BF16), v2) / l)
                lse_all[j] = jnp.where(lane == 2 * hp + s, (m + jnp.log2(l)) * LN2, lse_all[j])
            outs.append(jnp.where(low, pair[0], pair[1]).astype(o_ref.dtype))
        o_run = jnp.concatenate(outs, axis=0)
        for p in range(o_ref.shape[0]):
            o_ref[p, :, cols] = o_run[p * prow:(p + 1) * prow]
    lse_run = jnp.concatenate(lse_all, axis=0)
    for p in range(lse_ref.shape[0]):
        lse_ref[p] = lse_run[p * prow:(p + 1) * prow]


def _attn_prompt_call(qkv, gi, win, dil):
    b, nkinds, tiles, tm, width = qkv.shape
    per_res = tm // dil
    qkv_g = qkv.reshape(b, nkinds, tiles, dil, per_res, width)
    blk = win // dil
    assert (tiles * tm) % win == 0
    nb = tiles * per_res // blk
    nq = min(A_BLOCKS_PER_STEP, nb)
    assert nb % nq == 0

    def geometry(rows):
        prow = min(rows, per_res)
        return rows // prow, prow

    def spec_of(rows, block_of_step, kind, lanes=width, lead=True):
        pieces, prow = geometry(rows)
        per_tile = per_res // prow

        def index(bb, r, i):
            j = jnp.maximum(block_of_step(i), 0)
            tail = (lax.div(j, per_tile), r, lax.rem(j, per_tile), 0)
            return (bb, 3 * gi + kind) + tail if lead else (bb,) + tail

        shape = (None, pieces, None, prow, lanes)
        return pl.BlockSpec((None,) + shape if lead else shape, index)

    cur = lambda kind: spec_of(nq * blk, lambda i: i, kind)
    prev = lambda kind: spec_of(blk, lambda i: nq * i - 1, kind)
    return pl.pallas_call(
        _attn_prompt_kernel,
        grid=(b, dil, nb // nq),
        in_specs=[cur(0), prev(1), cur(1), prev(2), cur(2)],
        out_specs=[spec_of(nq * blk, lambda i: i, 0, width, False),
                   spec_of(nq * blk, lambda i: i, 0, LANES, False)],
        out_shape=[jax.ShapeDtypeStruct((b, tiles, dil, per_res, width), BF16),
                   jax.ShapeDtypeStruct((b, tiles, dil, per_res, LANES), F32)],
        compiler_params=_params("arbitrary", "arbitrary", "arbitrary"),
        name="dilated_attn_prompt",
    )(qkv_g, qkv_g, qkv_g, qkv_g, qkv_g)


def _attn_sample_kernel(q_ref, kn_ref, vn_ref, kc_ref, vc_ref,
                        ko_ref, vo_ref, o_ref, lse_ref,
                        m_scr, l_scr, acc_scr, kcar_scr, vcar_scr, *, lb, dil):
    t_new = q_ref.shape[0]
    width = q_ref.shape[1]
    chunk = kc_ref.shape[1]
    rows = A_HEADS * t_new
    step = pl.program_id(1)
    c = pl.num_programs(1) - 1 - step

    rq = lax.broadcasted_iota(jnp.int32, (rows, width), 0)
    lq = lax.broadcasted_iota(jnp.int32, (rows, width), 1)
    own = _div_pow2(lq, A_HEAD_DIM) == _div_pow2(rq, t_new)
    qbd = jnp.where(own, jnp.concatenate([q_ref[...]] * A_HEADS, axis=0), 0.0).astype(BF16)

    def accumulate(keys_t, vals_t, valid):
        sc = jnp.where(valid, _dot(qbd, keys_t.astype(BF16)), NEG_BIG)
        m_old = m_scr[...]
        m_new = jnp.maximum(m_old, jnp.max(sc, axis=1, keepdims=True))
        alpha = jnp.exp2(m_old - m_new)
        p = jnp.where(valid, jnp.exp2(sc - m_new), 0.0)
        l_scr[...] = alpha * l_scr[...] + jnp.sum(p, axis=1, keepdims=True)
        acc_scr[...] = alpha * acc_scr[...] + _dot_nt(p.astype(BF16), vals_t.astype(BF16))
        m_scr[...] = m_new

    @pl.when(step == 0)
    def _():
        m_scr[...] = jnp.full_like(m_scr, NEG_BIG)
        l_scr[...] = jnp.zeros_like(l_scr)
        acc_scr[...] = jnp.zeros_like(acc_scr)
        pad = jnp.zeros((LANES - t_new, width), F32)
        kcar_scr[...] = jnp.concatenate([kn_ref[...], pad], axis=0).T
        vcar_scr[...] = jnp.concatenate([vn_ref[...], pad], axis=0).T
        tq2 = _mod_pow2(lax.broadcasted_iota(jnp.int32, (rows, LANES), 0), t_new)
        tk2 = lax.broadcasted_iota(jnp.int32, (rows, LANES), 1)
        d2 = tq2 - tk2
        accumulate(kcar_scr[...], vcar_scr[...],
                   (tk2 < t_new) & (d2 >= 0) & (_mod_pow2(d2, dil) == 0))

    lane = lax.broadcasted_iota(jnp.int32, (width, LANES), 1)
    keep = lane < LANES - t_new
    nblk = chunk // LANES
    for src_ref, car_ref, dst_ref in ((kc_ref, kcar_scr, ko_ref), (vc_ref, vcar_scr, vo_ref)):
        nxt = pltpu.roll(car_ref[...], LANES - t_new, 1)
        for b in reversed(range(nblk)):
            cur = pltpu.roll(src_ref[:, b * LANES:(b + 1) * LANES], LANES - t_new, 1)
            dst_ref[:, b * LANES:(b + 1) * LANES] = jnp.where(keep, cur, nxt)
            nxt = cur
        car_ref[...] = src_ref[:, 0:LANES]

    tq = _mod_pow2(lax.broadcasted_iota(jnp.int32, (rows, chunk), 0), t_new)
    ik = lax.broadcasted_iota(jnp.int32, (rows, chunk), 1) + c * chunk
    dist = lb + tq - ik
    accumulate(kc_ref[...], vc_ref[...], (_mod_pow2(dist, dil) == 0) & (dist <= lb))

    @pl.when(c == 0)
    def _():
        full = jnp.where(own, acc_scr[...] / l_scr[...], 0.0)
        o_ref[...] = jnp.sum(full.reshape(A_HEADS, t_new, width), axis=0).astype(o_ref.dtype)
        lse_col = (m_scr[...] + jnp.log2(l_scr[...])) * LN2
        rl = lax.broadcasted_iota(jnp.int32, (rows, LANES), 0)
        ll = lax.broadcasted_iota(jnp.int32, (rows, LANES), 1)
        spread = jnp.where(ll == _div_pow2(rl, t_new), lse_col, 0.0)
        lse_ref[...] = jnp.sum(spread.reshape(A_HEADS, t_new, LANES), axis=0)


def _attn_sample_call(proj, gi, win, dil, k_cache_t, v_cache_t):
    db, t_new, n = proj.shape
    width = A_HEADS * A_HEAD_DIM
    lb = k_cache_t.shape[2]
    assert lb == win and t_new == SUBLANES
    chunk = min(lb, 512)
    nchunk = lb // chunk
    kern = functools.partial(_attn_sample_kernel, lb=lb, dil=dil)
    new_spec = lambda kind: pl.BlockSpec((None, t_new, width), lambda b, s: (b, 0, 3 * gi + kind))
    cur = pl.BlockSpec((None, width, chunk), lambda b, s: (b, 0, nchunk - 1 - s))
    return pl.pallas_call(
        kern,
        grid=(db, nchunk),
        in_specs=[new_spec(0), new_spec(1), new_spec(2), cur, cur],
        out_specs=[cur, cur,
                   pl.BlockSpec((None, t_new, width), lambda b, s: (b, 0, 0)),
                   pl.BlockSpec((None, t_new, LANES), lambda b, s: (b, 0, 0))],
        out_shape=[jax.ShapeDtypeStruct((db, width, lb), F32),
                   jax.ShapeDtypeStruct((db, width, lb), F32),
                   jax.ShapeDtypeStruct((db, t_new, width), BF16),
                   jax.ShapeDtypeStruct((db, t_new, LANES), F32)],
        scratch_shapes=[pltpu.VMEM((A_HEADS * t_new, 1), F32),
                        pltpu.VMEM((A_HEADS * t_new, 1), F32),
                        pltpu.VMEM((A_HEADS * t_new, width), F32),
                        pltpu.VMEM((width, LANES), F32),
                        pltpu.VMEM((width, LANES), F32)],
        compiler_params=_params("arbitrary", "arbitrary"),
        name="dilated_attn_sample",
    )(proj, proj, proj, k_cache_t, v_cache_t)


def _delta_kernel(*refs, t_valid, activated):
    if activated:
        p_ref, gates_ref, ng_ref, s0_ref, o_ref, ssm_ref, s_scr = refs
    else:
        (p_ref, prev_ref, gates_ref, cstate_ref, cw_ref, alog_ref, dtb_ref, ng_ref, s0_ref,
         o_ref, ssm_ref, cnew_ref, s_scr, ext_scr) = refs
    cl = B_CHUNK
    rows_in = p_ref.shape[0]
    nsub = max(rows_in // cl, 1)
    rows_pad = nsub * cl
    qk_width = B_QK_HEADS * B_HEAD
    conv_dim = 2 * qk_width + B_V_HEADS * B_HEAD
    c = pl.program_id(1)
    last = c == pl.num_programs(1) - 1

    @pl.when(c == 0)
    def _():
        s_scr[...] = s0_ref[...]

    if activated:
        assert t_valid == rows_pad
        qkv = p_ref[:, 0:conv_dim].astype(F32)
        qn, kn = qkv[:, 0:qk_width], qkv[:, qk_width:2 * qk_width]
        beta_all = g_all = gates_ref[...]
    else:
        qkv, beta_all, g_all = _delta_front(p_ref, prev_ref, gates_ref, cstate_ref, cw_ref, alog_ref,
                                            dtb_ref, cnew_ref, ext_scr, rows_pad, t_valid)
        qn = _l2n_heads(qkv[:, 0:qk_width]) * (B_HEAD ** -0.5)
        kn = _l2n_heads(qkv[:, qk_width:2 * qk_width])
    _delta_chunks(qkv, qn, kn, beta_all, g_all, p_ref, ng_ref, o_ref, s_scr, nsub)

    @pl.when(last)
    def _():
        ssm_ref[...] = s_scr[...]


def _delta_front(p_ref, prev_ref, gates_ref, cstate_ref, cw_ref, alog_ref, dtb_ref, cnew_ref, ext_scr,
                 rows_pad, t_valid):
    rows_in = p_ref.shape[0]
    conv_dim = cw_ref.shape[1]
    c = pl.program_id(1)
    last = c == pl.num_programs(1) - 1

    halo = prev_ref.shape[0]
    left = prev_ref[:, 0:conv_dim].astype(F32)
    if halo > SUBLANES:
        ext_scr[0:halo - SUBLANES, :] = left[0:halo - SUBLANES]
    ext_scr[halo - SUBLANES:halo, :] = jnp.where(c == 0, cstate_ref[...], left[halo - SUBLANES:halo])
    ext_scr[halo:halo + rows_in, :] = p_ref[:, 0:conv_dim].astype(F32)
    if rows_in < rows_pad:
        ext_scr[halo + rows_in:halo + rows_pad, :] = jnp.zeros((rows_pad - rows_in, conv_dim), F32)
    width = cw_ref.shape[0]
    ext = ext_scr[...]
    y = cw_ref[width - 1:width, :] * ext[halo:halo + rows_pad]
    for back in range(1, width):
        shifted = pltpu.roll(ext, back, 0)[halo:halo + rows_pad]
        y = y + cw_ref[width - 1 - back:width - back, :] * shifted
    qkv = _silu(y)
    if t_valid < rows_pad:
        rmask = lax.broadcasted_iota(jnp.int32, (rows_pad, 1), 0) < t_valid
        qkv = jnp.where(rmask, qkv, 0.0)

    @pl.when(last)
    def _():
        cnew_ref[...] = ext_scr[halo + t_valid - SUBLANES:halo + t_valid, :]

    graw = gates_ref[...]
    if rows_in < rows_pad:
        graw = jnp.concatenate([graw, jnp.zeros((rows_pad - rows_in, LANES), F32)], axis=0)
    beta_all, g_all = _delta_gates(graw, alog_ref[...], dtb_ref[...])
    if t_valid < rows_pad:
        beta_all = jnp.where(rmask, beta_all, 0.0)
        g_all = jnp.where(rmask, g_all, 0.0)
    return qkv, beta_all, g_all


def _delta_chunks(qkv, qn, kn, beta_all, g_all, p_ref, ng_ref, o_ref, s_scr, nsub):
    cl = B_CHUNK
    rows_in = p_ref.shape[0]
    qk_width = B_QK_HEADS * B_HEAD
    conv_dim = 2 * qk_width + B_V_HEADS * B_HEAD

    def rows_of(sub):
        return slice(sub * cl, (sub + 1) * cl)

    ri = lax.broadcasted_iota(jnp.int32, (2 * cl, cl), 0)
    ci = lax.broadcasted_iota(jnp.int32, (2 * cl, cl), 1)
    summat = jnp.where((ci <= ri) | (ri >= cl), 1.0, 0.0).astype(BF16)
    cg_all, gl_all = [], []
    for sub in range(nsub):
        sums = _dot(summat, jnp.concatenate(_split3(g_all[rows_of(sub)]), axis=1))
        sums = sums[:, 0:LANES] + sums[:, LANES:2 * LANES] + sums[:, 2 * LANES:3 * LANES]
        cg_all.append(sums[0:cl])
        gl_all.append(sums[cl:2 * cl])

    nst = B_STACK * cl
    rr = lax.broadcasted_iota(jnp.int32, (nst, nst), 0)
    cc = lax.broadcasted_iota(jnp.int32, (nst, nst), 1)
    same = _div_pow2(rr, cl) == _div_pow2(cc, cl)
    incl = same & (cc <= rr)
    strict = same & (cc < rr)
    lane6 = lax.broadcasted_iota(jnp.int32, (nst, LANES), 1)
    rblk = _div_pow2(lax.broadcasted_iota(jnp.int32, (nst, B_HEAD), 0), cl)

    def col_stack(arr, lane0, heads):
        return jnp.concatenate([arr[:, lane0 + h:lane0 + h + 1] for h in heads], axis=0)

    groups = [list(range(g * B_STACK, (g + 1) * B_STACK)) for g in range(B_V_HEADS // B_STACK)]
    problems = [(sub, hs) for sub in range(nsub) for hs in groups]
    every = range(len(problems))
    rep = B_V_HEADS // B_QK_HEADS

    def head_stack(arr, col0, sub, heads):
        return jnp.concatenate([arr[rows_of(sub), col0 + h * B_HEAD:col0 + (h + 1) * B_HEAD]
                                for h in heads], axis=0)

    q_st = [head_stack(qn, 0, sub, [h // rep for h in hs]) for sub, hs in problems]
    k_st = [head_stack(kn, 0, sub, [h // rep for h in hs]) for sub, hs in problems]
    v_st = [head_stack(qkv, 2 * qk_width, sub, hs) for sub, hs in problems]
    beta = [col_stack(beta_all[rows_of(sub)], 0, hs) for sub, hs in problems]
    cg = [col_stack(cg_all[sub], B_V_HEADS, hs) for sub, hs in problems]
    gl = [col_stack(gl_all[sub], B_V_HEADS, hs) for sub, hs in problems]

    def diff_operands(c):
        c_hi, c_mid, c_lo = (p.astype(F32) for p in _split3(c))
        lmat = jnp.where(lane6 == 0, c_hi, jnp.where(lane6 == 1, c_mid, jnp.where(
            lane6 == 2, c_lo, jnp.where(lane6 < 6, 1.0, 0.0))))
        rmat = jnp.where(lane6 == 3, -c_hi, jnp.where(lane6 == 4, -c_mid, jnp.where(
            lane6 == 5, -c_lo, jnp.where(lane6 < 3, 1.0, 0.0))))
        return lmat.astype(BF16), rmat.astype(BF16)

    diff = [_dot_nt(*diff_operands(cg[g])) for g in every]
    k_bf = [k_st[g].astype(BF16) for g in every]
    gram = [_dot_nt(k_bf[g], k_bf[g]) for g in every]
    qk = [_dot_nt(q_st[g].astype(BF16), k_bf[g]) for g in every]
    decay = [jnp.where(incl, jnp.exp(jnp.where(incl, diff[g], 0.0)), 0.0) for g in every]
    a_mat = [jnp.where(strict, beta[g] * gram[g] * decay[g], 0.0) for g in every]
    p_mat = [(qk[g] * decay[g]).astype(BF16) for g in every]

    blk = _div_pow2(rr, B_INV_BASE) == _div_pow2(cc, B_INV_BASE)
    apow = [jnp.where(blk, a_mat[g], 0.0) for g in every]
    nmat = [-apow[g] for g in every]
    for _ in range((B_INV_BASE - 1).bit_length() - 1):
        ab = [apow[g].astype(BF16) for g in every]
        apow = [_dot(ab[g], ab[g]) for g in every]
        nmat = [nmat[g] + apow[g] + _dot(nmat[g].astype(BF16), apow[g].astype(BF16)) for g in every]
    size = B_INV_BASE
    while size < cl:
        size *= 2
        merged = _div_pow2(rr, size) == _div_pow2(cc, size)
        join = merged & jnp.logical_not(blk)
        low = [jnp.where(join, a_mat[g], 0.0) for g in every]
        nb = [nmat[g].astype(BF16) for g in every]
        dl = [low[g] + _dot(nb[g], low[g].astype(BF16)) for g in every]
        nmat = [nmat[g] - (dl[g] + _dot(dl[g].astype(BF16), nb[g])) for g in every]
        blk = merged
    ecg = [jnp.exp(cg[g]) for g in every]
    rhs = [jnp.concatenate([beta[g] * v_st[g], (beta[g] * ecg[g]) * k_st[g]], axis=1) for g in every]
    sol = [rhs[g] + _dot(nmat[g].astype(BF16), rhs[g].astype(BF16)) for g in every]
    qg = [q_st[g] * ecg[g] for g in every]
    kdec_t = [(k_st[g] * jnp.exp(gl[g] - cg[g])).T.astype(BF16) for g in every]

    live = min(rows_in, cl)
    for sub in range(nsub):
        mine = [g for g in every if problems[g][0] == sub]
        both = {}
        for g in mine:
            for idx, h in enumerate(problems[g][1]):
                rsl = slice(idx * cl, (idx + 1) * cl)
                lhs = jnp.concatenate([sol[g][rsl, B_HEAD:2 * B_HEAD], qg[g][rsl]], axis=0)
                both[h] = _dot(lhs.astype(BF16), s_scr[h].astype(BF16))
        u_st = {g: jnp.concatenate([sol[g][idx * cl:(idx + 1) * cl, 0:B_HEAD] - both[h][0:cl]
                                    for idx, h in enumerate(problems[g][1])], axis=0) for g in mine}
        o_st = {g: jnp.concatenate([both[h][cl:2 * cl] for h in problems[g][1]], axis=0)
                + _dot(p_mat[g], u_st[g].astype(BF16)) for g in mine}
        for g in mine:
            for idx, h in enumerate(problems[g][1]):
                u_own = jnp.where(rblk == idx, u_st[g], 0.0).astype(BF16)
                s_scr[h] = jnp.exp(gl[g][idx * cl:idx * cl + 1]) * s_scr[h] + _dot(kdec_t[g], u_own)
        out_rows = slice(sub * cl, sub * cl + live)
        for g in mine:
            for idx, h in enumerate(problems[g][1]):
                o_h = o_st[g][idx * cl:idx * cl + live]
                ms = jnp.mean(o_h * o_h, axis=1, keepdims=True)
                z_h = p_ref[out_rows, conv_dim + h * B_HEAD:conv_dim + (h + 1) * B_HEAD].astype(F32)
                res = o_h * lax.rsqrt(ms + NORM_EPS) * ng_ref[...] * _silu(z_h)
                o_ref[out_rows, h * B_HEAD:(h + 1) * B_HEAD] = res.astype(o_ref.dtype)


def _delta_call(proj, gates, norm_g, s0, raw=None):
    bq, t, n = proj.shape
    v_width = B_V_HEADS * B_HEAD
    rows_in = min(t, B_CHUNK * B_CHUNKS_PER_STEP)
    assert t % rows_in == 0 and (rows_in % B_CHUNK == 0 or rows_in < B_CHUNK)
    nchunk = t // rows_in
    activated = raw is None
    kern = functools.partial(_delta_kernel, t_valid=rows_in, activated=activated)
    row_spec = lambda width: pl.BlockSpec((None, rows_in, width), lambda b, c: (b, c, 0))
    state_spec = pl.BlockSpec((None, B_V_HEADS, B_HEAD, B_HEAD), lambda b, c: (b, 0, 0, 0))
    const_spec = lambda a: pl.BlockSpec(a.shape, lambda b, c: (0,) * a.ndim)
    out_specs = [row_spec(v_width), state_spec]
    out_shape = [jax.ShapeDtypeStruct((bq, t, v_width), BF16),
                 jax.ShapeDtypeStruct((bq, B_V_HEADS, B_HEAD, B_HEAD), F32)]
    scratch = [pltpu.VMEM((B_V_HEADS, B_HEAD, B_HEAD), F32)]
    if activated:
        in_specs = [row_spec(n), row_spec(LANES), const_spec(norm_g), state_spec]
        args = (proj, gates, norm_g, s0)
    else:
        cstate8, conv_w, alog_l, dtb_l = raw
        conv_dim = conv_w.shape[1]
        halo = SUBLANES * (4 // proj.dtype.itemsize)
        per_halo = max(rows_in // halo, 1)
        in_specs = [
            row_spec(n),
            pl.BlockSpec((None, halo, n), lambda b, c: (b, jnp.maximum(c * per_halo - 1, 0), 0)),
            row_spec(LANES),
            pl.BlockSpec((None, SUBLANES, conv_dim), lambda b, c: (b, 0, 0)),
            const_spec(conv_w), const_spec(alog_l), const_spec(dtb_l), const_spec(norm_g), state_spec]
        args = (proj, proj, gates, cstate8, conv_w, alog_l, dtb_l, norm_g, s0)
        out_specs.append(pl.BlockSpec((None, SUBLANES, conv_dim), lambda b, c: (b, 0, 0)))
        out_shape.append(jax.ShapeDtypeStruct((bq, SUBLANES, conv_dim), F32))
        scratch.append(pltpu.VMEM((max(rows_in, B_CHUNK) + halo, conv_dim), F32))
    return pl.pallas_call(
        kern,
        grid=(bq, nchunk),
        in_specs=in_specs,
        out_specs=out_specs,
        out_shape=out_shape,
        scratch_shapes=scratch,
        compiler_params=_params("arbitrary", "arbitrary"),
        name="gated_deltanet",
    )(*args)


def _left_rows(state):
    return jnp.pad(state, ((0, 0), (SUBLANES - state.shape[1], 0), (0, 0)))


def _rope_tables(pos):
    half = A_HEAD_DIM // 2
    inv = ROPE_THETA ** (-jnp.arange(half, dtype=F32) / half)
    ang = pos.astype(F32)[:, None] * inv[None, :]
    cos, sin = jnp.cos(ang), jnp.sin(ang)
    reps = LANES // A_HEAD_DIM
    cos_t = jnp.tile(jnp.concatenate([cos, cos], axis=1), (1, reps))
    sin_t = jnp.tile(jnp.concatenate([-sin, sin], axis=1), (1, reps))
    return cos_t[None], sin_t[None]


def _trunk(x, mods, per_row, pos, a_caches, b_ssm, b_conv, f_conv, wts):
    seqs, t, d = x.shape
    (norm_mix_g, norm_ffn_g, norm_final_g, a_w_in, a_w_out, b_w_in, b_conv_w, b_a_log, b_dt_bias,
     b_norm_g, b_w_out, ffn_w_up, ffn_conv_w, ffn_conv_b, ffn_w_down, expand) = wts
    d_ff = ffn_w_down.shape[1]
    width = A_HEADS * A_HEAD_DIM

    def rows_view(a):
        return a.reshape(1, seqs * t, a.shape[2]) if per_row else a

    def seq_view(a):
        return a.reshape(seqs, t, a.shape[2])

    def mod_view(v):
        if per_row:
            return jnp.repeat(v, t, axis=0)[None]
        return v[:, None, :]

    cos_t, sin_t = _rope_tables(pos)
    prompt = a_caches is None
    if prompt:
        assert t % max(w for w, _ in A_GROUPS) == 0
        perms = _perm_matrices()
        unperms = jnp.swapaxes(perms, 1, 2)
    if per_row:
        cos_t, sin_t = jnp.tile(cos_t, (1, seqs, 1)), jnp.tile(sin_t, (1, seqs, 1))
    gfin = norm_final_g.reshape(1, d)
    xr = rows_view(x)
    new_a, new_ssm, new_bconv, new_fconv = [], None, None, []
    depth = norm_mix_g.shape[0]
    for layer in range(depth):
        sh1, sc1, g1, sh2, sc2, g2 = (mod_view(v) for v in mods[layer])
        gmix = norm_mix_g[layer].reshape(1, d)
        if layer % 2 == 0:
            outs, lses = [], []
            if prompt:
                tail_rows = max(w for w, _ in A_GROUPS)
                qkv, tail = _nm_rope_call(xr, gmix, sc1, sh1, a_w_in, cos_t, sin_t, tail_rows, perms)
                for gi, (win, dil) in enumerate(A_GROUPS):
                    o_g, lse_g = _attn_prompt_call(qkv, gi, win, dil)
                    outs.append(o_g.reshape(seqs, t, width))
                    lses.append(lse_g.reshape(seqs, t, LANES))
                    for kind in (1, 2):
                        col = slice((3 * gi + kind) * width, (3 * gi + kind + 1) * width)
                        keep = _natural_order(tail[:, :, col], dil)[:, tail_rows - win:]
                        new_a.append(keep.reshape(1, seqs, win, A_HEADS, A_HEAD_DIM))
                xr = _attn_merge_proj_call(outs, lses, expand, a_w_out[0], xr, g1, gfin, False,
                                           unperms)
            else:
                (tail,) = _nm_rope_call(xr, gmix, sc1, sh1, a_w_in, cos_t, sin_t, seqs * t)
                for gi, (win, dil) in enumerate(A_GROUPS):
                    as_t = lambda c: jnp.transpose(c[0], (0, 2, 3, 1)).reshape(seqs, width, c.shape[2])
                    k_new, v_new, o_g, lse_g = _attn_sample_call(
                        seq_view(tail), gi, win, dil, as_t(a_caches[2 * gi]), as_t(a_caches[2 * gi + 1]))
                    outs.append(rows_view(o_g))
                    lses.append(rows_view(lse_g))
                    for buf in (k_new, v_new):
                        buf = buf.reshape(seqs, A_HEADS, A_HEAD_DIM, buf.shape[2])
                        new_a.append(jnp.transpose(buf, (0, 3, 1, 2))[None])
                xr = _attn_merge_proj_call(outs, lses, expand, a_w_out[0], xr, g1, gfin, False)
        else:
            conv_w = b_conv_w[0]
            conv_dim = conv_w.shape[1]
            conv0 = jnp.zeros((seqs, conv_w.shape[0] - 1, conv_dim), F32) if b_conv is None else b_conv[0]
            s0 = jnp.zeros((seqs, B_V_HEADS, B_HEAD, B_HEAD), F32) if b_ssm is None else b_ssm[0]
            lane_pad = lambda v: jnp.pad(v.reshape(1, -1), ((0, 0), (B_V_HEADS, LANES - 2 * B_V_HEADS)))
            alog_l, dtb_l = lane_pad(b_a_log[0]), lane_pad(b_dt_bias[0])
            norm_g = b_norm_g[0].reshape(1, B_HEAD)
            if prompt:
                proj, gates, conv_tail = _nm_delta_act_call(
                    xr, gmix, sc1, sh1, b_w_in[0], _left_rows(conv0), conv_w, alog_l, dtb_l)
                o_b, ssm1 = _delta_call(proj, gates, norm_g, s0)
                cnew = conv_tail[:, -1]
            else:
                proj, gates, _ = _nm_delta_in_call(
                    xr, gmix, sc1, sh1, b_w_in[0], conv_dim + B_V_HEADS * B_HEAD, conv_dim, F32)
                o_b, ssm1, cnew = _delta_call(seq_view(proj), seq_view(gates), norm_g, s0,
                                              (_left_rows(conv0), conv_w, alog_l, dtb_l))
            new_ssm = ssm1[None]
            new_bconv = cnew[None, :, SUBLANES - (conv_w.shape[0] - 1):, :]
            xr = _proj_res_call(rows_view(o_b), b_w_out[0], xr, g1, gfin, False)
        gffn = norm_ffn_g[layer].reshape(1, d)
        keep = ffn_conv_w.shape[1] - 1
        f0 = jnp.zeros((seqs, keep, d_ff), F32) if f_conv is None else f_conv[layer]
        conv_b = ffn_conv_b[layer].reshape(1, d_ff)
        mg2 = mods[layer][5][:, None, :]
        final = layer == depth - 1
        if prompt:
            xr, gate_tail = _ffn_fused_call(xr, gffn, sc2, sh2, ffn_w_up[layer], _left_rows(f0),
                                            ffn_conv_w[layer], conv_b, ffn_w_down[layer], mg2, gfin, final)
            new_fconv.append(gate_tail[:, -1, SUBLANES - keep:, :])
        else:
            gate_pre, val = _nm_ffn_call(xr, gffn, sc2, sh2, ffn_w_up[layer])
            gate_seq = seq_view(gate_pre)
            new_fconv.append(gate_seq[:, t - keep:, :])
            x_seq = _ffn_down_call(gate_seq, _left_rows(f0), seq_view(val), ffn_conv_w[layer],
                                   conv_b, ffn_w_down[layer], seq_view(xr), mg2, gfin, final)
            xr = rows_view(x_seq)
    return seq_view(xr), new_a, new_ssm, new_bconv, jnp.stack(new_fconv)


def kernel(x_prompt, x_sample, c_prompt, c_sample, cache_a_k_w128, cache_a_v_w128, cache_a_k_w512, cache_a_v_w512, cache_a_k_w2048, cache_a_v_w2048, state_b_ssm, state_b_conv, state_ffn_conv, norm_mix_g, norm_ffn_g, norm_final_g, w_mod, b_mod, a_w_in, a_w_out, b_w_in, b_conv_w, b_a_log, b_dt_bias, b_norm_g, b_w_out, ffn_w_up, ffn_conv_w, ffn_conv_b, ffn_w_down):
    bp, s, d = x_prompt.shape
    db, t_new, _ = x_sample.shape
    depth = w_mod.shape[0]

    c_all = jnp.concatenate([c_prompt, c_sample], axis=0)
    c_all = jnp.pad(c_all, ((0, (-c_all.shape[0]) % SUBLANES), (0, 0)))
    mod = _mod_call(c_all, w_mod, b_mod)

    def mods_of(lo, hi):
        return [[mod[l, lo:hi, k * d:(k + 1) * d] for k in range(6)] for l in range(depth)]

    b_w_in_p = jnp.pad(b_w_in, ((0, 0), (0, 0), (0, (-b_w_in.shape[2]) % LANES)))
    ngroups = len(A_GROUPS)
    a_w_in_g = jnp.swapaxes(a_w_in[0].astype(BF16).reshape(d, ngroups, -1), 0, 1)
    heads = jnp.arange(A_HEADS * A_HEAD_DIM) // A_HEAD_DIM
    half = (jnp.arange(LANES)[:, None] == heads[None, :]).astype(BF16)
    expand = jnp.concatenate([half, half], axis=0)
    wts = (norm_mix_g, norm_ffn_g, norm_final_g, a_w_in_g, a_w_out.astype(BF16),
           b_w_in_p.astype(BF16), b_conv_w, b_a_log, b_dt_bias, b_norm_g, b_w_out.astype(BF16),
           ffn_w_up.astype(BF16), ffn_conv_w, ffn_conv_b, ffn_w_down.astype(BF16), expand)

    pos_p = jnp.arange(s, dtype=jnp.int32)
    pos_s = PAST_LEN + jnp.arange(t_new, dtype=jnp.int32)
    y_p, a_p, ssm_p, bconv_p, fconv_p = _trunk(
        x_prompt, mods_of(0, bp), False, pos_p, None, None, None, None, wts)
    a_caches = [cache_a_k_w128, cache_a_v_w128, cache_a_k_w512, cache_a_v_w512,
                cache_a_k_w2048, cache_a_v_w2048]
    y_s, a_s, ssm_s, bconv_s, fconv_s = _trunk(
        x_sample, mods_of(bp, bp + db), True, pos_s, a_caches, state_b_ssm, state_b_conv,
        state_ffn_conv, wts)
    return (y_p, y_s,
            a_p[0], a_s[0], a_p[1], a_s[1], a_p[2], a_s[2], a_p[3], a_s[3], a_p[4], a_s[4],
            a_p[5], a_s[5], ssm_p, ssm_s, bconv_p, bconv_s, fconv_p, fconv_s)
```

```python
import functools

import jax
import jax.numpy as jnp
from jax import lax
from jax.experimental import pallas as pl
from jax.experimental.pallas import tpu as pltpu

F32 = jnp.float32
BF16 = jnp.bfloat16

PAST_LEN = 16384
A_GROUPS = ((128, 1), (512, 4), (2048, 16))
A_HEADS = 16
A_HEAD_DIM = 64
A_ROW_TILE = 512
A_BLOCKS_PER_STEP = 4
ROPE_THETA = 10000.0
NORM_EPS = 1e-6
B_HEAD = 128
B_V_HEADS = 16
B_QK_HEADS = 8
B_CHUNK = 64
B_CHUNKS_PER_STEP = 2
B_STACK = 2
B_INV_BASE = 16
NEG_BIG = -1e30
LOG2E = 1.4426950408889634
LN2 = 0.6931471805599453

LANES = 128
SUBLANES = 8
VMEM_LIMIT_BYTES = 56 * 1024 * 1024


def _params(*sem):
    return pltpu.CompilerParams(dimension_semantics=sem, vmem_limit_bytes=VMEM_LIMIT_BYTES)


def _silu(x):
    return x * jax.nn.sigmoid(x)


def _dot(a, b):
    return jnp.dot(a, b, preferred_element_type=F32)


def _dot_nt(a, b):
    return lax.dot_general(a, b, (((1,), (1,)), ((), ())), preferred_element_type=F32)


def _shift_of(n):
    assert n > 0 and n & (n - 1) == 0, n
    return n.bit_length() - 1


def _div_pow2(v, n):
    return lax.shift_right_logical(v, _shift_of(n))


def _mod_pow2(v, n):
    assert n & (n - 1) == 0
    return v & (n - 1)


def _split3(x):
    hi = x.astype(BF16)
    r1 = x - hi.astype(F32)
    mid = r1.astype(BF16)
    lo = (r1 - mid.astype(F32)).astype(BF16)
    return hi, mid, lo


def _mod_kernel(c_ref, w_ref, b_ref, o_ref):
    cs = _silu(c_ref[...]).astype(BF16)
    o_ref[...] = _dot(cs, w_ref[...].astype(BF16)) + b_ref[...]


def _mod_call(c_all, w_mod, b_mod):
    depth, d, n = w_mod.shape
    rows = c_all.shape[0]
    tn = 1536
    return pl.pallas_call(
        _mod_kernel,
        grid=(depth, n // tn),
        in_specs=[
            pl.BlockSpec((rows, d), lambda l, j: (0, 0)),
            pl.BlockSpec((None, d, tn), lambda l, j: (l, 0, j)),
            pl.BlockSpec((None, 1, tn), lambda l, j: (l, 0, j)),
        ],
        out_specs=pl.BlockSpec((None, rows, tn), lambda l, j: (l, 0, j)),
        out_shape=jax.ShapeDtypeStruct((depth, rows, n), F32),
        compiler_params=_params("arbitrary", "arbitrary"),
        name="adaln_mod",
    )(c_all, w_mod, b_mod.reshape(depth, 1, n))


def _norm_mod(x, g, sc, sh):
    ms = jnp.mean(x * x, axis=-1, keepdims=True)
    return x * lax.rsqrt(ms + NORM_EPS) * g * (1.0 + sc) + sh


def _rope_tile(acc, cos, sin):
    lane = lax.broadcasted_iota(jnp.int32, (acc.shape[0], LANES), 1)
    first_half = _mod_pow2(lane, A_HEAD_DIM) < (A_HEAD_DIM // 2)
    outs = []
    for c in range(acc.shape[1] // LANES):
        a = acc[:, c * LANES:(c + 1) * LANES]
        swapped = jnp.where(first_half,
                            pltpu.roll(a, LANES - A_HEAD_DIM // 2, 1),
                            pltpu.roll(a, A_HEAD_DIM // 2, 1))
        outs.append(a * cos + swapped * sin)
    return jnp.concatenate(outs, axis=1)


def _nm_delta_in_kernel(x_ref, g_ref, sc_ref, sh_ref, w_ref, main_ref, gates_ref, tail_ref,
                        *, col_tile):
    h = _norm_mod(x_ref[...], g_ref[...], sc_ref[...], sh_ref[...]).astype(BF16)
    tm, n_main = main_ref.shape
    for c0 in range(0, n_main, col_tile):
        acc = _dot(h, w_ref[:, c0:c0 + col_tile])
        main_ref[:, c0:c0 + col_tile] = acc.astype(main_ref.dtype)
        if c0 < tail_ref.shape[1]:
            tail_ref[:, c0:c0 + col_tile] = acc[tm - SUBLANES:tm]
    gates_ref[...] = _dot(h, w_ref[:, n_main:n_main + LANES])


def _delta_gates(logits, alog, dtb):
    xg = logits + dtb
    softplus = jnp.maximum(xg, 0.0) + jnp.log1p(jnp.exp(-jnp.abs(xg)))
    return jax.nn.sigmoid(logits), -jnp.exp(alog) * softplus


def _l2n_heads(arr):
    outs = []
    for h0 in range(0, arr.shape[1], B_HEAD):
        a = arr[:, h0:h0 + B_HEAD]
        outs.append(a * lax.rsqrt(jnp.sum(a * a, axis=1, keepdims=True) + NORM_EPS))
    return jnp.concatenate(outs, axis=1)


def _nm_delta_act_kernel(x_ref, g_ref, sc_ref, sh_ref, w_ref, cstate_ref, cw_ref, alog_ref, dtb_ref,
                         main_ref, gates_ref, tail_ref, carry_scr, *, col_tile):
    tm, n_main = main_ref.shape
    width, conv_dim = cw_ref.shape
    qk_width = B_QK_HEADS * B_HEAD

    @pl.when(pl.program_id(1) == 0)
    def _():
        carry_scr[...] = cstate_ref[...]

    h = _norm_mod(x_ref[...], g_ref[...], sc_ref[...], sh_ref[...]).astype(BF16)
    ahead = _dot(h, w_ref[:, 0:col_tile])
    for c0 in range(0, n_main, col_tile):
        cols = slice(c0, c0 + col_tile)
        acc = ahead
        if c0 + col_tile < n_main:
            ahead = _dot(h, w_ref[:, c0 + col_tile:c0 + 2 * col_tile])
        if c0 < conv_dim:
            ext = jnp.concatenate([carry_scr[:, cols], acc], axis=0)
            y = cw_ref[width - 1:width, cols] * acc
            for back in range(1, width):
                y = y + (cw_ref[width - 1 - back:width - back, cols]
                         * pltpu.roll(ext, back, 0)[SUBLANES:SUBLANES + tm])
            carry_scr[:, cols] = acc[tm - SUBLANES:tm]
            tail_ref[:, cols] = acc[tm - SUBLANES:tm]
            acc = _silu(y)
            if c0 < 2 * qk_width:
                acc = _l2n_heads(acc)
            if c0 < qk_width:
                acc = acc * (B_HEAD ** -0.5)
        main_ref[:, cols] = acc.astype(main_ref.dtype)
    beta, decay = _delta_gates(_dot(h, w_ref[:, n_main:n_main + LANES]), alog_ref[...], dtb_ref[...])
    lane = lax.broadcasted_iota(jnp.int32, beta.shape, 1)
    gates_ref[...] = jnp.where(lane < B_V_HEADS, beta, decay)


def _nm_rope_kernel(x_ref, g_ref, sc_ref, sh_ref, w_ref, cos_ref, sin_ref, *rest, grouped):
    if grouped:
        perm_ref, o_ref, tail_ref, h_scr, tab_scr = rest
    else:
        tail_ref, h_scr = rest
    group = pl.program_id(2)

    @pl.when(group == 0)
    def _():
        h = _norm_mod(x_ref[...], g_ref[...], sc_ref[...], sh_ref[...]).astype(BF16)
        h_scr[0] = h
        if grouped:
            tab_scr[0, 0] = cos_ref[...]
            tab_scr[0, 1] = sin_ref[...]
            parts = jnp.concatenate(_split3(cos_ref[...]) + _split3(sin_ref[...]), axis=1)
            for p in range(perm_ref.shape[0]):
                h_scr[p + 1] = _dot(perm_ref[p], h).astype(BF16)
                moved = _dot(perm_ref[p], parts)
                for tab in range(2):
                    lo = 3 * tab * LANES
                    tab_scr[p + 1, tab] = (moved[:, lo:lo + LANES] + moved[:, lo + LANES:lo + 2 * LANES]
                                           + moved[:, lo + 2 * LANES:lo + 3 * LANES])

    h = h_scr[group] if grouped else h_scr[0]
    width = A_HEADS * A_HEAD_DIM
    if grouped:
        cos, sin = tab_scr[group, 0], tab_scr[group, 1]
    else:
        cos, sin = cos_ref[...], sin_ref[...]
    qscale = A_HEAD_DIM ** -0.5 * LOG2E
    raw = [_dot(h, w_ref[group, :, kind * width:(kind + 1) * width]) for kind in range(3)]
    for kind in range(3):
        cols = slice(kind * width, (kind + 1) * width)
        val = raw[kind]
        if kind == 0:
            val = _rope_tile(val, cos * qscale, sin * qscale)
        elif kind == 1:
            val = _rope_tile(val, cos, sin)
        tail_ref[:, cols] = val
        if grouped:
            o_ref[kind] = val.astype(o_ref.dtype)


def _nm_ffn_kernel(x_ref, g_ref, sc_ref, sh_ref, w_ref, gate_ref, val_ref, *, col_tile):
    h = _norm_mod(x_ref[...], g_ref[...], sc_ref[...], sh_ref[...]).astype(BF16)
    d_ff = gate_ref.shape[1]
    for c0 in range(0, d_ff, col_tile):
        gate_ref[:, c0:c0 + col_tile] = _dot(h, w_ref[:, c0:c0 + col_tile])
        val_ref[:, c0:c0 + col_tile] = _dot(
            h, w_ref[:, d_ff + c0:d_ff + c0 + col_tile]).astype(val_ref.dtype)


def _row_tile(rows):
    return min(rows, 512)


def _resident(shape):
    zeros = (0,) * len(shape)
    return pl.BlockSpec(shape, lambda *_: zeros, pipeline_mode=pl.Buffered(1))


def _nm_common_specs(sc, tm, d):
    per_row = sc.shape[1] != 1
    mod_spec = (pl.BlockSpec((None, tm, d), lambda b, i, *_: (b, i, 0)) if per_row
                else pl.BlockSpec((None, 1, d), lambda b, i, *_: (b, 0, 0)))
    return [
        pl.BlockSpec((None, tm, d), lambda b, i, *_: (b, i, 0)),
        pl.BlockSpec((1, d), lambda b, i, *_: (0, 0)),
        mod_spec,
        mod_spec,
    ]


def _nm_delta_in_call(x, g, sc, sh, w, n_main, conv_dim, main_dtype):
    bq, rows, d = x.shape
    tm = _row_tile(rows)
    tiles = rows // tm
    kern = functools.partial(_nm_delta_in_kernel, col_tile=1024)
    return pl.pallas_call(
        kern,
        grid=(bq, tiles),
        in_specs=_nm_common_specs(sc, tm, d) + [_resident(w.shape)],
        out_specs=[pl.BlockSpec((None, tm, n_main), lambda b, i: (b, i, 0)),
                   pl.BlockSpec((None, tm, LANES), lambda b, i: (b, i, 0)),
                   pl.BlockSpec((None, None, SUBLANES, conv_dim), lambda b, i: (b, i, 0, 0))],
        out_shape=[jax.ShapeDtypeStruct((bq, rows, n_main), main_dtype),
                   jax.ShapeDtypeStruct((bq, rows, LANES), F32),
                   jax.ShapeDtypeStruct((bq, tiles, SUBLANES, conv_dim), F32)],
        compiler_params=_params("arbitrary", "arbitrary"),
        name="norm_mod_delta_in",
    )(x, g, sc, sh, w)


def _nm_delta_act_call(x, g, sc, sh, w, cstate8, conv_w, alog_l, dtb_l):
    bq, rows, d = x.shape
    conv_dim = conv_w.shape[1]
    n_main = w.shape[1] - LANES
    tm = _row_tile(rows)
    tiles = rows // tm
    kern = functools.partial(_nm_delta_act_kernel, col_tile=1024)
    return pl.pallas_call(
        kern,
        grid=(bq, tiles),
        in_specs=_nm_common_specs(sc, tm, d) + [
            _resident(w.shape),
            pl.BlockSpec((None, SUBLANES, conv_dim), lambda b, i: (b, 0, 0)),
            _resident(conv_w.shape), _resident(alog_l.shape), _resident(dtb_l.shape)],
        out_specs=[pl.BlockSpec((None, tm, n_main), lambda b, i: (b, i, 0)),
                   pl.BlockSpec((None, tm, LANES), lambda b, i: (b, i, 0)),
                   pl.BlockSpec((None, None, SUBLANES, conv_dim), lambda b, i: (b, i, 0, 0))],
        out_shape=[jax.ShapeDtypeStruct((bq, rows, n_main), BF16),
                   jax.ShapeDtypeStruct((bq, rows, LANES), F32),
                   jax.ShapeDtypeStruct((bq, tiles, SUBLANES, conv_dim), F32)],
        scratch_shapes=[pltpu.VMEM((SUBLANES, conv_dim), F32)],
        compiler_params=_params("arbitrary", "arbitrary"),
        name="norm_mod_delta_in_act",
    )(x, g, sc, sh, w, cstate8, conv_w, alog_l, dtb_l)


def _residue_major(a, dil, tm=A_ROW_TILE):
    lead, (rows, c) = a.shape[:-2], a.shape[-2:]
    a = a.reshape(lead + (rows // tm, tm // dil, dil, c))
    return jnp.swapaxes(a, -2, -3).reshape(lead + (rows, c))


def _natural_order(a, dil, tm=A_ROW_TILE):
    lead, (rows, c) = a.shape[:-2], a.shape[-2:]
    a = a.reshape(lead + (rows // tm, dil, tm // dil, c))
    return jnp.swapaxes(a, -2, -3).reshape(lead + (rows, c))


def _perm_matrices(tm=A_ROW_TILE):
    eye = jnp.eye(tm, dtype=BF16)
    return jnp.stack([_residue_major(eye, dil, tm) for _, dil in A_GROUPS if dil > 1])


def _nm_rope_call(x, g, sc, sh, w, cos, sin, tail_rows, perms=None):
    bq, rows, d = x.shape
    ngroups, _, gw = w.shape
    width = gw // 3
    grouped = perms is not None
    tm = A_ROW_TILE if grouped else _row_tile(min(rows, tail_rows))
    tiles = rows // tm
    tail_tile0 = (rows - tail_rows) // tm
    kern = functools.partial(_nm_rope_kernel, grouped=grouped)
    tab_spec = pl.BlockSpec((None, tm, LANES), lambda b, i, j: (0, i, 0))

    def tail_map(b, i, j):
        return (b, jnp.maximum(i - tail_tile0, 0), jnp.where(i >= tail_tile0, j, 0))

    in_specs = _nm_common_specs(sc, tm, d) + [_resident(w.shape), tab_spec, tab_spec]
    out_specs, out_shape, args = [], [], [x, g, sc, sh, w, cos, sin]
    if grouped:
        in_specs.append(_resident(perms.shape))
        args.append(perms)
        out_specs.append(pl.BlockSpec((None, 3, None, tm, width), lambda b, i, j: (b, j, i, 0, 0)))
        out_shape.append(jax.ShapeDtypeStruct((bq, 3 * ngroups, tiles, tm, width), BF16))
    out_specs.append(pl.BlockSpec((None, tm, gw), tail_map))
    out_shape.append(jax.ShapeDtypeStruct((bq, tail_rows, ngroups * gw), F32))
    n_h = 1 + (perms.shape[0] if grouped else 0)
    scratch = [pltpu.VMEM((n_h, tm, d), BF16)]
    if grouped:
        scratch.append(pltpu.VMEM((n_h, 2, tm, LANES), F32))
    return pl.pallas_call(
        kern,
        grid=(bq, tiles, ngroups),
        in_specs=in_specs,
        out_specs=out_specs,
        out_shape=out_shape,
        scratch_shapes=scratch,
        compiler_params=_params("arbitrary", "arbitrary", "arbitrary"),
        name="norm_mod_qkv_rope",
    )(*args)


def _nm_ffn_call(x, g, sc, sh, w):
    bq, rows, d = x.shape
    d_ff = w.shape[1] // 2
    tm = _row_tile(rows)
    kern = functools.partial(_nm_ffn_kernel, col_tile=d_ff // 2)
    row_spec = pl.BlockSpec((None, tm, d_ff), lambda b, i: (b, i, 0))
    return pl.pallas_call(
        kern,
        grid=(bq, rows // tm),
        in_specs=_nm_common_specs(sc, tm, d) + [_resident(w.shape)],
        out_specs=[row_spec, row_spec],
        out_shape=[jax.ShapeDtypeStruct((bq, rows, d_ff), F32),
                   jax.ShapeDtypeStruct((bq, rows, d_ff), BF16)],
        compiler_params=_params("arbitrary", "arbitrary"),
        name="norm_mod_ffn_up",
    )(x, g, sc, sh, w)


def _finish(y, x_ref, gate_ref, gfin_ref, o_ref, final_norm):
    xn = x_ref[...] + gate_ref[...] * y
    if final_norm:
        ms = jnp.mean(xn * xn, axis=-1, keepdims=True)
        xn = xn * lax.rsqrt(ms + NORM_EPS) * gfin_ref[...]
    o_ref[...] = xn


def _proj_res_kernel(a_ref, w_ref, x_ref, gate_ref, gfin_ref, o_ref, *, final_norm):
    _finish(_dot(a_ref[...], w_ref[...]), x_ref, gate_ref, gfin_ref, o_ref, final_norm)


def _attn_merge_proj_kernel(o0_ref, o1_ref, o2_ref, l0_ref, l1_ref, l2_ref, e_ref, w_ref,
                            x_ref, gate_ref, gfin_ref, *rest, final_norm, grouped):
    if grouped:
        unperm_ref, o_ref = rest
    else:
        (o_ref,) = rest
    outs, lses = [], []
    slot = 0
    for (_, dil), og_ref, lg_ref in zip(A_GROUPS, (o0_ref, o1_ref, o2_ref), (l0_ref, l1_ref, l2_ref)):
        if grouped and dil > 1:
            pt = unperm_ref[slot]
            slot += 1
            outs.append(_dot(pt, og_ref[...]))
            parts = _dot(pt, jnp.concatenate(_split3(lg_ref[...]), axis=1))
            lses.append(parts[:, 0:LANES] + parts[:, LANES:2 * LANES] + parts[:, 2 * LANES:3 * LANES])
        else:
            outs.append(og_ref[...].astype(F32))
            lses.append(lg_ref[...])
    m = jnp.maximum(jnp.maximum(lses[0], lses[1]), lses[2])
    exps = [jnp.exp(l - m) for l in lses]
    inv = 1.0 / (exps[0] + exps[1] + exps[2])
    merged = None
    for e, og in zip(exps, outs):
        wgt = e * inv
        hi = wgt.astype(BF16)
        lo = (wgt - hi.astype(F32)).astype(BF16)
        term = _dot(jnp.concatenate([hi, lo], axis=1), e_ref[...]) * og
        merged = term if merged is None else merged + term
    _finish(_dot(merged.astype(BF16), w_ref[...]), x_ref, gate_ref, gfin_ref, o_ref, final_norm)


def _res_specs(tm, d, gate):
    per_row = gate.shape[1] != 1
    gate_spec = (pl.BlockSpec((None, tm, d), lambda b, i: (b, i, 0)) if per_row
                 else pl.BlockSpec((None, 1, d), lambda b, i: (b, 0, 0)))
    return [pl.BlockSpec((None, tm, d), lambda b, i: (b, i, 0)),
            gate_spec,
            pl.BlockSpec((1, d), lambda b, i: (0, 0))]


def _proj_res_call(a, w, x, gate, gfin, final_norm):
    bq, rows, d = x.shape
    k = a.shape[2]
    tm = _row_tile(rows)
    kern = functools.partial(_proj_res_kernel, final_norm=final_norm)
    return pl.pallas_call(
        kern,
        grid=(bq, rows // tm),
        in_specs=[pl.BlockSpec((None, tm, k), lambda b, i: (b, i, 0)),
                  pl.BlockSpec((k, d), lambda b, i: (0, 0))] + _res_specs(tm, d, gate),
        out_specs=pl.BlockSpec((None, tm, d), lambda b, i: (b, i, 0)),
        out_shape=jax.ShapeDtypeStruct((bq, rows, d), F32),
        compiler_params=_params("arbitrary", "arbitrary"),
        name="proj_residual",
    )(a, w, x, gate, gfin)


def _attn_merge_proj_call(outs, lses, expand, w, x, gate, gfin, final_norm, unperms=None):
    bq, rows, d = x.shape
    grouped = unperms is not None
    tm = A_ROW_TILE if grouped else _row_tile(rows)
    kern = functools.partial(_attn_merge_proj_kernel, final_norm=final_norm, grouped=grouped)
    o_spec = pl.BlockSpec((None, tm, d), lambda b, i: (b, i, 0))
    l_spec = pl.BlockSpec((None, tm, LANES), lambda b, i: (b, i, 0))
    extra_specs, extra = [], []
    if grouped:
        extra_specs.append(pl.BlockSpec(unperms.shape, lambda b, i: (0, 0, 0)))
        extra.append(unperms)
    return pl.pallas_call(
        kern,
        grid=(bq, rows // tm),
        in_specs=[o_spec] * 3 + [l_spec] * 3 + [
            pl.BlockSpec(expand.shape, lambda b, i: (0, 0)),
            pl.BlockSpec(w.shape, lambda b, i: (0, 0))] + _res_specs(tm, d, gate) + extra_specs,
        out_specs=pl.BlockSpec((None, tm, d), lambda b, i: (b, i, 0)),
        out_shape=jax.ShapeDtypeStruct((bq, rows, d), F32),
        compiler_params=_params("arbitrary", "arbitrary"),
        name="attn_merge_proj_residual",
    )(*outs, *lses, expand, w, x, gate, gfin, *extra)


def _ffn_down_short_kernel(gate_ref, left_ref, val_ref, cw_ref, cb_ref, w_ref,
                           x_ref, mgate_ref, gfin_ref, o_ref, *, final_norm, seq_len):
    gate = gate_ref[...]
    width = cw_ref.shape[0]
    tpos = _mod_pow2(lax.broadcasted_iota(jnp.int32, gate.shape, 0), seq_len)
    y = cb_ref[...] + cw_ref[width - 1:width, :] * gate
    for back in range(1, width):
        shifted = jnp.where(tpos >= back, pltpu.roll(gate, back, 0), left_ref[back - 1])
        y = y + cw_ref[width - 1 - back:width - back, :] * shifted
    act = (_silu(y) * val_ref[...].astype(F32)).astype(BF16)
    _finish(_dot(act, w_ref[...]), x_ref, mgate_ref, gfin_ref, o_ref, final_norm)


def _ffn_down_short_call(gate_pre, state8, val, conv_w, conv_b, w, x, mgate, gfin, final_norm, seq_len):
    _, rows, d = x.shape
    f = gate_pre.shape[2]
    width = conv_w.shape[0]
    lefts = jnp.stack([jnp.roll(state8, back, axis=1).reshape(rows, f) for back in range(1, width)])
    kern = functools.partial(_ffn_down_short_kernel, final_norm=final_norm, seq_len=seq_len)
    whole = lambda a: pl.BlockSpec(a.shape, lambda i: (0,) * a.ndim)
    rows_of = lambda a: pl.BlockSpec((None,) + a.shape[1:], lambda i: (0, 0, 0))
    return pl.pallas_call(
        kern,
        grid=(1,),
        in_specs=[rows_of(gate_pre), whole(lefts), rows_of(val), whole(conv_w), whole(conv_b), whole(w),
                  rows_of(x), rows_of(mgate), whole(gfin)],
        out_specs=rows_of(x),
        out_shape=jax.ShapeDtypeStruct(x.shape, F32),
        compiler_params=_params("arbitrary"),
        name="ffn_conv_down_residual_short",
    )(gate_pre, lefts, val, conv_w, conv_b, w, x, mgate, gfin)


def _ffn_fused_kernel(x_ref, g_ref, sc_ref, sh_ref, wup_ref, state_ref, cw_ref, cb_ref, wdn_ref,
                      mgate_ref, gfin_ref, o_ref, gtail_ref, carry_scr, *, final_norm, col_tile):
    tm = x_ref.shape[0]
    d_ff = wdn_ref.shape[0]
    width = cw_ref.shape[0]

    @pl.when(pl.program_id(1) == 0)
    def _():
        carry_scr[...] = state_ref[...]

    h = _norm_mod(x_ref[...], g_ref[...], sc_ref[...], sh_ref[...]).astype(BF16)

    def up(c0):
        return (_dot(h, wup_ref[:, c0:c0 + col_tile]),
                _dot(h, wup_ref[:, d_ff + c0:d_ff + c0 + col_tile]))

    acc = None
    ahead = up(0)
    for c0 in range(0, d_ff, col_tile):
        cols = slice(c0, c0 + col_tile)
        gate, val = ahead
        if c0 + col_tile < d_ff:
            ahead = up(c0 + col_tile)
        ext = jnp.concatenate([carry_scr[:, cols], gate], axis=0)
        y = cb_ref[:, cols] + cw_ref[width - 1:width, cols] * gate
        for back in range(1, width):
            y = y + (cw_ref[width - 1 - back:width - back, cols]
                     * pltpu.roll(ext, back, 0)[SUBLANES:SUBLANES + tm])
        act = (_silu(y) * val).astype(BF16)
        part = _dot(act, wdn_ref[cols, :])
        acc = part if acc is None else acc + part
        carry_scr[:, cols] = gate[tm - SUBLANES:tm]
        gtail_ref[:, cols] = gate[tm - SUBLANES:tm]
    _finish(acc, x_ref, mgate_ref, gfin_ref, o_ref, final_norm)


def _ffn_fused_call(x, g, sc, sh, w_up, state8, conv_w, conv_b, w_down, mgate, gfin, final_norm):
    bq, rows, d = x.shape
    d_ff = w_down.shape[0]
    tm = _row_tile(rows)
    tiles = rows // tm
    col_tile = 2 * LANES
    assert d_ff % col_tile == 0
    kern = functools.partial(_ffn_fused_kernel, final_norm=final_norm, col_tile=col_tile)
    return pl.pallas_call(
        kern,
        grid=(bq, tiles),
        in_specs=_nm_common_specs(sc, tm, d) + [
            _resident(w_up.shape),
            pl.BlockSpec((None, SUBLANES, d_ff), lambda b, i: (b, 0, 0)),
            _resident(conv_w.shape),
            _resident(conv_b.shape),
            _resident(w_down.shape),
        ] + _res_specs(tm, d, mgate)[1:],
        out_specs=[pl.BlockSpec((None, tm, d), lambda b, i: (b, i, 0)),
                   pl.BlockSpec((None, None, SUBLANES, d_ff), lambda b, i: (b, i, 0, 0))],
        out_shape=[jax.ShapeDtypeStruct((bq, rows, d), F32),
                   jax.ShapeDtypeStruct((bq, tiles, SUBLANES, d_ff), F32)],
        scratch_shapes=[pltpu.VMEM((SUBLANES, d_ff), F32)],
        compiler_params=_params("arbitrary", "arbitrary"),
        name="conv_ffn",
    )(x, g, sc, sh, w_up, state8, conv_w, conv_b, w_down, mgate, gfin)


def _attn_prompt_kernel(q_ref, kp_ref, kc_ref, vp_ref, vc_ref, o_ref, lse_ref):
    blk = kp_ref.shape[0] * kp_ref.shape[1]
    prow = q_ref.shape[1]
    run = q_ref.shape[0] * prow
    nq = run // blk

    def rows_of(ref, cols):
        return jnp.concatenate([ref[p, :, cols] for p in range(ref.shape[0])], axis=0)

    n = pl.program_id(2)
    qi = lax.broadcasted_iota(jnp.int32, (blk, 2 * blk), 0)
    kj = lax.broadcasted_iota(jnp.int32, (blk, 2 * blk), 1)
    band = (kj >= qi) & (kj <= qi + blk)
    first = band & ((kj >= blk) | (n > 0))
    lane = lax.broadcasted_iota(jnp.int32, (blk, LANES), 1)
    low = lane < A_HEAD_DIM
    lse_all = [jnp.zeros((blk, LANES), F32) for _ in range(nq)]
    for hp in range(A_HEADS // 2):
        cols = slice(hp * LANES, (hp + 1) * LANES)
        q_run = rows_of(q_ref, cols)
        k_run = jnp.concatenate([rows_of(kp_ref, cols), rows_of(kc_ref, cols)], axis=0)
        v_run = jnp.concatenate([rows_of(vp_ref, cols), rows_of(vc_ref, cols)], axis=0)
        outs = []
        for j in range(nq):
            q2 = q_run[j * blk:(j + 1) * blk]
            k2 = k_run[j * blk:(j + 2) * blk]
            v2 = v_run[j * blk:(j + 2) * blk]
            valid = first if j == 0 else band
            pair = []
            for s in range(2):
                mine = low if s == 0 else jnp.logical_not(low)
                qh = jnp.where(mine, q2, jnp.zeros_like(q2))
                sc = jnp.where(valid, _dot_nt(qh, k2), NEG_BIG)
                m = jnp.max(sc, axis=1, keepdims=True)
                p = jnp.exp2(sc - m)
                l = jnp.sum(p, axis=1, keepdims=True)
                pair.append(_dot(p.astype(BF16), v2) / l)
                lse_all[j] = jnp.where(lane == 2 * hp + s, (m + jnp.log2(l)) * LN2, lse_all[j])
            outs.append(jnp.where(low, pair[0], pair[1]).astype(o_ref.dtype))
        o_run = jnp.concatenate(outs, axis=0)
        for p in range(o_ref.shape[0]):
            o_ref[p, :, cols] = o_run[p * prow:(p + 1) * prow]
    lse_run = jnp.concatenate(lse_all, axis=0)
    for p in range(lse_ref.shape[0]):
        lse_ref[p] = lse_run[p * prow:(p + 1) * prow]


def _attn_prompt_call(qkv, gi, win, dil):
    b, nkinds, tiles, tm, width = qkv.shape
    per_res = tm // dil
    qkv_g = qkv.reshape(b, nkinds, tiles, dil, per_res, width)
    blk = win // dil
    assert (tiles * tm) % win == 0
    nb = tiles * per_res // blk
    nq = min(A_BLOCKS_PER_STEP, nb)
    assert nb % nq == 0

    def geometry(rows):
        prow = min(rows, per_res)
        return rows // prow, prow

    def spec_of(rows, block_of_step, kind, lanes=width, lead=True):
        pieces, prow = geometry(rows)
        per_tile = per_res // prow

        def index(bb, r, i):
            j = jnp.maximum(block_of_step(i), 0)
            tail = (lax.div(j, per_tile), r, lax.rem(j, per_tile), 0)
            return (bb, 3 * gi + kind) + tail if lead else (bb,) + tail

        shape = (None, pieces, None, prow, lanes)
        return pl.BlockSpec((None,) + shape if lead else shape, index)

    cur = lambda kind: spec_of(nq * blk, lambda i: i, kind)
    prev = lambda kind: spec_of(blk, lambda i: nq * i - 1, kind)
    return pl.pallas_call(
        _attn_prompt_kernel,
        grid=(b, dil, nb // nq),
        in_specs=[cur(0), prev(1), cur(1), prev(2), cur(2)],
        out_specs=[spec_of(nq * blk, lambda i: i, 0, width, False),
                   spec_of(nq * blk, lambda i: i, 0, LANES, False)],
        out_shape=[jax.ShapeDtypeStruct((b, tiles, dil, per_res, width), BF16),
                   jax.ShapeDtypeStruct((b, tiles, dil, per_res, LANES), F32)],
        compiler_params=_params("arbitrary", "arbitrary", "arbitrary"),
        name="dilated_attn_prompt",
    )(qkv_g, qkv_g, qkv_g, qkv_g, qkv_g)


def _attn_sample_kernel(q_ref, kn_ref, vn_ref, kc_ref, vc_ref,
                        ko_ref, vo_ref, o_ref, lse_ref,
                        m_scr, l_scr, acc_scr, kcar_scr, vcar_scr, *, lb, dil):
    t_new = q_ref.shape[0]
    width = q_ref.shape[1]
    chunk = kc_ref.shape[1]
    rows = A_HEADS * t_new
    step = pl.program_id(1)
    c = pl.num_programs(1) - 1 - step

    rq = lax.broadcasted_iota(jnp.int32, (rows, width), 0)
    lq = lax.broadcasted_iota(jnp.int32, (rows, width), 1)
    own = _div_pow2(lq, A_HEAD_DIM) == _div_pow2(rq, t_new)
    qbd = jnp.where(own, jnp.concatenate([q_ref[...]] * A_HEADS, axis=0), 0.0).astype(BF16)

    def accumulate(keys_t, vals_t, valid):
        sc = jnp.where(valid, _dot(qbd, keys_t.astype(BF16)), NEG_BIG)
        m_old = m_scr[...]
        m_new = jnp.maximum(m_old, jnp.max(sc, axis=1, keepdims=True))
        alpha = jnp.exp2(m_old - m_new)
        p = jnp.where(valid, jnp.exp2(sc - m_new), 0.0)
        l_scr[...] = alpha * l_scr[...] + jnp.sum(p, axis=1, keepdims=True)
        acc_scr[...] = alpha * acc_scr[...] + _dot_nt(p.astype(BF16), vals_t.astype(BF16))
        m_scr[...] = m_new

    @pl.when(step == 0)
    def _():
        m_scr[...] = jnp.full_like(m_scr, NEG_BIG)
        l_scr[...] = jnp.zeros_like(l_scr)
        acc_scr[...] = jnp.zeros_like(acc_scr)
        pad = jnp.zeros((LANES - t_new, width), F32)
        kcar_scr[...] = jnp.concatenate([kn_ref[...], pad], axis=0).T
        vcar_scr[...] = jnp.concatenate([vn_ref[...], pad], axis=0).T
        tq2 = _mod_pow2(lax.broadcasted_iota(jnp.int32, (rows, LANES), 0), t_new)
        tk2 = lax.broadcasted_iota(jnp.int32, (rows, LANES), 1)
        d2 = tq2 - tk2
        accumulate(kcar_scr[...], vcar_scr[...],
                   (tk2 < t_new) & (d2 >= 0) & (_mod_pow2(d2, dil) == 0))

    lane = lax.broadcasted_iota(jnp.int32, (width, LANES), 1)
    keep = lane < LANES - t_new
    nblk = chunk // LANES
    for src_ref, car_ref, dst_ref in ((kc_ref, kcar_scr, ko_ref), (vc_ref, vcar_scr, vo_ref)):
        nxt = pltpu.roll(car_ref[...], LANES - t_new, 1)
        for b in reversed(range(nblk)):
            cur = pltpu.roll(src_ref[:, b * LANES:(b + 1) * LANES], LANES - t_new, 1)
            dst_ref[:, b * LANES:(b + 1) * LANES] = jnp.where(keep, cur, nxt)
            nxt = cur
        car_ref[...] = src_ref[:, 0:LANES]

    tq = _mod_pow2(lax.broadcasted_iota(jnp.int32, (rows, chunk), 0), t_new)
    ik = lax.broadcasted_iota(jnp.int32, (rows, chunk), 1) + c * chunk
    dist = lb + tq - ik
    accumulate(kc_ref[...], vc_ref[...], (_mod_pow2(dist, dil) == 0) & (dist <= lb))

    @pl.when(c == 0)
    def _():
        full = jnp.where(own, acc_scr[...] / l_scr[...], 0.0)
        o_ref[...] = jnp.sum(full.reshape(A_HEADS, t_new, width), axis=0).astype(o_ref.dtype)
        lse_col = (m_scr[...] + jnp.log2(l_scr[...])) * LN2
        rl = lax.broadcasted_iota(jnp.int32, (rows, LANES), 0)
        ll = lax.broadcasted_iota(jnp.int32, (rows, LANES), 1)
        spread = jnp.where(ll == _div_pow2(rl, t_new), lse_col, 0.0)
        lse_ref[...] = jnp.sum(spread.reshape(A_HEADS, t_new, LANES), axis=0)


def _attn_sample_call(proj, gi, win, dil, k_cache_t, v_cache_t):
    db, t_new, n = proj.shape
    width = A_HEADS * A_HEAD_DIM
    lb = k_cache_t.shape[2]
    assert lb == win and t_new == SUBLANES
    chunk = min(lb, 1024)
    nchunk = lb // chunk
    kern = functools.partial(_attn_sample_kernel, lb=lb, dil=dil)
    new_spec = lambda kind: pl.BlockSpec((None, t_new, width), lambda b, s: (b, 0, 3 * gi + kind))
    cur = pl.BlockSpec((None, width, chunk), lambda b, s: (b, 0, nchunk - 1 - s))
    return pl.pallas_call(
        kern,
        grid=(db, nchunk),
        in_specs=[new_spec(0), new_spec(1), new_spec(2), cur, cur],
        out_specs=[cur, cur,
                   pl.BlockSpec((None, t_new, width), lambda b, s: (b, 0, 0)),
                   pl.BlockSpec((None, t_new, LANES), lambda b, s: (b, 0, 0))],
        out_shape=[jax.ShapeDtypeStruct((db, width, lb), F32),
                   jax.ShapeDtypeStruct((db, width, lb), F32),
                   jax.ShapeDtypeStruct((db, t_new, width), BF16),
                   jax.ShapeDtypeStruct((db, t_new, LANES), F32)],
        scratch_shapes=[pltpu.VMEM((A_HEADS * t_new, 1), F32),
                        pltpu.VMEM((A_HEADS * t_new, 1), F32),
                        pltpu.VMEM((A_HEADS * t_new, width), F32),
                        pltpu.VMEM((width, LANES), F32),
                        pltpu.VMEM((width, LANES), F32)],
        compiler_params=_params("arbitrary", "arbitrary"),
        name="dilated_attn_sample",
    )(proj, proj, proj, k_cache_t, v_cache_t)


def _delta_kernel(*refs, t_valid, activated):
    if activated:
        p_ref, gates_ref, ng_ref, s0_ref, o_ref, ssm_ref, s_scr = refs
    else:
        (p_ref, prev_ref, gates_ref, cstate_ref, cw_ref, alog_ref, dtb_ref, ng_ref, s0_ref,
         o_ref, ssm_ref, cnew_ref, s_scr, ext_scr) = refs
    cl = B_CHUNK
    rows_in = p_ref.shape[0]
    nsub = max(rows_in // cl, 1)
    rows_pad = nsub * cl
    qk_width = B_QK_HEADS * B_HEAD
    conv_dim = 2 * qk_width + B_V_HEADS * B_HEAD
    c = pl.program_id(1)
    last = c == pl.num_programs(1) - 1

    @pl.when(c == 0)
    def _():
        s_scr[...] = s0_ref[...]

    if activated:
        assert t_valid == rows_pad
        qkv = p_ref[:, 0:conv_dim].astype(F32)
        qn, kn = qkv[:, 0:qk_width], qkv[:, qk_width:2 * qk_width]
        beta_all = g_all = gates_ref[...]
    else:
        qkv, beta_all, g_all = _delta_front(p_ref, prev_ref, gates_ref, cstate_ref, cw_ref, alog_ref,
                                            dtb_ref, cnew_ref, ext_scr, rows_pad, t_valid)
        qn = _l2n_heads(qkv[:, 0:qk_width]) * (B_HEAD ** -0.5)
        kn = _l2n_heads(qkv[:, qk_width:2 * qk_width])
    _delta_chunks(qkv, qn, kn, beta_all, g_all, p_ref, ng_ref, o_ref, s_scr, nsub)

    @pl.when(last)
    def _():
        ssm_ref[...] = s_scr[...]


def _delta_front(p_ref, prev_ref, gates_ref, cstate_ref, cw_ref, alog_ref, dtb_ref, cnew_ref, ext_scr,
                 rows_pad, t_valid):
    rows_in = p_ref.shape[0]
    conv_dim = cw_ref.shape[1]
    c = pl.program_id(1)
    last = c == pl.num_programs(1) - 1

    halo = prev_ref.shape[0]
    left = prev_ref[:, 0:conv_dim].astype(F32)
    if halo > SUBLANES:
        ext_scr[0:halo - SUBLANES, :] = left[0:halo - SUBLANES]
    ext_scr[halo - SUBLANES:halo, :] = jnp.where(c == 0, cstate_ref[...], left[halo - SUBLANES:halo])
    ext_scr[halo:halo + rows_in, :] = p_ref[:, 0:conv_dim].astype(F32)
    if rows_in < rows_pad:
        ext_scr[halo + rows_in:halo + rows_pad, :] = jnp.zeros((rows_pad - rows_in, conv_dim), F32)
    width = cw_ref.shape[0]
    ext = ext_scr[...]
    y = cw_ref[width - 1:width, :] * ext[halo:halo + rows_pad]
    for back in range(1, width):
        shifted = pltpu.roll(ext, back, 0)[halo:halo + rows_pad]
        y = y + cw_ref[width - 1 - back:width - back, :] * shifted
    qkv = _silu(y)
    if t_valid < rows_pad:
        rmask = lax.broadcasted_iota(jnp.int32, (rows_pad, 1), 0) < t_valid
        qkv = jnp.where(rmask, qkv, 0.0)

    @pl.when(last)
    def _():
        cnew_ref[...] = ext_scr[halo + t_valid - SUBLANES:halo + t_valid, :]

    graw = gates_ref[...]
    if rows_in < rows_pad:
        graw = jnp.concatenate([graw, jnp.zeros((rows_pad - rows_in, LANES), F32)], axis=0)
    beta_all, g_all = _delta_gates(graw, alog_ref[...], dtb_ref[...])
    if t_valid < rows_pad:
        beta_all = jnp.where(rmask, beta_all, 0.0)
        g_all = jnp.where(rmask, g_all, 0.0)
    return qkv, beta_all, g_all


def _delta_chunks(qkv, qn, kn, beta_all, g_all, p_ref, ng_ref, o_ref, s_scr, nsub):
    cl = B_CHUNK
    rows_in = p_ref.shape[0]
    qk_width = B_QK_HEADS * B_HEAD
    conv_dim = 2 * qk_width + B_V_HEADS * B_HEAD

    def rows_of(sub):
        return slice(sub * cl, (sub + 1) * cl)

    ri = lax.broadcasted_iota(jnp.int32, (2 * cl, cl), 0)
    ci = lax.broadcasted_iota(jnp.int32, (2 * cl, cl), 1)
    summat = jnp.where((ci <= ri) | (ri >= cl), 1.0, 0.0).astype(BF16)
    cg_all, gl_all = [], []
    for sub in range(nsub):
        sums = _dot(summat, jnp.concatenate(_split3(g_all[rows_of(sub)]), axis=1))
        sums = sums[:, 0:LANES] + sums[:, LANES:2 * LANES] + sums[:, 2 * LANES:3 * LANES]
        cg_all.append(sums[0:cl])
        gl_all.append(sums[cl:2 * cl])

    nst = B_STACK * cl
    rr = lax.broadcasted_iota(jnp.int32, (nst, nst), 0)
    cc = lax.broadcasted_iota(jnp.int32, (nst, nst), 1)
    same = _div_pow2(rr, cl) == _div_pow2(cc, cl)
    incl = same & (cc <= rr)
    strict = same & (cc < rr)
    lane6 = lax.broadcasted_iota(jnp.int32, (nst, LANES), 1)
    rblk = _div_pow2(lax.broadcasted_iota(jnp.int32, (nst, B_HEAD), 0), cl)

    def col_stack(arr, lane0, heads):
        return jnp.concatenate([arr[:, lane0 + h:lane0 + h + 1] for h in heads], axis=0)

    groups = [list(range(g * B_STACK, (g + 1) * B_STACK)) for g in range(B_V_HEADS // B_STACK)]
    problems = [(sub, hs) for sub in range(nsub) for hs in groups]
    every = range(len(problems))
    rep = B_V_HEADS // B_QK_HEADS

    def head_stack(arr, col0, sub, heads):
        return jnp.concatenate([arr[rows_of(sub), col0 + h * B_HEAD:col0 + (h + 1) * B_HEAD]
                                for h in heads], axis=0)

    q_st = [head_stack(qn, 0, sub, [h // rep for h in hs]) for sub, hs in problems]
    k_st = [head_stack(kn, 0, sub, [h // rep for h in hs]) for sub, hs in problems]
    v_st = [head_stack(qkv, 2 * qk_width, sub, hs) for sub, hs in problems]
    beta = [col_stack(beta_all[rows_of(sub)], 0, hs) for sub, hs in problems]
    cg = [col_stack(cg_all[sub], B_V_HEADS, hs) for sub, hs in problems]
    gl = [col_stack(gl_all[sub], B_V_HEADS, hs) for sub, hs in problems]

    def diff_operands(c):
        c_hi, c_mid, c_lo = (p.astype(F32) for p in _split3(c))
        lmat = jnp.where(lane6 == 0, c_hi, jnp.where(lane6 == 1, c_mid, jnp.where(
            lane6 == 2, c_lo, jnp.where(lane6 < 6, 1.0, 0.0))))
        rmat = jnp.where(lane6 == 3, -c_hi, jnp.where(lane6 == 4, -c_mid, jnp.where(
            lane6 == 5, -c_lo, jnp.where(lane6 < 3, 1.0, 0.0))))
        return lmat.astype(BF16), rmat.astype(BF16)

    diff = [_dot_nt(*diff_operands(cg[g])) for g in every]
    k_bf = [k_st[g].astype(BF16) for g in every]
    gram = [_dot_nt(k_bf[g], k_bf[g]) for g in every]
    qk = [_dot_nt(q_st[g].astype(BF16), k_bf[g]) for g in every]
    decay = [jnp.where(incl, jnp.exp(jnp.where(incl, diff[g], 0.0)), 0.0) for g in every]
    a_mat = [jnp.where(strict, beta[g] * gram[g] * decay[g], 0.0) for g in every]
    p_mat = [(qk[g] * decay[g]).astype(BF16) for g in every]

    blk = _div_pow2(rr, B_INV_BASE) == _div_pow2(cc, B_INV_BASE)
    apow = [jnp.where(blk, a_mat[g], 0.0) for g in every]
    nmat = [-apow[g] for g in every]
    for _ in range((B_INV_BASE - 1).bit_length() - 1):
        ab = [apow[g].astype(BF16) for g in every]
        apow = [_dot(ab[g], ab[g]) for g in every]
        nmat = [nmat[g] + apow[g] + _dot(nmat[g].astype(BF16), apow[g].astype(BF16)) for g in every]
    size = B_INV_BASE
    while size < cl:
        size *= 2
        merged = _div_pow2(rr, size) == _div_pow2(cc, size)
        join = merged & jnp.logical_not(blk)
        low = [jnp.where(join, a_mat[g], 0.0) for g in every]
        nb = [nmat[g].astype(BF16) for g in every]
        dl = [low[g] + _dot(nb[g], low[g].astype(BF16)) for g in every]
        nmat = [nmat[g] - (dl[g] + _dot(dl[g].astype(BF16), nb[g])) for g in every]
        blk = merged
    ecg = [jnp.exp(cg[g]) for g in every]
    rhs = [jnp.concatenate([beta[g] * v_st[g], (beta[g] * ecg[g]) * k_st[g]], axis=1) for g in every]
    sol = [rhs[g] + _dot(nmat[g].astype(BF16), rhs[g].astype(BF16)) for g in every]
    qg = [q_st[g] * ecg[g] for g in every]
    kdec_t = [(k_st[g] * jnp.exp(gl[g] - cg[g])).T.astype(BF16) for g in every]

    live = min(rows_in, cl)
    for sub in range(nsub):
        mine = [g for g in every if problems[g][0] == sub]
        both = {}
        for g in mine:
            for idx, h in enumerate(problems[g][1]):
                rsl = slice(idx * cl, (idx + 1) * cl)
                lhs = jnp.concatenate([sol[g][rsl, B_HEAD:2 * B_HEAD], qg[g][rsl]], axis=0)
                both[h] = _dot(lhs.astype(BF16), s_scr[h].astype(BF16))
        u_st = {g: jnp.concatenate([sol[g][idx * cl:(idx + 1) * cl, 0:B_HEAD] - both[h][0:cl]
                                    for idx, h in enumerate(problems[g][1])], axis=0) for g in mine}
        o_st = {g: jnp.concatenate([both[h][cl:2 * cl] for h in problems[g][1]], axis=0)
                + _dot(p_mat[g], u_st[g].astype(BF16)) for g in mine}
        for g in mine:
            for idx, h in enumerate(problems[g][1]):
                u_own = jnp.where(rblk == idx, u_st[g], 0.0).astype(BF16)
                s_scr[h] = jnp.exp(gl[g][idx * cl:idx * cl + 1]) * s_scr[h] + _dot(kdec_t[g], u_own)
        out_rows = slice(sub * cl, sub * cl + live)
        for g in mine:
            for idx, h in enumerate(problems[g][1]):
                o_h = o_st[g][idx * cl:idx * cl + live]
                ms = jnp.mean(o_h * o_h, axis=1, keepdims=True)
                z_h = p_ref[out_rows, conv_dim + h * B_HEAD:conv_dim + (h + 1) * B_HEAD].astype(F32)
                res = o_h * lax.rsqrt(ms + NORM_EPS) * ng_ref[...] * _silu(z_h)
                o_ref[out_rows, h * B_HEAD:(h + 1) * B_HEAD] = res.astype(o_ref.dtype)


def _delta_call(proj, gates, norm_g, s0, raw=None):
    bq, t, n = proj.shape
    v_width = B_V_HEADS * B_HEAD
    rows_in = min(t, B_CHUNK * B_CHUNKS_PER_STEP)
    assert t % rows_in == 0 and (rows_in % B_CHUNK == 0 or rows_in < B_CHUNK)
    nchunk = t // rows_in
    activated = raw is None
    kern = functools.partial(_delta_kernel, t_valid=rows_in, activated=activated)
    row_spec = lambda width: pl.BlockSpec((None, rows_in, width), lambda b, c: (b, c, 0))
    state_spec = pl.BlockSpec((None, B_V_HEADS, B_HEAD, B_HEAD), lambda b, c: (b, 0, 0, 0))
    const_spec = lambda a: pl.BlockSpec(a.shape, lambda b, c: (0,) * a.ndim)
    out_specs = [row_spec(v_width), state_spec]
    out_shape = [jax.ShapeDtypeStruct((bq, t, v_width), BF16),
                 jax.ShapeDtypeStruct((bq, B_V_HEADS, B_HEAD, B_HEAD), F32)]
    scratch = [pltpu.VMEM((B_V_HEADS, B_HEAD, B_HEAD), F32)]
    if activated:
        in_specs = [row_spec(n), row_spec(LANES), const_spec(norm_g), state_spec]
        args = (proj, gates, norm_g, s0)
    else:
        cstate8, conv_w, alog_l, dtb_l = raw
        conv_dim = conv_w.shape[1]
        halo = SUBLANES * (4 // proj.dtype.itemsize)
        per_halo = max(rows_in // halo, 1)
        in_specs = [
            row_spec(n),
            pl.BlockSpec((None, halo, n), lambda b, c: (b, jnp.maximum(c * per_halo - 1, 0), 0)),
            row_spec(LANES),
            pl.BlockSpec((None, SUBLANES, conv_dim), lambda b, c: (b, 0, 0)),
            const_spec(conv_w), const_spec(alog_l), const_spec(dtb_l), const_spec(norm_g), state_spec]
        args = (proj, proj, gates, cstate8, conv_w, alog_l, dtb_l, norm_g, s0)
        out_specs.append(pl.BlockSpec((None, SUBLANES, conv_dim), lambda b, c: (b, 0, 0)))
        out_shape.append(jax.ShapeDtypeStruct((bq, SUBLANES, conv_dim), F32))
        scratch.append(pltpu.VMEM((max(rows_in, B_CHUNK) + halo, conv_dim), F32))
    return pl.pallas_call(
        kern,
        grid=(bq, nchunk),
        in_specs=in_specs,
        out_specs=out_specs,
        out_shape=out_shape,
        scratch_shapes=scratch,
        compiler_params=_params("arbitrary", "arbitrary"),
        name="gated_deltanet",
    )(*args)


def _left_rows(state):
    return jnp.pad(state, ((0, 0), (SUBLANES - state.shape[1], 0), (0, 0)))


def _rope_tables(pos):
    half = A_HEAD_DIM // 2
    inv = ROPE_THETA ** (-jnp.arange(half, dtype=F32) / half)
    ang = pos.astype(F32)[:, None] * inv[None, :]
    cos, sin = jnp.cos(ang), jnp.sin(ang)
    reps = LANES // A_HEAD_DIM
    cos_t = jnp.tile(jnp.concatenate([cos, cos], axis=1), (1, reps))
    sin_t = jnp.tile(jnp.concatenate([-sin, sin], axis=1), (1, reps))
    return cos_t[None], sin_t[None]


def _trunk(x, mods, per_row, pos, a_caches, b_ssm, b_conv, f_conv, wts):
    seqs, t, d = x.shape
    (norm_mix_g, norm_ffn_g, norm_final_g, a_w_in, a_w_out, b_w_in, b_conv_w, b_a_log, b_dt_bias,
     b_norm_g, b_w_out, ffn_w_up, ffn_conv_w, ffn_conv_b, ffn_w_down, expand) = wts
    d_ff = ffn_w_down.shape[1]
    width = A_HEADS * A_HEAD_DIM

    def rows_view(a):
        return a.reshape(1, seqs * t, a.shape[2]) if per_row else a

    def seq_view(a):
        return a.reshape(seqs, t, a.shape[2])

    def mod_view(v):
        if per_row:
            return jnp.repeat(v, t, axis=0)[None]
        return v[:, None, :]

    cos_t, sin_t = _rope_tables(pos)
    prompt = a_caches is None
    if prompt:
        assert t % max(w for w, _ in A_GROUPS) == 0
        perms = _perm_matrices()
        unperms = jnp.swapaxes(perms, 1, 2)
    if per_row:
        cos_t, sin_t = jnp.tile(cos_t, (1, seqs, 1)), jnp.tile(sin_t, (1, seqs, 1))
    gfin = norm_final_g.reshape(1, d)
    xr = rows_view(x)
    new_a, new_ssm, new_bconv, new_fconv = [], None, None, []
    depth = norm_mix_g.shape[0]
    for layer in range(depth):
        sh1, sc1, g1, sh2, sc2, g2 = (mod_view(v) for v in mods[layer])
        gmix = norm_mix_g[layer].reshape(1, d)
        if layer % 2 == 0:
            outs, lses = [], []
            if prompt:
                tail_rows = max(w for w, _ in A_GROUPS)
                qkv, tail = _nm_rope_call(xr, gmix, sc1, sh1, a_w_in, cos_t, sin_t, tail_rows, perms)
                for gi, (win, dil) in enumerate(A_GROUPS):
                    o_g, lse_g = _attn_prompt_call(qkv, gi, win, dil)
                    outs.append(o_g.reshape(seqs, t, width))
                    lses.append(lse_g.reshape(seqs, t, LANES))
                    for kind in (1, 2):
                        col = slice((3 * gi + kind) * width, (3 * gi + kind + 1) * width)
                        keep = _natural_order(tail[:, :, col], dil)[:, tail_rows - win:]
                        new_a.append(keep.reshape(1, seqs, win, A_HEADS, A_HEAD_DIM))
                xr = _attn_merge_proj_call(outs, lses, expand, a_w_out[0], xr, g1, gfin, False,
                                           unperms)
            else:
                (tail,) = _nm_rope_call(xr, gmix, sc1, sh1, a_w_in, cos_t, sin_t, seqs * t)
                for gi, (win, dil) in enumerate(A_GROUPS):
                    as_t = lambda c: jnp.transpose(c[0], (0, 2, 3, 1)).reshape(seqs, width, c.shape[2])
                    k_new, v_new, o_g, lse_g = _attn_sample_call(
                        seq_view(tail), gi, win, dil, as_t(a_caches[2 * gi]), as_t(a_caches[2 * gi + 1]))
                    outs.append(rows_view(o_g))
                    lses.append(rows_view(lse_g))
                    for buf in (k_new, v_new):
                        buf = buf.reshape(seqs, A_HEADS, A_HEAD_DIM, buf.shape[2])
                        new_a.append(jnp.transpose(buf, (0, 3, 1, 2))[None])
                xr = _attn_merge_proj_call(outs, lses, expand, a_w_out[0], xr, g1, gfin, False)
        else:
            conv_w = b_conv_w[0]
            conv_dim = conv_w.shape[1]
            conv0 = jnp.zeros((seqs, conv_w.shape[0] - 1, conv_dim), F32) if b_conv is None else b_conv[0]
            s0 = jnp.zeros((seqs, B_V_HEADS, B_HEAD, B_HEAD), F32) if b_ssm is None else b_ssm[0]
            lane_pad = lambda v: jnp.pad(v.reshape(1, -1), ((0, 0), (B_V_HEADS, LANES - 2 * B_V_HEADS)))
            alog_l, dtb_l = lane_pad(b_a_log[0]), lane_pad(b_dt_bias[0])
            norm_g = b_norm_g[0].reshape(1, B_HEAD)
            if prompt:
                proj, gates, conv_tail = _nm_delta_act_call(
                    xr, gmix, sc1, sh1, b_w_in[0], _left_rows(conv0), conv_w, alog_l, dtb_l)
                o_b, ssm1 = _delta_call(proj, gates, norm_g, s0)
                cnew = conv_tail[:, -1]
            else:
                proj, gates, _ = _nm_delta_in_call(
                    xr, gmix, sc1, sh1, b_w_in[0], conv_dim + B_V_HEADS * B_HEAD, conv_dim, F32)
                o_b, ssm1, cnew = _delta_call(seq_view(proj), seq_view(gates), norm_g, s0,
                                              (_left_rows(conv0), conv_w, alog_l, dtb_l))
            new_ssm = ssm1[None]
            new_bconv = cnew[None, :, SUBLANES - (conv_w.shape[0] - 1):, :]
            xr = _proj_res_call(rows_view(o_b), b_w_out[0], xr, g1, gfin, False)
        gffn = norm_ffn_g[layer].reshape(1, d)
        keep = ffn_conv_w.shape[1] - 1
        f0 = jnp.zeros((seqs, keep, d_ff), F32) if f_conv is None else f_conv[layer]
        conv_b = ffn_conv_b[layer].reshape(1, d_ff)
        mg2 = mods[layer][5][:, None, :]
        final = layer == depth - 1
        if prompt:
            xr, gate_tail = _ffn_fused_call(xr, gffn, sc2, sh2, ffn_w_up[layer], _left_rows(f0),
                                            ffn_conv_w[layer], conv_b, ffn_w_down[layer], mg2, gfin, final)
            new_fconv.append(gate_tail[:, -1, SUBLANES - keep:, :])
        else:
            gate_pre, val = _nm_ffn_call(xr, gffn, sc2, sh2, ffn_w_up[layer])
            new_fconv.append(seq_view(gate_pre)[:, t - keep:, :])
            xr = _ffn_down_short_call(gate_pre, _left_rows(f0), val, ffn_conv_w[layer], conv_b,
                                      ffn_w_down[layer], xr, g2, gfin, final, t)
    return seq_view(xr), new_a, new_ssm, new_bconv, jnp.stack(new_fconv)


def kernel(x_prompt, x_sample, c_prompt, c_sample, cache_a_k_w128, cache_a_v_w128, cache_a_k_w512, cache_a_v_w512, cache_a_k_w2048, cache_a_v_w2048, state_b_ssm, state_b_conv, state_ffn_conv, norm_mix_g, norm_ffn_g, norm_final_g, w_mod, b_mod, a_w_in, a_w_out, b_w_in, b_conv_w, b_a_log, b_dt_bias, b_norm_g, b_w_out, ffn_w_up, ffn_conv_w, ffn_conv_b, ffn_w_down):
    bp, s, d = x_prompt.shape
    db, t_new, _ = x_sample.shape
    depth = w_mod.shape[0]

    c_all = jnp.concatenate([c_prompt, c_sample], axis=0)
    c_all = jnp.pad(c_all, ((0, (-c_all.shape[0]) % SUBLANES), (0, 0)))
    mod = _mod_call(c_all, w_mod, b_mod)

    def mods_of(lo, hi):
        return [[mod[l, lo:hi, k * d:(k + 1) * d] for k in range(6)] for l in range(depth)]

    b_w_in_p = jnp.pad(b_w_in, ((0, 0), (0, 0), (0, (-b_w_in.shape[2]) % LANES)))
    ngroups = len(A_GROUPS)
    a_w_in_g = jnp.swapaxes(a_w_in[0].astype(BF16).reshape(d, ngroups, -1), 0, 1)
    heads = jnp.arange(A_HEADS * A_HEAD_DIM) // A_HEAD_DIM
    half = (jnp.arange(LANES)[:, None] == heads[None, :]).astype(BF16)
    expand = jnp.concatenate([half, half], axis=0)
    wts = (norm_mix_g, norm_ffn_g, norm_final_g, a_w_in_g, a_w_out.astype(BF16),
           b_w_in_p.astype(BF16), b_conv_w, b_a_log, b_dt_bias, b_norm_g, b_w_out.astype(BF16),
           ffn_w_up.astype(BF16), ffn_conv_w, ffn_conv_b, ffn_w_down.astype(BF16), expand)

    pos_p = jnp.arange(s, dtype=jnp.int32)
    pos_s = PAST_LEN + jnp.arange(t_new, dtype=jnp.int32)
    y_p, a_p, ssm_p, bconv_p, fconv_p = _trunk(
        x_prompt, mods_of(0, bp), False, pos_p, None, None, None, None, wts)
    a_caches = [cache_a_k_w128, cache_a_v_w128, cache_a_k_w512, cache_a_v_w512,
                cache_a_k_w2048, cache_a_v_w2048]
    y_s, a_s, ssm_s, bconv_s, fconv_s = _trunk(
        x_sample, mods_of(bp, bp + db), True, pos_s, a_caches, state_b_ssm, state_b_conv,
        state_ffn_conv, wts)
    return (y_p, y_s,
            a_p[0], a_s[0], a_p[1], a_s[1], a_p[2], a_s[2], a_p[3], a_s[3], a_p[4], a_s[4],
            a_p[5], a_s[5], ssm_p, ssm_s, bconv_p, bconv_s, fconv_p, fconv_s)
```

```python
import functools

import jax
import jax.numpy as jnp
from jax import lax
from jax.experimental import pallas as pl
from jax.experimental.pallas import tpu as pltpu

F32 = jnp.float32
BF16 = jnp.bfloat16

PAST_LEN = 16384
A_GROUPS = ((128, 1), (512, 4), (2048, 16))
A_HEADS = 16
A_HEAD_DIM = 64
A_ROW_TILE = 512
A_BLOCKS_PER_STEP = 4
ROPE_THETA = 10000.0
NORM_EPS = 1e-6
B_HEAD = 128
B_V_HEADS = 16
B_QK_HEADS = 8
B_CHUNK = 64
B_CHUNKS_PER_STEP = 2
B_STACK = 2
B_INV_BASE = 16
NEG_BIG = -1e30
LOG2E = 1.4426950408889634
LN2 = 0.6931471805599453

LANES = 128
SUBLANES = 8
VMEM_LIMIT_BYTES = 56 * 1024 * 1024


def _params(*sem):
    return pltpu.CompilerParams(dimension_semantics=sem, vmem_limit_bytes=VMEM_LIMIT_BYTES)


def _silu(x):
    return x * jax.nn.sigmoid(x)


def _dot(a, b):
    return jnp.dot(a, b, preferred_element_type=F32)


def _dot_nt(a, b):
    return lax.dot_general(a, b, (((1,), (1,)), ((), ())), preferred_element_type=F32)


def _shift_of(n):
    assert n > 0 and n & (n - 1) == 0, n
    return n.bit_length() - 1


def _div_pow2(v, n):
    return lax.shift_right_logical(v, _shift_of(n))


def _mod_pow2(v, n):
    assert n & (n - 1) == 0
    return v & (n - 1)


def _split3(x):
    hi = x.astype(BF16)
    r1 = x - hi.astype(F32)
    mid = r1.astype(BF16)
    lo = (r1 - mid.astype(F32)).astype(BF16)
    return hi, mid, lo


def _mod_kernel(c_ref, w_ref, b_ref, o_ref):
    cs = _silu(c_ref[...]).astype(BF16)
    o_ref[...] = _dot(cs, w_ref[...].astype(BF16)) + b_ref[...]


def _mod_call(c_all, w_mod, b_mod):
    depth, d, n = w_mod.shape
    rows = c_all.shape[0]
    tn = 1536
    return pl.pallas_call(
        _mod_kernel,
        grid=(depth, n // tn),
        in_specs=[
            pl.BlockSpec((rows, d), lambda l, j: (0, 0)),
            pl.BlockSpec((None, d, tn), lambda l, j: (l, 0, j)),
            pl.BlockSpec((None, 1, tn), lambda l, j: (l, 0, j)),
        ],
        out_specs=pl.BlockSpec((None, rows, tn), lambda l, j: (l, 0, j)),
        out_shape=jax.ShapeDtypeStruct((depth, rows, n), F32),
        compiler_params=_params("arbitrary", "arbitrary"),
        name="adaln_mod",
    )(c_all, w_mod, b_mod.reshape(depth, 1, n))


def _norm_mod(x, g, sc, sh):
    ms = jnp.mean(x * x, axis=-1, keepdims=True)
    return x * lax.rsqrt(ms + NORM_EPS) * g * (1.0 + sc) + sh


def _rope_tile(acc, cos, sin):
    lane = lax.broadcasted_iota(jnp.int32, (acc.shape[0], LANES), 1)
    first_half = _mod_pow2(lane, A_HEAD_DIM) < (A_HEAD_DIM // 2)
    outs = []
    for c in range(acc.shape[1] // LANES):
        a = acc[:, c * LANES:(c + 1) * LANES]
        swapped = jnp.where(first_half,
                            pltpu.roll(a, LANES - A_HEAD_DIM // 2, 1),
                            pltpu.roll(a, A_HEAD_DIM // 2, 1))
        outs.append(a * cos + swapped * sin)
    return jnp.concatenate(outs, axis=1)


def _nm_delta_in_kernel(x_ref, g_ref, sc_ref, sh_ref, w_ref, main_ref, gates_ref, tail_ref,
                        *, col_tile):
    h = _norm_mod(x_ref[...], g_ref[...], sc_ref[...], sh_ref[...]).astype(BF16)
    tm, n_main = main_ref.shape
    for c0 in range(0, n_main, col_tile):
        acc = _dot(h, w_ref[:, c0:c0 + col_tile])
        main_ref[:, c0:c0 + col_tile] = acc.astype(main_ref.dtype)
        if c0 < tail_ref.shape[1]:
            tail_ref[:, c0:c0 + col_tile] = acc[tm - SUBLANES:tm]
    gates_ref[...] = _dot(h, w_ref[:, n_main:n_main + LANES])


def _delta_gates(logits, alog, dtb):
    xg = logits + dtb
    softplus = jnp.maximum(xg, 0.0) + jnp.log1p(jnp.exp(-jnp.abs(xg)))
    return jax.nn.sigmoid(logits), -jnp.exp(alog) * softplus


def _l2n_heads(arr):
    outs = []
    for h0 in range(0, arr.shape[1], B_HEAD):
        a = arr[:, h0:h0 + B_HEAD]
        outs.append(a * lax.rsqrt(jnp.sum(a * a, axis=1, keepdims=True) + NORM_EPS))
    return jnp.concatenate(outs, axis=1)


def _nm_delta_act_kernel(x_ref, g_ref, sc_ref, sh_ref, w_ref, cstate_ref, cw_ref, alog_ref, dtb_ref,
                         main_ref, gates_ref, tail_ref, carry_scr, *, col_tile):
    tm, n_main = main_ref.shape
    width, conv_dim = cw_ref.shape
    qk_width = B_QK_HEADS * B_HEAD

    @pl.when(pl.program_id(1) == 0)
    def _():
        carry_scr[...] = cstate_ref[...]

    h = _norm_mod(x_ref[...], g_ref[...], sc_ref[...], sh_ref[...]).astype(BF16)
    ahead = _dot(h, w_ref[:, 0:col_tile])
    for c0 in range(0, n_main, col_tile):
        cols = slice(c0, c0 + col_tile)
        acc = ahead
        if c0 + col_tile < n_main:
            ahead = _dot(h, w_ref[:, c0 + col_tile:c0 + 2 * col_tile])
        if c0 < conv_dim:
            ext = jnp.concatenate([carry_scr[:, cols], acc], axis=0)
            y = cw_ref[width - 1:width, cols] * acc
            for back in range(1, width):
                y = y + (cw_ref[width - 1 - back:width - back, cols]
                         * pltpu.roll(ext, back, 0)[SUBLANES:SUBLANES + tm])
            carry_scr[:, cols] = acc[tm - SUBLANES:tm]
            tail_ref[:, cols] = acc[tm - SUBLANES:tm]
            acc = _silu(y)
            if c0 < 2 * qk_width:
                acc = _l2n_heads(acc)
            if c0 < qk_width:
                acc = acc * (B_HEAD ** -0.5)
        main_ref[:, cols] = acc.astype(main_ref.dtype)
    beta, decay = _delta_gates(_dot(h, w_ref[:, n_main:n_main + LANES]), alog_ref[...], dtb_ref[...])
    lane = lax.broadcasted_iota(jnp.int32, beta.shape, 1)
    gates_ref[...] = jnp.where(lane < B_V_HEADS, beta, decay)


def _nm_rope_kernel(x_ref, g_ref, sc_ref, sh_ref, w_ref, cos_ref, sin_ref, *rest, grouped):
    if grouped:
        perm_ref, o_ref, tail_ref, h_scr, tab_scr = rest
    else:
        tail_ref, h_scr = rest
    group = pl.program_id(2)

    @pl.when(group == 0)
    def _():
        h = _norm_mod(x_ref[...], g_ref[...], sc_ref[...], sh_ref[...]).astype(BF16)
        h_scr[0] = h
        if grouped:
            tab_scr[0, 0] = cos_ref[...]
            tab_scr[0, 1] = sin_ref[...]
            parts = jnp.concatenate(_split3(cos_ref[...]) + _split3(sin_ref[...]), axis=1)
            for p in range(perm_ref.shape[0]):
                h_scr[p + 1] = _dot(perm_ref[p], h).astype(BF16)
                moved = _dot(perm_ref[p], parts)
                for tab in range(2):
                    lo = 3 * tab * LANES
                    tab_scr[p + 1, tab] = (moved[:, lo:lo + LANES] + moved[:, lo + LANES:lo + 2 * LANES]
                                           + moved[:, lo + 2 * LANES:lo + 3 * LANES])

    h = h_scr[group] if grouped else h_scr[0]
    width = A_HEADS * A_HEAD_DIM
    if grouped:
        cos, sin = tab_scr[group, 0], tab_scr[group, 1]
    else:
        cos, sin = cos_ref[...], sin_ref[...]
    qscale = A_HEAD_DIM ** -0.5 * LOG2E
    raw = [_dot(h, w_ref[:, pl.ds(pl.multiple_of((3 * group + kind) * width, LANES), width)])
           for kind in range(3)]
    for kind in range(3):
        cols = slice(kind * width, (kind + 1) * width)
        val = raw[kind]
        if kind == 0:
            val = _rope_tile(val, cos * qscale, sin * qscale)
        elif kind == 1:
            val = _rope_tile(val, cos, sin)
        tail_ref[:, cols] = val
        if grouped:
            o_ref[kind] = val.astype(o_ref.dtype)


def _nm_ffn_kernel(x_ref, g_ref, sc_ref, sh_ref, w_ref, gate_ref, val_ref, *, col_tile):
    h = _norm_mod(x_ref[...], g_ref[...], sc_ref[...], sh_ref[...]).astype(BF16)
    d_ff = gate_ref.shape[1]
    for c0 in range(0, d_ff, col_tile):
        gate_ref[:, c0:c0 + col_tile] = _dot(h, w_ref[:, c0:c0 + col_tile])
        val_ref[:, c0:c0 + col_tile] = _dot(
            h, w_ref[:, d_ff + c0:d_ff + c0 + col_tile]).astype(val_ref.dtype)


def _row_tile(rows):
    return min(rows, 512)


def _resident(shape, layer=None):
    if layer is None:
        zeros = (0,) * len(shape)
        return pl.BlockSpec(shape, lambda *_: zeros, pipeline_mode=pl.Buffered(1))
    index = (layer,) + (0,) * (len(shape) - 1)
    return pl.BlockSpec((None,) + tuple(shape[1:]), lambda *_: index, pipeline_mode=pl.Buffered(1))


def _nm_common_specs(sc, tm, d):
    per_row = sc.shape[1] != 1
    mod_spec = (pl.BlockSpec((None, tm, d), lambda b, i, *_: (b, i, 0)) if per_row
                else pl.BlockSpec((None, 1, d), lambda b, i, *_: (b, 0, 0)))
    return [
        pl.BlockSpec((None, tm, d), lambda b, i, *_: (b, i, 0)),
        pl.BlockSpec((1, d), lambda b, i, *_: (0, 0)),
        mod_spec,
        mod_spec,
    ]


def _nm_delta_in_call(x, g, sc, sh, w, n_main, conv_dim, main_dtype):
    bq, rows, d = x.shape
    tm = _row_tile(rows)
    tiles = rows // tm
    kern = functools.partial(_nm_delta_in_kernel, col_tile=1024)
    return pl.pallas_call(
        kern,
        grid=(bq, tiles),
        in_specs=_nm_common_specs(sc, tm, d) + [_resident(w.shape)],
        out_specs=[pl.BlockSpec((None, tm, n_main), lambda b, i: (b, i, 0)),
                   pl.BlockSpec((None, tm, LANES), lambda b, i: (b, i, 0)),
                   pl.BlockSpec((None, None, SUBLANES, conv_dim), lambda b, i: (b, i, 0, 0))],
        out_shape=[jax.ShapeDtypeStruct((bq, rows, n_main), main_dtype),
                   jax.ShapeDtypeStruct((bq, rows, LANES), F32),
                   jax.ShapeDtypeStruct((bq, tiles, SUBLANES, conv_dim), F32)],
        compiler_params=_params("arbitrary", "arbitrary"),
        name="norm_mod_delta_in",
    )(x, g, sc, sh, w)


def _nm_delta_act_call(x, g, sc, sh, w, cstate8, conv_w, alog_l, dtb_l):
    bq, rows, d = x.shape
    conv_dim = conv_w.shape[1]
    n_main = w.shape[1] - LANES
    tm = _row_tile(rows)
    tiles = rows // tm
    kern = functools.partial(_nm_delta_act_kernel, col_tile=1024)
    return pl.pallas_call(
        kern,
        grid=(bq, tiles),
        in_specs=_nm_common_specs(sc, tm, d) + [
            _resident(w.shape),
            pl.BlockSpec((None, SUBLANES, conv_dim), lambda b, i: (b, 0, 0)),
            _resident(conv_w.shape), _resident(alog_l.shape), _resident(dtb_l.shape)],
        out_specs=[pl.BlockSpec((None, tm, n_main), lambda b, i: (b, i, 0)),
                   pl.BlockSpec((None, tm, LANES), lambda b, i: (b, i, 0)),
                   pl.BlockSpec((None, None, SUBLANES, conv_dim), lambda b, i: (b, i, 0, 0))],
        out_shape=[jax.ShapeDtypeStruct((bq, rows, n_main), BF16),
                   jax.ShapeDtypeStruct((bq, rows, LANES), F32),
                   jax.ShapeDtypeStruct((bq, tiles, SUBLANES, conv_dim), F32)],
        scratch_shapes=[pltpu.VMEM((SUBLANES, conv_dim), F32)],
        compiler_params=_params("arbitrary", "arbitrary"),
        name="norm_mod_delta_in_act",
    )(x, g, sc, sh, w, cstate8, conv_w, alog_l, dtb_l)


def _residue_major(a, dil, tm=A_ROW_TILE):
    lead, (rows, c) = a.shape[:-2], a.shape[-2:]
    a = a.reshape(lead + (rows // tm, tm // dil, dil, c))
    return jnp.swapaxes(a, -2, -3).reshape(lead + (rows, c))


def _natural_order(a, dil, tm=A_ROW_TILE):
    lead, (rows, c) = a.shape[:-2], a.shape[-2:]
    a = a.reshape(lead + (rows // tm, dil, tm // dil, c))
    return jnp.swapaxes(a, -2, -3).reshape(lead + (rows, c))


def _perm_matrices(tm=A_ROW_TILE):
    eye = jnp.eye(tm, dtype=BF16)
    return jnp.stack([_residue_major(eye, dil, tm) for _, dil in A_GROUPS if dil > 1])


def _nm_rope_call(x, g, sc, sh, w, cos, sin, tail_rows, perms=None):
    bq, rows, d = x.shape
    ngroups = len(A_GROUPS)
    gw = w.shape[1] // ngroups
    width = gw // 3
    grouped = perms is not None
    tm = A_ROW_TILE if grouped else _row_tile(min(rows, tail_rows))
    tiles = rows // tm
    tail_tile0 = (rows - tail_rows) // tm
    kern = functools.partial(_nm_rope_kernel, grouped=grouped)
    tab_spec = pl.BlockSpec((None, tm, LANES), lambda b, i, j: (0, i, 0))

    def tail_map(b, i, j):
        return (b, jnp.maximum(i - tail_tile0, 0), jnp.where(i >= tail_tile0, j, 0))

    in_specs = _nm_common_specs(sc, tm, d) + [_resident(w.shape), tab_spec, tab_spec]
    out_specs, out_shape, args = [], [], [x, g, sc, sh, w, cos, sin]
    if grouped:
        in_specs.append(_resident(perms.shape))
        args.append(perms)
        out_specs.append(pl.BlockSpec((None, 3, None, tm, width), lambda b, i, j: (b, j, i, 0, 0)))
        out_shape.append(jax.ShapeDtypeStruct((bq, 3 * ngroups, tiles, tm, width), BF16))
    out_specs.append(pl.BlockSpec((None, tm, gw), tail_map))
    out_shape.append(jax.ShapeDtypeStruct((bq, tail_rows, ngroups * gw), F32))
    n_h = 1 + (perms.shape[0] if grouped else 0)
    scratch = [pltpu.VMEM((n_h, tm, d), BF16)]
    if grouped:
        scratch.append(pltpu.VMEM((n_h, 2, tm, LANES), F32))
    return pl.pallas_call(
        kern,
        grid=(bq, tiles, ngroups),
        in_specs=in_specs,
        out_specs=out_specs,
        out_shape=out_shape,
        scratch_shapes=scratch,
        compiler_params=_params("arbitrary", "arbitrary", "arbitrary"),
        name="norm_mod_qkv_rope",
    )(*args)


def _nm_ffn_call(x, g, sc, sh, w, layer):
    bq, rows, d = x.shape
    d_ff = w.shape[2] // 2
    tm = _row_tile(rows)
    kern = functools.partial(_nm_ffn_kernel, col_tile=d_ff // 2)
    row_spec = pl.BlockSpec((None, tm, d_ff), lambda b, i: (b, i, 0))
    return pl.pallas_call(
        kern,
        grid=(bq, rows // tm),
        in_specs=_nm_common_specs(sc, tm, d) + [_resident(w.shape, layer)],
        out_specs=[row_spec, row_spec],
        out_shape=[jax.ShapeDtypeStruct((bq, rows, d_ff), F32),
                   jax.ShapeDtypeStruct((bq, rows, d_ff), BF16)],
        compiler_params=_params("arbitrary", "arbitrary"),
        name="norm_mod_ffn_up",
    )(x, g, sc, sh, w)


def _finish(y, x_ref, gate_ref, gfin_ref, o_ref, final_norm):
    xn = x_ref[...] + gate_ref[...] * y
    if final_norm:
        ms = jnp.mean(xn * xn, axis=-1, keepdims=True)
        xn = xn * lax.rsqrt(ms + NORM_EPS) * gfin_ref[...]
    o_ref[...] = xn


def _proj_res_kernel(a_ref, w_ref, x_ref, gate_ref, gfin_ref, o_ref, *, final_norm):
    _finish(_dot(a_ref[...], w_ref[...]), x_ref, gate_ref, gfin_ref, o_ref, final_norm)


def _attn_merge_proj_kernel(o0_ref, o1_ref, o2_ref, l0_ref, l1_ref, l2_ref, e_ref, w_ref,
                            x_ref, gate_ref, gfin_ref, *rest, final_norm, grouped):
    if grouped:
        unperm_ref, o_ref = rest
    else:
        (o_ref,) = rest
    outs, lses = [], []
    slot = 0
    for (_, dil), og_ref, lg_ref in zip(A_GROUPS, (o0_ref, o1_ref, o2_ref), (l0_ref, l1_ref, l2_ref)):
        if grouped and dil > 1:
            pt = unperm_ref[slot]
            slot += 1
            outs.append(_dot(pt, og_ref[...]))
            parts = _dot(pt, jnp.concatenate(_split3(lg_ref[...]), axis=1))
            lses.append(parts[:, 0:LANES] + parts[:, LANES:2 * LANES] + parts[:, 2 * LANES:3 * LANES])
        else:
            outs.append(og_ref[...].astype(F32))
            lses.append(lg_ref[...])
    m = jnp.maximum(jnp.maximum(lses[0], lses[1]), lses[2])
    exps = [jnp.exp(l - m) for l in lses]
    inv = 1.0 / (exps[0] + exps[1] + exps[2])
    merged = None
    for e, og in zip(exps, outs):
        wgt = e * inv
        hi = wgt.astype(BF16)
        lo = (wgt - hi.astype(F32)).astype(BF16)
        term = _dot(jnp.concatenate([hi, lo], axis=1), e_ref[...]) * og
        merged = term if merged is None else merged + term
    _finish(_dot(merged.astype(BF16), w_ref[...]), x_ref, gate_ref, gfin_ref, o_ref, final_norm)


def _res_specs(tm, d, gate):
    per_row = gate.shape[1] != 1
    gate_spec = (pl.BlockSpec((None, tm, d), lambda b, i: (b, i, 0)) if per_row
                 else pl.BlockSpec((None, 1, d), lambda b, i: (b, 0, 0)))
    return [pl.BlockSpec((None, tm, d), lambda b, i: (b, i, 0)),
            gate_spec,
            pl.BlockSpec((1, d), lambda b, i: (0, 0))]


def _proj_res_call(a, w, x, gate, gfin, final_norm):
    bq, rows, d = x.shape
    k = a.shape[2]
    tm = _row_tile(rows)
    kern = functools.partial(_proj_res_kernel, final_norm=final_norm)
    return pl.pallas_call(
        kern,
        grid=(bq, rows // tm),
        in_specs=[pl.BlockSpec((None, tm, k), lambda b, i: (b, i, 0)),
                  pl.BlockSpec((k, d), lambda b, i: (0, 0))] + _res_specs(tm, d, gate),
        out_specs=pl.BlockSpec((None, tm, d), lambda b, i: (b, i, 0)),
        out_shape=jax.ShapeDtypeStruct((bq, rows, d), F32),
        compiler_params=_params("arbitrary", "arbitrary"),
        name="proj_residual",
    )(a, w, x, gate, gfin)


def _attn_merge_proj_call(outs, lses, expand, w, x, gate, gfin, final_norm, unperms=None):
    bq, rows, d = x.shape
    grouped = unperms is not None
    tm = A_ROW_TILE if grouped else _row_tile(rows)
    kern = functools.partial(_attn_merge_proj_kernel, final_norm=final_norm, grouped=grouped)
    o_spec = pl.BlockSpec((None, tm, d), lambda b, i: (b, i, 0))
    l_spec = pl.BlockSpec((None, tm, LANES), lambda b, i: (b, i, 0))
    extra_specs, extra = [], []
    if grouped:
        extra_specs.append(pl.BlockSpec(unperms.shape, lambda b, i: (0, 0, 0)))
        extra.append(unperms)
    return pl.pallas_call(
        kern,
        grid=(bq, rows // tm),
        in_specs=[o_spec] * 3 + [l_spec] * 3 + [
            pl.BlockSpec(expand.shape, lambda b, i: (0, 0)),
            pl.BlockSpec(w.shape, lambda b, i: (0, 0))] + _res_specs(tm, d, gate) + extra_specs,
        out_specs=pl.BlockSpec((None, tm, d), lambda b, i: (b, i, 0)),
        out_shape=jax.ShapeDtypeStruct((bq, rows, d), F32),
        compiler_params=_params("arbitrary", "arbitrary"),
        name="attn_merge_proj_residual",
    )(*outs, *lses, expand, w, x, gate, gfin, *extra)


def _ffn_down_short_kernel(gate_ref, left_ref, val_ref, cw_ref, cb_ref, w_ref,
                           x_ref, mgate_ref, gfin_ref, o_ref, *, final_norm, seq_len):
    gate = gate_ref[...]
    width = cw_ref.shape[0]
    tpos = _mod_pow2(lax.broadcasted_iota(jnp.int32, gate.shape, 0), seq_len)
    y = cb_ref[...] + cw_ref[width - 1:width, :] * gate
    for back in range(1, width):
        shifted = jnp.where(tpos >= back, pltpu.roll(gate, back, 0), left_ref[back - 1])
        y = y + cw_ref[width - 1 - back:width - back, :] * shifted
    act = (_silu(y) * val_ref[...].astype(F32)).astype(BF16)
    _finish(_dot(act, w_ref[...]), x_ref, mgate_ref, gfin_ref, o_ref, final_norm)


def _ffn_down_short_call(gate_pre, state8, val, conv_w, conv_b, w, x, mgate, gfin, final_norm, seq_len,
                         layer):
    _, rows, d = x.shape
    f = gate_pre.shape[2]
    width = conv_w.shape[0]
    lefts = jnp.stack([jnp.roll(state8, back, axis=1).reshape(rows, f) for back in range(1, width)])
    kern = functools.partial(_ffn_down_short_kernel, final_norm=final_norm, seq_len=seq_len)
    whole = lambda a: pl.BlockSpec(a.shape, lambda i: (0,) * a.ndim)
    rows_of = lambda a: pl.BlockSpec((None,) + a.shape[1:], lambda i: (0, 0, 0))
    return pl.pallas_call(
        kern,
        grid=(1,),
        in_specs=[rows_of(gate_pre), whole(lefts), rows_of(val), whole(conv_w), whole(conv_b),
                  pl.BlockSpec((None,) + w.shape[1:], lambda i: (layer, 0, 0)),
                  rows_of(x), rows_of(mgate), whole(gfin)],
        out_specs=rows_of(x),
        out_shape=jax.ShapeDtypeStruct(x.shape, F32),
        compiler_params=_params("arbitrary"),
        name="ffn_conv_down_residual_short",
    )(gate_pre, lefts, val, conv_w, conv_b, w, x, mgate, gfin)


def _ffn_fused_kernel(x_ref, g_ref, sc_ref, sh_ref, wup_ref, state_ref, cw_ref, cb_ref, wdn_ref,
                      mgate_ref, gfin_ref, o_ref, gtail_ref, carry_scr, *, final_norm, col_tile):
    tm = x_ref.shape[0]
    d_ff = wdn_ref.shape[0]
    width = cw_ref.shape[0]

    @pl.when(pl.program_id(1) == 0)
    def _():
        carry_scr[...] = state_ref[...]

    h = _norm_mod(x_ref[...], g_ref[...], sc_ref[...], sh_ref[...]).astype(BF16)

    def up(c0):
        return (_dot(h, wup_ref[:, c0:c0 + col_tile]),
                _dot(h, wup_ref[:, d_ff + c0:d_ff + c0 + col_tile]))

    acc = None
    ahead = up(0)
    for c0 in range(0, d_ff, col_tile):
        cols = slice(c0, c0 + col_tile)
        gate, val = ahead
        if c0 + col_tile < d_ff:
            ahead = up(c0 + col_tile)
        ext = jnp.concatenate([carry_scr[:, cols], gate], axis=0)
        y = cb_ref[:, cols] + cw_ref[width - 1:width, cols] * gate
        for back in range(1, width):
            y = y + (cw_ref[width - 1 - back:width - back, cols]
                     * pltpu.roll(ext, back, 0)[SUBLANES:SUBLANES + tm])
        act = (_silu(y) * val).astype(BF16)
        part = _dot(act, wdn_ref[cols, :])
        acc = part if acc is None else acc + part
        carry_scr[:, cols] = gate[tm - SUBLANES:tm]
        gtail_ref[:, cols] = gate[tm - SUBLANES:tm]
    _finish(acc, x_ref, mgate_ref, gfin_ref, o_ref, final_norm)


def _ffn_fused_call(x, g, sc, sh, w_up, state8, conv_w, conv_b, w_down, mgate, gfin, final_norm,
                    layer):
    bq, rows, d = x.shape
    d_ff = w_down.shape[1]
    tm = _row_tile(rows)
    tiles = rows // tm
    col_tile = 2 * LANES
    assert d_ff % col_tile == 0
    kern = functools.partial(_ffn_fused_kernel, final_norm=final_norm, col_tile=col_tile)
    return pl.pallas_call(
        kern,
        grid=(bq, tiles),
        in_specs=_nm_common_specs(sc, tm, d) + [
            _resident(w_up.shape, layer),
            pl.BlockSpec((None, SUBLANES, d_ff), lambda b, i: (b, 0, 0)),
            _resident(conv_w.shape),
            _resident(conv_b.shape),
            _resident(w_down.shape, layer),
        ] + _res_specs(tm, d, mgate)[1:],
        out_specs=[pl.BlockSpec((None, tm, d), lambda b, i: (b, i, 0)),
                   pl.BlockSpec((None, None, SUBLANES, d_ff), lambda b, i: (b, i, 0, 0))],
        out_shape=[jax.ShapeDtypeStruct((bq, rows, d), F32),
                   jax.ShapeDtypeStruct((bq, tiles, SUBLANES, d_ff), F32)],
        scratch_shapes=[pltpu.VMEM((SUBLANES, d_ff), F32)],
        compiler_params=_params("arbitrary", "arbitrary"),
        name="conv_ffn",
    )(x, g, sc, sh, w_up, state8, conv_w, conv_b, w_down, mgate, gfin)


def _attn_prompt_kernel(q_ref, kp_ref, kc_ref, vp_ref, vc_ref, o_ref, lse_ref):
    blk = kp_ref.shape[0] * kp_ref.shape[1]
    prow = q_ref.shape[1]
    run = q_ref.shape[0] * prow
    nq = run // blk

    def rows_of(ref, cols):
        return jnp.concatenate([ref[p, :, cols] for p in range(ref.shape[0])], axis=0)

    n = pl.program_id(2)
    qi = lax.broadcasted_iota(jnp.int32, (blk, 2 * blk), 0)
    kj = lax.broadcasted_iota(jnp.int32, (blk, 2 * blk), 1)
    band = (kj >= qi) & (kj <= qi + blk)
    bias = jnp.where(band, 0.0, NEG_BIG)
    bias_first = jnp.where(band & ((kj >= blk) | (n > 0)), 0.0, NEG_BIG)
    lane = lax.broadcasted_iota(jnp.int32, (blk, LANES), 1)
    low = lane < A_HEAD_DIM
    lse_all = [jnp.zeros((blk, LANES), F32) for _ in range(nq)]
    for hp in range(A_HEADS // 2):
        cols = slice(hp * LANES, (hp + 1) * LANES)
        q_run = rows_of(q_ref, cols)
        k_run = jnp.concatenate([rows_of(kp_ref, cols), rows_of(kc_ref, cols)], axis=0)
        v_run = jnp.concatenate([rows_of(vp_ref, cols), rows_of(vc_ref, cols)], axis=0)
        outs = []
        for j in range(nq):
            q2 = q_run[j * blk:(j + 1) * blk]
            k2 = k_run[j * blk:(j + 2) * blk]
            v2 = v_run[j * blk:(j + 2) * blk]
            mask = bias_first if j == 0 else bias
            pair = []
            for s in range(2):
                mine = low if s == 0 else jnp.logical_not(low)
                qh = jnp.where(mine, q2, jnp.zeros_like(q2))
                sc = _dot_nt(qh, k2) + mask
                m = jnp.max(sc, axis=1, keepdims=True)
                p = jnp.exp2(sc - m)
                l = jnp.sum(p, axis=1, keepdims=True)
                pair.append(_dot(p.astype(BF16), v2) / l)
                lse_all[j] = jnp.where(lane == 2 * hp + s, (m + jnp.log2(l)) * LN2, lse_all[j])
            outs.append(jnp.where(low, pair[0], pair[1]).astype(o_ref.dtype))
        o_run = jnp.concatenate(outs, axis=0)
        for p in range(o_ref.shape[0]):
            o_ref[p, :, cols] = o_run[p * prow:(p + 1) * prow]
    lse_run = jnp.concatenate(lse_all, axis=0)
    for p in range(lse_ref.shape[0]):
        lse_ref[p] = lse_run[p * prow:(p + 1) * prow]


def _attn_prompt_call(qkv, gi, win, dil):
    b, nkinds, tiles, tm, width = qkv.shape
    per_res = tm // dil
    qkv_g = qkv.reshape(b, nkinds, tiles, dil, per_res, width)
    blk = win // dil
    assert (tiles * tm) % win == 0
    nb = tiles * per_res // blk
    nq = min(A_BLOCKS_PER_STEP, nb)
    assert nb % nq == 0

    def geometry(rows):
        prow = min(rows, per_res)
        return rows // prow, prow

    def spec_of(rows, block_of_step, kind, lanes=width, lead=True):
        pieces, prow = geometry(rows)
        per_tile = per_res // prow

        def index(bb, r, i):
            j = jnp.maximum(block_of_step(i), 0)
            tail = (lax.div(j, per_tile), r, lax.rem(j, per_tile), 0)
            return (bb, 3 * gi + kind) + tail if lead else (bb,) + tail

        shape = (None, pieces, None, prow, lanes)
        return pl.BlockSpec((None,) + shape if lead else shape, index)

    cur = lambda kind: spec_of(nq * blk, lambda i: i, kind)
    prev = lambda kind: spec_of(blk, lambda i: nq * i - 1, kind)
    return pl.pallas_call(
        _attn_prompt_kernel,
        grid=(b, dil, nb // nq),
        in_specs=[cur(0), prev(1), cur(1), prev(2), cur(2)],
        out_specs=[spec_of(nq * blk, lambda i: i, 0, width, False),
                   spec_of(nq * blk, lambda i: i, 0, LANES, False)],
        out_shape=[jax.ShapeDtypeStruct((b, tiles, dil, per_res, width), BF16),
                   jax.ShapeDtypeStruct((b, tiles, dil, per_res, LANES), F32)],
        compiler_params=_params("arbitrary", "arbitrary", "arbitrary"),
        name="dilated_attn_prompt",
    )(qkv_g, qkv_g, qkv_g, qkv_g, qkv_g)


def _attn_sample_kernel(q_ref, kn_ref, vn_ref, kc_ref, vc_ref,
                        ko_ref, vo_ref, o_ref, lse_ref,
                        m_scr, l_scr, acc_scr, kcar_scr, vcar_scr, *, lb, dil):
    t_new = q_ref.shape[0]
    width = q_ref.shape[1]
    chunk = kc_ref.shape[1]
    rows = A_HEADS * t_new
    step = pl.program_id(1)
    c = pl.num_programs(1) - 1 - step

    rq = lax.broadcasted_iota(jnp.int32, (rows, width), 0)
    lq = lax.broadcasted_iota(jnp.int32, (rows, width), 1)
    own = _div_pow2(lq, A_HEAD_DIM) == _div_pow2(rq, t_new)
    qbd = jnp.where(own, jnp.concatenate([q_ref[...]] * A_HEADS, axis=0), 0.0).astype(BF16)

    def accumulate(keys_t, vals_t, valid):
        sc = jnp.where(valid, _dot(qbd, keys_t.astype(BF16)), NEG_BIG)
        m_old = m_scr[...]
        m_new = jnp.maximum(m_old, jnp.max(sc, axis=1, keepdims=True))
        alpha = jnp.exp2(m_old - m_new)
        p = jnp.where(valid, jnp.exp2(sc - m_new), 0.0)
        l_scr[...] = alpha * l_scr[...] + jnp.sum(p, axis=1, keepdims=True)
        acc_scr[...] = alpha * acc_scr[...] + _dot_nt(p.astype(BF16), vals_t.astype(BF16))
        m_scr[...] = m_new

    @pl.when(step == 0)
    def _():
        m_scr[...] = jnp.full_like(m_scr, NEG_BIG)
        l_scr[...] = jnp.zeros_like(l_scr)
        acc_scr[...] = jnp.zeros_like(acc_scr)
        pad = jnp.zeros((LANES - t_new, width), F32)
        kcar_scr[...] = jnp.concatenate([kn_ref[...], pad], axis=0).T
        vcar_scr[...] = jnp.concatenate([vn_ref[...], pad], axis=0).T
        tq2 = _mod_pow2(lax.broadcasted_iota(jnp.int32, (rows, LANES), 0), t_new)
        tk2 = lax.broadcasted_iota(jnp.int32, (rows, LANES), 1)
        d2 = tq2 - tk2
        accumulate(kcar_scr[...], vcar_scr[...],
                   (tk2 < t_new) & (d2 >= 0) & (_mod_pow2(d2, dil) == 0))

    lane = lax.broadcasted_iota(jnp.int32, (width, LANES), 1)
    keep = lane < LANES - t_new
    nblk = chunk // LANES
    for src_ref, car_ref, dst_ref in ((kc_ref, kcar_scr, ko_ref), (vc_ref, vcar_scr, vo_ref)):
        nxt = pltpu.roll(car_ref[...], LANES - t_new, 1)
        for b in reversed(range(nblk)):
            cur = pltpu.roll(src_ref[:, b * LANES:(b + 1) * LANES], LANES - t_new, 1)
            dst_ref[:, b * LANES:(b + 1) * LANES] = jnp.where(keep, cur, nxt)
            nxt = cur
        car_ref[...] = src_ref[:, 0:LANES]

    tq = _mod_pow2(lax.broadcasted_iota(jnp.int32, (rows, chunk), 0), t_new)
    ik = lax.broadcasted_iota(jnp.int32, (rows, chunk), 1) + c * chunk
    dist = lb + tq - ik
    accumulate(kc_ref[...], vc_ref[...], (_mod_pow2(dist, dil) == 0) & (dist <= lb))

    @pl.when(c == 0)
    def _():
        full = jnp.where(own, acc_scr[...] / l_scr[...], 0.0)
        o_ref[...] = jnp.sum(full.reshape(A_HEADS, t_new, width), axis=0).astype(o_ref.dtype)
        lse_col = (m_scr[...] + jnp.log2(l_scr[...])) * LN2
        rl = lax.broadcasted_iota(jnp.int32, (rows, LANES), 0)
        ll = lax.broadcasted_iota(jnp.int32, (rows, LANES), 1)
        spread = jnp.where(ll == _div_pow2(rl, t_new), lse_col, 0.0)
        lse_ref[...] = jnp.sum(spread.reshape(A_HEADS, t_new, LANES), axis=0)


def _attn_sample_call(proj, gi, win, dil, k_cache_t, v_cache_t):
    db, t_new, n = proj.shape
    width = A_HEADS * A_HEAD_DIM
    lb = k_cache_t.shape[2]
    assert lb == win and t_new == SUBLANES
    chunk = min(lb, 1024)
    nchunk = lb // chunk
    kern = functools.partial(_attn_sample_kernel, lb=lb, dil=dil)
    new_spec = lambda kind: pl.BlockSpec((None, t_new, width), lambda b, s: (b, 0, 3 * gi + kind))
    cur = pl.BlockSpec((None, width, chunk), lambda b, s: (b, 0, nchunk - 1 - s))
    return pl.pallas_call(
        kern,
        grid=(db, nchunk),
        in_specs=[new_spec(0), new_spec(1), new_spec(2), cur, cur],
        out_specs=[cur, cur,
                   pl.BlockSpec((None, t_new, width), lambda b, s: (b, 0, 0)),
                   pl.BlockSpec((None, t_new, LANES), lambda b, s: (b, 0, 0))],
        out_shape=[jax.ShapeDtypeStruct((db, width, lb), F32),
                   jax.ShapeDtypeStruct((db, width, lb), F32),
                   jax.ShapeDtypeStruct((db, t_new, width), BF16),
                   jax.ShapeDtypeStruct((db, t_new, LANES), F32)],
        scratch_shapes=[pltpu.VMEM((A_HEADS * t_new, 1), F32),
                        pltpu.VMEM((A_HEADS * t_new, 1), F32),
                        pltpu.VMEM((A_HEADS * t_new, width), F32),
                        pltpu.VMEM((width, LANES), F32),
                        pltpu.VMEM((width, LANES), F32)],
        compiler_params=_params("arbitrary", "arbitrary"),
        name="dilated_attn_sample",
    )(proj, proj, proj, k_cache_t, v_cache_t)


def _delta_kernel(*refs, t_valid, activated):
    if activated:
        p_ref, gates_ref, ng_ref, s0_ref, o_ref, ssm_ref, s_scr = refs
    else:
        (p_ref, prev_ref, gates_ref, cstate_ref, cw_ref, alog_ref, dtb_ref, ng_ref, s0_ref,
         o_ref, ssm_ref, cnew_ref, s_scr, ext_scr) = refs
    cl = B_CHUNK
    rows_in = p_ref.shape[0]
    nsub = max(rows_in // cl, 1)
    rows_pad = nsub * cl
    qk_width = B_QK_HEADS * B_HEAD
    conv_dim = 2 * qk_width + B_V_HEADS * B_HEAD
    c = pl.program_id(1)
    last = c == pl.num_programs(1) - 1

    @pl.when(c == 0)
    def _():
        s_scr[...] = s0_ref[...]

    if activated:
        assert t_valid == rows_pad
        qkv = p_ref[:, 0:conv_dim].astype(F32)
        qn, kn = qkv[:, 0:qk_width], qkv[:, qk_width:2 * qk_width]
        beta_all = g_all = gates_ref[...]
    else:
        qkv, beta_all, g_all = _delta_front(p_ref, prev_ref, gates_ref, cstate_ref, cw_ref, alog_ref,
                                            dtb_ref, cnew_ref, ext_scr, rows_pad, t_valid)
        qn = _l2n_heads(qkv[:, 0:qk_width]) * (B_HEAD ** -0.5)
        kn = _l2n_heads(qkv[:, qk_width:2 * qk_width])
    _delta_chunks(qkv, qn, kn, beta_all, g_all, p_ref, ng_ref, o_ref, s_scr, nsub)

    @pl.when(last)
    def _():
        ssm_ref[...] = s_scr[...]


def _delta_front(p_ref, prev_ref, gates_ref, cstate_ref, cw_ref, alog_ref, dtb_ref, cnew_ref, ext_scr,
                 rows_pad, t_valid):
    rows_in = p_ref.shape[0]
    conv_dim = cw_ref.shape[1]
    c = pl.program_id(1)
    last = c == pl.num_programs(1) - 1

    halo = prev_ref.shape[0]
    left = prev_ref[:, 0:conv_dim].astype(F32)
    if halo > SUBLANES:
        ext_scr[0:halo - SUBLANES, :] = left[0:halo - SUBLANES]
    ext_scr[halo - SUBLANES:halo, :] = jnp.where(c == 0, cstate_ref[...], left[halo - SUBLANES:halo])
    ext_scr[halo:halo + rows_in, :] = p_ref[:, 0:conv_dim].astype(F32)
    if rows_in < rows_pad:
        ext_scr[halo + rows_in:halo + rows_pad, :] = jnp.zeros((rows_pad - rows_in, conv_dim), F32)
    width = cw_ref.shape[0]
    ext = ext_scr[...]
    y = cw_ref[width - 1:width, :] * ext[halo:halo + rows_pad]
    for back in range(1, width):
        shifted = pltpu.roll(ext, back, 0)[halo:halo + rows_pad]
        y = y + cw_ref[width - 1 - back:width - back, :] * shifted
    qkv = _silu(y)
    if t_valid < rows_pad:
        rmask = lax.broadcasted_iota(jnp.int32, (rows_pad, 1), 0) < t_valid
        qkv = jnp.where(rmask, qkv, 0.0)

    @pl.when(last)
    def _():
        cnew_ref[...] = ext_scr[halo + t_valid - SUBLANES:halo + t_valid, :]

    graw = gates_ref[...]
    if rows_in < rows_pad:
        graw = jnp.concatenate([graw, jnp.zeros((rows_pad - rows_in, LANES), F32)], axis=0)
    beta_all, g_all = _delta_gates(graw, alog_ref[...], dtb_ref[...])
    if t_valid < rows_pad:
        beta_all = jnp.where(rmask, beta_all, 0.0)
        g_all = jnp.where(rmask, g_all, 0.0)
    return qkv, beta_all, g_all


def _delta_chunks(qkv, qn, kn, beta_all, g_all, p_ref, ng_ref, o_ref, s_scr, nsub):
    cl = B_CHUNK
    rows_in = p_ref.shape[0]
    qk_width = B_QK_HEADS * B_HEAD
    conv_dim = 2 * qk_width + B_V_HEADS * B_HEAD

    def rows_of(sub):
        return slice(sub * cl, (sub + 1) * cl)

    ri = lax.broadcasted_iota(jnp.int32, (2 * cl, cl), 0)
    ci = lax.broadcasted_iota(jnp.int32, (2 * cl, cl), 1)
    summat = jnp.where((ci <= ri) | (ri >= cl), 1.0, 0.0).astype(BF16)
    cg_all, gl_all = [], []
    for sub in range(nsub):
        sums = _dot(summat, jnp.concatenate(_split3(g_all[rows_of(sub)]), axis=1))
        sums = sums[:, 0:LANES] + sums[:, LANES:2 * LANES] + sums[:, 2 * LANES:3 * LANES]
        cg_all.append(sums[0:cl])
        gl_all.append(sums[cl:2 * cl])

    nst = B_STACK * cl
    rr = lax.broadcasted_iota(jnp.int32, (nst, nst), 0)
    cc = lax.broadcasted_iota(jnp.int32, (nst, nst), 1)
    same = _div_pow2(rr, cl) == _div_pow2(cc, cl)
    incl = same & (cc <= rr)
    strict = same & (cc < rr)
    lane6 = lax.broadcasted_iota(jnp.int32, (nst, LANES), 1)
    rblk = _div_pow2(lax.broadcasted_iota(jnp.int32, (nst, B_HEAD), 0), cl)

    def col_stack(arr, lane0, heads):
        return jnp.concatenate([arr[:, lane0 + h:lane0 + h + 1] for h in heads], axis=0)

    groups = [list(range(g * B_STACK, (g + 1) * B_STACK)) for g in range(B_V_HEADS // B_STACK)]
    problems = [(sub, hs) for sub in range(nsub) for hs in groups]
    every = range(len(problems))
    rep = B_V_HEADS // B_QK_HEADS

    def head_stack(arr, col0, sub, heads):
        return jnp.concatenate([arr[rows_of(sub), col0 + h * B_HEAD:col0 + (h + 1) * B_HEAD]
                                for h in heads], axis=0)

    q_st = [head_stack(qn, 0, sub, [h // rep for h in hs]) for sub, hs in problems]
    k_st = [head_stack(kn, 0, sub, [h // rep for h in hs]) for sub, hs in problems]
    v_st = [head_stack(qkv, 2 * qk_width, sub, hs) for sub, hs in problems]
    beta = [col_stack(beta_all[rows_of(sub)], 0, hs) for sub, hs in problems]
    cg = [col_stack(cg_all[sub], B_V_HEADS, hs) for sub, hs in problems]
    gl = [col_stack(gl_all[sub], B_V_HEADS, hs) for sub, hs in problems]

    def diff_operands(c):
        c_hi, c_mid, c_lo = (p.astype(F32) for p in _split3(c))
        lmat = jnp.where(lane6 == 0, c_hi, jnp.where(lane6 == 1, c_mid, jnp.where(
            lane6 == 2, c_lo, jnp.where(lane6 < 6, 1.0, 0.0))))
        rmat = jnp.where(lane6 == 3, -c_hi, jnp.where(lane6 == 4, -c_mid, jnp.where(
            lane6 == 5, -c_lo, jnp.where(lane6 < 3, 1.0, 0.0))))
        return lmat.astype(BF16), rmat.astype(BF16)

    diff = [_dot_nt(*diff_operands(cg[g])) for g in every]
    k_bf = [k_st[g].astype(BF16) for g in every]
    gram = [_dot_nt(k_bf[g], k_bf[g]) for g in every]
    qk = [_dot_nt(q_st[g].astype(BF16), k_bf[g]) for g in every]
    decay = [jnp.where(incl, jnp.exp(jnp.where(incl, diff[g], 0.0)), 0.0) for g in every]
    a_mat = [jnp.where(strict, beta[g] * gram[g] * decay[g], 0.0) for g in every]
    p_mat = [(qk[g] * decay[g]).astype(BF16) for g in every]

    blk = _div_pow2(rr, B_INV_BASE) == _div_pow2(cc, B_INV_BASE)
    apow = [jnp.where(blk, a_mat[g], 0.0) for g in every]
    nmat = [-apow[g] for g in every]
    for _ in range((B_INV_BASE - 1).bit_length() - 1):
        ab = [apow[g].astype(BF16) for g in every]
        apow = [_dot(ab[g], ab[g]) for g in every]
        nmat = [nmat[g] + apow[g] + _dot(nmat[g].astype(BF16), apow[g].astype(BF16)) for g in every]
    size = B_INV_BASE
    while size < cl:
        size *= 2
        merged = _div_pow2(rr, size) == _div_pow2(cc, size)
        join = merged & jnp.logical_not(blk)
        low = [jnp.where(join, a_mat[g], 0.0) for g in every]
        nb = [nmat[g].astype(BF16) for g in every]
        dl = [low[g] + _dot(nb[g], low[g].astype(BF16)) for g in every]
        nmat = [nmat[g] - (dl[g] + _dot(dl[g].astype(BF16), nb[g])) for g in every]
        blk = merged
    ecg = [jnp.exp(cg[g]) for g in every]
    rhs = [jnp.concatenate([beta[g] * v_st[g], (beta[g] * ecg[g]) * k_st[g]], axis=1) for g in every]
    sol = [rhs[g] + _dot(nmat[g].astype(BF16), rhs[g].astype(BF16)) for g in every]
    qg = [q_st[g] * ecg[g] for g in every]
    kdec_t = [(k_st[g] * jnp.exp(gl[g] - cg[g])).T.astype(BF16) for g in every]

    live = min(rows_in, cl)
    for sub in range(nsub):
        mine = [g for g in every if problems[g][0] == sub]
        both = {}
        for g in mine:
            for idx, h in enumerate(problems[g][1]):
                rsl = slice(idx * cl, (idx + 1) * cl)
                lhs = jnp.concatenate([sol[g][rsl, B_HEAD:2 * B_HEAD], qg[g][rsl]], axis=0)
                both[h] = _dot(lhs.astype(BF16), s_scr[h].astype(BF16))
        u_st = {g: jnp.concatenate([sol[g][idx * cl:(idx + 1) * cl, 0:B_HEAD] - both[h][0:cl]
                                    for idx, h in enumerate(problems[g][1])], axis=0) for g in mine}
        o_st = {g: jnp.concatenate([both[h][cl:2 * cl] for h in problems[g][1]], axis=0)
                + _dot(p_mat[g], u_st[g].astype(BF16)) for g in mine}
        for g in mine:
            for idx, h in enumerate(problems[g][1]):
                u_own = jnp.where(rblk == idx, u_st[g], 0.0).astype(BF16)
                s_scr[h] = jnp.exp(gl[g][idx * cl:idx * cl + 1]) * s_scr[h] + _dot(kdec_t[g], u_own)
        out_rows = slice(sub * cl, sub * cl + live)
        for g in mine:
            for idx, h in enumerate(problems[g][1]):
                o_h = o_st[g][idx * cl:idx * cl + live]
                ms = jnp.mean(o_h * o_h, axis=1, keepdims=True)
                z_h = p_ref[out_rows, conv_dim + h * B_HEAD:conv_dim + (h + 1) * B_HEAD].astype(F32)
                res = o_h * lax.rsqrt(ms + NORM_EPS) * ng_ref[...] * _silu(z_h)
                o_ref[out_rows, h * B_HEAD:(h + 1) * B_HEAD] = res.astype(o_ref.dtype)


def _delta_call(proj, gates, norm_g, s0, raw=None):
    bq, t, n = proj.shape
    v_width = B_V_HEADS * B_HEAD
    rows_in = min(t, B_CHUNK * B_CHUNKS_PER_STEP)
    assert t % rows_in == 0 and (rows_in % B_CHUNK == 0 or rows_in < B_CHUNK)
    nchunk = t // rows_in
    activated = raw is None
    kern = functools.partial(_delta_kernel, t_valid=rows_in, activated=activated)
    row_spec = lambda width: pl.BlockSpec((None, rows_in, width), lambda b, c: (b, c, 0))
    state_spec = pl.BlockSpec((None, B_V_HEADS, B_HEAD, B_HEAD), lambda b, c: (b, 0, 0, 0))
    const_spec = lambda a: pl.BlockSpec(a.shape, lambda b, c: (0,) * a.ndim)
    out_specs = [row_spec(v_width), state_spec]
    out_shape = [jax.ShapeDtypeStruct((bq, t, v_width), BF16),
                 jax.ShapeDtypeStruct((bq, B_V_HEADS, B_HEAD, B_HEAD), F32)]
    scratch = [pltpu.VMEM((B_V_HEADS, B_HEAD, B_HEAD), F32)]
    if activated:
        in_specs = [row_spec(n), row_spec(LANES), const_spec(norm_g), state_spec]
        args = (proj, gates, norm_g, s0)
    else:
        cstate8, conv_w, alog_l, dtb_l = raw
        conv_dim = conv_w.shape[1]
        halo = SUBLANES * (4 // proj.dtype.itemsize)
        per_halo = max(rows_in // halo, 1)
        in_specs = [
            row_spec(n),
            pl.BlockSpec((None, halo, n), lambda b, c: (b, jnp.maximum(c * per_halo - 1, 0), 0)),
            row_spec(LANES),
            pl.BlockSpec((None, SUBLANES, conv_dim), lambda b, c: (b, 0, 0)),
            const_spec(conv_w), const_spec(alog_l), const_spec(dtb_l), const_spec(norm_g), state_spec]
        args = (proj, proj, gates, cstate8, conv_w, alog_l, dtb_l, norm_g, s0)
        out_specs.append(pl.BlockSpec((None, SUBLANES, conv_dim), lambda b, c: (b, 0, 0)))
        out_shape.append(jax.ShapeDtypeStruct((bq, SUBLANES, conv_dim), F32))
        scratch.append(pltpu.VMEM((max(rows_in, B_CHUNK) + halo, conv_dim), F32))
    return pl.pallas_call(
        kern,
        grid=(bq, nchunk),
        in_specs=in_specs,
        out_specs=out_specs,
        out_shape=out_shape,
        scratch_shapes=scratch,
        compiler_params=_params("arbitrary", "arbitrary"),
        name="gated_deltanet",
    )(*args)


def _left_rows(state):
    return jnp.pad(state, ((0, 0), (SUBLANES - state.shape[1], 0), (0, 0)))


def _rope_tables(pos):
    half = A_HEAD_DIM // 2
    inv = ROPE_THETA ** (-jnp.arange(half, dtype=F32) / half)
    ang = pos.astype(F32)[:, None] * inv[None, :]
    cos, sin = jnp.cos(ang), jnp.sin(ang)
    reps = LANES // A_HEAD_DIM
    cos_t = jnp.tile(jnp.concatenate([cos, cos], axis=1), (1, reps))
    sin_t = jnp.tile(jnp.concatenate([-sin, sin], axis=1), (1, reps))
    return cos_t[None], sin_t[None]


def _trunk(x, mods, per_row, pos, a_caches, b_ssm, b_conv, f_conv, wts):
    seqs, t, d = x.shape
    (norm_mix_g, norm_ffn_g, norm_final_g, a_w_in, a_w_out, b_w_in, b_conv_w, b_a_log, b_dt_bias,
     b_norm_g, b_w_out, ffn_w_up, ffn_conv_w, ffn_conv_b, ffn_w_down, expand) = wts
    d_ff = ffn_w_down.shape[1]
    width = A_HEADS * A_HEAD_DIM

    def rows_view(a):
        return a.reshape(1, seqs * t, a.shape[2]) if per_row else a

    def seq_view(a):
        return a.reshape(seqs, t, a.shape[2])

    def mod_view(v):
        if per_row:
            return jnp.repeat(v, t, axis=0)[None]
        return v[:, None, :]

    cos_t, sin_t = _rope_tables(pos)
    prompt = a_caches is None
    if prompt:
        assert t % max(w for w, _ in A_GROUPS) == 0
        perms = _perm_matrices()
        unperms = jnp.swapaxes(perms, 1, 2)
    if per_row:
        cos_t, sin_t = jnp.tile(cos_t, (1, seqs, 1)), jnp.tile(sin_t, (1, seqs, 1))
    gfin = norm_final_g.reshape(1, d)
    xr = rows_view(x)
    new_a, new_ssm, new_bconv, new_fconv = [], None, None, []
    depth = norm_mix_g.shape[0]
    for layer in range(depth):
        sh1, sc1, g1, sh2, sc2, g2 = (mod_view(v) for v in mods[layer])
        gmix = norm_mix_g[layer].reshape(1, d)
        if layer % 2 == 0:
            outs, lses = [], []
            if prompt:
                tail_rows = max(w for w, _ in A_GROUPS)
                qkv, tail = _nm_rope_call(xr, gmix, sc1, sh1, a_w_in, cos_t, sin_t, tail_rows, perms)
                for gi, (win, dil) in enumerate(A_GROUPS):
                    o_g, lse_g = _attn_prompt_call(qkv, gi, win, dil)
                    outs.append(o_g.reshape(seqs, t, width))
                    lses.append(lse_g.reshape(seqs, t, LANES))
                    for kind in (1, 2):
                        col = slice((3 * gi + kind) * width, (3 * gi + kind + 1) * width)
                        keep = _natural_order(tail[:, :, col], dil)[:, tail_rows - win:]
                        new_a.append(keep.reshape(1, seqs, win, A_HEADS, A_HEAD_DIM))
                xr = _attn_merge_proj_call(outs, lses, expand, a_w_out[0], xr, g1, gfin, False,
                                           unperms)
            else:
                (tail,) = _nm_rope_call(xr, gmix, sc1, sh1, a_w_in, cos_t, sin_t, seqs * t)
                for gi, (win, dil) in enumerate(A_GROUPS):
                    as_t = lambda c: jnp.transpose(c[0], (0, 2, 3, 1)).reshape(seqs, width, c.shape[2])
                    k_new, v_new, o_g, lse_g = _attn_sample_call(
                        seq_view(tail), gi, win, dil, as_t(a_caches[2 * gi]), as_t(a_caches[2 * gi + 1]))
                    outs.append(rows_view(o_g))
                    lses.append(rows_view(lse_g))
                    for buf in (k_new, v_new):
                        buf = buf.reshape(seqs, A_HEADS, A_HEAD_DIM, buf.shape[2])
                        new_a.append(jnp.transpose(buf, (0, 3, 1, 2))[None])
                xr = _attn_merge_proj_call(outs, lses, expand, a_w_out[0], xr, g1, gfin, False)
        else:
            conv_w = b_conv_w[0]
            conv_dim = conv_w.shape[1]
            conv0 = jnp.zeros((seqs, conv_w.shape[0] - 1, conv_dim), F32) if b_conv is None else b_conv[0]
            s0 = jnp.zeros((seqs, B_V_HEADS, B_HEAD, B_HEAD), F32) if b_ssm is None else b_ssm[0]
            lane_pad = lambda v: jnp.pad(v.reshape(1, -1), ((0, 0), (B_V_HEADS, LANES - 2 * B_V_HEADS)))
            alog_l, dtb_l = lane_pad(b_a_log[0]), lane_pad(b_dt_bias[0])
            norm_g = b_norm_g[0].reshape(1, B_HEAD)
            if prompt:
                proj, gates, conv_tail = _nm_delta_act_call(
                    xr, gmix, sc1, sh1, b_w_in[0], _left_rows(conv0), conv_w, alog_l, dtb_l)
                o_b, ssm1 = _delta_call(proj, gates, norm_g, s0)
                cnew = conv_tail[:, -1]
            else:
                proj, gates, _ = _nm_delta_in_call(
                    xr, gmix, sc1, sh1, b_w_in[0], conv_dim + B_V_HEADS * B_HEAD, conv_dim, F32)
                o_b, ssm1, cnew = _delta_call(seq_view(proj), seq_view(gates), norm_g, s0,
                                              (_left_rows(conv0), conv_w, alog_l, dtb_l))
            new_ssm = ssm1[None]
            new_bconv = cnew[None, :, SUBLANES - (conv_w.shape[0] - 1):, :]
            xr = _proj_res_call(rows_view(o_b), b_w_out[0], xr, g1, gfin, False)
        gffn = norm_ffn_g[layer].reshape(1, d)
        keep = ffn_conv_w.shape[1] - 1
        f0 = jnp.zeros((seqs, keep, d_ff), F32) if f_conv is None else f_conv[layer]
        conv_b = ffn_conv_b[layer].reshape(1, d_ff)
        mg2 = mods[layer][5][:, None, :]
        final = layer == depth - 1
        if prompt:
            xr, gate_tail = _ffn_fused_call(xr, gffn, sc2, sh2, ffn_w_up, _left_rows(f0),
                                            ffn_conv_w[layer], conv_b, ffn_w_down, mg2, gfin, final, layer)
            new_fconv.append(gate_tail[:, -1, SUBLANES - keep:, :])
        else:
            gate_pre, val = _nm_ffn_call(xr, gffn, sc2, sh2, ffn_w_up, layer)
            new_fconv.append(seq_view(gate_pre)[:, t - keep:, :])
            xr = _ffn_down_short_call(gate_pre, _left_rows(f0), val, ffn_conv_w[layer], conv_b,
                                      ffn_w_down, xr, g2, gfin, final, t, layer)
    return seq_view(xr), new_a, new_ssm, new_bconv, jnp.stack(new_fconv)


def kernel(x_prompt, x_sample, c_prompt, c_sample, cache_a_k_w128, cache_a_v_w128, cache_a_k_w512, cache_a_v_w512, cache_a_k_w2048, cache_a_v_w2048, state_b_ssm, state_b_conv, state_ffn_conv, norm_mix_g, norm_ffn_g, norm_final_g, w_mod, b_mod, a_w_in, a_w_out, b_w_in, b_conv_w, b_a_log, b_dt_bias, b_norm_g, b_w_out, ffn_w_up, ffn_conv_w, ffn_conv_b, ffn_w_down):
    bp, s, d = x_prompt.shape
    db, t_new, _ = x_sample.shape
    depth = w_mod.shape[0]

    c_all = jnp.concatenate([c_prompt, c_sample], axis=0)
    c_all = jnp.pad(c_all, ((0, (-c_all.shape[0]) % SUBLANES), (0, 0)))
    mod = _mod_call(c_all, w_mod, b_mod)

    def mods_of(lo, hi):
        return [[mod[l, lo:hi, k * d:(k + 1) * d] for k in range(6)] for l in range(depth)]

    b_w_in_p = jnp.pad(b_w_in, ((0, 0), (0, 0), (0, (-b_w_in.shape[2]) % LANES)))
    heads = jnp.arange(A_HEADS * A_HEAD_DIM) // A_HEAD_DIM
    half = (jnp.arange(LANES)[:, None] == heads[None, :]).astype(BF16)
    expand = jnp.concatenate([half, half], axis=0)
    wts = (norm_mix_g, norm_ffn_g, norm_final_g, a_w_in[0].astype(BF16), a_w_out.astype(BF16),
           b_w_in_p.astype(BF16), b_conv_w, b_a_log, b_dt_bias, b_norm_g, b_w_out.astype(BF16),
           ffn_w_up.astype(BF16), ffn_conv_w, ffn_conv_b, ffn_w_down.astype(BF16), expand)

    pos_p = jnp.arange(s, dtype=jnp.int32)
    pos_s = PAST_LEN + jnp.arange(t_new, dtype=jnp.int32)
    y_p, a_p, ssm_p, bconv_p, fconv_p = _trunk(
        x_prompt, mods_of(0, bp), False, pos_p, None, None, None, None, wts)
    a_caches = [cache_a_k_w128, cache_a_v_w128, cache_a_k_w512, cache_a_v_w512,
                cache_a_k_w2048, cache_a_v_w2048]
    y_s, a_s, ssm_s, bconv_s, fconv_s = _trunk(
        x_sample, mods_of(bp, bp + db), True, pos_s, a_caches, state_b_ssm, state_b_conv,
        state_ffn_conv, wts)
    return (y_p, y_s,
            a_p[0], a_s[0], a_p[1], a_s[1], a_p[2], a_s[2], a_p[3], a_s[3], a_p[4], a_s[4],
            a_p[5], a_s[5], ssm_p, ssm_s, bconv_p, bconv_s, fconv_p, fconv_s)
```

```python
import functools

import jax
import jax.numpy as jnp
from jax import lax
from jax.experimental import pallas as pl
from jax.experimental.pallas import tpu as pltpu

F32 = jnp.float32
BF16 = jnp.bfloat16

PAST_LEN = 16384
A_GROUPS = ((128, 1), (512, 4), (2048, 16))
A_HEADS = 16
A_HEAD_DIM = 64
A_ROW_TILE = 512
A_BLOCKS_PER_STEP = 4
FFN_ROW_TILE = 1024
ROPE_THETA = 10000.0
NORM_EPS = 1e-6
B_HEAD = 128
B_V_HEADS = 16
B_QK_HEADS = 8
B_CHUNK = 64
B_CHUNKS_PER_STEP = 2
B_STACK = 2
B_INV_BASE = 16
NEG_BIG = -1e30
LOG2E = 1.4426950408889634
LN2 = 0.6931471805599453

LANES = 128
SUBLANES = 8
VMEM_LIMIT_BYTES = 56 * 1024 * 1024


def _params(*sem):
    return pltpu.CompilerParams(dimension_semantics=sem, vmem_limit_bytes=VMEM_LIMIT_BYTES)


def _silu(x):
    return x * jax.nn.sigmoid(x)


def _dot(a, b):
    return jnp.dot(a, b, preferred_element_type=F32)


def _dot_nt(a, b):
    return lax.dot_general(a, b, (((1,), (1,)), ((), ())), preferred_element_type=F32)


def _shift_of(n):
    assert n > 0 and n & (n - 1) == 0, n
    return n.bit_length() - 1


def _div_pow2(v, n):
    return lax.shift_right_logical(v, _shift_of(n))


def _mod_pow2(v, n):
    assert n & (n - 1) == 0
    return v & (n - 1)


def _split3(x):
    hi = x.astype(BF16)
    r1 = x - hi.astype(F32)
    mid = r1.astype(BF16)
    lo = (r1 - mid.astype(F32)).astype(BF16)
    return hi, mid, lo


def _mod_kernel(c_ref, w_ref, b_ref, o_ref):
    cs = _silu(c_ref[...]).astype(BF16)
    o_ref[...] = _dot(cs, w_ref[...].astype(BF16)) + b_ref[...]


def _mod_call(c_all, w_mod, b_mod):
    depth, d, n = w_mod.shape
    rows = c_all.shape[0]
    tn = 1536
    return pl.pallas_call(
        _mod_kernel,
        grid=(depth, n // tn),
        in_specs=[
            pl.BlockSpec((rows, d), lambda l, j: (0, 0)),
            pl.BlockSpec((None, d, tn), lambda l, j: (l, 0, j)),
            pl.BlockSpec((None, 1, tn), lambda l, j: (l, 0, j)),
        ],
        out_specs=pl.BlockSpec((None, rows, tn), lambda l, j: (l, 0, j)),
        out_shape=jax.ShapeDtypeStruct((depth, rows, n), F32),
        compiler_params=_params("arbitrary", "arbitrary"),
        name="adaln_mod",
    )(c_all, w_mod, b_mod.reshape(depth, 1, n))


def _norm_mod(x, g, sc, sh):
    ms = jnp.mean(x * x, axis=-1, keepdims=True)
    return x * lax.rsqrt(ms + NORM_EPS) * g * (1.0 + sc) + sh


def _rope_tile(acc, cos, sin):
    lane = lax.broadcasted_iota(jnp.int32, (acc.shape[0], LANES), 1)
    first_half = _mod_pow2(lane, A_HEAD_DIM) < (A_HEAD_DIM // 2)
    outs = []
    for c in range(acc.shape[1] // LANES):
        a = acc[:, c * LANES:(c + 1) * LANES]
        swapped = jnp.where(first_half,
                            pltpu.roll(a, LANES - A_HEAD_DIM // 2, 1),
                            pltpu.roll(a, A_HEAD_DIM // 2, 1))
        outs.append(a * cos + swapped * sin)
    return jnp.concatenate(outs, axis=1)


def _nm_delta_in_kernel(x_ref, g_ref, sc_ref, sh_ref, w_ref, main_ref, gates_ref, tail_ref,
                        *, col_tile):
    h = _norm_mod(x_ref[...], g_ref[...], sc_ref[...], sh_ref[...]).astype(BF16)
    tm, n_main = main_ref.shape
    for c0 in range(0, n_main, col_tile):
        acc = _dot(h, w_ref[:, c0:c0 + col_tile])
        main_ref[:, c0:c0 + col_tile] = acc.astype(main_ref.dtype)
        if c0 < tail_ref.shape[1]:
            tail_ref[:, c0:c0 + col_tile] = acc[tm - SUBLANES:tm]
    gates_ref[...] = _dot(h, w_ref[:, n_main:n_main + LANES])


def _delta_gates(logits, alog, dtb):
    xg = logits + dtb
    softplus = jnp.maximum(xg, 0.0) + jnp.log1p(jnp.exp(-jnp.abs(xg)))
    return jax.nn.sigmoid(logits), -jnp.exp(alog) * softplus


def _l2n_heads(arr):
    outs = []
    for h0 in range(0, arr.shape[1], B_HEAD):
        a = arr[:, h0:h0 + B_HEAD]
        outs.append(a * lax.rsqrt(jnp.sum(a * a, axis=1, keepdims=True) + NORM_EPS))
    return jnp.concatenate(outs, axis=1)


def _nm_delta_act_kernel(x_ref, g_ref, sc_ref, sh_ref, w_ref, cstate_ref, cw_ref, alog_ref, dtb_ref,
                         main_ref, gates_ref, tail_ref, carry_scr, *, col_tile):
    tm, n_main = main_ref.shape
    width, conv_dim = cw_ref.shape
    qk_width = B_QK_HEADS * B_HEAD

    @pl.when(pl.program_id(1) == 0)
    def _():
        carry_scr[...] = cstate_ref[...]

    h = _norm_mod(x_ref[...], g_ref[...], sc_ref[...], sh_ref[...]).astype(BF16)
    ahead = _dot(h, w_ref[:, 0:col_tile])
    for c0 in range(0, n_main, col_tile):
        cols = slice(c0, c0 + col_tile)
        acc = ahead
        if c0 + col_tile < n_main:
            ahead = _dot(h, w_ref[:, c0 + col_tile:c0 + 2 * col_tile])
        if c0 < conv_dim:
            ext = jnp.concatenate([carry_scr[:, cols], acc], axis=0)
            y = cw_ref[width - 1:width, cols] * acc
            for back in range(1, width):
                y = y + (cw_ref[width - 1 - back:width - back, cols]
                         * pltpu.roll(ext, back, 0)[SUBLANES:SUBLANES + tm])
            carry_scr[:, cols] = acc[tm - SUBLANES:tm]
            tail_ref[:, cols] = acc[tm - SUBLANES:tm]
            acc = _silu(y)
            if c0 < 2 * qk_width:
                acc = _l2n_heads(acc)
            if c0 < qk_width:
                acc = acc * (B_HEAD ** -0.5)
        main_ref[:, cols] = acc.astype(main_ref.dtype)
    beta, decay = _delta_gates(_dot(h, w_ref[:, n_main:n_main + LANES]), alog_ref[...], dtb_ref[...])
    lane = lax.broadcasted_iota(jnp.int32, beta.shape, 1)
    gates_ref[...] = jnp.where(lane < B_V_HEADS, beta, decay)


def _nm_rope_kernel(x_ref, g_ref, sc_ref, sh_ref, w_ref, cos_ref, sin_ref, *rest, grouped):
    if grouped:
        perm_ref, o_ref, tail_ref, h_scr, tab_scr = rest
    else:
        tail_ref, h_scr = rest
    group = pl.program_id(2)

    @pl.when(group == 0)
    def _():
        h = _norm_mod(x_ref[...], g_ref[...], sc_ref[...], sh_ref[...]).astype(BF16)
        h_scr[0] = h
        if grouped:
            tab_scr[0, 0] = cos_ref[...]
            tab_scr[0, 1] = sin_ref[...]
            parts = jnp.concatenate(_split3(cos_ref[...]) + _split3(sin_ref[...]), axis=1)
            for p in range(perm_ref.shape[0]):
                h_scr[p + 1] = _dot(perm_ref[p], h).astype(BF16)
                moved = _dot(perm_ref[p], parts)
                for tab in range(2):
                    lo = 3 * tab * LANES
                    tab_scr[p + 1, tab] = (moved[:, lo:lo + LANES] + moved[:, lo + LANES:lo + 2 * LANES]
                                           + moved[:, lo + 2 * LANES:lo + 3 * LANES])

    h = h_scr[group] if grouped else h_scr[0]
    width = A_HEADS * A_HEAD_DIM
    if grouped:
        cos, sin = tab_scr[group, 0], tab_scr[group, 1]
    else:
        cos, sin = cos_ref[...], sin_ref[...]
    qscale = A_HEAD_DIM ** -0.5 * LOG2E
    raw = [_dot(h, w_ref[:, pl.ds(pl.multiple_of((3 * group + kind) * width, LANES), width)])
           for kind in range(3)]
    for kind in range(3):
        cols = slice(kind * width, (kind + 1) * width)
        val = raw[kind]
        if kind == 0:
            val = _rope_tile(val, cos * qscale, sin * qscale)
        elif kind == 1:
            val = _rope_tile(val, cos, sin)
        tail_ref[:, cols] = val
        if grouped:
            o_ref[kind] = val.astype(o_ref.dtype)


def _nm_ffn_kernel(x_ref, g_ref, sc_ref, sh_ref, w_ref, gate_ref, val_ref, *, col_tile):
    h = _norm_mod(x_ref[...], g_ref[...], sc_ref[...], sh_ref[...]).astype(BF16)
    d_ff = gate_ref.shape[1]
    for c0 in range(0, d_ff, col_tile):
        gate_ref[:, c0:c0 + col_tile] = _dot(h, w_ref[:, c0:c0 + col_tile])
        val_ref[:, c0:c0 + col_tile] = _dot(
            h, w_ref[:, d_ff + c0:d_ff + c0 + col_tile]).astype(val_ref.dtype)


def _row_tile(rows):
    return min(rows, 512)


def _resident(shape, layer=None):
    if layer is None:
        zeros = (0,) * len(shape)
        return pl.BlockSpec(shape, lambda *_: zeros, pipeline_mode=pl.Buffered(1))
    index = (layer,) + (0,) * (len(shape) - 1)
    return pl.BlockSpec((None,) + tuple(shape[1:]), lambda *_: index, pipeline_mode=pl.Buffered(1))


def _nm_common_specs(sc, tm, d):
    per_row = sc.shape[1] != 1
    mod_spec = (pl.BlockSpec((None, tm, d), lambda b, i, *_: (b, i, 0)) if per_row
                else pl.BlockSpec((None, 1, d), lambda b, i, *_: (b, 0, 0)))
    return [
        pl.BlockSpec((None, tm, d), lambda b, i, *_: (b, i, 0)),
        pl.BlockSpec((1, d), lambda b, i, *_: (0, 0)),
        mod_spec,
        mod_spec,
    ]


def _nm_delta_in_call(x, g, sc, sh, w, n_main, conv_dim, main_dtype):
    bq, rows, d = x.shape
    tm = _row_tile(rows)
    tiles = rows // tm
    kern = functools.partial(_nm_delta_in_kernel, col_tile=1024)
    return pl.pallas_call(
        kern,
        grid=(bq, tiles),
        in_specs=_nm_common_specs(sc, tm, d) + [_resident(w.shape)],
        out_specs=[pl.BlockSpec((None, tm, n_main), lambda b, i: (b, i, 0)),
                   pl.BlockSpec((None, tm, LANES), lambda b, i: (b, i, 0)),
                   pl.BlockSpec((None, None, SUBLANES, conv_dim), lambda b, i: (b, i, 0, 0))],
        out_shape=[jax.ShapeDtypeStruct((bq, rows, n_main), main_dtype),
                   jax.ShapeDtypeStruct((bq, rows, LANES), F32),
                   jax.ShapeDtypeStruct((bq, tiles, SUBLANES, conv_dim), F32)],
        compiler_params=_params("arbitrary", "arbitrary"),
        name="norm_mod_delta_in",
    )(x, g, sc, sh, w)


def _nm_delta_act_call(x, g, sc, sh, w, cstate8, conv_w, alog_l, dtb_l):
    bq, rows, d = x.shape
    conv_dim = conv_w.shape[1]
    n_main = w.shape[1] - LANES
    tm = _row_tile(rows)
    tiles = rows // tm
    kern = functools.partial(_nm_delta_act_kernel, col_tile=1024)
    return pl.pallas_call(
        kern,
        grid=(bq, tiles),
        in_specs=_nm_common_specs(sc, tm, d) + [
            _resident(w.shape),
            pl.BlockSpec((None, SUBLANES, conv_dim), lambda b, i: (b, 0, 0)),
            _resident(conv_w.shape), _resident(alog_l.shape), _resident(dtb_l.shape)],
        out_specs=[pl.BlockSpec((None, tm, n_main), lambda b, i: (b, i, 0)),
                   pl.BlockSpec((None, tm, LANES), lambda b, i: (b, i, 0)),
                   pl.BlockSpec((None, None, SUBLANES, conv_dim), lambda b, i: (b, i, 0, 0))],
        out_shape=[jax.ShapeDtypeStruct((bq, rows, n_main), BF16),
                   jax.ShapeDtypeStruct((bq, rows, LANES), F32),
                   jax.ShapeDtypeStruct((bq, tiles, SUBLANES, conv_dim), F32)],
        scratch_shapes=[pltpu.VMEM((SUBLANES, conv_dim), F32)],
        compiler_params=_params("arbitrary", "arbitrary"),
        name="norm_mod_delta_in_act",
    )(x, g, sc, sh, w, cstate8, conv_w, alog_l, dtb_l)


def _residue_major(a, dil, tm=A_ROW_TILE):
    lead, (rows, c) = a.shape[:-2], a.shape[-2:]
    a = a.reshape(lead + (rows // tm, tm // dil, dil, c))
    return jnp.swapaxes(a, -2, -3).reshape(lead + (rows, c))


def _natural_order(a, dil, tm=A_ROW_TILE):
    lead, (rows, c) = a.shape[:-2], a.shape[-2:]
    a = a.reshape(lead + (rows // tm, dil, tm // dil, c))
    return jnp.swapaxes(a, -2, -3).reshape(lead + (rows, c))


def _perm_matrices(tm=A_ROW_TILE):
    eye = jnp.eye(tm, dtype=BF16)
    return jnp.stack([_residue_major(eye, dil, tm) for _, dil in A_GROUPS if dil > 1])


def _nm_rope_call(x, g, sc, sh, w, cos, sin, tail_rows, perms=None):
    bq, rows, d = x.shape
    ngroups = len(A_GROUPS)
    gw = w.shape[1] // ngroups
    width = gw // 3
    grouped = perms is not None
    tm = A_ROW_TILE if grouped else _row_tile(min(rows, tail_rows))
    tiles = rows // tm
    tail_tile0 = (rows - tail_rows) // tm
    kern = functools.partial(_nm_rope_kernel, grouped=grouped)
    tab_spec = pl.BlockSpec((None, tm, LANES), lambda b, i, j: (0, i, 0))

    def tail_map(b, i, j):
        return (b, jnp.maximum(i - tail_tile0, 0), jnp.where(i >= tail_tile0, j, 0))

    in_specs = _nm_common_specs(sc, tm, d) + [_resident(w.shape), tab_spec, tab_spec]
    out_specs, out_shape, args = [], [], [x, g, sc, sh, w, cos, sin]
    if grouped:
        in_specs.append(_resident(perms.shape))
        args.append(perms)
        out_specs.append(pl.BlockSpec((None, 3, None, tm, width), lambda b, i, j: (b, j, i, 0, 0)))
        out_shape.append(jax.ShapeDtypeStruct((bq, 3 * ngroups, tiles, tm, width), BF16))
    out_specs.append(pl.BlockSpec((None, tm, gw), tail_map))
    out_shape.append(jax.ShapeDtypeStruct((bq, tail_rows, ngroups * gw), F32))
    n_h = 1 + (perms.shape[0] if grouped else 0)
    scratch = [pltpu.VMEM((n_h, tm, d), BF16)]
    if grouped:
        scratch.append(pltpu.VMEM((n_h, 2, tm, LANES), F32))
    return pl.pallas_call(
        kern,
        grid=(bq, tiles, ngroups),
        in_specs=in_specs,
        out_specs=out_specs,
        out_shape=out_shape,
        scratch_shapes=scratch,
        compiler_params=_params("arbitrary", "arbitrary", "arbitrary"),
        name="norm_mod_qkv_rope",
    )(*args)


def _nm_ffn_call(x, g, sc, sh, w, layer):
    bq, rows, d = x.shape
    d_ff = w.shape[2] // 2
    tm = _row_tile(rows)
    kern = functools.partial(_nm_ffn_kernel, col_tile=d_ff // 2)
    row_spec = pl.BlockSpec((None, tm, d_ff), lambda b, i: (b, i, 0))
    return pl.pallas_call(
        kern,
        grid=(bq, rows // tm),
        in_specs=_nm_common_specs(sc, tm, d) + [_resident(w.shape, layer)],
        out_specs=[row_spec, row_spec],
        out_shape=[jax.ShapeDtypeStruct((bq, rows, d_ff), F32),
                   jax.ShapeDtypeStruct((bq, rows, d_ff), BF16)],
        compiler_params=_params("arbitrary", "arbitrary"),
        name="norm_mod_ffn_up",
    )(x, g, sc, sh, w)


def _finish(y, x_ref, gate_ref, gfin_ref, o_ref, final_norm):
    xn = x_ref[...] + gate_ref[...] * y
    if final_norm:
        ms = jnp.mean(xn * xn, axis=-1, keepdims=True)
        xn = xn * lax.rsqrt(ms + NORM_EPS) * gfin_ref[...]
    o_ref[...] = xn


def _proj_res_kernel(a_ref, w_ref, x_ref, gate_ref, gfin_ref, o_ref, *, final_norm):
    _finish(_dot(a_ref[...], w_ref[...]), x_ref, gate_ref, gfin_ref, o_ref, final_norm)


def _attn_merge_proj_kernel(o0_ref, o1_ref, o2_ref, l0_ref, l1_ref, l2_ref, e_ref, w_ref,
                            x_ref, gate_ref, gfin_ref, *rest, final_norm, grouped):
    if grouped:
        unperm_ref, o_ref = rest
    else:
        (o_ref,) = rest
    outs, lses = [], []
    slot = 0
    for (_, dil), og_ref, lg_ref in zip(A_GROUPS, (o0_ref, o1_ref, o2_ref), (l0_ref, l1_ref, l2_ref)):
        if grouped and dil > 1:
            pt = unperm_ref[slot]
            slot += 1
            outs.append(_dot(pt, og_ref[...]))
            parts = _dot(pt, jnp.concatenate(_split3(lg_ref[...]), axis=1))
            lses.append(parts[:, 0:LANES] + parts[:, LANES:2 * LANES] + parts[:, 2 * LANES:3 * LANES])
        else:
            outs.append(og_ref[...].astype(F32))
            lses.append(lg_ref[...])
    m = jnp.maximum(jnp.maximum(lses[0], lses[1]), lses[2])
    exps = [jnp.exp(l - m) for l in lses]
    inv = 1.0 / (exps[0] + exps[1] + exps[2])
    merged = None
    for e, og in zip(exps, outs):
        wgt = e * inv
        hi = wgt.astype(BF16)
        lo = (wgt - hi.astype(F32)).astype(BF16)
        term = _dot(jnp.concatenate([hi, lo], axis=1), e_ref[...]) * og
        merged = term if merged is None else merged + term
    _finish(_dot(merged.astype(BF16), w_ref[...]), x_ref, gate_ref, gfin_ref, o_ref, final_norm)


def _res_specs(tm, d, gate):
    per_row = gate.shape[1] != 1
    gate_spec = (pl.BlockSpec((None, tm, d), lambda b, i: (b, i, 0)) if per_row
                 else pl.BlockSpec((None, 1, d), lambda b, i: (b, 0, 0)))
    return [pl.BlockSpec((None, tm, d), lambda b, i: (b, i, 0)),
            gate_spec,
            pl.BlockSpec((1, d), lambda b, i: (0, 0))]


def _proj_res_call(a, w, x, gate, gfin, final_norm):
    bq, rows, d = x.shape
    k = a.shape[2]
    tm = _row_tile(rows)
    kern = functools.partial(_proj_res_kernel, final_norm=final_norm)
    return pl.pallas_call(
        kern,
        grid=(bq, rows // tm),
        in_specs=[pl.BlockSpec((None, tm, k), lambda b, i: (b, i, 0)),
                  pl.BlockSpec((k, d), lambda b, i: (0, 0))] + _res_specs(tm, d, gate),
        out_specs=pl.BlockSpec((None, tm, d), lambda b, i: (b, i, 0)),
        out_shape=jax.ShapeDtypeStruct((bq, rows, d), F32),
        compiler_params=_params("arbitrary", "arbitrary"),
        name="proj_residual",
    )(a, w, x, gate, gfin)


def _attn_merge_proj_call(outs, lses, expand, w, x, gate, gfin, final_norm, unperms=None):
    bq, rows, d = x.shape
    grouped = unperms is not None
    tm = A_ROW_TILE if grouped else _row_tile(rows)
    kern = functools.partial(_attn_merge_proj_kernel, final_norm=final_norm, grouped=grouped)
    o_spec = pl.BlockSpec((None, tm, d), lambda b, i: (b, i, 0))
    l_spec = pl.BlockSpec((None, tm, LANES), lambda b, i: (b, i, 0))
    extra_specs, extra = [], []
    if grouped:
        extra_specs.append(pl.BlockSpec(unperms.shape, lambda b, i: (0, 0, 0)))
        extra.append(unperms)
    return pl.pallas_call(
        kern,
        grid=(bq, rows // tm),
        in_specs=[o_spec] * 3 + [l_spec] * 3 + [
            pl.BlockSpec(expand.shape, lambda b, i: (0, 0)),
            pl.BlockSpec(w.shape, lambda b, i: (0, 0))] + _res_specs(tm, d, gate) + extra_specs,
        out_specs=pl.BlockSpec((None, tm, d), lambda b, i: (b, i, 0)),
        out_shape=jax.ShapeDtypeStruct((bq, rows, d), F32),
        compiler_params=_params("arbitrary", "arbitrary"),
        name="attn_merge_proj_residual",
    )(*outs, *lses, expand, w, x, gate, gfin, *extra)


def _ffn_down_short_kernel(gate_ref, left_ref, val_ref, cw_ref, cb_ref, w_ref,
                           x_ref, mgate_ref, gfin_ref, o_ref, *, final_norm, seq_len):
    gate = gate_ref[...]
    width = cw_ref.shape[0]
    tpos = _mod_pow2(lax.broadcasted_iota(jnp.int32, gate.shape, 0), seq_len)
    y = cb_ref[...] + cw_ref[width - 1:width, :] * gate
    for back in range(1, width):
        shifted = jnp.where(tpos >= back, pltpu.roll(gate, back, 0), left_ref[back - 1])
        y = y + cw_ref[width - 1 - back:width - back, :] * shifted
    act = (_silu(y) * val_ref[...].astype(F32)).astype(BF16)
    _finish(_dot(act, w_ref[...]), x_ref, mgate_ref, gfin_ref, o_ref, final_norm)


def _ffn_down_short_call(gate_pre, state8, val, conv_w, conv_b, w, x, mgate, gfin, final_norm, seq_len,
                         layer):
    _, rows, d = x.shape
    f = gate_pre.shape[2]
    width = conv_w.shape[0]
    lefts = jnp.stack([jnp.roll(state8, back, axis=1).reshape(rows, f) for back in range(1, width)])
    kern = functools.partial(_ffn_down_short_kernel, final_norm=final_norm, seq_len=seq_len)
    whole = lambda a: pl.BlockSpec(a.shape, lambda i: (0,) * a.ndim)
    rows_of = lambda a: pl.BlockSpec((None,) + a.shape[1:], lambda i: (0, 0, 0))
    return pl.pallas_call(
        kern,
        grid=(1,),
        in_specs=[rows_of(gate_pre), whole(lefts), rows_of(val), whole(conv_w), whole(conv_b),
                  pl.BlockSpec((None,) + w.shape[1:], lambda i: (layer, 0, 0)),
                  rows_of(x), rows_of(mgate), whole(gfin)],
        out_specs=rows_of(x),
        out_shape=jax.ShapeDtypeStruct(x.shape, F32),
        compiler_params=_params("arbitrary"),
        name="ffn_conv_down_residual_short",
    )(gate_pre, lefts, val, conv_w, conv_b, w, x, mgate, gfin)


def _ffn_fused_kernel(x_ref, g_ref, sc_ref, sh_ref, wup_ref, state_ref, cw_ref, cb_ref, wdn_ref,
                      mgate_ref, gfin_ref, o_ref, gtail_ref, carry_scr, *, final_norm, col_tile):
    tm = x_ref.shape[0]
    d_ff = wdn_ref.shape[0]
    width = cw_ref.shape[0]

    @pl.when(pl.program_id(1) == 0)
    def _():
        carry_scr[...] = state_ref[...]

    h = _norm_mod(x_ref[...], g_ref[...], sc_ref[...], sh_ref[...]).astype(BF16)

    def up(c0):
        return (_dot(h, wup_ref[:, c0:c0 + col_tile]),
                _dot(h, wup_ref[:, d_ff + c0:d_ff + c0 + col_tile]))

    acc = None
    ahead = up(0)
    for c0 in range(0, d_ff, col_tile):
        cols = slice(c0, c0 + col_tile)
        gate, val = ahead
        if c0 + col_tile < d_ff:
            ahead = up(c0 + col_tile)
        ext = jnp.concatenate([carry_scr[:, cols], gate], axis=0)
        y = cb_ref[:, cols] + cw_ref[width - 1:width, cols] * gate
        for back in range(1, width):
            y = y + (cw_ref[width - 1 - back:width - back, cols]
                     * pltpu.roll(ext, back, 0)[SUBLANES:SUBLANES + tm])
        act = (_silu(y) * val).astype(BF16)
        part = _dot(act, wdn_ref[cols, :])
        acc = part if acc is None else acc + part
        carry_scr[:, cols] = gate[tm - SUBLANES:tm]
        gtail_ref[:, cols] = gate[tm - SUBLANES:tm]
    _finish(acc, x_ref, mgate_ref, gfin_ref, o_ref, final_norm)


def _ffn_fused_call(x, g, sc, sh, w_up, state8, conv_w, conv_b, w_down, mgate, gfin, final_norm,
                    layer):
    bq, rows, d = x.shape
    d_ff = w_down.shape[1]
    tm = min(rows, FFN_ROW_TILE)
    tiles = rows // tm
    col_tile = 2 * LANES
    assert d_ff % col_tile == 0
    kern = functools.partial(_ffn_fused_kernel, final_norm=final_norm, col_tile=col_tile)
    return pl.pallas_call(
        kern,
        grid=(bq, tiles),
        in_specs=_nm_common_specs(sc, tm, d) + [
            _resident(w_up.shape, layer),
            pl.BlockSpec((None, SUBLANES, d_ff), lambda b, i: (b, 0, 0)),
            _resident(conv_w.shape),
            _resident(conv_b.shape),
            _resident(w_down.shape, layer),
        ] + _res_specs(tm, d, mgate)[1:],
        out_specs=[pl.BlockSpec((None, tm, d), lambda b, i: (b, i, 0)),
                   pl.BlockSpec((None, None, SUBLANES, d_ff), lambda b, i: (b, i, 0, 0))],
        out_shape=[jax.ShapeDtypeStruct((bq, rows, d), F32),
                   jax.ShapeDtypeStruct((bq, tiles, SUBLANES, d_ff), F32)],
        scratch_shapes=[pltpu.VMEM((SUBLANES, d_ff), F32)],
        compiler_params=_params("arbitrary", "arbitrary"),
        name="conv_ffn",
    )(x, g, sc, sh, w_up, state8, conv_w, conv_b, w_down, mgate, gfin)


def _attn_prompt_kernel(q_ref, kp_ref, kc_ref, vp_ref, vc_ref, o_ref, lse_ref):
    blk = kp_ref.shape[0] * kp_ref.shape[1]
    prow = q_ref.shape[1]
    run = q_ref.shape[0] * prow
    nq = run // blk

    def rows_of(ref, cols):
        return jnp.concatenate([ref[p, :, cols] for p in range(ref.shape[0])], axis=0)

    n = pl.program_id(2)
    qi = lax.broadcasted_iota(jnp.int32, (blk, 2 * blk), 0)
    kj = lax.broadcasted_iota(jnp.int32, (blk, 2 * blk), 1)
    band = (kj >= qi) & (kj <= qi + blk)
    bias = jnp.where(band, 0.0, NEG_BIG)
    bias_first = jnp.where(band & ((kj >= blk) | (n > 0)), 0.0, NEG_BIG)
    lane = lax.broadcasted_iota(jnp.int32, (blk, LANES), 1)
    low = lane < A_HEAD_DIM
    m_all = [jnp.zeros((blk, LANES), F32) for _ in range(nq)]
    l_all = [jnp.ones((blk, LANES), F32) for _ in range(nq)]
    for hp in range(A_HEADS // 2):
        cols = slice(hp * LANES, (hp + 1) * LANES)
        q_run = rows_of(q_ref, cols)
        k_run = jnp.concatenate([rows_of(kp_ref, cols), rows_of(kc_ref, cols)], axis=0)
        v_run = jnp.concatenate([rows_of(vp_ref, cols), rows_of(vc_ref, cols)], axis=0)
        outs = []
        for j in range(nq):
            q2 = q_run[j * blk:(j + 1) * blk]
            k2 = k_run[j * blk:(j + 2) * blk]
            v2 = v_run[j * blk:(j + 2) * blk]
            mask = bias_first if j == 0 else bias
            pair = []
            for s in range(2):
                mine = low if s == 0 else jnp.logical_not(low)
                qh = jnp.where(mine, q2, jnp.zeros_like(q2))
                sc = _dot_nt(qh, k2) + mask
                m = jnp.max(sc, axis=1, keepdims=True)
                p = jnp.exp2(sc - m)
                l = jnp.sum(p, axis=1, keepdims=True)
                pair.append(_dot(p.astype(BF16), v2) / l)
                m_all[j] = jnp.where(lane == 2 * hp + s, m, m_all[j])
                l_all[j] = jnp.where(lane == 2 * hp + s, l, l_all[j])
            outs.append(jnp.where(low, pair[0], pair[1]).astype(o_ref.dtype))
        o_run = jnp.concatenate(outs, axis=0)
        for p in range(o_ref.shape[0]):
            o_ref[p, :, cols] = o_run[p * prow:(p + 1) * prow]
    lse_run = jnp.concatenate([(m_all[j] + jnp.log2(l_all[j])) * LN2 for j in range(nq)], axis=0)
    for p in range(lse_ref.shape[0]):
        lse_ref[p] = lse_run[p * prow:(p + 1) * prow]


def _attn_prompt_call(qkv, gi, win, dil):
    b, nkinds, tiles, tm, width = qkv.shape
    per_res = tm // dil
    qkv_g = qkv.reshape(b, nkinds, tiles, dil, per_res, width)
    blk = win // dil
    assert (tiles * tm) % win == 0
    nb = tiles * per_res // blk
    nq = min(A_BLOCKS_PER_STEP, nb)
    assert nb % nq == 0

    def geometry(rows):
        prow = min(rows, per_res)
        return rows // prow, prow

    def spec_of(rows, block_of_step, kind, lanes=width, lead=True):
        pieces, prow = geometry(rows)
        per_tile = per_res // prow

        def index(bb, r, i):
            j = jnp.maximum(block_of_step(i), 0)
            tail = (lax.div(j, per_tile), r, lax.rem(j, per_tile), 0)
            return (bb, 3 * gi + kind) + tail if lead else (bb,) + tail

        shape = (None, pieces, None, prow, lanes)
        return pl.BlockSpec((None,) + shape if lead else shape, index)

    cur = lambda kind: spec_of(nq * blk, lambda i: i, kind)
    prev = lambda kind: spec_of(blk, lambda i: nq * i - 1, kind)
    return pl.pallas_call(
        _attn_prompt_kernel,
        grid=(b, dil, nb // nq),
        in_specs=[cur(0), prev(1), cur(1), prev(2), cur(2)],
        out_specs=[spec_of(nq * blk, lambda i: i, 0, width, False),
                   spec_of(nq * blk, lambda i: i, 0, LANES, False)],
        out_shape=[jax.ShapeDtypeStruct((b, tiles, dil, per_res, width), BF16),
                   jax.ShapeDtypeStruct((b, tiles, dil, per_res, LANES), F32)],
        compiler_params=_params("arbitrary", "arbitrary", "arbitrary"),
        name="dilated_attn_prompt",
    )(qkv_g, qkv_g, qkv_g, qkv_g, qkv_g)


def _attn_sample_kernel(q_ref, kn_ref, vn_ref, kc_ref, vc_ref,
                        ko_ref, vo_ref, o_ref, lse_ref,
                        m_scr, l_scr, acc_scr, kcar_scr, vcar_scr, *, lb, dil):
    t_new = q_ref.shape[0]
    width = q_ref.shape[1]
    chunk = kc_ref.shape[1]
    rows = A_HEADS * t_new
    step = pl.program_id(1)
    c = pl.num_programs(1) - 1 - step

    rq = lax.broadcasted_iota(jnp.int32, (rows, width), 0)
    lq = lax.broadcasted_iota(jnp.int32, (rows, width), 1)
    own = _div_pow2(lq, A_HEAD_DIM) == _div_pow2(rq, t_new)
    qbd = jnp.where(own, jnp.concatenate([q_ref[...]] * A_HEADS, axis=0), 0.0).astype(BF16)

    def accumulate(keys_t, vals_t, valid):
        sc = jnp.where(valid, _dot(qbd, keys_t.astype(BF16)), NEG_BIG)
        m_old = m_scr[...]
        m_new = jnp.maximum(m_old, jnp.max(sc, axis=1, keepdims=True))
        alpha = jnp.exp2(m_old - m_new)
        p = jnp.where(valid, jnp.exp2(sc - m_new), 0.0)
        l_scr[...] = alpha * l_scr[...] + jnp.sum(p, axis=1, keepdims=True)
        acc_scr[...] = alpha * acc_scr[...] + _dot_nt(p.astype(BF16), vals_t.astype(BF16))
        m_scr[...] = m_new

    @pl.when(step == 0)
    def _():
        m_scr[...] = jnp.full_like(m_scr, NEG_BIG)
        l_scr[...] = jnp.zeros_like(l_scr)
        acc_scr[...] = jnp.zeros_like(acc_scr)
        pad = jnp.zeros((LANES - t_new, width), F32)
        kcar_scr[...] = jnp.concatenate([kn_ref[...], pad], axis=0).T
        vcar_scr[...] = jnp.concatenate([vn_ref[...], pad], axis=0).T
        tq2 = _mod_pow2(lax.broadcasted_iota(jnp.int32, (rows, LANES), 0), t_new)
        tk2 = lax.broadcasted_iota(jnp.int32, (rows, LANES), 1)
        d2 = tq2 - tk2
        accumulate(kcar_scr[...], vcar_scr[...],
                   (tk2 < t_new) & (d2 >= 0) & (_mod_pow2(d2, dil) == 0))

    lane = lax.broadcasted_iota(jnp.int32, (width, LANES), 1)
    keep = lane < LANES - t_new
    nblk = chunk // LANES
    for src_ref, car_ref, dst_ref in ((kc_ref, kcar_scr, ko_ref), (vc_ref, vcar_scr, vo_ref)):
        nxt = pltpu.roll(car_ref[...], LANES - t_new, 1)
        for b in reversed(range(nblk)):
            cur = pltpu.roll(src_ref[:, b * LANES:(b + 1) * LANES], LANES - t_new, 1)
            dst_ref[:, b * LANES:(b + 1) * LANES] = jnp.where(keep, cur, nxt)
            nxt = cur
        car_ref[...] = src_ref[:, 0:LANES]

    tq = _mod_pow2(lax.broadcasted_iota(jnp.int32, (rows, chunk), 0), t_new)
    ik = lax.broadcasted_iota(jnp.int32, (rows, chunk), 1) + c * chunk
    dist = lb + tq - ik
    accumulate(kc_ref[...], vc_ref[...], (_mod_pow2(dist, dil) == 0) & (dist <= lb))

    @pl.when(c == 0)
    def _():
        full = jnp.where(own, acc_scr[...] / l_scr[...], 0.0)
        o_ref[...] = jnp.sum(full.reshape(A_HEADS, t_new, width), axis=0).astype(o_ref.dtype)
        lse_col = (m_scr[...] + jnp.log2(l_scr[...])) * LN2
        rl = lax.broadcasted_iota(jnp.int32, (rows, LANES), 0)
        ll = lax.broadcasted_iota(jnp.int32, (rows, LANES), 1)
        spread = jnp.where(ll == _div_pow2(rl, t_new), lse_col, 0.0)
        lse_ref[...] = jnp.sum(spread.reshape(A_HEADS, t_new, LANES), axis=0)


def _attn_sample_call(proj, gi, win, dil, k_cache_t, v_cache_t):
    db, t_new, n = proj.shape
    width = A_HEADS * A_HEAD_DIM
    lb = k_cache_t.shape[2]
    assert lb == win and t_new == SUBLANES
    chunk = min(lb, 1024)
    nchunk = lb // chunk
    kern = functools.partial(_attn_sample_kernel, lb=lb, dil=dil)
    new_spec = lambda kind: pl.BlockSpec((None, t_new, width), lambda b, s: (b, 0, 3 * gi + kind))
    cur = pl.BlockSpec((None, width, chunk), lambda b, s: (b, 0, nchunk - 1 - s))
    return pl.pallas_call(
        kern,
        grid=(db, nchunk),
        in_specs=[new_spec(0), new_spec(1), new_spec(2), cur, cur],
        out_specs=[cur, cur,
                   pl.BlockSpec((None, t_new, width), lambda b, s: (b, 0, 0)),
                   pl.BlockSpec((None, t_new, LANES), lambda b, s: (b, 0, 0))],
        out_shape=[jax.ShapeDtypeStruct((db, width, lb), F32),
                   jax.ShapeDtypeStruct((db, width, lb), F32),
                   jax.ShapeDtypeStruct((db, t_new, width), BF16),
                   jax.ShapeDtypeStruct((db, t_new, LANES), F32)],
        scratch_shapes=[pltpu.VMEM((A_HEADS * t_new, 1), F32),
                        pltpu.VMEM((A_HEADS * t_new, 1), F32),
                        pltpu.VMEM((A_HEADS * t_new, width), F32),
                        pltpu.VMEM((width, LANES), F32),
                        pltpu.VMEM((width, LANES), F32)],
        compiler_params=_params("arbitrary", "arbitrary"),
        name="dilated_attn_sample",
    )(proj, proj, proj, k_cache_t, v_cache_t)


def _delta_kernel(*refs, t_valid, activated):
    if activated:
        p_ref, gates_ref, ng_ref, s0_ref, o_ref, ssm_ref, s_scr = refs
    else:
        (p_ref, prev_ref, gates_ref, cstate_ref, cw_ref, alog_ref, dtb_ref, ng_ref, s0_ref,
         o_ref, ssm_ref, cnew_ref, s_scr, ext_scr) = refs
    cl = B_CHUNK
    rows_in = p_ref.shape[0]
    nsub = max(rows_in // cl, 1)
    rows_pad = nsub * cl
    qk_width = B_QK_HEADS * B_HEAD
    conv_dim = 2 * qk_width + B_V_HEADS * B_HEAD
    c = pl.program_id(1)
    last = c == pl.num_programs(1) - 1

    @pl.when(c == 0)
    def _():
        s_scr[...] = s0_ref[...]

    if activated:
        assert t_valid == rows_pad
        qkv = p_ref[:, 0:conv_dim].astype(F32)
        qn, kn = qkv[:, 0:qk_width], qkv[:, qk_width:2 * qk_width]
        beta_all = g_all = gates_ref[...]
    else:
        qkv, beta_all, g_all = _delta_front(p_ref, prev_ref, gates_ref, cstate_ref, cw_ref, alog_ref,
                                            dtb_ref, cnew_ref, ext_scr, rows_pad, t_valid)
        qn = _l2n_heads(qkv[:, 0:qk_width]) * (B_HEAD ** -0.5)
        kn = _l2n_heads(qkv[:, qk_width:2 * qk_width])
    _delta_chunks(qkv, qn, kn, beta_all, g_all, p_ref, ng_ref, o_ref, s_scr, nsub)

    @pl.when(last)
    def _():
        ssm_ref[...] = s_scr[...]


def _delta_front(p_ref, prev_ref, gates_ref, cstate_ref, cw_ref, alog_ref, dtb_ref, cnew_ref, ext_scr,
                 rows_pad, t_valid):
    rows_in = p_ref.shape[0]
    conv_dim = cw_ref.shape[1]
    c = pl.program_id(1)
    last = c == pl.num_programs(1) - 1

    halo = prev_ref.shape[0]
    left = prev_ref[:, 0:conv_dim].astype(F32)
    if halo > SUBLANES:
        ext_scr[0:halo - SUBLANES, :] = left[0:halo - SUBLANES]
    ext_scr[halo - SUBLANES:halo, :] = jnp.where(c == 0, cstate_ref[...], left[halo - SUBLANES:halo])
    ext_scr[halo:halo + rows_in, :] = p_ref[:, 0:conv_dim].astype(F32)
    if rows_in < rows_pad:
        ext_scr[halo + rows_in:halo + rows_pad, :] = jnp.zeros((rows_pad - rows_in, conv_dim), F32)
    width = cw_ref.shape[0]
    ext = ext_scr[...]
    y = cw_ref[width - 1:width, :] * ext[halo:halo + rows_pad]
    for back in range(1, width):
        shifted = pltpu.roll(ext, back, 0)[halo:halo + rows_pad]
        y = y + cw_ref[width - 1 - back:width - back, :] * shifted
    qkv = _silu(y)
    if t_valid < rows_pad:
        rmask = lax.broadcasted_iota(jnp.int32, (rows_pad, 1), 0) < t_valid
        qkv = jnp.where(rmask, qkv, 0.0)

    @pl.when(last)
    def _():
        cnew_ref[...] = ext_scr[halo + t_valid - SUBLANES:halo + t_valid, :]

    graw = gates_ref[...]
    if rows_in < rows_pad:
        graw = jnp.concatenate([graw, jnp.zeros((rows_pad - rows_in, LANES), F32)], axis=0)
    beta_all, g_all = _delta_gates(graw, alog_ref[...], dtb_ref[...])
    if t_valid < rows_pad:
        beta_all = jnp.where(rmask, beta_all, 0.0)
        g_all = jnp.where(rmask, g_all, 0.0)
    return qkv, beta_all, g_all


def _delta_chunks(qkv, qn, kn, beta_all, g_all, p_ref, ng_ref, o_ref, s_scr, nsub):
    cl = B_CHUNK
    rows_in = p_ref.shape[0]
    qk_width = B_QK_HEADS * B_HEAD
    conv_dim = 2 * qk_width + B_V_HEADS * B_HEAD

    def rows_of(sub):
        return slice(sub * cl, (sub + 1) * cl)

    ri = lax.broadcasted_iota(jnp.int32, (2 * cl, cl), 0)
    ci = lax.broadcasted_iota(jnp.int32, (2 * cl, cl), 1)
    summat = jnp.where((ci <= ri) | (ri >= cl), 1.0, 0.0).astype(BF16)
    cg_all, gl_all = [], []
    for sub in range(nsub):
        sums = _dot(summat, jnp.concatenate(_split3(g_all[rows_of(sub)]), axis=1))
        sums = sums[:, 0:LANES] + sums[:, LANES:2 * LANES] + sums[:, 2 * LANES:3 * LANES]
        cg_all.append(sums[0:cl])
        gl_all.append(sums[cl:2 * cl])

    nst = B_STACK * cl
    rr = lax.broadcasted_iota(jnp.int32, (nst, nst), 0)
    cc = lax.broadcasted_iota(jnp.int32, (nst, nst), 1)
    same = _div_pow2(rr, cl) == _div_pow2(cc, cl)
    incl = same & (cc <= rr)
    strict = same & (cc < rr)
    lane6 = lax.broadcasted_iota(jnp.int32, (nst, LANES), 1)
    rblk = _div_pow2(lax.broadcasted_iota(jnp.int32, (nst, B_HEAD), 0), cl)

    def col_stack(arr, lane0, heads):
        return jnp.concatenate([arr[:, lane0 + h:lane0 + h + 1] for h in heads], axis=0)

    groups = [list(range(g * B_STACK, (g + 1) * B_STACK)) for g in range(B_V_HEADS // B_STACK)]
    problems = [(sub, hs) for sub in range(nsub) for hs in groups]
    every = range(len(problems))
    rep = B_V_HEADS // B_QK_HEADS

    def head_stack(arr, col0, sub, heads):
        return jnp.concatenate([arr[rows_of(sub), col0 + h * B_HEAD:col0 + (h + 1) * B_HEAD]
                                for h in heads], axis=0)

    q_st = [head_stack(qn, 0, sub, [h // rep for h in hs]) for sub, hs in problems]
    k_st = [head_stack(kn, 0, sub, [h // rep for h in hs]) for sub, hs in problems]
    v_st = [head_stack(qkv, 2 * qk_width, sub, hs) for sub, hs in problems]
    beta = [col_stack(beta_all[rows_of(sub)], 0, hs) for sub, hs in problems]
    cg = [col_stack(cg_all[sub], B_V_HEADS, hs) for sub, hs in problems]
    gl = [col_stack(gl_all[sub], B_V_HEADS, hs) for sub, hs in problems]

    def diff_operands(c):
        c_hi, c_mid, c_lo = (p.astype(F32) for p in _split3(c))
        lmat = jnp.where(lane6 == 0, c_hi, jnp.where(lane6 == 1, c_mid, jnp.where(
            lane6 == 2, c_lo, jnp.where(lane6 < 6, 1.0, 0.0))))
        rmat = jnp.where(lane6 == 3, -c_hi, jnp.where(lane6 == 4, -c_mid, jnp.where(
            lane6 == 5, -c_lo, jnp.where(lane6 < 3, 1.0, 0.0))))
        return lmat.astype(BF16), rmat.astype(BF16)

    diff = [_dot_nt(*diff_operands(cg[g])) for g in every]
    k_bf = [k_st[g].astype(BF16) for g in every]
    gram = [_dot_nt(k_bf[g], k_bf[g]) for g in every]
    qk = [_dot_nt(q_st[g].astype(BF16), k_bf[g]) for g in every]
    decay = [jnp.where(incl, jnp.exp(jnp.where(incl, diff[g], 0.0)), 0.0) for g in every]
    a_mat = [jnp.where(strict, beta[g] * gram[g] * decay[g], 0.0) for g in every]
    p_mat = [(qk[g] * decay[g]).astype(BF16) for g in every]

    blk = _div_pow2(rr, B_INV_BASE) == _div_pow2(cc, B_INV_BASE)
    apow = [jnp.where(blk, a_mat[g], 0.0) for g in every]
    nmat = [-apow[g] for g in every]
    for _ in range((B_INV_BASE - 1).bit_length() - 1):
        ab = [apow[g].astype(BF16) for g in every]
        apow = [_dot(ab[g], ab[g]) for g in every]
        nmat = [nmat[g] + apow[g] + _dot(nmat[g].astype(BF16), apow[g].astype(BF16)) for g in every]
    size = B_INV_BASE
    while size < cl:
        size *= 2
        merged = _div_pow2(rr, size) == _div_pow2(cc, size)
        join = merged & jnp.logical_not(blk)
        low = [jnp.where(join, a_mat[g], 0.0) for g in every]
        nb = [nmat[g].astype(BF16) for g in every]
        dl = [low[g] + _dot(nb[g], low[g].astype(BF16)) for g in every]
        nmat = [nmat[g] - (dl[g] + _dot(dl[g].astype(BF16), nb[g])) for g in every]
        blk = merged
    ecg = [jnp.exp(cg[g]) for g in every]
    rhs = [jnp.concatenate([beta[g] * v_st[g], (beta[g] * ecg[g]) * k_st[g]], axis=1) for g in every]
    sol = [rhs[g] + _dot(nmat[g].astype(BF16), rhs[g].astype(BF16)) for g in every]
    qg = [q_st[g] * ecg[g] for g in every]
    kdec_t = [(k_st[g] * jnp.exp(gl[g] - cg[g])).T.astype(BF16) for g in every]

    live = min(rows_in, cl)
    for sub in range(nsub):
        mine = [g for g in every if problems[g][0] == sub]
        both = {}
        for g in mine:
            for idx, h in enumerate(problems[g][1]):
                rsl = slice(idx * cl, (idx + 1) * cl)
                lhs = jnp.concatenate([sol[g][rsl, B_HEAD:2 * B_HEAD], qg[g][rsl]], axis=0)
                both[h] = _dot(lhs.astype(BF16), s_scr[h].astype(BF16))
        u_st = {g: jnp.concatenate([sol[g][idx * cl:(idx + 1) * cl, 0:B_HEAD] - both[h][0:cl]
                                    for idx, h in enumerate(problems[g][1])], axis=0) for g in mine}
        o_st = {g: jnp.concatenate([both[h][cl:2 * cl] for h in problems[g][1]], axis=0)
                + _dot(p_mat[g], u_st[g].astype(BF16)) for g in mine}
        for g in mine:
            for idx, h in enumerate(problems[g][1]):
                u_own = jnp.where(rblk == idx, u_st[g], 0.0).astype(BF16)
                s_scr[h] = jnp.exp(gl[g][idx * cl:idx * cl + 1]) * s_scr[h] + _dot(kdec_t[g], u_own)
        out_rows = slice(sub * cl, sub * cl + live)
        for g in mine:
            for idx, h in enumerate(problems[g][1]):
                o_h = o_st[g][idx * cl:idx * cl + live]
                ms = jnp.mean(o_h * o_h, axis=1, keepdims=True)
                z_h = p_ref[out_rows, conv_dim + h * B_HEAD:conv_dim + (h + 1) * B_HEAD].astype(F32)
                res = o_h * lax.rsqrt(ms + NORM_EPS) * ng_ref[...] * _silu(z_h)
                o_ref[out_rows, h * B_HEAD:(h + 1) * B_HEAD] = res.astype(o_ref.dtype)


def _delta_call(proj, gates, norm_g, s0, raw=None):
    bq, t, n = proj.shape
    v_width = B_V_HEADS * B_HEAD
    rows_in = min(t, B_CHUNK * B_CHUNKS_PER_STEP)
    assert t % rows_in == 0 and (rows_in % B_CHUNK == 0 or rows_in < B_CHUNK)
    nchunk = t // rows_in
    activated = raw is None
    kern = functools.partial(_delta_kernel, t_valid=rows_in, activated=activated)
    row_spec = lambda width: pl.BlockSpec((None, rows_in, width), lambda b, c: (b, c, 0))
    state_spec = pl.BlockSpec((None, B_V_HEADS, B_HEAD, B_HEAD), lambda b, c: (b, 0, 0, 0))
    const_spec = lambda a: pl.BlockSpec(a.shape, lambda b, c: (0,) * a.ndim)
    out_specs = [row_spec(v_width), state_spec]
    out_shape = [jax.ShapeDtypeStruct((bq, t, v_width), BF16),
                 jax.ShapeDtypeStruct((bq, B_V_HEADS, B_HEAD, B_HEAD), F32)]
    scratch = [pltpu.VMEM((B_V_HEADS, B_HEAD, B_HEAD), F32)]
    if activated:
        in_specs = [row_spec(n), row_spec(LANES), const_spec(norm_g), state_spec]
        args = (proj, gates, norm_g, s0)
    else:
        cstate8, conv_w, alog_l, dtb_l = raw
        conv_dim = conv_w.shape[1]
        halo = SUBLANES * (4 // proj.dtype.itemsize)
        per_halo = max(rows_in // halo, 1)
        in_specs = [
            row_spec(n),
            pl.BlockSpec((None, halo, n), lambda b, c: (b, jnp.maximum(c * per_halo - 1, 0), 0)),
            row_spec(LANES),
            pl.BlockSpec((None, SUBLANES, conv_dim), lambda b, c: (b, 0, 0)),
            const_spec(conv_w), const_spec(alog_l), const_spec(dtb_l), const_spec(norm_g), state_spec]
        args = (proj, proj, gates, cstate8, conv_w, alog_l, dtb_l, norm_g, s0)
        out_specs.append(pl.BlockSpec((None, SUBLANES, conv_dim), lambda b, c: (b, 0, 0)))
        out_shape.append(jax.ShapeDtypeStruct((bq, SUBLANES, conv_dim), F32))
        scratch.append(pltpu.VMEM((max(rows_in, B_CHUNK) + halo, conv_dim), F32))
    return pl.pallas_call(
        kern,
        grid=(bq, nchunk),
        in_specs=in_specs,
        out_specs=out_specs,
        out_shape=out_shape,
        scratch_shapes=scratch,
        compiler_params=_params("arbitrary", "arbitrary"),
        name="gated_deltanet",
    )(*args)


def _left_rows(state):
    return jnp.pad(state, ((0, 0), (SUBLANES - state.shape[1], 0), (0, 0)))


def _rope_tables(pos):
    half = A_HEAD_DIM // 2
    inv = ROPE_THETA ** (-jnp.arange(half, dtype=F32) / half)
    ang = pos.astype(F32)[:, None] * inv[None, :]
    cos, sin = jnp.cos(ang), jnp.sin(ang)
    reps = LANES // A_HEAD_DIM
    cos_t = jnp.tile(jnp.concatenate([cos, cos], axis=1), (1, reps))
    sin_t = jnp.tile(jnp.concatenate([-sin, sin], axis=1), (1, reps))
    return cos_t[None], sin_t[None]


def _trunk(x, mods, per_row, pos, a_caches, b_ssm, b_conv, f_conv, wts):
    seqs, t, d = x.shape
    (norm_mix_g, norm_ffn_g, norm_final_g, a_w_in, a_w_out, b_w_in, b_conv_w, b_a_log, b_dt_bias,
     b_norm_g, b_w_out, ffn_w_up, ffn_conv_w, ffn_conv_b, ffn_w_down, expand) = wts
    d_ff = ffn_w_down.shape[1]
    width = A_HEADS * A_HEAD_DIM

    def rows_view(a):
        return a.reshape(1, seqs * t, a.shape[2]) if per_row else a

    def seq_view(a):
        return a.reshape(seqs, t, a.shape[2])

    def mod_view(v):
        if per_row:
            return jnp.repeat(v, t, axis=0)[None]
        return v[:, None, :]

    cos_t, sin_t = _rope_tables(pos)
    prompt = a_caches is None
    if prompt:
        assert t % max(w for w, _ in A_GROUPS) == 0
        perms = _perm_matrices()
        unperms = jnp.swapaxes(perms, 1, 2)
    if per_row:
        cos_t, sin_t = jnp.tile(cos_t, (1, seqs, 1)), jnp.tile(sin_t, (1, seqs, 1))
    gfin = norm_final_g.reshape(1, d)
    xr = rows_view(x)
    new_a, new_ssm, new_bconv, new_fconv = [], None, None, []
    depth = norm_mix_g.shape[0]
    for layer in range(depth):
        sh1, sc1, g1, sh2, sc2, g2 = (mod_view(v) for v in mods[layer])
        gmix = norm_mix_g[layer].reshape(1, d)
        if layer % 2 == 0:
            outs, lses = [], []
            if prompt:
                tail_rows = max(w for w, _ in A_GROUPS)
                qkv, tail = _nm_rope_call(xr, gmix, sc1, sh1, a_w_in, cos_t, sin_t, tail_rows, perms)
                for gi, (win, dil) in enumerate(A_GROUPS):
                    o_g, lse_g = _attn_prompt_call(qkv, gi, win, dil)
                    outs.append(o_g.reshape(seqs, t, width))
                    lses.append(lse_g.reshape(seqs, t, LANES))
                    for kind in (1, 2):
                        col = slice((3 * gi + kind) * width, (3 * gi + kind + 1) * width)
                        keep = _natural_order(tail[:, :, col], dil)[:, tail_rows - win:]
                        new_a.append(keep.reshape(1, seqs, win, A_HEADS, A_HEAD_DIM))
                xr = _attn_merge_proj_call(outs, lses, expand, a_w_out[0], xr, g1, gfin, False,
                                           unperms)
            else:
                (tail,) = _nm_rope_call(xr, gmix, sc1, sh1, a_w_in, cos_t, sin_t, seqs * t)
                for gi, (win, dil) in enumerate(A_GROUPS):
                    as_t = lambda c: jnp.transpose(c[0], (0, 2, 3, 1)).reshape(seqs, width, c.shape[2])
                    k_new, v_new, o_g, lse_g = _attn_sample_call(
                        seq_view(tail), gi, win, dil, as_t(a_caches[2 * gi]), as_t(a_caches[2 * gi + 1]))
                    outs.append(rows_view(o_g))
                    lses.append(rows_view(lse_g))
                    for buf in (k_new, v_new):
                        buf = buf.reshape(seqs, A_HEADS, A_HEAD_DIM, buf.shape[2])
                        new_a.append(jnp.transpose(buf, (0, 3, 1, 2))[None])
                xr = _attn_merge_proj_call(outs, lses, expand, a_w_out[0], xr, g1, gfin, False)
        else:
            conv_w = b_conv_w[0]
            conv_dim = conv_w.shape[1]
            conv0 = jnp.zeros((seqs, conv_w.shape[0] - 1, conv_dim), F32) if b_conv is None else b_conv[0]
            s0 = jnp.zeros((seqs, B_V_HEADS, B_HEAD, B_HEAD), F32) if b_ssm is None else b_ssm[0]
            lane_pad = lambda v: jnp.pad(v.reshape(1, -1), ((0, 0), (B_V_HEADS, LANES - 2 * B_V_HEADS)))
            alog_l, dtb_l = lane_pad(b_a_log[0]), lane_pad(b_dt_bias[0])
            norm_g = b_norm_g[0].reshape(1, B_HEAD)
            if prompt:
                proj, gates, conv_tail = _nm_delta_act_call(
                    xr, gmix, sc1, sh1, b_w_in[0], _left_rows(conv0), conv_w, alog_l, dtb_l)
                o_b, ssm1 = _delta_call(proj, gates, norm_g, s0)
                cnew = conv_tail[:, -1]
            else:
                proj, gates, _ = _nm_delta_in_call(
                    xr, gmix, sc1, sh1, b_w_in[0], conv_dim + B_V_HEADS * B_HEAD, conv_dim, F32)
                o_b, ssm1, cnew = _delta_call(seq_view(proj), seq_view(gates), norm_g, s0,
                                              (_left_rows(conv0), conv_w, alog_l, dtb_l))
            new_ssm = ssm1[None]
            new_bconv = cnew[None, :, SUBLANES - (conv_w.shape[0] - 1):, :]
            xr = _proj_res_call(rows_view(o_b), b_w_out[0], xr, g1, gfin, False)
        gffn = norm_ffn_g[layer].reshape(1, d)
        keep = ffn_conv_w.shape[1] - 1
        f0 = jnp.zeros((seqs, keep, d_ff), F32) if f_conv is None else f_conv[layer]
        conv_b = ffn_conv_b[layer].reshape(1, d_ff)
        mg2 = mods[layer][5][:, None, :]
        final = layer == depth - 1
        if prompt:
            xr, gate_tail = _ffn_fused_call(xr, gffn, sc2, sh2, ffn_w_up, _left_rows(f0),
                                            ffn_conv_w[layer], conv_b, ffn_w_down, mg2, gfin, final, layer)
            new_fconv.append(gate_tail[:, -1, SUBLANES - keep:, :])
        else:
            gate_pre, val = _nm_ffn_call(xr, gffn, sc2, sh2, ffn_w_up, layer)
            new_fconv.append(seq_view(gate_pre)[:, t - keep:, :])
            xr = _ffn_down_short_call(gate_pre, _left_rows(f0), val, ffn_conv_w[layer], conv_b,
                                      ffn_w_down, xr, g2, gfin, final, t, layer)
    return seq_view(xr), new_a, new_ssm, new_bconv, jnp.stack(new_fconv)


def kernel(x_prompt, x_sample, c_prompt, c_sample, cache_a_k_w128, cache_a_v_w128, cache_a_k_w512, cache_a_v_w512, cache_a_k_w2048, cache_a_v_w2048, state_b_ssm, state_b_conv, state_ffn_conv, norm_mix_g, norm_ffn_g, norm_final_g, w_mod, b_mod, a_w_in, a_w_out, b_w_in, b_conv_w, b_a_log, b_dt_bias, b_norm_g, b_w_out, ffn_w_up, ffn_conv_w, ffn_conv_b, ffn_w_down):
    bp, s, d = x_prompt.shape
    db, t_new, _ = x_sample.shape
    depth = w_mod.shape[0]

    c_all = jnp.concatenate([c_prompt, c_sample], axis=0)
    c_all = jnp.pad(c_all, ((0, (-c_all.shape[0]) % SUBLANES), (0, 0)))
    mod = _mod_call(c_all, w_mod, b_mod)

    def mods_of(lo, hi):
        return [[mod[l, lo:hi, k * d:(k + 1) * d] for k in range(6)] for l in range(depth)]

    b_w_in_p = jnp.pad(b_w_in, ((0, 0), (0, 0), (0, (-b_w_in.shape[2]) % LANES)))
    heads = jnp.arange(A_HEADS * A_HEAD_DIM) // A_HEAD_DIM
    half = (jnp.arange(LANES)[:, None] == heads[None, :]).astype(BF16)
    expand = jnp.concatenate([half, half], axis=0)
    wts = (norm_mix_g, norm_ffn_g, norm_final_g, a_w_in[0].astype(BF16), a_w_out.astype(BF16),
           b_w_in_p.astype(BF16), b_conv_w, b_a_log, b_dt_bias, b_norm_g, b_w_out.astype(BF16),
           ffn_w_up.astype(BF16), ffn_conv_w, ffn_conv_b, ffn_w_down.astype(BF16), expand)

    pos_p = jnp.arange(s, dtype=jnp.int32)
    pos_s = PAST_LEN + jnp.arange(t_new, dtype=jnp.int32)
    y_p, a_p, ssm_p, bconv_p, fconv_p = _trunk(
        x_prompt, mods_of(0, bp), False, pos_p, None, None, None, None, wts)
    a_caches = [cache_a_k_w128, cache_a_v_w128, cache_a_k_w512, cache_a_v_w512,
                cache_a_k_w2048, cache_a_v_w2048]
    y_s, a_s, ssm_s, bconv_s, fconv_s = _trunk(
        x_sample, mods_of(bp, bp + db), True, pos_s, a_caches, state_b_ssm, state_b_conv,
        state_ffn_conv, wts)
    return (y_p, y_s,
            a_p[0], a_s[0], a_p[1], a_s[1], a_p[2], a_s[2], a_p[3], a_s[3], a_p[4], a_s[4],
            a_p[5], a_s[5], ssm_p, ssm_s, bconv_p, bconv_s, fconv_p, fconv_s)
```

```python
import functools

import jax
import jax.numpy as jnp
from jax import lax
from jax.experimental import pallas as pl
from jax.experimental.pallas import tpu as pltpu

F32 = jnp.float32
BF16 = jnp.bfloat16

PAST_LEN = 16384
A_GROUPS = ((128, 1), (512, 4), (2048, 16))
A_HEADS = 16
A_HEAD_DIM = 64
A_ROW_TILE = 512
A_BLOCKS_PER_STEP = 4
FFN_ROW_TILE = 1024
ROPE_THETA = 10000.0
NORM_EPS = 1e-6
B_HEAD = 128
B_V_HEADS = 16
B_QK_HEADS = 8
B_CHUNK = 64
B_CHUNKS_PER_STEP = 2
B_STACK = 2
B_INV_BASE = 16
NEG_BIG = -1e30
LOG2E = 1.4426950408889634
LN2 = 0.6931471805599453

LANES = 128
SUBLANES = 8
VMEM_LIMIT_BYTES = 56 * 1024 * 1024


def _params(*sem):
    return pltpu.CompilerParams(dimension_semantics=sem, vmem_limit_bytes=VMEM_LIMIT_BYTES)


def _silu(x):
    return x * jax.nn.sigmoid(x)


def _dot(a, b):
    return jnp.dot(a, b, preferred_element_type=F32)


def _dot_nt(a, b):
    return lax.dot_general(a, b, (((1,), (1,)), ((), ())), preferred_element_type=F32)


def _shift_of(n):
    assert n > 0 and n & (n - 1) == 0, n
    return n.bit_length() - 1


def _div_pow2(v, n):
    return lax.shift_right_logical(v, _shift_of(n))


def _mod_pow2(v, n):
    assert n & (n - 1) == 0
    return v & (n - 1)


def _split3(x):
    hi = x.astype(BF16)
    r1 = x - hi.astype(F32)
    mid = r1.astype(BF16)
    lo = (r1 - mid.astype(F32)).astype(BF16)
    return hi, mid, lo


def _mod_kernel(c_ref, w_ref, b_ref, o_ref):
    cs = _silu(c_ref[...]).astype(BF16)
    o_ref[...] = _dot(cs, w_ref[...].astype(BF16)) + b_ref[...]


def _mod_call(c_all, w_mod, b_mod):
    depth, d, n = w_mod.shape
    rows = c_all.shape[0]
    tn = 1536
    return pl.pallas_call(
        _mod_kernel,
        grid=(depth, n // tn),
        in_specs=[
            pl.BlockSpec((rows, d), lambda l, j: (0, 0)),
            pl.BlockSpec((None, d, tn), lambda l, j: (l, 0, j)),
            pl.BlockSpec((None, 1, tn), lambda l, j: (l, 0, j)),
        ],
        out_specs=pl.BlockSpec((None, rows, tn), lambda l, j: (l, 0, j)),
        out_shape=jax.ShapeDtypeStruct((depth, rows, n), F32),
        compiler_params=_params("arbitrary", "arbitrary"),
        name="adaln_mod",
    )(c_all, w_mod, b_mod.reshape(depth, 1, n))


def _norm_mod(x, g, sc, sh):
    ms = jnp.mean(x * x, axis=-1, keepdims=True)
    return x * lax.rsqrt(ms + NORM_EPS) * g * (1.0 + sc) + sh


def _rope_tile(acc, cos, sin):
    lane = lax.broadcasted_iota(jnp.int32, (acc.shape[0], LANES), 1)
    first_half = _mod_pow2(lane, A_HEAD_DIM) < (A_HEAD_DIM // 2)
    outs = []
    for c in range(acc.shape[1] // LANES):
        a = acc[:, c * LANES:(c + 1) * LANES]
        swapped = jnp.where(first_half,
                            pltpu.roll(a, LANES - A_HEAD_DIM // 2, 1),
                            pltpu.roll(a, A_HEAD_DIM // 2, 1))
        outs.append(a * cos + swapped * sin)
    return jnp.concatenate(outs, axis=1)


def _nm_delta_in_kernel(x_ref, g_ref, sc_ref, sh_ref, w_ref, main_ref, gates_ref, tail_ref,
                        *, col_tile):
    h = _norm_mod(x_ref[...], g_ref[...], sc_ref[...], sh_ref[...]).astype(BF16)
    tm, n_main = main_ref.shape
    for c0 in range(0, n_main, col_tile):
        acc = _dot(h, w_ref[:, c0:c0 + col_tile])
        main_ref[:, c0:c0 + col_tile] = acc.astype(main_ref.dtype)
        if c0 < tail_ref.shape[1]:
            tail_ref[:, c0:c0 + col_tile] = acc[tm - SUBLANES:tm]
    gates_ref[...] = _dot(h, w_ref[:, n_main:n_main + LANES])


def _delta_gates(logits, alog, dtb):
    xg = logits + dtb
    softplus = jnp.maximum(xg, 0.0) + jnp.log1p(jnp.exp(-jnp.abs(xg)))
    return jax.nn.sigmoid(logits), -jnp.exp(alog) * softplus


def _l2n_heads(arr):
    outs = []
    for h0 in range(0, arr.shape[1], B_HEAD):
        a = arr[:, h0:h0 + B_HEAD]
        outs.append(a * lax.rsqrt(jnp.sum(a * a, axis=1, keepdims=True) + NORM_EPS))
    return jnp.concatenate(outs, axis=1)


def _nm_delta_act_kernel(x_ref, g_ref, sc_ref, sh_ref, w_ref, cstate_ref, cw_ref, alog_ref, dtb_ref,
                         main_ref, gates_ref, tail_ref, carry_scr, *, col_tile):
    tm, n_main = main_ref.shape
    width, conv_dim = cw_ref.shape
    qk_width = B_QK_HEADS * B_HEAD

    @pl.when(pl.program_id(1) == 0)
    def _():
        carry_scr[...] = cstate_ref[...]

    h = _norm_mod(x_ref[...], g_ref[...], sc_ref[...], sh_ref[...]).astype(BF16)
    ahead = _dot(h, w_ref[:, 0:col_tile])
    for c0 in range(0, n_main, col_tile):
        cols = slice(c0, c0 + col_tile)
        acc = ahead
        if c0 + col_tile < n_main:
            ahead = _dot(h, w_ref[:, c0 + col_tile:c0 + 2 * col_tile])
        if c0 < conv_dim:
            ext = jnp.concatenate([carry_scr[:, cols], acc], axis=0)
            y = cw_ref[width - 1:width, cols] * acc
            for back in range(1, width):
                y = y + (cw_ref[width - 1 - back:width - back, cols]
                         * pltpu.roll(ext, back, 0)[SUBLANES:SUBLANES + tm])
            carry_scr[:, cols] = acc[tm - SUBLANES:tm]
            tail_ref[:, cols] = acc[tm - SUBLANES:tm]
            acc = _silu(y)
            if c0 < 2 * qk_width:
                acc = _l2n_heads(acc)
            if c0 < qk_width:
                acc = acc * (B_HEAD ** -0.5)
        main_ref[:, cols] = acc.astype(main_ref.dtype)
    beta, decay = _delta_gates(_dot(h, w_ref[:, n_main:n_main + LANES]), alog_ref[...], dtb_ref[...])
    lane = lax.broadcasted_iota(jnp.int32, beta.shape, 1)
    gates_ref[...] = jnp.where(lane < B_V_HEADS, beta, decay)


def _nm_rope_kernel(x_ref, g_ref, sc_ref, sh_ref, w_ref, cos_ref, sin_ref, *rest, grouped,
                    tail_tile0):
    if grouped:
        perm_ref, wvt_ref, o_ref, tailt_ref, h_scr, tab_scr = rest
    else:
        tail_ref, h_scr = rest
    group = pl.program_id(2)

    @pl.when(group == 0)
    def _():
        h = _norm_mod(x_ref[...], g_ref[...], sc_ref[...], sh_ref[...]).astype(BF16)
        h_scr[0] = h
        if grouped:
            tab_scr[0, 0] = cos_ref[...]
            tab_scr[0, 1] = sin_ref[...]
            parts = jnp.concatenate(_split3(cos_ref[...]) + _split3(sin_ref[...]), axis=1)
            for p in range(perm_ref.shape[0]):
                h_scr[p + 1] = _dot(perm_ref[p], h).astype(BF16)
                moved = _dot(perm_ref[p], parts)
                for tab in range(2):
                    lo = 3 * tab * LANES
                    tab_scr[p + 1, tab] = (moved[:, lo:lo + LANES] + moved[:, lo + LANES:lo + 2 * LANES]
                                           + moved[:, lo + 2 * LANES:lo + 3 * LANES])

    h = h_scr[group] if grouped else h_scr[0]
    width = A_HEADS * A_HEAD_DIM
    if grouped:
        cos, sin = tab_scr[group, 0], tab_scr[group, 1]
    else:
        cos, sin = cos_ref[...], sin_ref[...]
    qscale = A_HEAD_DIM ** -0.5 * LOG2E
    def w_block(kind):
        return w_ref[:, pl.ds(pl.multiple_of((3 * group + kind) * width, LANES), width)]

    raw = [_dot(h, w_block(kind)) for kind in range(3)]
    for kind in range(3):
        val = raw[kind]
        if kind == 0:
            val = _rope_tile(val, cos * qscale, sin * qscale)
        elif kind == 1:
            val = _rope_tile(val, cos, sin)
        if grouped:
            o_ref[kind] = val.astype(o_ref.dtype)
        else:
            tail_ref[:, kind * width:(kind + 1) * width] = val

    if grouped:
        @pl.when(pl.program_id(1) >= tail_tile0)
        def _():
            keys = _rope_tile(_dot(h_scr[0], w_block(1)), tab_scr[0, 0], tab_scr[0, 1])
            tailt_ref[0] = keys.T
            tailt_ref[1] = _dot_nt(wvt_ref[group], h_scr[0])


def _nm_ffn_kernel(x_ref, g_ref, sc_ref, sh_ref, w_ref, gate_ref, val_ref, *, col_tile):
    h = _norm_mod(x_ref[...], g_ref[...], sc_ref[...], sh_ref[...]).astype(BF16)
    d_ff = gate_ref.shape[1]
    for c0 in range(0, d_ff, col_tile):
        gate_ref[:, c0:c0 + col_tile] = _dot(h, w_ref[:, c0:c0 + col_tile])
        val_ref[:, c0:c0 + col_tile] = _dot(
            h, w_ref[:, d_ff + c0:d_ff + c0 + col_tile]).astype(val_ref.dtype)


def _row_tile(rows):
    return min(rows, 512)


def _resident(shape, layer=None):
    if layer is None:
        zeros = (0,) * len(shape)
        return pl.BlockSpec(shape, lambda *_: zeros, pipeline_mode=pl.Buffered(1))
    index = (layer,) + (0,) * (len(shape) - 1)
    return pl.BlockSpec((None,) + tuple(shape[1:]), lambda *_: index, pipeline_mode=pl.Buffered(1))


def _nm_common_specs(sc, tm, d):
    per_row = sc.shape[1] != 1
    mod_spec = (pl.BlockSpec((None, tm, d), lambda b, i, *_: (b, i, 0)) if per_row
                else pl.BlockSpec((None, 1, d), lambda b, i, *_: (b, 0, 0)))
    return [
        pl.BlockSpec((None, tm, d), lambda b, i, *_: (b, i, 0)),
        pl.BlockSpec((1, d), lambda b, i, *_: (0, 0)),
        mod_spec,
        mod_spec,
    ]


def _nm_delta_in_call(x, g, sc, sh, w, n_main, conv_dim, main_dtype):
    bq, rows, d = x.shape
    tm = _row_tile(rows)
    tiles = rows // tm
    kern = functools.partial(_nm_delta_in_kernel, col_tile=1024)
    return pl.pallas_call(
        kern,
        grid=(bq, tiles),
        in_specs=_nm_common_specs(sc, tm, d) + [_resident(w.shape)],
        out_specs=[pl.BlockSpec((None, tm, n_main), lambda b, i: (b, i, 0)),
                   pl.BlockSpec((None, tm, LANES), lambda b, i: (b, i, 0)),
                   pl.BlockSpec((None, None, SUBLANES, conv_dim), lambda b, i: (b, i, 0, 0))],
        out_shape=[jax.ShapeDtypeStruct((bq, rows, n_main), main_dtype),
                   jax.ShapeDtypeStruct((bq, rows, LANES), F32),
                   jax.ShapeDtypeStruct((bq, tiles, SUBLANES, conv_dim), F32)],
        compiler_params=_params("arbitrary", "arbitrary"),
        name="norm_mod_delta_in",
    )(x, g, sc, sh, w)


def _nm_delta_act_call(x, g, sc, sh, w, cstate8, conv_w, alog_l, dtb_l):
    bq, rows, d = x.shape
    conv_dim = conv_w.shape[1]
    n_main = w.shape[1] - LANES
    tm = _row_tile(rows)
    tiles = rows // tm
    kern = functools.partial(_nm_delta_act_kernel, col_tile=1024)
    return pl.pallas_call(
        kern,
        grid=(bq, tiles),
        in_specs=_nm_common_specs(sc, tm, d) + [
            _resident(w.shape),
            pl.BlockSpec((None, SUBLANES, conv_dim), lambda b, i: (b, 0, 0)),
            _resident(conv_w.shape), _resident(alog_l.shape), _resident(dtb_l.shape)],
        out_specs=[pl.BlockSpec((None, tm, n_main), lambda b, i: (b, i, 0)),
                   pl.BlockSpec((None, tm, LANES), lambda b, i: (b, i, 0)),
                   pl.BlockSpec((None, None, SUBLANES, conv_dim), lambda b, i: (b, i, 0, 0))],
        out_shape=[jax.ShapeDtypeStruct((bq, rows, n_main), BF16),
                   jax.ShapeDtypeStruct((bq, rows, LANES), F32),
                   jax.ShapeDtypeStruct((bq, tiles, SUBLANES, conv_dim), F32)],
        scratch_shapes=[pltpu.VMEM((SUBLANES, conv_dim), F32)],
        compiler_params=_params("arbitrary", "arbitrary"),
        name="norm_mod_delta_in_act",
    )(x, g, sc, sh, w, cstate8, conv_w, alog_l, dtb_l)


def _residue_major(a, dil, tm=A_ROW_TILE):
    lead, (rows, c) = a.shape[:-2], a.shape[-2:]
    a = a.reshape(lead + (rows // tm, tm // dil, dil, c))
    return jnp.swapaxes(a, -2, -3).reshape(lead + (rows, c))


def _perm_matrices(tm=A_ROW_TILE):
    eye = jnp.eye(tm, dtype=BF16)
    return jnp.stack([_residue_major(eye, dil, tm) for _, dil in A_GROUPS if dil > 1])


def _nm_rope_call(x, g, sc, sh, w, cos, sin, tail_rows, perms=None, w_vt=None):
    bq, rows, d = x.shape
    ngroups = len(A_GROUPS)
    gw = w.shape[1] // ngroups
    width = gw // 3
    grouped = perms is not None
    tm = A_ROW_TILE if grouped else _row_tile(min(rows, tail_rows))
    tiles = rows // tm
    tail_tile0 = (rows - tail_rows) // tm
    kern = functools.partial(_nm_rope_kernel, grouped=grouped, tail_tile0=tail_tile0)
    tab_spec = pl.BlockSpec((None, tm, LANES), lambda b, i, j: (0, i, 0))

    def tail_map(b, i, j):
        return (b, jnp.where(i >= tail_tile0, j, 0), 0, jnp.maximum(i - tail_tile0, 0))

    in_specs = _nm_common_specs(sc, tm, d) + [_resident(w.shape), tab_spec, tab_spec]
    out_specs, out_shape, args = [], [], [x, g, sc, sh, w, cos, sin]
    if grouped:
        in_specs += [_resident(perms.shape), _resident(w_vt.shape)]
        args += [perms, w_vt]
        out_specs.append(pl.BlockSpec((None, 3, None, tm, width), lambda b, i, j: (b, j, i, 0, 0)))
        out_shape.append(jax.ShapeDtypeStruct((bq, 3 * ngroups, tiles, tm, width), BF16))
        out_specs.append(pl.BlockSpec((None, 2, width, tm), tail_map))
        out_shape.append(jax.ShapeDtypeStruct((bq, 2 * ngroups, width, tail_rows), F32))
    else:
        assert tail_tile0 == 0
        out_specs.append(pl.BlockSpec((None, tm, gw), lambda b, i, j: (b, i, j)))
        out_shape.append(jax.ShapeDtypeStruct((bq, rows, ngroups * gw), F32))
    n_h = 1 + (perms.shape[0] if grouped else 0)
    scratch = [pltpu.VMEM((n_h, tm, d), BF16)]
    if grouped:
        scratch.append(pltpu.VMEM((n_h, 2, tm, LANES), F32))
    return pl.pallas_call(
        kern,
        grid=(bq, tiles, ngroups),
        in_specs=in_specs,
        out_specs=out_specs,
        out_shape=out_shape,
        scratch_shapes=scratch,
        compiler_params=_params("arbitrary", "arbitrary", "arbitrary"),
        name="norm_mod_qkv_rope",
    )(*args)


def _nm_ffn_call(x, g, sc, sh, w, layer):
    bq, rows, d = x.shape
    d_ff = w.shape[2] // 2
    tm = _row_tile(rows)
    kern = functools.partial(_nm_ffn_kernel, col_tile=d_ff // 2)
    row_spec = pl.BlockSpec((None, tm, d_ff), lambda b, i: (b, i, 0))
    return pl.pallas_call(
        kern,
        grid=(bq, rows // tm),
        in_specs=_nm_common_specs(sc, tm, d) + [_resident(w.shape, layer)],
        out_specs=[row_spec, row_spec],
        out_shape=[jax.ShapeDtypeStruct((bq, rows, d_ff), F32),
                   jax.ShapeDtypeStruct((bq, rows, d_ff), BF16)],
        compiler_params=_params("arbitrary", "arbitrary"),
        name="norm_mod_ffn_up",
    )(x, g, sc, sh, w)


def _finish(y, x_ref, gate_ref, gfin_ref, o_ref, final_norm):
    xn = x_ref[...] + gate_ref[...] * y
    if final_norm:
        ms = jnp.mean(xn * xn, axis=-1, keepdims=True)
        xn = xn * lax.rsqrt(ms + NORM_EPS) * gfin_ref[...]
    o_ref[...] = xn


def _proj_res_kernel(a_ref, w_ref, x_ref, gate_ref, gfin_ref, o_ref, *, final_norm):
    _finish(_dot(a_ref[...], w_ref[...]), x_ref, gate_ref, gfin_ref, o_ref, final_norm)


def _attn_merge_proj_kernel(o0_ref, o1_ref, o2_ref, l0_ref, l1_ref, l2_ref, e_ref, w_ref,
                            x_ref, gate_ref, gfin_ref, *rest, final_norm, grouped):
    if grouped:
        unperm_ref, o_ref = rest
    else:
        (o_ref,) = rest
    outs, lses = [], []
    slot = 0
    for (_, dil), og_ref, lg_ref in zip(A_GROUPS, (o0_ref, o1_ref, o2_ref), (l0_ref, l1_ref, l2_ref)):
        if grouped and dil > 1:
            pt = unperm_ref[slot]
            slot += 1
            outs.append(_dot(pt, og_ref[...]))
            parts = _dot(pt, jnp.concatenate(_split3(lg_ref[...]), axis=1))
            lses.append(parts[:, 0:LANES] + parts[:, LANES:2 * LANES] + parts[:, 2 * LANES:3 * LANES])
        else:
            outs.append(og_ref[...].astype(F32))
            lses.append(lg_ref[...])
    m = jnp.maximum(jnp.maximum(lses[0], lses[1]), lses[2])
    exps = [jnp.exp(l - m) for l in lses]
    inv = 1.0 / (exps[0] + exps[1] + exps[2])
    merged = None
    for e, og in zip(exps, outs):
        wgt = e * inv
        hi = wgt.astype(BF16)
        lo = (wgt - hi.astype(F32)).astype(BF16)
        term = _dot(jnp.concatenate([hi, lo], axis=1), e_ref[...]) * og
        merged = term if merged is None else merged + term
    _finish(_dot(merged.astype(BF16), w_ref[...]), x_ref, gate_ref, gfin_ref, o_ref, final_norm)


def _res_specs(tm, d, gate):
    per_row = gate.shape[1] != 1
    gate_spec = (pl.BlockSpec((None, tm, d), lambda b, i: (b, i, 0)) if per_row
                 else pl.BlockSpec((None, 1, d), lambda b, i: (b, 0, 0)))
    return [pl.BlockSpec((None, tm, d), lambda b, i: (b, i, 0)),
            gate_spec,
            pl.BlockSpec((1, d), lambda b, i: (0, 0))]


def _proj_res_call(a, w, x, gate, gfin, final_norm):
    bq, rows, d = x.shape
    k = a.shape[2]
    tm = _row_tile(rows)
    kern = functools.partial(_proj_res_kernel, final_norm=final_norm)
    return pl.pallas_call(
        kern,
        grid=(bq, rows // tm),
        in_specs=[pl.BlockSpec((None, tm, k), lambda b, i: (b, i, 0)),
                  pl.BlockSpec((k, d), lambda b, i: (0, 0))] + _res_specs(tm, d, gate),
        out_specs=pl.BlockSpec((None, tm, d), lambda b, i: (b, i, 0)),
        out_shape=jax.ShapeDtypeStruct((bq, rows, d), F32),
        compiler_params=_params("arbitrary", "arbitrary"),
        name="proj_residual",
    )(a, w, x, gate, gfin)


def _attn_merge_proj_call(outs, lses, expand, w, x, gate, gfin, final_norm, unperms=None):
    bq, rows, d = x.shape
    grouped = unperms is not None
    tm = A_ROW_TILE if grouped else _row_tile(rows)
    kern = functools.partial(_attn_merge_proj_kernel, final_norm=final_norm, grouped=grouped)
    o_spec = pl.BlockSpec((None, tm, d), lambda b, i: (b, i, 0))
    l_spec = pl.BlockSpec((None, tm, LANES), lambda b, i: (b, i, 0))
    extra_specs, extra = [], []
    if grouped:
        extra_specs.append(pl.BlockSpec(unperms.shape, lambda b, i: (0, 0, 0)))
        extra.append(unperms)
    return pl.pallas_call(
        kern,
        grid=(bq, rows // tm),
        in_specs=[o_spec] * 3 + [l_spec] * 3 + [
            pl.BlockSpec(expand.shape, lambda b, i: (0, 0)),
            pl.BlockSpec(w.shape, lambda b, i: (0, 0))] + _res_specs(tm, d, gate) + extra_specs,
        out_specs=pl.BlockSpec((None, tm, d), lambda b, i: (b, i, 0)),
        out_shape=jax.ShapeDtypeStruct((bq, rows, d), F32),
        compiler_params=_params("arbitrary", "arbitrary"),
        name="attn_merge_proj_residual",
    )(*outs, *lses, expand, w, x, gate, gfin, *extra)


def _ffn_down_short_kernel(gate_ref, left_ref, val_ref, cw_ref, cb_ref, w_ref,
                           x_ref, mgate_ref, gfin_ref, o_ref, *, final_norm, seq_len):
    gate = gate_ref[...]
    width = cw_ref.shape[0]
    tpos = _mod_pow2(lax.broadcasted_iota(jnp.int32, gate.shape, 0), seq_len)
    y = cb_ref[...] + cw_ref[width - 1:width, :] * gate
    for back in range(1, width):
        shifted = jnp.where(tpos >= back, pltpu.roll(gate, back, 0), left_ref[back - 1])
        y = y + cw_ref[width - 1 - back:width - back, :] * shifted
    act = (_silu(y) * val_ref[...].astype(F32)).astype(BF16)
    _finish(_dot(act, w_ref[...]), x_ref, mgate_ref, gfin_ref, o_ref, final_norm)


def _ffn_down_short_call(gate_pre, state8, val, conv_w, conv_b, w, x, mgate, gfin, final_norm, seq_len,
                         layer):
    _, rows, d = x.shape
    f = gate_pre.shape[2]
    width = conv_w.shape[0]
    lefts = jnp.stack([jnp.roll(state8, back, axis=1).reshape(rows, f) for back in range(1, width)])
    kern = functools.partial(_ffn_down_short_kernel, final_norm=final_norm, seq_len=seq_len)
    whole = lambda a: pl.BlockSpec(a.shape, lambda i: (0,) * a.ndim)
    rows_of = lambda a: pl.BlockSpec((None,) + a.shape[1:], lambda i: (0, 0, 0))
    return pl.pallas_call(
        kern,
        grid=(1,),
        in_specs=[rows_of(gate_pre), whole(lefts), rows_of(val), whole(conv_w), whole(conv_b),
                  pl.BlockSpec((None,) + w.shape[1:], lambda i: (layer, 0, 0)),
                  rows_of(x), rows_of(mgate), whole(gfin)],
        out_specs=rows_of(x),
        out_shape=jax.ShapeDtypeStruct(x.shape, F32),
        compiler_params=_params("arbitrary"),
        name="ffn_conv_down_residual_short",
    )(gate_pre, lefts, val, conv_w, conv_b, w, x, mgate, gfin)


def _ffn_fused_kernel(x_ref, g_ref, sc_ref, sh_ref, wup_ref, state_ref, cw_ref, cb_ref, wdn_ref,
                      mgate_ref, gfin_ref, o_ref, gtail_ref, carry_scr, *, final_norm, col_tile):
    tm = x_ref.shape[0]
    d_ff = wdn_ref.shape[0]
    width = cw_ref.shape[0]

    @pl.when(pl.program_id(1) == 0)
    def _():
        carry_scr[...] = state_ref[...]

    h = _norm_mod(x_ref[...], g_ref[...], sc_ref[...], sh_ref[...]).astype(BF16)

    def up(c0):
        return (_dot(h, wup_ref[:, c0:c0 + col_tile]),
                _dot(h, wup_ref[:, d_ff + c0:d_ff + c0 + col_tile]))

    acc = None
    ahead = up(0)
    for c0 in range(0, d_ff, col_tile):
        cols = slice(c0, c0 + col_tile)
        gate, val = ahead
        if c0 + col_tile < d_ff:
            ahead = up(c0 + col_tile)
        ext = jnp.concatenate([carry_scr[:, cols], gate], axis=0)
        y = cb_ref[:, cols] + cw_ref[width - 1:width, cols] * gate
        for back in range(1, width):
            y = y + (cw_ref[width - 1 - back:width - back, cols]
                     * pltpu.roll(ext, back, 0)[SUBLANES:SUBLANES + tm])
        act = (_silu(y) * val).astype(BF16)
        part = _dot(act, wdn_ref[cols, :])
        acc = part if acc is None else acc + part
        carry_scr[:, cols] = gate[tm - SUBLANES:tm]
        gtail_ref[:, cols] = gate[tm - SUBLANES:tm]
    _finish(acc, x_ref, mgate_ref, gfin_ref, o_ref, final_norm)


def _ffn_fused_call(x, g, sc, sh, w_up, state8, conv_w, conv_b, w_down, mgate, gfin, final_norm,
                    layer):
    bq, rows, d = x.shape
    d_ff = w_down.shape[1]
    tm = min(rows, FFN_ROW_TILE)
    tiles = rows // tm
    col_tile = 2 * LANES
    assert d_ff % col_tile == 0
    kern = functools.partial(_ffn_fused_kernel, final_norm=final_norm, col_tile=col_tile)
    return pl.pallas_call(
        kern,
        grid=(bq, tiles),
        in_specs=_nm_common_specs(sc, tm, d) + [
            _resident(w_up.shape, layer),
            pl.BlockSpec((None, SUBLANES, d_ff), lambda b, i: (b, 0, 0)),
            _resident(conv_w.shape),
            _resident(conv_b.shape),
            _resident(w_down.shape, layer),
        ] + _res_specs(tm, d, mgate)[1:],
        out_specs=[pl.BlockSpec((None, tm, d), lambda b, i: (b, i, 0)),
                   pl.BlockSpec((None, None, SUBLANES, d_ff), lambda b, i: (b, i, 0, 0))],
        out_shape=[jax.ShapeDtypeStruct((bq, rows, d), F32),
                   jax.ShapeDtypeStruct((bq, tiles, SUBLANES, d_ff), F32)],
        scratch_shapes=[pltpu.VMEM((SUBLANES, d_ff), F32)],
        compiler_params=_params("arbitrary", "arbitrary"),
        name="conv_ffn",
    )(x, g, sc, sh, w_up, state8, conv_w, conv_b, w_down, mgate, gfin)


def _attn_prompt_kernel(q_ref, kp_ref, kc_ref, vp_ref, vc_ref, o_ref, lse_ref):
    blk = kp_ref.shape[0] * kp_ref.shape[1]
    prow = q_ref.shape[1]
    run = q_ref.shape[0] * prow
    nq = run // blk

    def rows_of(ref, cols):
        return jnp.concatenate([ref[p, :, cols] for p in range(ref.shape[0])], axis=0)

    n = pl.program_id(2)
    qi = lax.broadcasted_iota(jnp.int32, (blk, 2 * blk), 0)
    kj = lax.broadcasted_iota(jnp.int32, (blk, 2 * blk), 1)
    band = (kj >= qi) & (kj <= qi + blk)
    bias = jnp.where(band, 0.0, NEG_BIG)
    bias_first = jnp.where(band & ((kj >= blk) | (n > 0)), 0.0, NEG_BIG)
    lane = lax.broadcasted_iota(jnp.int32, (blk, LANES), 1)
    low = lane < A_HEAD_DIM
    m_all = [jnp.zeros((blk, LANES), F32) for _ in range(nq)]
    l_all = [jnp.ones((blk, LANES), F32) for _ in range(nq)]
    for hp in range(A_HEADS // 2):
        cols = slice(hp * LANES, (hp + 1) * LANES)
        q_run = rows_of(q_ref, cols)
        k_run = jnp.concatenate([rows_of(kp_ref, cols), rows_of(kc_ref, cols)], axis=0)
        v_run = jnp.concatenate([rows_of(vp_ref, cols), rows_of(vc_ref, cols)], axis=0)
        outs = []
        for j in range(nq):
            q2 = q_run[j * blk:(j + 1) * blk]
            k2 = k_run[j * blk:(j + 2) * blk]
            v2 = v_run[j * blk:(j + 2) * blk]
            mask = bias_first if j == 0 else bias
            pair = []
            for s in range(2):
                mine = low if s == 0 else jnp.logical_not(low)
                qh = jnp.where(mine, q2, jnp.zeros_like(q2))
                sc = _dot_nt(qh, k2) + mask
                m = jnp.max(sc, axis=1, keepdims=True)
                p = jnp.exp2(sc - m)
                l = jnp.sum(p, axis=1, keepdims=True)
                pair.append(_dot(p.astype(BF16), v2) / l)
                m_all[j] = jnp.where(lane == 2 * hp + s, m, m_all[j])
                l_all[j] = jnp.where(lane == 2 * hp + s, l, l_all[j])
            outs.append(jnp.where(low, pair[0], pair[1]).astype(o_ref.dtype))
        o_run = jnp.concatenate(outs, axis=0)
        for p in range(o_ref.shape[0]):
            o_ref[p, :, cols] = o_run[p * prow:(p + 1) * prow]
    lse_run = jnp.concatenate([(m_all[j] + jnp.log2(l_all[j])) * LN2 for j in range(nq)], axis=0)
    for p in range(lse_ref.shape[0]):
        lse_ref[p] = lse_run[p * prow:(p + 1) * prow]


def _attn_prompt_call(qkv, gi, win, dil):
    b, nkinds, tiles, tm, width = qkv.shape
    per_res = tm // dil
    qkv_g = qkv.reshape(b, nkinds, tiles, dil, per_res, width)
    blk = win // dil
    assert (tiles * tm) % win == 0
    nb = tiles * per_res // blk
    nq = min(A_BLOCKS_PER_STEP, nb)
    assert nb % nq == 0

    def geometry(rows):
        prow = min(rows, per_res)
        return rows // prow, prow

    def spec_of(rows, block_of_step, kind, lanes=width, lead=True):
        pieces, prow = geometry(rows)
        per_tile = per_res // prow

        def index(bb, r, i):
            j = jnp.maximum(block_of_step(i), 0)
            tail = (lax.div(j, per_tile), r, lax.rem(j, per_tile), 0)
            return (bb, 3 * gi + kind) + tail if lead else (bb,) + tail

        shape = (None, pieces, None, prow, lanes)
        return pl.BlockSpec((None,) + shape if lead else shape, index)

    cur = lambda kind: spec_of(nq * blk, lambda i: i, kind)
    prev = lambda kind: spec_of(blk, lambda i: nq * i - 1, kind)
    return pl.pallas_call(
        _attn_prompt_kernel,
        grid=(b, dil, nb // nq),
        in_specs=[cur(0), prev(1), cur(1), prev(2), cur(2)],
        out_specs=[spec_of(nq * blk, lambda i: i, 0, width, False),
                   spec_of(nq * blk, lambda i: i, 0, LANES, False)],
        out_shape=[jax.ShapeDtypeStruct((b, tiles, dil, per_res, width), BF16),
                   jax.ShapeDtypeStruct((b, tiles, dil, per_res, LANES), F32)],
        compiler_params=_params("arbitrary", "arbitrary", "arbitrary"),
        name="dilated_attn_prompt",
    )(qkv_g, qkv_g, qkv_g, qkv_g, qkv_g)


def _attn_sample_kernel(q_ref, kn_ref, vn_ref, kc_ref, vc_ref,
                        ko_ref, vo_ref, o_ref, lse_ref,
                        m_scr, l_scr, acc_scr, kcar_scr, vcar_scr, *, lb, dil):
    t_new = q_ref.shape[0]
    width = q_ref.shape[1]
    chunk = kc_ref.shape[1]
    rows = A_HEADS * t_new
    step = pl.program_id(1)
    c = pl.num_programs(1) - 1 - step

    rq = lax.broadcasted_iota(jnp.int32, (rows, width), 0)
    lq = lax.broadcasted_iota(jnp.int32, (rows, width), 1)
    own = _div_pow2(lq, A_HEAD_DIM) == _div_pow2(rq, t_new)
    qbd = jnp.where(own, jnp.concatenate([q_ref[...]] * A_HEADS, axis=0), 0.0).astype(BF16)

    def accumulate(keys_t, vals_t, valid):
        sc = jnp.where(valid, _dot(qbd, keys_t.astype(BF16)), NEG_BIG)
        m_old = m_scr[...]
        m_new = jnp.maximum(m_old, jnp.max(sc, axis=1, keepdims=True))
        alpha = jnp.exp2(m_old - m_new)
        p = jnp.where(valid, jnp.exp2(sc - m_new), 0.0)
        l_scr[...] = alpha * l_scr[...] + jnp.sum(p, axis=1, keepdims=True)
        acc_scr[...] = alpha * acc_scr[...] + _dot_nt(p.astype(BF16), vals_t.astype(BF16))
        m_scr[...] = m_new

    @pl.when(step == 0)
    def _():
        m_scr[...] = jnp.full_like(m_scr, NEG_BIG)
        l_scr[...] = jnp.zeros_like(l_scr)
        acc_scr[...] = jnp.zeros_like(acc_scr)
        pad = jnp.zeros((LANES - t_new, width), F32)
        kcar_scr[...] = jnp.concatenate([kn_ref[...], pad], axis=0).T
        vcar_scr[...] = jnp.concatenate([vn_ref[...], pad], axis=0).T
        tq2 = _mod_pow2(lax.broadcasted_iota(jnp.int32, (rows, LANES), 0), t_new)
        tk2 = lax.broadcasted_iota(jnp.int32, (rows, LANES), 1)
        d2 = tq2 - tk2
        accumulate(kcar_scr[...], vcar_scr[...],
                   (tk2 < t_new) & (d2 >= 0) & (_mod_pow2(d2, dil) == 0))

    lane = lax.broadcasted_iota(jnp.int32, (width, LANES), 1)
    keep = lane < LANES - t_new
    nblk = chunk // LANES
    for src_ref, car_ref, dst_ref in ((kc_ref, kcar_scr, ko_ref), (vc_ref, vcar_scr, vo_ref)):
        nxt = pltpu.roll(car_ref[...], LANES - t_new, 1)
        for b in reversed(range(nblk)):
            cur = pltpu.roll(src_ref[:, b * LANES:(b + 1) * LANES], LANES - t_new, 1)
            dst_ref[:, b * LANES:(b + 1) * LANES] = jnp.where(keep, cur, nxt)
            nxt = cur
        car_ref[...] = src_ref[:, 0:LANES]

    tq = _mod_pow2(lax.broadcasted_iota(jnp.int32, (rows, chunk), 0), t_new)
    ik = lax.broadcasted_iota(jnp.int32, (rows, chunk), 1) + c * chunk
    dist = lb + tq - ik
    accumulate(kc_ref[...], vc_ref[...], (_mod_pow2(dist, dil) == 0) & (dist <= lb))

    @pl.when(c == 0)
    def _():
        full = jnp.where(own, acc_scr[...] / l_scr[...], 0.0)
        o_ref[...] = jnp.sum(full.reshape(A_HEADS, t_new, width), axis=0).astype(o_ref.dtype)
        lse_col = (m_scr[...] + jnp.log2(l_scr[...])) * LN2
        rl = lax.broadcasted_iota(jnp.int32, (rows, LANES), 0)
        ll = lax.broadcasted_iota(jnp.int32, (rows, LANES), 1)
        spread = jnp.where(ll == _div_pow2(rl, t_new), lse_col, 0.0)
        lse_ref[...] = jnp.sum(spread.reshape(A_HEADS, t_new, LANES), axis=0)


def _attn_sample_call(proj, gi, win, dil, k_cache_t, v_cache_t):
    db, t_new, n = proj.shape
    width = A_HEADS * A_HEAD_DIM
    lb = k_cache_t.shape[2]
    assert lb == win and t_new == SUBLANES
    chunk = min(lb, 1024)
    nchunk = lb // chunk
    kern = functools.partial(_attn_sample_kernel, lb=lb, dil=dil)
    new_spec = lambda kind: pl.BlockSpec((None, t_new, width), lambda b, s: (b, 0, 3 * gi + kind))
    cur = pl.BlockSpec((None, width, chunk), lambda b, s: (b, 0, nchunk - 1 - s))
    return pl.pallas_call(
        kern,
        grid=(db, nchunk),
        in_specs=[new_spec(0), new_spec(1), new_spec(2), cur, cur],
        out_specs=[cur, cur,
                   pl.BlockSpec((None, t_new, width), lambda b, s: (b, 0, 0)),
                   pl.BlockSpec((None, t_new, LANES), lambda b, s: (b, 0, 0))],
        out_shape=[jax.ShapeDtypeStruct((db, width, lb), F32),
                   jax.ShapeDtypeStruct((db, width, lb), F32),
                   jax.ShapeDtypeStruct((db, t_new, width), BF16),
                   jax.ShapeDtypeStruct((db, t_new, LANES), F32)],
        scratch_shapes=[pltpu.VMEM((A_HEADS * t_new, 1), F32),
                        pltpu.VMEM((A_HEADS * t_new, 1), F32),
                        pltpu.VMEM((A_HEADS * t_new, width), F32),
                        pltpu.VMEM((width, LANES), F32),
                        pltpu.VMEM((width, LANES), F32)],
        compiler_params=_params("arbitrary", "arbitrary"),
        name="dilated_attn_sample",
    )(proj, proj, proj, k_cache_t, v_cache_t)


def _delta_kernel(*refs, t_valid, activated):
    if activated:
        p_ref, gates_ref, ng_ref, s0_ref, o_ref, ssm_ref, s_scr = refs
    else:
        (p_ref, prev_ref, gates_ref, cstate_ref, cw_ref, alog_ref, dtb_ref, ng_ref, s0_ref,
         o_ref, ssm_ref, cnew_ref, s_scr, ext_scr) = refs
    cl = B_CHUNK
    rows_in = p_ref.shape[0]
    nsub = max(rows_in // cl, 1)
    rows_pad = nsub * cl
    qk_width = B_QK_HEADS * B_HEAD
    conv_dim = 2 * qk_width + B_V_HEADS * B_HEAD
    c = pl.program_id(1)
    last = c == pl.num_programs(1) - 1

    @pl.when(c == 0)
    def _():
        s_scr[...] = s0_ref[...]

    if activated:
        assert t_valid == rows_pad
        qkv = p_ref[:, 0:conv_dim].astype(F32)
        qn, kn = qkv[:, 0:qk_width], qkv[:, qk_width:2 * qk_width]
        beta_all = g_all = gates_ref[...]
    else:
        qkv, beta_all, g_all = _delta_front(p_ref, prev_ref, gates_ref, cstate_ref, cw_ref, alog_ref,
                                            dtb_ref, cnew_ref, ext_scr, rows_pad, t_valid)
        qn = _l2n_heads(qkv[:, 0:qk_width]) * (B_HEAD ** -0.5)
        kn = _l2n_heads(qkv[:, qk_width:2 * qk_width])
    _delta_chunks(qkv, qn, kn, beta_all, g_all, p_ref, ng_ref, o_ref, s_scr, nsub)

    @pl.when(last)
    def _():
        ssm_ref[...] = s_scr[...]


def _delta_front(p_ref, prev_ref, gates_ref, cstate_ref, cw_ref, alog_ref, dtb_ref, cnew_ref, ext_scr,
                 rows_pad, t_valid):
    rows_in = p_ref.shape[0]
    conv_dim = cw_ref.shape[1]
    c = pl.program_id(1)
    last = c == pl.num_programs(1) - 1

    halo = prev_ref.shape[0]
    left = prev_ref[:, 0:conv_dim].astype(F32)
    if halo > SUBLANES:
        ext_scr[0:halo - SUBLANES, :] = left[0:halo - SUBLANES]
    ext_scr[halo - SUBLANES:halo, :] = jnp.where(c == 0, cstate_ref[...], left[halo - SUBLANES:halo])
    ext_scr[halo:halo + rows_in, :] = p_ref[:, 0:conv_dim].astype(F32)
    if rows_in < rows_pad:
        ext_scr[halo + rows_in:halo + rows_pad, :] = jnp.zeros((rows_pad - rows_in, conv_dim), F32)
    width = cw_ref.shape[0]
    ext = ext_scr[...]
    y = cw_ref[width - 1:width, :] * ext[halo:halo + rows_pad]
    for back in range(1, width):
        shifted = pltpu.roll(ext, back, 0)[halo:halo + rows_pad]
        y = y + cw_ref[width - 1 - back:width - back, :] * shifted
    qkv = _silu(y)
    if t_valid < rows_pad:
        rmask = lax.broadcasted_iota(jnp.int32, (rows_pad, 1), 0) < t_valid
        qkv = jnp.where(rmask, qkv, 0.0)

    @pl.when(last)
    def _():
        cnew_ref[...] = ext_scr[halo + t_valid - SUBLANES:halo + t_valid, :]

    graw = gates_ref[...]
    if rows_in < rows_pad:
        graw = jnp.concatenate([graw, jnp.zeros((rows_pad - rows_in, LANES), F32)], axis=0)
    beta_all, g_all = _delta_gates(graw, alog_ref[...], dtb_ref[...])
    if t_valid < rows_pad:
        beta_all = jnp.where(rmask, beta_all, 0.0)
        g_all = jnp.where(rmask, g_all, 0.0)
    return qkv, beta_all, g_all


def _delta_chunks(qkv, qn, kn, beta_all, g_all, p_ref, ng_ref, o_ref, s_scr, nsub):
    cl = B_CHUNK
    rows_in = p_ref.shape[0]
    qk_width = B_QK_HEADS * B_HEAD
    conv_dim = 2 * qk_width + B_V_HEADS * B_HEAD

    def rows_of(sub):
        return slice(sub * cl, (sub + 1) * cl)

    ri = lax.broadcasted_iota(jnp.int32, (2 * cl, cl), 0)
    ci = lax.broadcasted_iota(jnp.int32, (2 * cl, cl), 1)
    summat = jnp.where((ci <= ri) | (ri >= cl), 1.0, 0.0).astype(BF16)
    cg_all, gl_all = [], []
    for sub in range(nsub):
        sums = _dot(summat, jnp.concatenate(_split3(g_all[rows_of(sub)]), axis=1))
        sums = sums[:, 0:LANES] + sums[:, LANES:2 * LANES] + sums[:, 2 * LANES:3 * LANES]
        cg_all.append(sums[0:cl])
        gl_all.append(sums[cl:2 * cl])

    nst = B_STACK * cl
    rr = lax.broadcasted_iota(jnp.int32, (nst, nst), 0)
    cc = lax.broadcasted_iota(jnp.int32, (nst, nst), 1)
    same = _div_pow2(rr, cl) == _div_pow2(cc, cl)
    incl = same & (cc <= rr)
    strict = same & (cc < rr)
    lane6 = lax.broadcasted_iota(jnp.int32, (nst, LANES), 1)
    rblk = _div_pow2(lax.broadcasted_iota(jnp.int32, (nst, B_HEAD), 0), cl)

    def col_stack(arr, lane0, heads):
        return jnp.concatenate([arr[:, lane0 + h:lane0 + h + 1] for h in heads], axis=0)

    groups = [list(range(g * B_STACK, (g + 1) * B_STACK)) for g in range(B_V_HEADS // B_STACK)]
    problems = [(sub, hs) for sub in range(nsub) for hs in groups]
    every = range(len(problems))
    rep = B_V_HEADS // B_QK_HEADS

    def head_stack(arr, col0, sub, heads):
        return jnp.concatenate([arr[rows_of(sub), col0 + h * B_HEAD:col0 + (h + 1) * B_HEAD]
                                for h in heads], axis=0)

    q_st = [head_stack(qn, 0, sub, [h // rep for h in hs]) for sub, hs in problems]
    k_st = [head_stack(kn, 0, sub, [h // rep for h in hs]) for sub, hs in problems]
    v_st = [head_stack(qkv, 2 * qk_width, sub, hs) for sub, hs in problems]
    beta = [col_stack(beta_all[rows_of(sub)], 0, hs) for sub, hs in problems]
    cg = [col_stack(cg_all[sub], B_V_HEADS, hs) for sub, hs in problems]
    gl = [col_stack(gl_all[sub], B_V_HEADS, hs) for sub, hs in problems]

    def diff_operands(c):
        c_hi, c_mid, c_lo = (p.astype(F32) for p in _split3(c))
        lmat = jnp.where(lane6 == 0, c_hi, jnp.where(lane6 == 1, c_mid, jnp.where(
            lane6 == 2, c_lo, jnp.where(lane6 < 6, 1.0, 0.0))))
        rmat = jnp.where(lane6 == 3, -c_hi, jnp.where(lane6 == 4, -c_mid, jnp.where(
            lane6 == 5, -c_lo, jnp.where(lane6 < 3, 1.0, 0.0))))
        return lmat.astype(BF16), rmat.astype(BF16)

    diff = [_dot_nt(*diff_operands(cg[g])) for g in every]
    k_bf = [k_st[g].astype(BF16) for g in every]
    gram = [_dot_nt(k_bf[g], k_bf[g]) for g in every]
    qk = [_dot_nt(q_st[g].astype(BF16), k_bf[g]) for g in every]
    decay = [jnp.where(incl, jnp.exp(jnp.where(incl, diff[g], 0.0)), 0.0) for g in every]
    a_mat = [jnp.where(strict, beta[g] * gram[g] * decay[g], 0.0) for g in every]
    p_mat = [(qk[g] * decay[g]).astype(BF16) for g in every]

    blk = _div_pow2(rr, B_INV_BASE) == _div_pow2(cc, B_INV_BASE)
    apow = [jnp.where(blk, a_mat[g], 0.0) for g in every]
    nmat = [-apow[g] for g in every]
    for _ in range((B_INV_BASE - 1).bit_length() - 1):
        ab = [apow[g].astype(BF16) for g in every]
        apow = [_dot(ab[g], ab[g]) for g in every]
        nmat = [nmat[g] + apow[g] + _dot(nmat[g].astype(BF16), apow[g].astype(BF16)) for g in every]
    size = B_INV_BASE
    while size < cl:
        size *= 2
        merged = _div_pow2(rr, size) == _div_pow2(cc, size)
        join = merged & jnp.logical_not(blk)
        low = [jnp.where(join, a_mat[g], 0.0) for g in every]
        nb = [nmat[g].astype(BF16) for g in every]
        dl = [low[g] + _dot(nb[g], low[g].astype(BF16)) for g in every]
        nmat = [nmat[g] - (dl[g] + _dot(dl[g].astype(BF16), nb[g])) for g in every]
        blk = merged
    ecg = [jnp.exp(cg[g]) for g in every]
    rhs = [jnp.concatenate([beta[g] * v_st[g], (beta[g] * ecg[g]) * k_st[g]], axis=1) for g in every]
    sol = [rhs[g] + _dot(nmat[g].astype(BF16), rhs[g].astype(BF16)) for g in every]
    qg = [q_st[g] * ecg[g] for g in every]
    kdec_t = [(k_st[g] * jnp.exp(gl[g] - cg[g])).T.astype(BF16) for g in every]

    live = min(rows_in, cl)
    for sub in range(nsub):
        mine = [g for g in every if problems[g][0] == sub]
        both = {}
        for g in mine:
            for idx, h in enumerate(problems[g][1]):
                rsl = slice(idx * cl, (idx + 1) * cl)
                lhs = jnp.concatenate([sol[g][rsl, B_HEAD:2 * B_HEAD], qg[g][rsl]], axis=0)
                both[h] = _dot(lhs.astype(BF16), s_scr[h].astype(BF16))
        u_st = {g: jnp.concatenate([sol[g][idx * cl:(idx + 1) * cl, 0:B_HEAD] - both[h][0:cl]
                                    for idx, h in enumerate(problems[g][1])], axis=0) for g in mine}
        o_st = {g: jnp.concatenate([both[h][cl:2 * cl] for h in problems[g][1]], axis=0)
                + _dot(p_mat[g], u_st[g].astype(BF16)) for g in mine}
        for g in mine:
            for idx, h in enumerate(problems[g][1]):
                u_own = jnp.where(rblk == idx, u_st[g], 0.0).astype(BF16)
                s_scr[h] = jnp.exp(gl[g][idx * cl:idx * cl + 1]) * s_scr[h] + _dot(kdec_t[g], u_own)
        out_rows = slice(sub * cl, sub * cl + live)
        for g in mine:
            for idx, h in enumerate(problems[g][1]):
                o_h = o_st[g][idx * cl:idx * cl + live]
                ms = jnp.mean(o_h * o_h, axis=1, keepdims=True)
                z_h = p_ref[out_rows, conv_dim + h * B_HEAD:conv_dim + (h + 1) * B_HEAD].astype(F32)
                res = o_h * lax.rsqrt(ms + NORM_EPS) * ng_ref[...] * _silu(z_h)
                o_ref[out_rows, h * B_HEAD:(h + 1) * B_HEAD] = res.astype(o_ref.dtype)


def _delta_call(proj, gates, norm_g, s0, raw=None):
    bq, t, n = proj.shape
    v_width = B_V_HEADS * B_HEAD
    rows_in = min(t, B_CHUNK * B_CHUNKS_PER_STEP)
    assert t % rows_in == 0 and (rows_in % B_CHUNK == 0 or rows_in < B_CHUNK)
    nchunk = t // rows_in
    activated = raw is None
    kern = functools.partial(_delta_kernel, t_valid=rows_in, activated=activated)
    row_spec = lambda width: pl.BlockSpec((None, rows_in, width), lambda b, c: (b, c, 0))
    state_spec = pl.BlockSpec((None, B_V_HEADS, B_HEAD, B_HEAD), lambda b, c: (b, 0, 0, 0))
    const_spec = lambda a: pl.BlockSpec(a.shape, lambda b, c: (0,) * a.ndim)
    out_specs = [row_spec(v_width), state_spec]
    out_shape = [jax.ShapeDtypeStruct((bq, t, v_width), BF16),
                 jax.ShapeDtypeStruct((bq, B_V_HEADS, B_HEAD, B_HEAD), F32)]
    scratch = [pltpu.VMEM((B_V_HEADS, B_HEAD, B_HEAD), F32)]
    if activated:
        in_specs = [row_spec(n), row_spec(LANES), const_spec(norm_g), state_spec]
        args = (proj, gates, norm_g, s0)
    else:
        cstate8, conv_w, alog_l, dtb_l = raw
        conv_dim = conv_w.shape[1]
        halo = SUBLANES * (4 // proj.dtype.itemsize)
        per_halo = max(rows_in // halo, 1)
        in_specs = [
            row_spec(n),
            pl.BlockSpec((None, halo, n), lambda b, c: (b, jnp.maximum(c * per_halo - 1, 0), 0)),
            row_spec(LANES),
            pl.BlockSpec((None, SUBLANES, conv_dim), lambda b, c: (b, 0, 0)),
            const_spec(conv_w), const_spec(alog_l), const_spec(dtb_l), const_spec(norm_g), state_spec]
        args = (proj, proj, gates, cstate8, conv_w, alog_l, dtb_l, norm_g, s0)
        out_specs.append(pl.BlockSpec((None, SUBLANES, conv_dim), lambda b, c: (b, 0, 0)))
        out_shape.append(jax.ShapeDtypeStruct((bq, SUBLANES, conv_dim), F32))
        scratch.append(pltpu.VMEM((max(rows_in, B_CHUNK) + halo, conv_dim), F32))
    return pl.pallas_call(
        kern,
        grid=(bq, nchunk),
        in_specs=in_specs,
        out_specs=out_specs,
        out_shape=out_shape,
        scratch_shapes=scratch,
        compiler_params=_params("arbitrary", "arbitrary"),
        name="gated_deltanet",
    )(*args)


def _left_rows(state):
    return jnp.pad(state, ((0, 0), (SUBLANES - state.shape[1], 0), (0, 0)))


def _rope_tables(pos):
    half = A_HEAD_DIM // 2
    inv = ROPE_THETA ** (-jnp.arange(half, dtype=F32) / half)
    ang = pos.astype(F32)[:, None] * inv[None, :]
    cos, sin = jnp.cos(ang), jnp.sin(ang)
    reps = LANES // A_HEAD_DIM
    cos_t = jnp.tile(jnp.concatenate([cos, cos], axis=1), (1, reps))
    sin_t = jnp.tile(jnp.concatenate([-sin, sin], axis=1), (1, reps))
    return cos_t[None], sin_t[None]


def _trunk(x, mods, per_row, pos, a_caches, b_ssm, b_conv, f_conv, wts):
    seqs, t, d = x.shape
    (norm_mix_g, norm_ffn_g, norm_final_g, a_w_in, a_w_out, b_w_in, b_conv_w, b_a_log, b_dt_bias,
     b_norm_g, b_w_out, ffn_w_up, ffn_conv_w, ffn_conv_b, ffn_w_down, expand) = wts
    d_ff = ffn_w_down.shape[1]
    width = A_HEADS * A_HEAD_DIM

    def rows_view(a):
        return a.reshape(1, seqs * t, a.shape[2]) if per_row else a

    def seq_view(a):
        return a.reshape(seqs, t, a.shape[2])

    def mod_view(v):
        if per_row:
            return jnp.repeat(v, t, axis=0)[None]
        return v[:, None, :]

    cos_t, sin_t = _rope_tables(pos)
    prompt = a_caches is None
    if prompt:
        assert t % max(w for w, _ in A_GROUPS) == 0
        perms = _perm_matrices()
        unperms = jnp.swapaxes(perms, 1, 2)
    if per_row:
        cos_t, sin_t = jnp.tile(cos_t, (1, seqs, 1)), jnp.tile(sin_t, (1, seqs, 1))
    gfin = norm_final_g.reshape(1, d)
    xr = rows_view(x)
    new_a, new_ssm, new_bconv, new_fconv = [], None, None, []
    depth = norm_mix_g.shape[0]
    for layer in range(depth):
        sh1, sc1, g1, sh2, sc2, g2 = (mod_view(v) for v in mods[layer])
        gmix = norm_mix_g[layer].reshape(1, d)
        if layer % 2 == 0:
            outs, lses = [], []
            if prompt:
                tail_rows = max(w for w, _ in A_GROUPS)
                w_vt = jnp.transpose(a_w_in.reshape(d, len(A_GROUPS), 3, width)[:, :, 2], (1, 2, 0))
                qkv, tail_t = _nm_rope_call(xr, gmix, sc1, sh1, a_w_in, cos_t, sin_t, tail_rows, perms,
                                            w_vt)
                for gi, (win, dil) in enumerate(A_GROUPS):
                    o_g, lse_g = _attn_prompt_call(qkv, gi, win, dil)
                    outs.append(o_g.reshape(seqs, t, width))
                    lses.append(lse_g.reshape(seqs, t, LANES))
                    for kind in (0, 1):
                        keep = tail_t[:, 2 * gi + kind, :, tail_rows - win:]
                        keep = keep.reshape(seqs, A_HEADS, A_HEAD_DIM, win)
                        new_a.append(jnp.transpose(keep, (0, 3, 1, 2))[None])
                xr = _attn_merge_proj_call(outs, lses, expand, a_w_out[0], xr, g1, gfin, False,
                                           unperms)
            else:
                (tail,) = _nm_rope_call(xr, gmix, sc1, sh1, a_w_in, cos_t, sin_t, seqs * t)
                for gi, (win, dil) in enumerate(A_GROUPS):
                    as_t = lambda c: jnp.transpose(c[0], (0, 2, 3, 1)).reshape(seqs, width, c.shape[2])
                    k_new, v_new, o_g, lse_g = _attn_sample_call(
                        seq_view(tail), gi, win, dil, as_t(a_caches[2 * gi]), as_t(a_caches[2 * gi + 1]))
                    outs.append(rows_view(o_g))
                    lses.append(rows_view(lse_g))
                    for buf in (k_new, v_new):
                        buf = buf.reshape(seqs, A_HEADS, A_HEAD_DIM, buf.shape[2])
                        new_a.append(jnp.transpose(buf, (0, 3, 1, 2))[None])
                xr = _attn_merge_proj_call(outs, lses, expand, a_w_out[0], xr, g1, gfin, False)
        else:
            conv_w = b_conv_w[0]
            conv_dim = conv_w.shape[1]
            conv0 = jnp.zeros((seqs, conv_w.shape[0] - 1, conv_dim), F32) if b_conv is None else b_conv[0]
            s0 = jnp.zeros((seqs, B_V_HEADS, B_HEAD, B_HEAD), F32) if b_ssm is None else b_ssm[0]
            lane_pad = lambda v: jnp.pad(v.reshape(1, -1), ((0, 0), (B_V_HEADS, LANES - 2 * B_V_HEADS)))
            alog_l, dtb_l = lane_pad(b_a_log[0]), lane_pad(b_dt_bias[0])
            norm_g = b_norm_g[0].reshape(1, B_HEAD)
            if prompt:
                proj, gates, conv_tail = _nm_delta_act_call(
                    xr, gmix, sc1, sh1, b_w_in[0], _left_rows(conv0), conv_w, alog_l, dtb_l)
                o_b, ssm1 = _delta_call(proj, gates, norm_g, s0)
                cnew = conv_tail[:, -1]
            else:
                proj, gates, _ = _nm_delta_in_call(
                    xr, gmix, sc1, sh1, b_w_in[0], conv_dim + B_V_HEADS * B_HEAD, conv_dim, F32)
                o_b, ssm1, cnew = _delta_call(seq_view(proj), seq_view(gates), norm_g, s0,
                                              (_left_rows(conv0), conv_w, alog_l, dtb_l))
            new_ssm = ssm1[None]
            new_bconv = cnew[None, :, SUBLANES - (conv_w.shape[0] - 1):, :]
            xr = _proj_res_call(rows_view(o_b), b_w_out[0], xr, g1, gfin, False)
        gffn = norm_ffn_g[layer].reshape(1, d)
        keep = ffn_conv_w.shape[1] - 1
        f0 = jnp.zeros((seqs, keep, d_ff), F32) if f_conv is None else f_conv[layer]
        conv_b = ffn_conv_b[layer].reshape(1, d_ff)
        mg2 = mods[layer][5][:, None, :]
        final = layer == depth - 1
        if prompt:
            xr, gate_tail = _ffn_fused_call(xr, gffn, sc2, sh2, ffn_w_up, _left_rows(f0),
                                            ffn_conv_w[layer], conv_b, ffn_w_down, mg2, gfin, final, layer)
            new_fconv.append(gate_tail[:, -1, SUBLANES - keep:, :])
        else:
            gate_pre, val = _nm_ffn_call(xr, gffn, sc2, sh2, ffn_w_up, layer)
            new_fconv.append(seq_view(gate_pre)[:, t - keep:, :])
            xr = _ffn_down_short_call(gate_pre, _left_rows(f0), val, ffn_conv_w[layer], conv_b,
                                      ffn_w_down, xr, g2, gfin, final, t, layer)
    return seq_view(xr), new_a, new_ssm, new_bconv, jnp.stack(new_fconv)


def kernel(x_prompt, x_sample, c_prompt, c_sample, cache_a_k_w128, cache_a_v_w128, cache_a_k_w512, cache_a_v_w512, cache_a_k_w2048, cache_a_v_w2048, state_b_ssm, state_b_conv, state_ffn_conv, norm_mix_g, norm_ffn_g, norm_final_g, w_mod, b_mod, a_w_in, a_w_out, b_w_in, b_conv_w, b_a_log, b_dt_bias, b_norm_g, b_w_out, ffn_w_up, ffn_conv_w, ffn_conv_b, ffn_w_down):
    bp, s, d = x_prompt.shape
    db, t_new, _ = x_sample.shape
    depth = w_mod.shape[0]

    c_all = jnp.concatenate([c_prompt, c_sample], axis=0)
    c_all = jnp.pad(c_all, ((0, (-c_all.shape[0]) % SUBLANES), (0, 0)))
    mod = _mod_call(c_all, w_mod, b_mod)

    def mods_of(lo, hi):
        return [[mod[l, lo:hi, k * d:(k + 1) * d] for k in range(6)] for l in range(depth)]

    b_w_in_p = jnp.pad(b_w_in, ((0, 0), (0, 0), (0, (-b_w_in.shape[2]) % LANES)))
    heads = jnp.arange(A_HEADS * A_HEAD_DIM) // A_HEAD_DIM
    half = (jnp.arange(LANES)[:, None] == heads[None, :]).astype(BF16)
    expand = jnp.concatenate([half, half], axis=0)
    wts = (norm_mix_g, norm_ffn_g, norm_final_g, a_w_in[0].astype(BF16), a_w_out.astype(BF16),
           b_w_in_p.astype(BF16), b_conv_w, b_a_log, b_dt_bias, b_norm_g, b_w_out.astype(BF16),
           ffn_w_up.astype(BF16), ffn_conv_w, ffn_conv_b, ffn_w_down.astype(BF16), expand)

    pos_p = jnp.arange(s, dtype=jnp.int32)
    pos_s = PAST_LEN + jnp.arange(t_new, dtype=jnp.int32)
    y_p, a_p, ssm_p, bconv_p, fconv_p = _trunk(
        x_prompt, mods_of(0, bp), False, pos_p, None, None, None, None, wts)
    a_caches = [cache_a_k_w128, cache_a_v_w128, cache_a_k_w512, cache_a_v_w512,
                cache_a_k_w2048, cache_a_v_w2048]
    y_s, a_s, ssm_s, bconv_s, fconv_s = _trunk(
        x_sample, mods_of(bp, bp + db), True, pos_s, a_caches, state_b_ssm, state_b_conv,
        state_ffn_conv, wts)
    return (y_p, y_s,
            a_p[0], a_s[0], a_p[1], a_s[1], a_p[2], a_s[2], a_p[3], a_s[3], a_p[4], a_s[4],
            a_p[5], a_s[5], ssm_p, ssm_s, bconv_p, bconv_s, fconv_p, fconv_s)
```

```python
import functools

import jax
import jax.numpy as jnp
from jax import lax
from jax.experimental import pallas as pl
from jax.experimental.pallas import tpu as pltpu

F32 = jnp.float32
BF16 = jnp.bfloat16

PAST_LEN = 16384
A_GROUPS = ((128, 1), (512, 4), (2048, 16))
A_HEADS = 16
A_HEAD_DIM = 64
A_ROW_TILE = 512
A_BLOCKS_PER_STEP = 8
FFN_ROW_TILE = 1024
ROPE_THETA = 10000.0
NORM_EPS = 1e-6
B_HEAD = 128
B_V_HEADS = 16
B_QK_HEADS = 8
B_CHUNK = 64
B_CHUNKS_PER_STEP = 2
B_STACK = 2
B_INV_BASE = 16
NEG_BIG = -1e30
LOG2E = 1.4426950408889634
LN2 = 0.6931471805599453

LANES = 128
SUBLANES = 8
VMEM_LIMIT_BYTES = 56 * 1024 * 1024


def _params(*sem):
    return pltpu.CompilerParams(dimension_semantics=sem, vmem_limit_bytes=VMEM_LIMIT_BYTES)


def _silu(x):
    return x * jax.nn.sigmoid(x)


def _dot(a, b):
    return jnp.dot(a, b, preferred_element_type=F32)


def _dot_nt(a, b):
    return lax.dot_general(a, b, (((1,), (1,)), ((), ())), preferred_element_type=F32)


def _shift_of(n):
    assert n > 0 and n & (n - 1) == 0, n
    return n.bit_length() - 1


def _div_pow2(v, n):
    return lax.shift_right_logical(v, _shift_of(n))


def _mod_pow2(v, n):
    assert n & (n - 1) == 0
    return v & (n - 1)


def _split3(x):
    hi = x.astype(BF16)
    r1 = x - hi.astype(F32)
    mid = r1.astype(BF16)
    lo = (r1 - mid.astype(F32)).astype(BF16)
    return hi, mid, lo


def _mod_kernel(c_ref, w_ref, b_ref, o_ref):
    cs = _silu(c_ref[...]).astype(BF16)
    o_ref[...] = _dot(cs, w_ref[...].astype(BF16)) + b_ref[...]


def _mod_call(c_all, w_mod, b_mod):
    depth, d, n = w_mod.shape
    rows = c_all.shape[0]
    tn = 1536
    return pl.pallas_call(
        _mod_kernel,
        grid=(depth, n // tn),
        in_specs=[
            pl.BlockSpec((rows, d), lambda l, j: (0, 0)),
            pl.BlockSpec((None, d, tn), lambda l, j: (l, 0, j)),
            pl.BlockSpec((None, 1, tn), lambda l, j: (l, 0, j)),
        ],
        out_specs=pl.BlockSpec((None, rows, tn), lambda l, j: (l, 0, j)),
        out_shape=jax.ShapeDtypeStruct((depth, rows, n), F32),
        compiler_params=_params("arbitrary", "arbitrary"),
        name="adaln_mod",
    )(c_all, w_mod, b_mod.reshape(depth, 1, n))


def _norm_mod(x, g, sc, sh):
    ms = jnp.mean(x * x, axis=-1, keepdims=True)
    return x * lax.rsqrt(ms + NORM_EPS) * g * (1.0 + sc) + sh


def _rope_tile(acc, cos, sin):
    lane = lax.broadcasted_iota(jnp.int32, (acc.shape[0], LANES), 1)
    first_half = _mod_pow2(lane, A_HEAD_DIM) < (A_HEAD_DIM // 2)
    outs = []
    for c in range(acc.shape[1] // LANES):
        a = acc[:, c * LANES:(c + 1) * LANES]
        swapped = jnp.where(first_half,
                            pltpu.roll(a, LANES - A_HEAD_DIM // 2, 1),
                            pltpu.roll(a, A_HEAD_DIM // 2, 1))
        outs.append(a * cos + swapped * sin)
    return jnp.concatenate(outs, axis=1)


def _nm_delta_in_kernel(x_ref, g_ref, sc_ref, sh_ref, w_ref, main_ref, gates_ref, tail_ref,
                        *, col_tile):
    h = _norm_mod(x_ref[...], g_ref[...], sc_ref[...], sh_ref[...]).astype(BF16)
    tm, n_main = main_ref.shape
    for c0 in range(0, n_main, col_tile):
        acc = _dot(h, w_ref[:, c0:c0 + col_tile])
        main_ref[:, c0:c0 + col_tile] = acc.astype(main_ref.dtype)
        if c0 < tail_ref.shape[1]:
            tail_ref[:, c0:c0 + col_tile] = acc[tm - SUBLANES:tm]
    gates_ref[...] = _dot(h, w_ref[:, n_main:n_main + LANES])


def _delta_gates(logits, alog, dtb):
    xg = logits + dtb
    softplus = jnp.maximum(xg, 0.0) + jnp.log1p(jnp.exp(-jnp.abs(xg)))
    return jax.nn.sigmoid(logits), -jnp.exp(alog) * softplus


def _l2n_heads(arr):
    outs = []
    for h0 in range(0, arr.shape[1], B_HEAD):
        a = arr[:, h0:h0 + B_HEAD]
        outs.append(a * lax.rsqrt(jnp.sum(a * a, axis=1, keepdims=True) + NORM_EPS))
    return jnp.concatenate(outs, axis=1)


def _nm_delta_act_kernel(x_ref, g_ref, sc_ref, sh_ref, w_ref, cstate_ref, cw_ref, alog_ref, dtb_ref,
                         main_ref, gates_ref, tail_ref, carry_scr, *, col_tile):
    tm, n_main = main_ref.shape
    width, conv_dim = cw_ref.shape
    qk_width = B_QK_HEADS * B_HEAD

    @pl.when(pl.program_id(1) == 0)
    def _():
        carry_scr[...] = cstate_ref[...]

    h = _norm_mod(x_ref[...], g_ref[...], sc_ref[...], sh_ref[...]).astype(BF16)
    ahead = _dot(h, w_ref[:, 0:col_tile])
    for c0 in range(0, n_main, col_tile):
        cols = slice(c0, c0 + col_tile)
        acc = ahead
        if c0 + col_tile < n_main:
            ahead = _dot(h, w_ref[:, c0 + col_tile:c0 + 2 * col_tile])
        if c0 < conv_dim:
            ext = jnp.concatenate([carry_scr[:, cols], acc], axis=0)
            y = cw_ref[width - 1:width, cols] * acc
            for back in range(1, width):
                y = y + (cw_ref[width - 1 - back:width - back, cols]
                         * pltpu.roll(ext, back, 0)[SUBLANES:SUBLANES + tm])
            carry_scr[:, cols] = acc[tm - SUBLANES:tm]
            tail_ref[:, cols] = acc[tm - SUBLANES:tm]
            acc = _silu(y)
            if c0 < 2 * qk_width:
                acc = _l2n_heads(acc)
            if c0 < qk_width:
                acc = acc * (B_HEAD ** -0.5)
        main_ref[:, cols] = acc.astype(main_ref.dtype)
    beta, decay = _delta_gates(_dot(h, w_ref[:, n_main:n_main + LANES]), alog_ref[...], dtb_ref[...])
    lane = lax.broadcasted_iota(jnp.int32, beta.shape, 1)
    gates_ref[...] = jnp.where(lane < B_V_HEADS, beta, decay)


def _nm_rope_kernel(x_ref, g_ref, sc_ref, sh_ref, w_ref, cos_ref, sin_ref, *rest, grouped,
                    tail_tile0):
    if grouped:
        perm_ref, wvt_ref, o_ref, tailt_ref, h_scr, tab_scr = rest
    else:
        tail_ref, h_scr = rest
    group = pl.program_id(2)

    @pl.when(group == 0)
    def _():
        h = _norm_mod(x_ref[...], g_ref[...], sc_ref[...], sh_ref[...]).astype(BF16)
        h_scr[0] = h
        if grouped:
            tab_scr[0, 0] = cos_ref[...]
            tab_scr[0, 1] = sin_ref[...]
            parts = jnp.concatenate(_split3(cos_ref[...]) + _split3(sin_ref[...]), axis=1)
            for p in range(perm_ref.shape[0]):
                h_scr[p + 1] = _dot(perm_ref[p], h).astype(BF16)
                moved = _dot(perm_ref[p], parts)
                for tab in range(2):
                    lo = 3 * tab * LANES
                    tab_scr[p + 1, tab] = (moved[:, lo:lo + LANES] + moved[:, lo + LANES:lo + 2 * LANES]
                                           + moved[:, lo + 2 * LANES:lo + 3 * LANES])

    h = h_scr[group] if grouped else h_scr[0]
    width = A_HEADS * A_HEAD_DIM
    if grouped:
        cos, sin = tab_scr[group, 0], tab_scr[group, 1]
    else:
        cos, sin = cos_ref[...], sin_ref[...]
    qscale = A_HEAD_DIM ** -0.5 * LOG2E
    def w_block(kind):
        return w_ref[:, pl.ds(pl.multiple_of((3 * group + kind) * width, LANES), width)]

    raw = [_dot(h, w_block(kind)) for kind in range(3)]
    for kind in range(3):
        val = raw[kind]
        if kind == 0:
            val = _rope_tile(val, cos * qscale, sin * qscale)
        elif kind == 1:
            val = _rope_tile(val, cos, sin)
        if grouped:
            o_ref[kind] = val.astype(o_ref.dtype)
        else:
            tail_ref[:, kind * width:(kind + 1) * width] = val

    if grouped:
        @pl.when(pl.program_id(1) >= tail_tile0)
        def _():
            keys = _rope_tile(_dot(h_scr[0], w_block(1)), tab_scr[0, 0], tab_scr[0, 1])
            tailt_ref[0] = keys.T
            tailt_ref[1] = _dot_nt(wvt_ref[group], h_scr[0])


def _nm_ffn_kernel(x_ref, g_ref, sc_ref, sh_ref, w_ref, gate_ref, val_ref, *, col_tile):
    h = _norm_mod(x_ref[...], g_ref[...], sc_ref[...], sh_ref[...]).astype(BF16)
    d_ff = gate_ref.shape[1]
    for c0 in range(0, d_ff, col_tile):
        gate_ref[:, c0:c0 + col_tile] = _dot(h, w_ref[:, c0:c0 + col_tile])
        val_ref[:, c0:c0 + col_tile] = _dot(
            h, w_ref[:, d_ff + c0:d_ff + c0 + col_tile]).astype(val_ref.dtype)


def _row_tile(rows):
    return min(rows, 512)


def _resident(shape, layer=None):
    if layer is None:
        zeros = (0,) * len(shape)
        return pl.BlockSpec(shape, lambda *_: zeros, pipeline_mode=pl.Buffered(1))
    index = (layer,) + (0,) * (len(shape) - 1)
    return pl.BlockSpec((None,) + tuple(shape[1:]), lambda *_: index, pipeline_mode=pl.Buffered(1))


def _nm_common_specs(sc, tm, d):
    per_row = sc.shape[1] != 1
    mod_spec = (pl.BlockSpec((None, tm, d), lambda b, i, *_: (b, i, 0)) if per_row
                else pl.BlockSpec((None, 1, d), lambda b, i, *_: (b, 0, 0)))
    return [
        pl.BlockSpec((None, tm, d), lambda b, i, *_: (b, i, 0)),
        pl.BlockSpec((1, d), lambda b, i, *_: (0, 0)),
        mod_spec,
        mod_spec,
    ]


def _nm_delta_in_call(x, g, sc, sh, w, n_main, conv_dim, main_dtype):
    bq, rows, d = x.shape
    tm = _row_tile(rows)
    tiles = rows // tm
    kern = functools.partial(_nm_delta_in_kernel, col_tile=1024)
    return pl.pallas_call(
        kern,
        grid=(bq, tiles),
        in_specs=_nm_common_specs(sc, tm, d) + [_resident(w.shape)],
        out_specs=[pl.BlockSpec((None, tm, n_main), lambda b, i: (b, i, 0)),
                   pl.BlockSpec((None, tm, LANES), lambda b, i: (b, i, 0)),
                   pl.BlockSpec((None, None, SUBLANES, conv_dim), lambda b, i: (b, i, 0, 0))],
        out_shape=[jax.ShapeDtypeStruct((bq, rows, n_main), main_dtype),
                   jax.ShapeDtypeStruct((bq, rows, LANES), F32),
                   jax.ShapeDtypeStruct((bq, tiles, SUBLANES, conv_dim), F32)],
        compiler_params=_params("arbitrary", "arbitrary"),
        name="norm_mod_delta_in",
    )(x, g, sc, sh, w)


def _nm_delta_act_call(x, g, sc, sh, w, cstate8, conv_w, alog_l, dtb_l):
    bq, rows, d = x.shape
    conv_dim = conv_w.shape[1]
    n_main = w.shape[1] - LANES
    tm = _row_tile(rows)
    tiles = rows // tm
    kern = functools.partial(_nm_delta_act_kernel, col_tile=1024)
    return pl.pallas_call(
        kern,
        grid=(bq, tiles),
        in_specs=_nm_common_specs(sc, tm, d) + [
            _resident(w.shape),
            pl.BlockSpec((None, SUBLANES, conv_dim), lambda b, i: (b, 0, 0)),
            _resident(conv_w.shape), _resident(alog_l.shape), _resident(dtb_l.shape)],
        out_specs=[pl.BlockSpec((None, tm, n_main), lambda b, i: (b, i, 0)),
                   pl.BlockSpec((None, tm, LANES), lambda b, i: (b, i, 0)),
                   pl.BlockSpec((None, None, SUBLANES, conv_dim), lambda b, i: (b, i, 0, 0))],
        out_shape=[jax.ShapeDtypeStruct((bq, rows, n_main), BF16),
                   jax.ShapeDtypeStruct((bq, rows, LANES), F32),
                   jax.ShapeDtypeStruct((bq, tiles, SUBLANES, conv_dim), F32)],
        scratch_shapes=[pltpu.VMEM((SUBLANES, conv_dim), F32)],
        compiler_params=_params("arbitrary", "arbitrary"),
        name="norm_mod_delta_in_act",
    )(x, g, sc, sh, w, cstate8, conv_w, alog_l, dtb_l)


def _residue_major(a, dil, tm=A_ROW_TILE):
    lead, (rows, c) = a.shape[:-2], a.shape[-2:]
    a = a.reshape(lead + (rows // tm, tm // dil, dil, c))
    return jnp.swapaxes(a, -2, -3).reshape(lead + (rows, c))


def _perm_matrices(tm=A_ROW_TILE):
    eye = jnp.eye(tm, dtype=BF16)
    return jnp.stack([_residue_major(eye, dil, tm) for _, dil in A_GROUPS if dil > 1])


def _nm_rope_call(x, g, sc, sh, w, cos, sin, tail_rows, perms=None, w_vt=None):
    bq, rows, d = x.shape
    ngroups = len(A_GROUPS)
    gw = w.shape[1] // ngroups
    width = gw // 3
    grouped = perms is not None
    tm = A_ROW_TILE if grouped else _row_tile(min(rows, tail_rows))
    tiles = rows // tm
    tail_tile0 = (rows - tail_rows) // tm
    kern = functools.partial(_nm_rope_kernel, grouped=grouped, tail_tile0=tail_tile0)
    tab_spec = pl.BlockSpec((None, tm, LANES), lambda b, i, j: (0, i, 0))

    def tail_map(b, i, j):
        return (b, jnp.where(i >= tail_tile0, j, 0), 0, jnp.maximum(i - tail_tile0, 0))

    in_specs = _nm_common_specs(sc, tm, d) + [_resident(w.shape), tab_spec, tab_spec]
    out_specs, out_shape, args = [], [], [x, g, sc, sh, w, cos, sin]
    if grouped:
        in_specs += [_resident(perms.shape), _resident(w_vt.shape)]
        args += [perms, w_vt]
        out_specs.append(pl.BlockSpec((None, 3, None, tm, width), lambda b, i, j: (b, j, i, 0, 0)))
        out_shape.append(jax.ShapeDtypeStruct((bq, 3 * ngroups, tiles, tm, width), BF16))
        out_specs.append(pl.BlockSpec((None, 2, width, tm), tail_map))
        out_shape.append(jax.ShapeDtypeStruct((bq, 2 * ngroups, width, tail_rows), F32))
    else:
        assert tail_tile0 == 0
        out_specs.append(pl.BlockSpec((None, tm, gw), lambda b, i, j: (b, i, j)))
        out_shape.append(jax.ShapeDtypeStruct((bq, rows, ngroups * gw), F32))
    n_h = 1 + (perms.shape[0] if grouped else 0)
    scratch = [pltpu.VMEM((n_h, tm, d), BF16)]
    if grouped:
        scratch.append(pltpu.VMEM((n_h, 2, tm, LANES), F32))
    return pl.pallas_call(
        kern,
        grid=(bq, tiles, ngroups),
        in_specs=in_specs,
        out_specs=out_specs,
        out_shape=out_shape,
        scratch_shapes=scratch,
        compiler_params=_params("arbitrary", "arbitrary", "arbitrary"),
        name="norm_mod_qkv_rope",
    )(*args)


def _nm_ffn_call(x, g, sc, sh, w, layer):
    bq, rows, d = x.shape
    d_ff = w.shape[2] // 2
    tm = _row_tile(rows)
    kern = functools.partial(_nm_ffn_kernel, col_tile=d_ff // 2)
    row_spec = pl.BlockSpec((None, tm, d_ff), lambda b, i: (b, i, 0))
    return pl.pallas_call(
        kern,
        grid=(bq, rows // tm),
        in_specs=_nm_common_specs(sc, tm, d) + [_resident(w.shape, layer)],
        out_specs=[row_spec, row_spec],
        out_shape=[jax.ShapeDtypeStruct((bq, rows, d_ff), F32),
                   jax.ShapeDtypeStruct((bq, rows, d_ff), BF16)],
        compiler_params=_params("arbitrary", "arbitrary"),
        name="norm_mod_ffn_up",
    )(x, g, sc, sh, w)


def _finish(y, x_ref, gate_ref, gfin_ref, o_ref, final_norm):
    xn = x_ref[...] + gate_ref[...] * y
    if final_norm:
        ms = jnp.mean(xn * xn, axis=-1, keepdims=True)
        xn = xn * lax.rsqrt(ms + NORM_EPS) * gfin_ref[...]
    o_ref[...] = xn


def _proj_res_kernel(a_ref, w_ref, x_ref, gate_ref, gfin_ref, o_ref, *, final_norm):
    _finish(_dot(a_ref[...], w_ref[...]), x_ref, gate_ref, gfin_ref, o_ref, final_norm)


def _attn_merge_proj_kernel(o0_ref, o1_ref, o2_ref, l0_ref, l1_ref, l2_ref, e_ref, w_ref,
                            x_ref, gate_ref, gfin_ref, *rest, final_norm, grouped):
    if grouped:
        unperm_ref, o_ref = rest
    else:
        (o_ref,) = rest
    outs, lses = [], []
    slot = 0
    for (_, dil), og_ref, lg_ref in zip(A_GROUPS, (o0_ref, o1_ref, o2_ref), (l0_ref, l1_ref, l2_ref)):
        if grouped and dil > 1:
            pt = unperm_ref[slot]
            slot += 1
            outs.append(_dot(pt, og_ref[...]))
            parts = _dot(pt, jnp.concatenate(_split3(lg_ref[...]), axis=1))
            lses.append(parts[:, 0:LANES] + parts[:, LANES:2 * LANES] + parts[:, 2 * LANES:3 * LANES])
        else:
            outs.append(og_ref[...].astype(F32))
            lses.append(lg_ref[...])
    m = jnp.maximum(jnp.maximum(lses[0], lses[1]), lses[2])
    exps = [jnp.exp(l - m) for l in lses]
    inv = 1.0 / (exps[0] + exps[1] + exps[2])
    merged = None
    for e, og in zip(exps, outs):
        wgt = e * inv
        hi = wgt.astype(BF16)
        lo = (wgt - hi.astype(F32)).astype(BF16)
        term = _dot(jnp.concatenate([hi, lo], axis=1), e_ref[...]) * og
        merged = term if merged is None else merged + term
    _finish(_dot(merged.astype(BF16), w_ref[...]), x_ref, gate_ref, gfin_ref, o_ref, final_norm)


def _res_specs(tm, d, gate):
    per_row = gate.shape[1] != 1
    gate_spec = (pl.BlockSpec((None, tm, d), lambda b, i: (b, i, 0)) if per_row
                 else pl.BlockSpec((None, 1, d), lambda b, i: (b, 0, 0)))
    return [pl.BlockSpec((None, tm, d), lambda b, i: (b, i, 0)),
            gate_spec,
            pl.BlockSpec((1, d), lambda b, i: (0, 0))]


def _proj_res_call(a, w, x, gate, gfin, final_norm):
    bq, rows, d = x.shape
    k = a.shape[2]
    tm = _row_tile(rows)
    kern = functools.partial(_proj_res_kernel, final_norm=final_norm)
    return pl.pallas_call(
        kern,
        grid=(bq, rows // tm),
        in_specs=[pl.BlockSpec((None, tm, k), lambda b, i: (b, i, 0)),
                  pl.BlockSpec((k, d), lambda b, i: (0, 0))] + _res_specs(tm, d, gate),
        out_specs=pl.BlockSpec((None, tm, d), lambda b, i: (b, i, 0)),
        out_shape=jax.ShapeDtypeStruct((bq, rows, d), F32),
        compiler_params=_params("arbitrary", "arbitrary"),
        name="proj_residual",
    )(a, w, x, gate, gfin)


def _attn_merge_proj_call(outs, lses, expand, w, x, gate, gfin, final_norm, unperms=None):
    bq, rows, d = x.shape
    grouped = unperms is not None
    tm = A_ROW_TILE if grouped else _row_tile(rows)
    kern = functools.partial(_attn_merge_proj_kernel, final_norm=final_norm, grouped=grouped)
    o_spec = pl.BlockSpec((None, tm, d), lambda b, i: (b, i, 0))
    l_spec = pl.BlockSpec((None, tm, LANES), lambda b, i: (b, i, 0))
    extra_specs, extra = [], []
    if grouped:
        extra_specs.append(pl.BlockSpec(unperms.shape, lambda b, i: (0, 0, 0)))
        extra.append(unperms)
    return pl.pallas_call(
        kern,
        grid=(bq, rows // tm),
        in_specs=[o_spec] * 3 + [l_spec] * 3 + [
            pl.BlockSpec(expand.shape, lambda b, i: (0, 0)),
            pl.BlockSpec(w.shape, lambda b, i: (0, 0))] + _res_specs(tm, d, gate) + extra_specs,
        out_specs=pl.BlockSpec((None, tm, d), lambda b, i: (b, i, 0)),
        out_shape=jax.ShapeDtypeStruct((bq, rows, d), F32),
        compiler_params=_params("arbitrary", "arbitrary"),
        name="attn_merge_proj_residual",
    )(*outs, *lses, expand, w, x, gate, gfin, *extra)


def _ffn_down_short_kernel(gate_ref, left_ref, val_ref, cw_ref, cb_ref, w_ref,
                           x_ref, mgate_ref, gfin_ref, o_ref, *, final_norm, seq_len):
    gate = gate_ref[...]
    width = cw_ref.shape[0]
    tpos = _mod_pow2(lax.broadcasted_iota(jnp.int32, gate.shape, 0), seq_len)
    y = cb_ref[...] + cw_ref[width - 1:width, :] * gate
    for back in range(1, width):
        shifted = jnp.where(tpos >= back, pltpu.roll(gate, back, 0), left_ref[back - 1])
        y = y + cw_ref[width - 1 - back:width - back, :] * shifted
    act = (_silu(y) * val_ref[...].astype(F32)).astype(BF16)
    _finish(_dot(act, w_ref[...]), x_ref, mgate_ref, gfin_ref, o_ref, final_norm)


def _ffn_down_short_call(gate_pre, state8, val, conv_w, conv_b, w, x, mgate, gfin, final_norm, seq_len,
                         layer):
    _, rows, d = x.shape
    f = gate_pre.shape[2]
    width = conv_w.shape[0]
    lefts = jnp.stack([jnp.roll(state8, back, axis=1).reshape(rows, f) for back in range(1, width)])
    kern = functools.partial(_ffn_down_short_kernel, final_norm=final_norm, seq_len=seq_len)
    whole = lambda a: pl.BlockSpec(a.shape, lambda i: (0,) * a.ndim)
    rows_of = lambda a: pl.BlockSpec((None,) + a.shape[1:], lambda i: (0, 0, 0))
    return pl.pallas_call(
        kern,
        grid=(1,),
        in_specs=[rows_of(gate_pre), whole(lefts), rows_of(val), whole(conv_w), whole(conv_b),
                  pl.BlockSpec((None,) + w.shape[1:], lambda i: (layer, 0, 0)),
                  rows_of(x), rows_of(mgate), whole(gfin)],
        out_specs=rows_of(x),
        out_shape=jax.ShapeDtypeStruct(x.shape, F32),
        compiler_params=_params("arbitrary"),
        name="ffn_conv_down_residual_short",
    )(gate_pre, lefts, val, conv_w, conv_b, w, x, mgate, gfin)


def _ffn_fused_kernel(x_ref, g_ref, sc_ref, sh_ref, wup_ref, state_ref, cw_ref, cb_ref, wdn_ref,
                      mgate_ref, gfin_ref, o_ref, gtail_ref, carry_scr, *, final_norm, col_tile):
    tm = x_ref.shape[0]
    d_ff = wdn_ref.shape[0]
    width = cw_ref.shape[0]

    @pl.when(pl.program_id(1) == 0)
    def _():
        carry_scr[...] = state_ref[...]

    h = _norm_mod(x_ref[...], g_ref[...], sc_ref[...], sh_ref[...]).astype(BF16)

    def up(c0):
        return (_dot(h, wup_ref[:, c0:c0 + col_tile]),
                _dot(h, wup_ref[:, d_ff + c0:d_ff + c0 + col_tile]))

    acc = None
    ahead = up(0)
    for c0 in range(0, d_ff, col_tile):
        cols = slice(c0, c0 + col_tile)
        gate, val = ahead
        if c0 + col_tile < d_ff:
            ahead = up(c0 + col_tile)
        ext = jnp.concatenate([carry_scr[:, cols], gate], axis=0)
        y = cb_ref[:, cols] + cw_ref[width - 1:width, cols] * gate
        for back in range(1, width):
            y = y + (cw_ref[width - 1 - back:width - back, cols]
                     * pltpu.roll(ext, back, 0)[SUBLANES:SUBLANES + tm])
        act = (_silu(y) * val).astype(BF16)
        part = _dot(act, wdn_ref[cols, :])
        acc = part if acc is None else acc + part
        carry_scr[:, cols] = gate[tm - SUBLANES:tm]
        gtail_ref[:, cols] = gate[tm - SUBLANES:tm]
    _finish(acc, x_ref, mgate_ref, gfin_ref, o_ref, final_norm)


def _ffn_fused_call(x, g, sc, sh, w_up, state8, conv_w, conv_b, w_down, mgate, gfin, final_norm,
                    layer):
    bq, rows, d = x.shape
    d_ff = w_down.shape[1]
    tm = min(rows, FFN_ROW_TILE)
    tiles = rows // tm
    col_tile = 2 * LANES
    assert d_ff % col_tile == 0
    kern = functools.partial(_ffn_fused_kernel, final_norm=final_norm, col_tile=col_tile)
    return pl.pallas_call(
        kern,
        grid=(bq, tiles),
        in_specs=_nm_common_specs(sc, tm, d) + [
            _resident(w_up.shape, layer),
            pl.BlockSpec((None, SUBLANES, d_ff), lambda b, i: (b, 0, 0)),
            _resident(conv_w.shape),
            _resident(conv_b.shape),
            _resident(w_down.shape, layer),
        ] + _res_specs(tm, d, mgate)[1:],
        out_specs=[pl.BlockSpec((None, tm, d), lambda b, i: (b, i, 0)),
                   pl.BlockSpec((None, None, SUBLANES, d_ff), lambda b, i: (b, i, 0, 0))],
        out_shape=[jax.ShapeDtypeStruct((bq, rows, d), F32),
                   jax.ShapeDtypeStruct((bq, tiles, SUBLANES, d_ff), F32)],
        scratch_shapes=[pltpu.VMEM((SUBLANES, d_ff), F32)],
        compiler_params=_params("arbitrary", "arbitrary"),
        name="conv_ffn",
    )(x, g, sc, sh, w_up, state8, conv_w, conv_b, w_down, mgate, gfin)


def _attn_prompt_kernel(q_ref, kp_ref, kc_ref, vp_ref, vc_ref, o_ref, lse_ref):
    blk = kp_ref.shape[0] * kp_ref.shape[1]
    prow = q_ref.shape[1]
    run = q_ref.shape[0] * prow
    nq = run // blk

    def rows_of(ref, cols):
        return jnp.concatenate([ref[p, :, cols] for p in range(ref.shape[0])], axis=0)

    n = pl.program_id(2)
    qi = lax.broadcasted_iota(jnp.int32, (blk, 2 * blk), 0)
    kj = lax.broadcasted_iota(jnp.int32, (blk, 2 * blk), 1)
    band = (kj >= qi) & (kj <= qi + blk)
    bias = jnp.where(band, 0.0, NEG_BIG)
    bias_first = jnp.where(band & ((kj >= blk) | (n > 0)), 0.0, NEG_BIG)
    lane = lax.broadcasted_iota(jnp.int32, (blk, LANES), 1)
    low = lane < A_HEAD_DIM
    m_all = [jnp.zeros((blk, LANES), F32) for _ in range(nq)]
    l_all = [jnp.ones((blk, LANES), F32) for _ in range(nq)]
    for hp in range(A_HEADS // 2):
        cols = slice(hp * LANES, (hp + 1) * LANES)
        q_run = rows_of(q_ref, cols)
        k_run = jnp.concatenate([rows_of(kp_ref, cols), rows_of(kc_ref, cols)], axis=0)
        v_run = jnp.concatenate([rows_of(vp_ref, cols), rows_of(vc_ref, cols)], axis=0)
        outs = []
        for j in range(nq):
            q2 = q_run[j * blk:(j + 1) * blk]
            k2 = k_run[j * blk:(j + 2) * blk]
            v2 = v_run[j * blk:(j + 2) * blk]
            mask = bias_first if j == 0 else bias
            pair = []
            for s in range(2):
                mine = low if s == 0 else jnp.logical_not(low)
                qh = jnp.where(mine, q2, jnp.zeros_like(q2))
                sc = _dot_nt(qh, k2) + mask
                m = jnp.max(sc, axis=1, keepdims=True)
                p = jnp.exp2(sc - m)
                l = jnp.sum(p, axis=1, keepdims=True)
                pair.append(_dot(p.astype(BF16), v2) / l)
                m_all[j] = jnp.where(lane == 2 * hp + s, m, m_all[j])
                l_all[j] = jnp.where(lane == 2 * hp + s, l, l_all[j])
            outs.append(jnp.where(low, pair[0], pair[1]).astype(o_ref.dtype))
        o_run = jnp.concatenate(outs, axis=0)
        for p in range(o_ref.shape[0]):
            o_ref[p, :, cols] = o_run[p * prow:(p + 1) * prow]
    lse_run = jnp.concatenate([(m_all[j] + jnp.log2(l_all[j])) * LN2 for j in range(nq)], axis=0)
    for p in range(lse_ref.shape[0]):
        lse_ref[p] = lse_run[p * prow:(p + 1) * prow]


def _attn_prompt_call(qkv, gi, win, dil):
    b, nkinds, tiles, tm, width = qkv.shape
    per_res = tm // dil
    qkv_g = qkv.reshape(b, nkinds, tiles, dil, per_res, width)
    blk = win // dil
    assert (tiles * tm) % win == 0
    nb = tiles * per_res // blk
    nq = min(A_BLOCKS_PER_STEP, nb)
    assert nb % nq == 0

    def geometry(rows):
        prow = min(rows, per_res)
        return rows // prow, prow

    def spec_of(rows, block_of_step, kind, lanes=width, lead=True):
        pieces, prow = geometry(rows)
        per_tile = per_res // prow

        def index(bb, r, i):
            j = jnp.maximum(block_of_step(i), 0)
            tail = (lax.div(j, per_tile), r, lax.rem(j, per_tile), 0)
            return (bb, 3 * gi + kind) + tail if lead else (bb,) + tail

        shape = (None, pieces, None, prow, lanes)
        return pl.BlockSpec((None,) + shape if lead else shape, index)

    cur = lambda kind: spec_of(nq * blk, lambda i: i, kind)
    prev = lambda kind: spec_of(blk, lambda i: nq * i - 1, kind)
    return pl.pallas_call(
        _attn_prompt_kernel,
        grid=(b, dil, nb // nq),
        in_specs=[cur(0), prev(1), cur(1), prev(2), cur(2)],
        out_specs=[spec_of(nq * blk, lambda i: i, 0, width, False),
                   spec_of(nq * blk, lambda i: i, 0, LANES, False)],
        out_shape=[jax.ShapeDtypeStruct((b, tiles, dil, per_res, width), BF16),
                   jax.ShapeDtypeStruct((b, tiles, dil, per_res, LANES), F32)],
        compiler_params=_params("arbitrary", "arbitrary", "arbitrary"),
        name="dilated_attn_prompt",
    )(qkv_g, qkv_g, qkv_g, qkv_g, qkv_g)


def _attn_sample_kernel(q_ref, kn_ref, vn_ref, kc_ref, vc_ref,
                        ko_ref, vo_ref, o_ref, lse_ref,
                        m_scr, l_scr, acc_scr, kcar_scr, vcar_scr, *, lb, dil):
    t_new = q_ref.shape[0]
    width = q_ref.shape[1]
    chunk = kc_ref.shape[1]
    rows = A_HEADS * t_new
    step = pl.program_id(1)
    c = pl.num_programs(1) - 1 - step

    rq = lax.broadcasted_iota(jnp.int32, (rows, width), 0)
    lq = lax.broadcasted_iota(jnp.int32, (rows, width), 1)
    own = _div_pow2(lq, A_HEAD_DIM) == _div_pow2(rq, t_new)
    qbd = jnp.where(own, jnp.concatenate([q_ref[...]] * A_HEADS, axis=0), 0.0).astype(BF16)

    def accumulate(keys_t, vals_t, valid):
        sc = jnp.where(valid, _dot(qbd, keys_t.astype(BF16)), NEG_BIG)
        m_old = m_scr[...]
        m_new = jnp.maximum(m_old, jnp.max(sc, axis=1, keepdims=True))
        alpha = jnp.exp2(m_old - m_new)
        p = jnp.where(valid, jnp.exp2(sc - m_new), 0.0)
        l_scr[...] = alpha * l_scr[...] + jnp.sum(p, axis=1, keepdims=True)
        acc_scr[...] = alpha * acc_scr[...] + _dot_nt(p.astype(BF16), vals_t.astype(BF16))
        m_scr[...] = m_new

    @pl.when(step == 0)
    def _():
        m_scr[...] = jnp.full_like(m_scr, NEG_BIG)
        l_scr[...] = jnp.zeros_like(l_scr)
        acc_scr[...] = jnp.zeros_like(acc_scr)
        pad = jnp.zeros((LANES - t_new, width), F32)
        kcar_scr[...] = jnp.concatenate([kn_ref[...], pad], axis=0).T
        vcar_scr[...] = jnp.concatenate([vn_ref[...], pad], axis=0).T
        tq2 = _mod_pow2(lax.broadcasted_iota(jnp.int32, (rows, LANES), 0), t_new)
        tk2 = lax.broadcasted_iota(jnp.int32, (rows, LANES), 1)
        d2 = tq2 - tk2
        accumulate(kcar_scr[...], vcar_scr[...],
                   (tk2 < t_new) & (d2 >= 0) & (_mod_pow2(d2, dil) == 0))

    lane = lax.broadcasted_iota(jnp.int32, (width, LANES), 1)
    keep = lane < LANES - t_new
    nblk = chunk // LANES
    for src_ref, car_ref, dst_ref in ((kc_ref, kcar_scr, ko_ref), (vc_ref, vcar_scr, vo_ref)):
        nxt = pltpu.roll(car_ref[...], LANES - t_new, 1)
        for b in reversed(range(nblk)):
            cur = pltpu.roll(src_ref[:, b * LANES:(b + 1) * LANES], LANES - t_new, 1)
            dst_ref[:, b * LANES:(b + 1) * LANES] = jnp.where(keep, cur, nxt)
            nxt = cur
        car_ref[...] = src_ref[:, 0:LANES]

    tq = _mod_pow2(lax.broadcasted_iota(jnp.int32, (rows, chunk), 0), t_new)
    ik = lax.broadcasted_iota(jnp.int32, (rows, chunk), 1) + c * chunk
    dist = lb + tq - ik
    accumulate(kc_ref[...], vc_ref[...], (_mod_pow2(dist, dil) == 0) & (dist <= lb))

    @pl.when(c == 0)
    def _():
        full = jnp.where(own, acc_scr[...] / l_scr[...], 0.0)
        o_ref[...] = jnp.sum(full.reshape(A_HEADS, t_new, width), axis=0).astype(o_ref.dtype)
        lse_col = (m_scr[...] + jnp.log2(l_scr[...])) * LN2
        rl = lax.broadcasted_iota(jnp.int32, (rows, LANES), 0)
        ll = lax.broadcasted_iota(jnp.int32, (rows, LANES), 1)
        spread = jnp.where(ll == _div_pow2(rl, t_new), lse_col, 0.0)
        lse_ref[...] = jnp.sum(spread.reshape(A_HEADS, t_new, LANES), axis=0)


def _attn_sample_call(proj, gi, win, dil, k_cache_t, v_cache_t):
    db, t_new, n = proj.shape
    width = A_HEADS * A_HEAD_DIM
    lb = k_cache_t.shape[2]
    assert lb == win and t_new == SUBLANES
    chunk = min(lb, 1024)
    nchunk = lb // chunk
    kern = functools.partial(_attn_sample_kernel, lb=lb, dil=dil)
    new_spec = lambda kind: pl.BlockSpec((None, t_new, width), lambda b, s: (b, 0, 3 * gi + kind))
    cur = pl.BlockSpec((None, width, chunk), lambda b, s: (b, 0, nchunk - 1 - s))
    return pl.pallas_call(
        kern,
        grid=(db, nchunk),
        in_specs=[new_spec(0), new_spec(1), new_spec(2), cur, cur],
        out_specs=[cur, cur,
                   pl.BlockSpec((None, t_new, width), lambda b, s: (b, 0, 0)),
                   pl.BlockSpec((None, t_new, LANES), lambda b, s: (b, 0, 0))],
        out_shape=[jax.ShapeDtypeStruct((db, width, lb), F32),
                   jax.ShapeDtypeStruct((db, width, lb), F32),
                   jax.ShapeDtypeStruct((db, t_new, width), BF16),
                   jax.ShapeDtypeStruct((db, t_new, LANES), F32)],
        scratch_shapes=[pltpu.VMEM((A_HEADS * t_new, 1), F32),
                        pltpu.VMEM((A_HEADS * t_new, 1), F32),
                        pltpu.VMEM((A_HEADS * t_new, width), F32),
                        pltpu.VMEM((width, LANES), F32),
                        pltpu.VMEM((width, LANES), F32)],
        compiler_params=_params("arbitrary", "arbitrary"),
        name="dilated_attn_sample",
    )(proj, proj, proj, k_cache_t, v_cache_t)


def _delta_kernel(*refs, t_valid, activated):
    if activated:
        p_ref, gates_ref, ng_ref, s0_ref, o_ref, ssm_ref, s_scr = refs
    else:
        (p_ref, prev_ref, gates_ref, cstate_ref, cw_ref, alog_ref, dtb_ref, ng_ref, s0_ref,
         o_ref, ssm_ref, cnew_ref, s_scr, ext_scr) = refs
    cl = B_CHUNK
    rows_in = p_ref.shape[0]
    nsub = max(rows_in // cl, 1)
    rows_pad = nsub * cl
    qk_width = B_QK_HEADS * B_HEAD
    conv_dim = 2 * qk_width + B_V_HEADS * B_HEAD
    c = pl.program_id(1)
    last = c == pl.num_programs(1) - 1

    @pl.when(c == 0)
    def _():
        s_scr[...] = s0_ref[...]

    if activated:
        assert t_valid == rows_pad
        qkv = p_ref[:, 0:conv_dim].astype(F32)
        qn, kn = qkv[:, 0:qk_width], qkv[:, qk_width:2 * qk_width]
        beta_all = g_all = gates_ref[...]
    else:
        qkv, beta_all, g_all = _delta_front(p_ref, prev_ref, gates_ref, cstate_ref, cw_ref, alog_ref,
                                            dtb_ref, cnew_ref, ext_scr, rows_pad, t_valid)
        qn = _l2n_heads(qkv[:, 0:qk_width]) * (B_HEAD ** -0.5)
        kn = _l2n_heads(qkv[:, qk_width:2 * qk_width])
    _delta_chunks(qkv, qn, kn, beta_all, g_all, p_ref, ng_ref, o_ref, s_scr, nsub)

    @pl.when(last)
    def _():
        ssm_ref[...] = s_scr[...]


def _delta_front(p_ref, prev_ref, gates_ref, cstate_ref, cw_ref, alog_ref, dtb_ref, cnew_ref, ext_scr,
                 rows_pad, t_valid):
    rows_in = p_ref.shape[0]
    conv_dim = cw_ref.shape[1]
    c = pl.program_id(1)
    last = c == pl.num_programs(1) - 1

    halo = prev_ref.shape[0]
    left = prev_ref[:, 0:conv_dim].astype(F32)
    if halo > SUBLANES:
        ext_scr[0:halo - SUBLANES, :] = left[0:halo - SUBLANES]
    ext_scr[halo - SUBLANES:halo, :] = jnp.where(c == 0, cstate_ref[...], left[halo - SUBLANES:halo])
    ext_scr[halo:halo + rows_in, :] = p_ref[:, 0:conv_dim].astype(F32)
    if rows_in < rows_pad:
        ext_scr[halo + rows_in:halo + rows_pad, :] = jnp.zeros((rows_pad - rows_in, conv_dim), F32)
    width = cw_ref.shape[0]
    ext = ext_scr[...]
    y = cw_ref[width - 1:width, :] * ext[halo:halo + rows_pad]
    for back in range(1, width):
        shifted = pltpu.roll(ext, back, 0)[halo:halo + rows_pad]
        y = y + cw_ref[width - 1 - back:width - back, :] * shifted
    qkv = _silu(y)
    if t_valid < rows_pad:
        rmask = lax.broadcasted_iota(jnp.int32, (rows_pad, 1), 0) < t_valid
        qkv = jnp.where(rmask, qkv, 0.0)

    @pl.when(last)
    def _():
        cnew_ref[...] = ext_scr[halo + t_valid - SUBLANES:halo + t_valid, :]

    graw = gates_ref[...]
    if rows_in < rows_pad:
        graw = jnp.concatenate([graw, jnp.zeros((rows_pad - rows_in, LANES), F32)], axis=0)
    beta_all, g_all = _delta_gates(graw, alog_ref[...], dtb_ref[...])
    if t_valid < rows_pad:
        beta_all = jnp.where(rmask, beta_all, 0.0)
        g_all = jnp.where(rmask, g_all, 0.0)
    return qkv, beta_all, g_all


def _delta_chunks(qkv, qn, kn, beta_all, g_all, p_ref, ng_ref, o_ref, s_scr, nsub):
    cl = B_CHUNK
    rows_in = p_ref.shape[0]
    qk_width = B_QK_HEADS * B_HEAD
    conv_dim = 2 * qk_width + B_V_HEADS * B_HEAD

    def rows_of(sub):
        return slice(sub * cl, (sub + 1) * cl)

    ri = lax.broadcasted_iota(jnp.int32, (2 * cl, cl), 0)
    ci = lax.broadcasted_iota(jnp.int32, (2 * cl, cl), 1)
    summat = jnp.where((ci <= ri) | (ri >= cl), 1.0, 0.0).astype(BF16)
    cg_all, gl_all = [], []
    for sub in range(nsub):
        sums = _dot(summat, jnp.concatenate(_split3(g_all[rows_of(sub)]), axis=1))
        sums = sums[:, 0:LANES] + sums[:, LANES:2 * LANES] + sums[:, 2 * LANES:3 * LANES]
        cg_all.append(sums[0:cl])
        gl_all.append(sums[cl:2 * cl])

    nst = B_STACK * cl
    rr = lax.broadcasted_iota(jnp.int32, (nst, nst), 0)
    cc = lax.broadcasted_iota(jnp.int32, (nst, nst), 1)
    same = _div_pow2(rr, cl) == _div_pow2(cc, cl)
    incl = same & (cc <= rr)
    strict = same & (cc < rr)
    lane6 = lax.broadcasted_iota(jnp.int32, (nst, LANES), 1)
    rblk = _div_pow2(lax.broadcasted_iota(jnp.int32, (nst, B_HEAD), 0), cl)

    def col_stack(arr, lane0, heads):
        return jnp.concatenate([arr[:, lane0 + h:lane0 + h + 1] for h in heads], axis=0)

    groups = [list(range(g * B_STACK, (g + 1) * B_STACK)) for g in range(B_V_HEADS // B_STACK)]
    problems = [(sub, hs) for sub in range(nsub) for hs in groups]
    every = range(len(problems))
    rep = B_V_HEADS // B_QK_HEADS

    def head_stack(arr, col0, sub, heads):
        return jnp.concatenate([arr[rows_of(sub), col0 + h * B_HEAD:col0 + (h + 1) * B_HEAD]
                                for h in heads], axis=0)

    q_st = [head_stack(qn, 0, sub, [h // rep for h in hs]) for sub, hs in problems]
    k_st = [head_stack(kn, 0, sub, [h // rep for h in hs]) for sub, hs in problems]
    v_st = [head_stack(qkv, 2 * qk_width, sub, hs) for sub, hs in problems]
    beta = [col_stack(beta_all[rows_of(sub)], 0, hs) for sub, hs in problems]
    cg = [col_stack(cg_all[sub], B_V_HEADS, hs) for sub, hs in problems]
    gl = [col_stack(gl_all[sub], B_V_HEADS, hs) for sub, hs in problems]

    def diff_operands(c):
        c_hi, c_mid, c_lo = (p.astype(F32) for p in _split3(c))
        lmat = jnp.where(lane6 == 0, c_hi, jnp.where(lane6 == 1, c_mid, jnp.where(
            lane6 == 2, c_lo, jnp.where(lane6 < 6, 1.0, 0.0))))
        rmat = jnp.where(lane6 == 3, -c_hi, jnp.where(lane6 == 4, -c_mid, jnp.where(
            lane6 == 5, -c_lo, jnp.where(lane6 < 3, 1.0, 0.0))))
        return lmat.astype(BF16), rmat.astype(BF16)

    diff = [_dot_nt(*diff_operands(cg[g])) for g in every]
    k_bf = [k_st[g].astype(BF16) for g in every]
    gram = [_dot_nt(k_bf[g], k_bf[g]) for g in every]
    qk = [_dot_nt(q_st[g].astype(BF16), k_bf[g]) for g in every]
    decay = [jnp.where(incl, jnp.exp(jnp.where(incl, diff[g], 0.0)), 0.0) for g in every]
    a_mat = [jnp.where(strict, beta[g] * gram[g] * decay[g], 0.0) for g in every]
    p_mat = [(qk[g] * decay[g]).astype(BF16) for g in every]

    blk = _div_pow2(rr, B_INV_BASE) == _div_pow2(cc, B_INV_BASE)
    apow = [jnp.where(blk, a_mat[g], 0.0) for g in every]
    nmat = [-apow[g] for g in every]
    for _ in range((B_INV_BASE - 1).bit_length() - 1):
        ab = [apow[g].astype(BF16) for g in every]
        apow = [_dot(ab[g], ab[g]) for g in every]
        nmat = [nmat[g] + apow[g] + _dot(nmat[g].astype(BF16), apow[g].astype(BF16)) for g in every]
    size = B_INV_BASE
    while size < cl:
        size *= 2
        merged = _div_pow2(rr, size) == _div_pow2(cc, size)
        join = merged & jnp.logical_not(blk)
        low = [jnp.where(join, a_mat[g], 0.0) for g in every]
        nb = [nmat[g].astype(BF16) for g in every]
        dl = [low[g] + _dot(nb[g], low[g].astype(BF16)) for g in every]
        nmat = [nmat[g] - (dl[g] + _dot(dl[g].astype(BF16), nb[g])) for g in every]
        blk = merged
    ecg = [jnp.exp(cg[g]) for g in every]
    rhs = [jnp.concatenate([beta[g] * v_st[g], (beta[g] * ecg[g]) * k_st[g]], axis=1) for g in every]
    sol = [rhs[g] + _dot(nmat[g].astype(BF16), rhs[g].astype(BF16)) for g in every]
    qg = [q_st[g] * ecg[g] for g in every]
    kdec_t = [(k_st[g] * jnp.exp(gl[g] - cg[g])).T.astype(BF16) for g in every]

    live = min(rows_in, cl)
    for sub in range(nsub):
        mine = [g for g in every if problems[g][0] == sub]
        both = {}
        for g in mine:
            for idx, h in enumerate(problems[g][1]):
                rsl = slice(idx * cl, (idx + 1) * cl)
                lhs = jnp.concatenate([sol[g][rsl, B_HEAD:2 * B_HEAD], qg[g][rsl]], axis=0)
                both[h] = _dot(lhs.astype(BF16), s_scr[h].astype(BF16))
        u_st = {g: jnp.concatenate([sol[g][idx * cl:(idx + 1) * cl, 0:B_HEAD] - both[h][0:cl]
                                    for idx, h in enumerate(problems[g][1])], axis=0) for g in mine}
        o_st = {g: jnp.concatenate([both[h][cl:2 * cl] for h in problems[g][1]], axis=0)
                + _dot(p_mat[g], u_st[g].astype(BF16)) for g in mine}
        for g in mine:
            for idx, h in enumerate(problems[g][1]):
                u_own = jnp.where(rblk == idx, u_st[g], 0.0).astype(BF16)
                s_scr[h] = jnp.exp(gl[g][idx * cl:idx * cl + 1]) * s_scr[h] + _dot(kdec_t[g], u_own)
        out_rows = slice(sub * cl, sub * cl + live)
        for g in mine:
            for idx, h in enumerate(problems[g][1]):
                o_h = o_st[g][idx * cl:idx * cl + live]
                ms = jnp.mean(o_h * o_h, axis=1, keepdims=True)
                z_h = p_ref[out_rows, conv_dim + h * B_HEAD:conv_dim + (h + 1) * B_HEAD].astype(F32)
                res = o_h * lax.rsqrt(ms + NORM_EPS) * ng_ref[...] * _silu(z_h)
                o_ref[out_rows, h * B_HEAD:(h + 1) * B_HEAD] = res.astype(o_ref.dtype)


def _delta_call(proj, gates, norm_g, s0, raw=None):
    bq, t, n = proj.shape
    v_width = B_V_HEADS * B_HEAD
    rows_in = min(t, B_CHUNK * B_CHUNKS_PER_STEP)
    assert t % rows_in == 0 and (rows_in % B_CHUNK == 0 or rows_in < B_CHUNK)
    nchunk = t // rows_in
    activated = raw is None
    kern = functools.partial(_delta_kernel, t_valid=rows_in, activated=activated)
    row_spec = lambda width: pl.BlockSpec((None, rows_in, width), lambda b, c: (b, c, 0))
    state_spec = pl.BlockSpec((None, B_V_HEADS, B_HEAD, B_HEAD), lambda b, c: (b, 0, 0, 0))
    const_spec = lambda a: pl.BlockSpec(a.shape, lambda b, c: (0,) * a.ndim)
    out_specs = [row_spec(v_width), state_spec]
    out_shape = [jax.ShapeDtypeStruct((bq, t, v_width), BF16),
                 jax.ShapeDtypeStruct((bq, B_V_HEADS, B_HEAD, B_HEAD), F32)]
    scratch = [pltpu.VMEM((B_V_HEADS, B_HEAD, B_HEAD), F32)]
    if activated:
        in_specs = [row_spec(n), row_spec(LANES), const_spec(norm_g), state_spec]
        args = (proj, gates, norm_g, s0)
    else:
        cstate8, conv_w, alog_l, dtb_l = raw
        conv_dim = conv_w.shape[1]
        halo = SUBLANES * (4 // proj.dtype.itemsize)
        per_halo = max(rows_in // halo, 1)
        in_specs = [
            row_spec(n),
            pl.BlockSpec((None, halo, n), lambda b, c: (b, jnp.maximum(c * per_halo - 1, 0), 0)),
            row_spec(LANES),
            pl.BlockSpec((None, SUBLANES, conv_dim), lambda b, c: (b, 0, 0)),
            const_spec(conv_w), const_spec(alog_l), const_spec(dtb_l), const_spec(norm_g), state_spec]
        args = (proj, proj, gates, cstate8, conv_w, alog_l, dtb_l, norm_g, s0)
        out_specs.append(pl.BlockSpec((None, SUBLANES, conv_dim), lambda b, c: (b, 0, 0)))
        out_shape.append(jax.ShapeDtypeStruct((bq, SUBLANES, conv_dim), F32))
        scratch.append(pltpu.VMEM((max(rows_in, B_CHUNK) + halo, conv_dim), F32))
    return pl.pallas_call(
        kern,
        grid=(bq, nchunk),
        in_specs=in_specs,
        out_specs=out_specs,
        out_shape=out_shape,
        scratch_shapes=scratch,
        compiler_params=_params("arbitrary", "arbitrary"),
        name="gated_deltanet",
    )(*args)


def _left_rows(state):
    return jnp.pad(state, ((0, 0), (SUBLANES - state.shape[1], 0), (0, 0)))


def _rope_tables(pos):
    half = A_HEAD_DIM // 2
    inv = ROPE_THETA ** (-jnp.arange(half, dtype=F32) / half)
    ang = pos.astype(F32)[:, None] * inv[None, :]
    cos, sin = jnp.cos(ang), jnp.sin(ang)
    reps = LANES // A_HEAD_DIM
    cos_t = jnp.tile(jnp.concatenate([cos, cos], axis=1), (1, reps))
    sin_t = jnp.tile(jnp.concatenate([-sin, sin], axis=1), (1, reps))
    return cos_t[None], sin_t[None]


def _trunk(x, mods, per_row, pos, a_caches, b_ssm, b_conv, f_conv, wts):
    seqs, t, d = x.shape
    (norm_mix_g, norm_ffn_g, norm_final_g, a_w_in, a_w_out, b_w_in, b_conv_w, b_a_log, b_dt_bias,
     b_norm_g, b_w_out, ffn_w_up, ffn_conv_w, ffn_conv_b, ffn_w_down, expand) = wts
    d_ff = ffn_w_down.shape[1]
    width = A_HEADS * A_HEAD_DIM

    def rows_view(a):
        return a.reshape(1, seqs * t, a.shape[2]) if per_row else a

    def seq_view(a):
        return a.reshape(seqs, t, a.shape[2])

    def mod_view(v):
        if per_row:
            return jnp.repeat(v, t, axis=0)[None]
        return v[:, None, :]

    cos_t, sin_t = _rope_tables(pos)
    prompt = a_caches is None
    if prompt:
        assert t % max(w for w, _ in A_GROUPS) == 0
        perms = _perm_matrices()
        unperms = jnp.swapaxes(perms, 1, 2)
    if per_row:
        cos_t, sin_t = jnp.tile(cos_t, (1, seqs, 1)), jnp.tile(sin_t, (1, seqs, 1))
    gfin = norm_final_g.reshape(1, d)
    xr = rows_view(x)
    new_a, new_ssm, new_bconv, new_fconv = [], None, None, []
    depth = norm_mix_g.shape[0]
    for layer in range(depth):
        sh1, sc1, g1, sh2, sc2, g2 = (mod_view(v) for v in mods[layer])
        gmix = norm_mix_g[layer].reshape(1, d)
        if layer % 2 == 0:
            outs, lses = [], []
            if prompt:
                tail_rows = max(w for w, _ in A_GROUPS)
                w_vt = jnp.transpose(a_w_in.reshape(d, len(A_GROUPS), 3, width)[:, :, 2], (1, 2, 0))
                qkv, tail_t = _nm_rope_call(xr, gmix, sc1, sh1, a_w_in, cos_t, sin_t, tail_rows, perms,
                                            w_vt)
                for gi, (win, dil) in enumerate(A_GROUPS):
                    o_g, lse_g = _attn_prompt_call(qkv, gi, win, dil)
                    outs.append(o_g.reshape(seqs, t, width))
                    lses.append(lse_g.reshape(seqs, t, LANES))
                    for kind in (0, 1):
                        keep = tail_t[:, 2 * gi + kind, :, tail_rows - win:]
                        keep = keep.reshape(seqs, A_HEADS, A_HEAD_DIM, win)
                        new_a.append(jnp.transpose(keep, (0, 3, 1, 2))[None])
                xr = _attn_merge_proj_call(outs, lses, expand, a_w_out[0], xr, g1, gfin, False,
                                           unperms)
            else:
                (tail,) = _nm_rope_call(xr, gmix, sc1, sh1, a_w_in, cos_t, sin_t, seqs * t)
                for gi, (win, dil) in enumerate(A_GROUPS):
                    as_t = lambda c: jnp.transpose(c[0], (0, 2, 3, 1)).reshape(seqs, width, c.shape[2])
                    k_new, v_new, o_g, lse_g = _attn_sample_call(
                        seq_view(tail), gi, win, dil, as_t(a_caches[2 * gi]), as_t(a_caches[2 * gi + 1]))
                    outs.append(rows_view(o_g))
                    lses.append(rows_view(lse_g))
                    for buf in (k_new, v_new):
                        buf = buf.reshape(seqs, A_HEADS, A_HEAD_DIM, buf.shape[2])
                        new_a.append(jnp.transpose(buf, (0, 3, 1, 2))[None])
                xr = _attn_merge_proj_call(outs, lses, expand, a_w_out[0], xr, g1, gfin, False)
        else:
            conv_w = b_conv_w[0]
            conv_dim = conv_w.shape[1]
            conv0 = jnp.zeros((seqs, conv_w.shape[0] - 1, conv_dim), F32) if b_conv is None else b_conv[0]
            s0 = jnp.zeros((seqs, B_V_HEADS, B_HEAD, B_HEAD), F32) if b_ssm is None else b_ssm[0]
            lane_pad = lambda v: jnp.pad(v.reshape(1, -1), ((0, 0), (B_V_HEADS, LANES - 2 * B_V_HEADS)))
            alog_l, dtb_l = lane_pad(b_a_log[0]), lane_pad(b_dt_bias[0])
            norm_g = b_norm_g[0].reshape(1, B_HEAD)
            if prompt:
                proj, gates, conv_tail = _nm_delta_act_call(
                    xr, gmix, sc1, sh1, b_w_in[0], _left_rows(conv0), conv_w, alog_l, dtb_l)
                o_b, ssm1 = _delta_call(proj, gates, norm_g, s0)
                cnew = conv_tail[:, -1]
            else:
                proj, gates, _ = _nm_delta_in_call(
                    xr, gmix, sc1, sh1, b_w_in[0], conv_dim + B_V_HEADS * B_HEAD, conv_dim, F32)
                o_b, ssm1, cnew = _delta_call(seq_view(proj), seq_view(gates), norm_g, s0,
                                              (_left_rows(conv0), conv_w, alog_l, dtb_l))
            new_ssm = ssm1[None]
            new_bconv = cnew[None, :, SUBLANES - (conv_w.shape[0] - 1):, :]
            xr = _proj_res_call(rows_view(o_b), b_w_out[0], xr, g1, gfin, False)
        gffn = norm_ffn_g[layer].reshape(1, d)
        keep = ffn_conv_w.shape[1] - 1
        f0 = jnp.zeros((seqs, keep, d_ff), F32) if f_conv is None else f_conv[layer]
        conv_b = ffn_conv_b[layer].reshape(1, d_ff)
        mg2 = mods[layer][5][:, None, :]
        final = layer == depth - 1
        if prompt:
            xr, gate_tail = _ffn_fused_call(xr, gffn, sc2, sh2, ffn_w_up, _left_rows(f0),
                                            ffn_conv_w[layer], conv_b, ffn_w_down, mg2, gfin, final, layer)
            new_fconv.append(gate_tail[:, -1, SUBLANES - keep:, :])
        else:
            gate_pre, val = _nm_ffn_call(xr, gffn, sc2, sh2, ffn_w_up, layer)
            new_fconv.append(seq_view(gate_pre)[:, t - keep:, :])
            xr = _ffn_down_short_call(gate_pre, _left_rows(f0), val, ffn_conv_w[layer], conv_b,
                                      ffn_w_down, xr, g2, gfin, final, t, layer)
    return seq_view(xr), new_a, new_ssm, new_bconv, jnp.stack(new_fconv)


def kernel(x_prompt, x_sample, c_prompt, c_sample, cache_a_k_w128, cache_a_v_w128, cache_a_k_w512, cache_a_v_w512, cache_a_k_w2048, cache_a_v_w2048, state_b_ssm, state_b_conv, state_ffn_conv, norm_mix_g, norm_ffn_g, norm_final_g, w_mod, b_mod, a_w_in, a_w_out, b_w_in, b_conv_w, b_a_log, b_dt_bias, b_norm_g, b_w_out, ffn_w_up, ffn_conv_w, ffn_conv_b, ffn_w_down):
    bp, s, d = x_prompt.shape
    db, t_new, _ = x_sample.shape
    depth = w_mod.shape[0]

    c_all = jnp.concatenate([c_prompt, c_sample], axis=0)
    c_all = jnp.pad(c_all, ((0, (-c_all.shape[0]) % SUBLANES), (0, 0)))
    mod = _mod_call(c_all, w_mod, b_mod)

    def mods_of(lo, hi):
        return [[mod[l, lo:hi, k * d:(k + 1) * d] for k in range(6)] for l in range(depth)]

    b_w_in_p = jnp.pad(b_w_in, ((0, 0), (0, 0), (0, (-b_w_in.shape[2]) % LANES)))
    heads = jnp.arange(A_HEADS * A_HEAD_DIM) // A_HEAD_DIM
    half = (jnp.arange(LANES)[:, None] == heads[None, :]).astype(BF16)
    expand = jnp.concatenate([half, half], axis=0)
    wts = (norm_mix_g, norm_ffn_g, norm_final_g, a_w_in[0].astype(BF16), a_w_out.astype(BF16),
           b_w_in_p.astype(BF16), b_conv_w, b_a_log, b_dt_bias, b_norm_g, b_w_out.astype(BF16),
           ffn_w_up.astype(BF16), ffn_conv_w, ffn_conv_b, ffn_w_down.astype(BF16), expand)

    pos_p = jnp.arange(s, dtype=jnp.int32)
    pos_s = PAST_LEN + jnp.arange(t_new, dtype=jnp.int32)
    y_p, a_p, ssm_p, bconv_p, fconv_p = _trunk(
        x_prompt, mods_of(0, bp), False, pos_p, None, None, None, None, wts)
    a_caches = [cache_a_k_w128, cache_a_v_w128, cache_a_k_w512, cache_a_v_w512,
                cache_a_k_w2048, cache_a_v_w2048]
    y_s, a_s, ssm_s, bconv_s, fconv_s = _trunk(
        x_sample, mods_of(bp, bp + db), True, pos_s, a_caches, state_b_ssm, state_b_conv,
        state_ffn_conv, wts)
    return (y_p, y_s,
            a_p[0], a_s[0], a_p[1], a_s[1], a_p[2], a_s[2], a_p[3], a_s[3], a_p[4], a_s[4],
            a_p[5], a_s[5], ssm_p, ssm_s, bconv_p, bconv_s, fconv_p, fconv_s)
```
